```python
import jax, jax.numpy as jnp
from jax import lax
import numpy as np

D_MODEL = 1024
BATCH = 16
SEQ = 4096
DEPTH = 4

N_MIXERS = 4
D_FF = 2816
RMS_EPS = 1e-6
LN_EPS = 1e-5
CONV_WIDTH = 31
FOX_HEADS = 16
FOX_HEAD_DIM = D_MODEL // FOX_HEADS
FOX_BLOCK = 128
HGRN_EXPAND = 128
HGRN_HEADS = D_MODEL // HGRN_EXPAND
HGRN_DK = HGRN_HEADS * HGRN_EXPAND
HGRN_DV = D_MODEL
HGRN_HEAD_DV = HGRN_DV // HGRN_HEADS
HGRN_CHUNK = 32
POOL_WINDOWS = (2, 4, 8, 16)
POOL_GROUP = D_MODEL // len(POOL_WINDOWS)

kernel_name = "hybrid_conv_fox_hgrn2_pool_macaron"


def rms_norm(x, g):
    x32 = x.astype(jnp.float32)
    y = x32 * lax.rsqrt(jnp.mean(x32 * x32, axis=-1, keepdims=True) + RMS_EPS)
    return (y * g.astype(jnp.float32)).astype(x.dtype)


def layer_norm(x, g, b):
    x32 = x.astype(jnp.float32)
    mu = jnp.mean(x32, axis=-1, keepdims=True)
    xc = x32 - mu
    y = xc * lax.rsqrt(jnp.mean(xc * xc, axis=-1, keepdims=True) + LN_EPS)
    return (y * g.astype(jnp.float32) + b.astype(jnp.float32)).astype(x.dtype)


def swiglu(h, w_gate, w_up, w_down):
    return (jax.nn.silu(h @ w_gate) * (h @ w_up)) @ w_down


def conv_module(h, w_in, b_in, dw, dw_b, ln_g, ln_b, w_out):
    a, b = jnp.split(h @ w_in + b_in, 2, axis=-1)
    u = a * jax.nn.sigmoid(b)
    u = lax.conv_general_dilated(
        u, dw[:, None, :], window_strides=(1,),
        padding=((CONV_WIDTH - 1, 0),),
        dimension_numbers=('NWC', 'WIO', 'NWC'),
        feature_group_count=D_MODEL) + dw_b
    u = jax.nn.silu(layer_norm(u, ln_g, ln_b))
    return u @ w_out


def fox_attention(h, w_in, b_f, w_out):
    B, S, _ = h.shape
    proj = h @ w_in
    def heads(t):
        return t.reshape(B, S, FOX_HEADS, FOX_HEAD_DIM).transpose(0, 2, 1, 3)
    q = heads(proj[..., :D_MODEL])
    k = heads(proj[..., D_MODEL:2 * D_MODEL])
    v = heads(proj[..., 2 * D_MODEL:3 * D_MODEL])
    log_f = jax.nn.log_sigmoid((proj[..., 3 * D_MODEL:] + b_f).astype(jnp.float32))
    c = jnp.cumsum(log_f, axis=1).transpose(0, 2, 1)
    scale = FOX_HEAD_DIM ** -0.5
    outs = []
    for blk in range(S // FOX_BLOCK):
        q0, q1 = blk * FOX_BLOCK, (blk + 1) * FOX_BLOCK
        logits = (jnp.einsum('bhqd,bhkd->bhqk', q[:, :, q0:q1], k[:, :, :q1]).astype(jnp.float32) * scale
                  + c[:, :, q0:q1, None] - c[:, :, None, :q1])
        causal = (q0 + jnp.arange(FOX_BLOCK))[:, None] >= jnp.arange(q1)[None, :]
        p = jax.nn.softmax(jnp.where(causal, logits, -jnp.inf), axis=-1)
        outs.append(jnp.einsum('bhqk,bhkd->bhqd', p.astype(v.dtype), v[:, :, :q1]))
    o = jnp.concatenate(outs, axis=2).transpose(0, 2, 1, 3).reshape(B, S, D_MODEL)
    return o @ w_out


def hgrn2_mixer(h, w_in, lb, norm_g, w_out):
    B, S, _ = h.shape
    proj = h @ w_in
    q = jax.nn.silu(proj[..., :HGRN_DK]).astype(jnp.float32)
    f_raw = proj[..., HGRN_DK:2 * HGRN_DK].astype(jnp.float32)
    i_in = proj[..., 2 * HGRN_DK:2 * HGRN_DK + HGRN_DV].astype(jnp.float32)
    g_out = proj[..., 2 * HGRN_DK + HGRN_DV:]
    lb = lb.astype(jnp.float32)
    log_f = jnp.logaddexp(jnp.log(lb), jnp.log1p(-lb) + jax.nn.log_sigmoid(f_raw))
    k = (1.0 - lb) * jax.nn.sigmoid(-f_raw)
    n_chunks = S // HGRN_CHUNK

    def to_chunks(t, d):
        return t.reshape(B, n_chunks, HGRN_CHUNK, HGRN_HEADS, d).transpose(1, 0, 3, 2, 4)

    qc = to_chunks(q, HGRN_EXPAND)
    kc = to_chunks(k, HGRN_EXPAND)
    gc = to_chunks(log_f, HGRN_EXPAND)
    vc = to_chunks(i_in, HGRN_HEAD_DV)
    causal = jnp.tril(jnp.ones((HGRN_CHUNK, HGRN_CHUNK), dtype=bool))[:, :, None]

    def step(state, inp):
        q_t, k_t, g_t, v_t = inp
        G = jnp.cumsum(g_t, axis=2)
        o_inter = jnp.einsum('bhtk,bhkv->bhtv', q_t * jnp.exp(G), state)
        diff = G[:, :, :, None, :] - G[:, :, None, :, :]
        decay = jnp.exp(jnp.where(causal, diff, -jnp.inf))
        A = jnp.einsum('bhtk,bhtsk,bhsk->bhts', q_t, decay, k_t)
        o = o_inter + jnp.einsum('bhts,bhsv->bhtv', A, v_t)
        G_last = G[:, :, -1:, :]
        state = (jnp.exp(G_last[:, :, 0, :])[..., None] * state
                 + jnp.einsum('bhsk,bhsv->bhkv', k_t * jnp.exp(G_last - G), v_t))
        return state, o

    state0 = jnp.zeros((B, HGRN_HEADS, HGRN_EXPAND, HGRN_HEAD_DV), jnp.float32)
    _, o = lax.scan(step, state0, (qc, kc, gc, vc))
    o = o.transpose(1, 0, 3, 2, 4).reshape(B, S, HGRN_HEADS, HGRN_HEAD_DV)
    o = o * lax.rsqrt(jnp.mean(o * o, axis=-1, keepdims=True) + RMS_EPS)
    o = o * norm_g.astype(jnp.float32).reshape(HGRN_HEADS, HGRN_HEAD_DV)
    o = o.reshape(B, S, HGRN_DV) * jax.nn.silu(g_out.astype(jnp.float32))
    return o.astype(h.dtype) @ w_out


def pool_mixer(h, w, scale):
    B, S, _ = h.shape
    h32 = h.astype(jnp.float32)
    pos = jnp.arange(1, S + 1, dtype=jnp.float32)[None, :, None]
    outs = []
    for gi, win in enumerate(POOL_WINDOWS):
        xg = h32[..., gi * POOL_GROUP:(gi + 1) * POOL_GROUP]
        cs = jnp.cumsum(xg, axis=1)
        lag = jnp.pad(cs, ((0, 0), (win, 0), (0, 0)))[:, :S]
        mean = (cs - lag) / jnp.minimum(pos, float(win))
        outs.append((mean - xg).astype(h.dtype) @ w[gi])
    return jnp.concatenate(outs, axis=-1) * scale


def _fwd_setup_inputs(seed: int = 0) -> dict:
    key = jax.random.key(seed)
    ks = jax.random.split(key, 32)
    n_a, n_b, n_c, n_d = (len(range(m, DEPTH, N_MIXERS)) for m in range(N_MIXERS))
    f32 = jnp.float32

    def w(k, shape, fan_in):
        return jax.random.normal(k, shape, f32) * fan_in ** -0.5

    def gain(k, shape):
        return 1.0 + 0.02 * jax.random.normal(k, shape, f32)

    def bias(k, shape):
        return 0.02 * jax.random.normal(k, shape, f32)

    return {
        "x": jax.random.normal(ks[0], (BATCH, SEQ, D_MODEL), f32),
        "ffn_norm": gain(ks[1], (DEPTH, 2, D_MODEL)),
        "ffn_w_gate": w(ks[2], (DEPTH, 2, D_MODEL, D_FF), D_MODEL),
        "ffn_w_up": w(ks[3], (DEPTH, 2, D_MODEL, D_FF), D_MODEL),
        "ffn_w_down": w(ks[4], (DEPTH, 2, D_FF, D_MODEL), D_FF),
        "mix_norm": gain(ks[5], (DEPTH, D_MODEL)),
        "final_norm": gain(ks[6], (D_MODEL,)),
        "conv_w_in": w(ks[7], (n_a, D_MODEL, 2 * D_MODEL), D_MODEL),
        "conv_b_in": bias(ks[8], (n_a, 2 * D_MODEL)),
        "conv_dw": w(ks[9], (n_a, CONV_WIDTH, D_MODEL), CONV_WIDTH),
        "conv_dw_b": bias(ks[10], (n_a, D_MODEL)),
        "conv_ln_g": gain(ks[11], (n_a, D_MODEL)),
        "conv_ln_b": bias(ks[12], (n_a, D_MODEL)),
        "conv_w_out": w(ks[13], (n_a, D_MODEL, D_MODEL), D_MODEL),
        "fox_w_in": w(ks[14], (n_b, D_MODEL, 3 * D_MODEL + FOX_HEADS), D_MODEL),
        "fox_b_f": 1.0 + 3.0 * jax.random.uniform(ks[15], (n_b, FOX_HEADS), f32),
        "fox_w_out": w(ks[16], (n_b, D_MODEL, D_MODEL), D_MODEL),
        "hgrn_w_in": w(ks[17], (n_c, D_MODEL, 2 * HGRN_DK + 2 * HGRN_DV), D_MODEL),
        "hgrn_lb_logits": 0.1 * jax.random.normal(ks[18], (DEPTH, HGRN_DK), f32),
        "hgrn_norm": gain(ks[19], (n_c, HGRN_DV)),
        "hgrn_w_out": w(ks[20], (n_c, HGRN_DV, D_MODEL), HGRN_DV),
        "pool_w": w(ks[21], (n_d, len(POOL_WINDOWS), POOL_GROUP, POOL_GROUP), POOL_GROUP),
        "pool_scale": 1.0 + 0.1 * jax.random.normal(ks[22], (n_d, D_MODEL), f32),
    }


def _fwd_reference(x, ffn_norm, ffn_w_gate, ffn_w_up, ffn_w_down, mix_norm, final_norm,
              conv_w_in, conv_b_in, conv_dw, conv_dw_b, conv_ln_g, conv_ln_b, conv_w_out,
              fox_w_in, fox_b_f, fox_w_out,
              hgrn_w_in, hgrn_lb_logits, hgrn_norm, hgrn_w_out,
              pool_w, pool_scale):
    p = jax.nn.softmax(hgrn_lb_logits.astype(jnp.float32), axis=0)
    lower_bounds = jnp.cumsum(p, axis=0) - p[0]
    for i in range(DEPTH):
        m, j = i % N_MIXERS, i // N_MIXERS
        x = x + 0.5 * swiglu(rms_norm(x, ffn_norm[i, 0]), ffn_w_gate[i, 0], ffn_w_up[i, 0], ffn_w_down[i, 0])
        h = rms_norm(x, mix_norm[i])
        if m == 0:
            y = conv_module(h, conv_w_in[j], conv_b_in[j], conv_dw[j], conv_dw_b[j],
                            conv_ln_g[j], conv_ln_b[j], conv_w_out[j])
        elif m == 1:
            y = fox_attention(h, fox_w_in[j], fox_b_f[j], fox_w_out[j])
        elif m == 2:
            y = hgrn2_mixer(h, hgrn_w_in[j], lower_bounds[i], hgrn_norm[j], hgrn_w_out[j])
        else:
            y = pool_mixer(h, pool_w[j], pool_scale[j])
        x = x + y
        x = x + 0.5 * swiglu(rms_norm(x, ffn_norm[i, 1]), ffn_w_gate[i, 1], ffn_w_up[i, 1], ffn_w_down[i, 1])
    return rms_norm(x, final_norm)


import jax as _jax
import jax.numpy as _jnp

TWIN_FORMAT = 'train_step'
FWD_PARAMS = ['x', 'ffn_norm', 'ffn_w_gate', 'ffn_w_up', 'ffn_w_down', 'mix_norm', 'final_norm', 'conv_w_in', 'conv_b_in', 'conv_dw', 'conv_dw_b', 'conv_ln_g', 'conv_ln_b', 'conv_w_out', 'fox_w_in', 'fox_b_f', 'fox_w_out', 'hgrn_w_in', 'hgrn_lb_logits', 'hgrn_norm', 'hgrn_w_out', 'pool_w', 'pool_scale']
TWIN_WEIGHTS = ['ffn_norm', 'ffn_w_gate', 'ffn_w_up', 'ffn_w_down', 'mix_norm', 'final_norm', 'conv_w_in', 'conv_b_in', 'conv_dw', 'conv_dw_b', 'conv_ln_g', 'conv_ln_b', 'conv_w_out', 'fox_w_in', 'fox_b_f', 'fox_w_out', 'hgrn_w_in', 'hgrn_lb_logits', 'hgrn_norm', 'hgrn_w_out', 'pool_w', 'pool_scale']
TWIN_DIFF_INPUT = 'x'
TWIN_INPUTS = ['x', 'ffn_norm', 'ffn_w_gate', 'ffn_w_up', 'ffn_w_down', 'mix_norm', 'final_norm', 'conv_w_in', 'conv_b_in', 'conv_dw', 'conv_dw_b', 'conv_ln_g', 'conv_ln_b', 'conv_w_out', 'fox_w_in', 'fox_b_f', 'fox_w_out', 'hgrn_w_in', 'hgrn_lb_logits', 'hgrn_norm', 'hgrn_w_out', 'pool_w', 'pool_scale', 'loss_target', 'm_ffn_norm', 'm_ffn_w_gate', 'm_ffn_w_up', 'm_ffn_w_down', 'm_mix_norm', 'm_final_norm', 'm_conv_w_in', 'm_conv_b_in', 'm_conv_dw', 'm_conv_dw_b', 'm_conv_ln_g', 'm_conv_ln_b', 'm_conv_w_out', 'm_fox_w_in', 'm_fox_b_f', 'm_fox_w_out', 'm_hgrn_w_in', 'm_hgrn_lb_logits', 'm_hgrn_norm', 'm_hgrn_w_out', 'm_pool_w', 'm_pool_scale', 'v_ffn_norm', 'v_ffn_w_gate', 'v_ffn_w_up', 'v_ffn_w_down', 'v_mix_norm', 'v_final_norm', 'v_conv_w_in', 'v_conv_b_in', 'v_conv_dw', 'v_conv_dw_b', 'v_conv_ln_g', 'v_conv_ln_b', 'v_conv_w_out', 'v_fox_w_in', 'v_fox_b_f', 'v_fox_w_out', 'v_hgrn_w_in', 'v_hgrn_lb_logits', 'v_hgrn_norm', 'v_hgrn_w_out', 'v_pool_w', 'v_pool_scale']
TWIN_OUTPUTS = ['loss', 'grad_x', 'grad_ffn_norm', 'grad_ffn_w_gate', 'grad_ffn_w_up', 'grad_ffn_w_down', 'grad_mix_norm', 'grad_final_norm', 'grad_conv_w_in', 'grad_conv_b_in', 'grad_conv_dw', 'grad_conv_dw_b', 'grad_conv_ln_g', 'grad_conv_ln_b', 'grad_conv_w_out', 'grad_fox_w_in', 'grad_fox_b_f', 'grad_fox_w_out', 'grad_hgrn_w_in', 'grad_hgrn_lb_logits', 'grad_hgrn_norm', 'grad_hgrn_w_out', 'grad_pool_w', 'grad_pool_scale', 'delta_ffn_norm', 'delta_ffn_w_gate', 'delta_ffn_w_up', 'delta_ffn_w_down', 'delta_mix_norm', 'delta_final_norm', 'delta_conv_w_in', 'delta_conv_b_in', 'delta_conv_dw', 'delta_conv_dw_b', 'delta_conv_ln_g', 'delta_conv_ln_b', 'delta_conv_w_out', 'delta_fox_w_in', 'delta_fox_b_f', 'delta_fox_w_out', 'delta_hgrn_w_in', 'delta_hgrn_lb_logits', 'delta_hgrn_norm', 'delta_hgrn_w_out', 'delta_pool_w', 'delta_pool_scale', 'new_m_ffn_norm', 'new_m_ffn_w_gate', 'new_m_ffn_w_up', 'new_m_ffn_w_down', 'new_m_mix_norm', 'new_m_final_norm', 'new_m_conv_w_in', 'new_m_conv_b_in', 'new_m_conv_dw', 'new_m_conv_dw_b', 'new_m_conv_ln_g', 'new_m_conv_ln_b', 'new_m_conv_w_out', 'new_m_fox_w_in', 'new_m_fox_b_f', 'new_m_fox_w_out', 'new_m_hgrn_w_in', 'new_m_hgrn_lb_logits', 'new_m_hgrn_norm', 'new_m_hgrn_w_out', 'new_m_pool_w', 'new_m_pool_scale', 'new_v_ffn_norm', 'new_v_ffn_w_gate', 'new_v_ffn_w_up', 'new_v_ffn_w_down', 'new_v_mix_norm', 'new_v_final_norm', 'new_v_conv_w_in', 'new_v_conv_b_in', 'new_v_conv_dw', 'new_v_conv_dw_b', 'new_v_conv_ln_g', 'new_v_conv_ln_b', 'new_v_conv_w_out', 'new_v_fox_w_in', 'new_v_fox_b_f', 'new_v_fox_w_out', 'new_v_hgrn_w_in', 'new_v_hgrn_lb_logits', 'new_v_hgrn_norm', 'new_v_hgrn_w_out', 'new_v_pool_w', 'new_v_pool_scale']
TWIN_LEAF_KINDS = {'loss': 'loss', 'grad_x': 'grad_x', 'grad_ffn_norm': 'grad_w', 'grad_ffn_w_gate': 'grad_w', 'grad_ffn_w_up': 'grad_w', 'grad_ffn_w_down': 'grad_w', 'grad_mix_norm': 'grad_w', 'grad_final_norm': 'grad_w', 'grad_conv_w_in': 'grad_w', 'grad_conv_b_in': 'grad_w', 'grad_conv_dw': 'grad_w', 'grad_conv_dw_b': 'grad_w', 'grad_conv_ln_g': 'grad_w', 'grad_conv_ln_b': 'grad_w', 'grad_conv_w_out': 'grad_w', 'grad_fox_w_in': 'grad_w', 'grad_fox_b_f': 'grad_w', 'grad_fox_w_out': 'grad_w', 'grad_hgrn_w_in': 'grad_w', 'grad_hgrn_lb_logits': 'grad_w', 'grad_hgrn_norm': 'grad_w', 'grad_hgrn_w_out': 'grad_w', 'grad_pool_w': 'grad_w', 'grad_pool_scale': 'grad_w', 'delta_ffn_norm': 'delta_w', 'delta_ffn_w_gate': 'delta_w', 'delta_ffn_w_up': 'delta_w', 'delta_ffn_w_down': 'delta_w', 'delta_mix_norm': 'delta_w', 'delta_final_norm': 'delta_w', 'delta_conv_w_in': 'delta_w', 'delta_conv_b_in': 'delta_w', 'delta_conv_dw': 'delta_w', 'delta_conv_dw_b': 'delta_w', 'delta_conv_ln_g': 'delta_w', 'delta_conv_ln_b': 'delta_w', 'delta_conv_w_out': 'delta_w', 'delta_fox_w_in': 'delta_w', 'delta_fox_b_f': 'delta_w', 'delta_fox_w_out': 'delta_w', 'delta_hgrn_w_in': 'delta_w', 'delta_hgrn_lb_logits': 'delta_w', 'delta_hgrn_norm': 'delta_w', 'delta_hgrn_w_out': 'delta_w', 'delta_pool_w': 'delta_w', 'delta_pool_scale': 'delta_w', 'new_m_ffn_norm': 'new_m', 'new_m_ffn_w_gate': 'new_m', 'new_m_ffn_w_up': 'new_m', 'new_m_ffn_w_down': 'new_m', 'new_m_mix_norm': 'new_m', 'new_m_final_norm': 'new_m', 'new_m_conv_w_in': 'new_m', 'new_m_conv_b_in': 'new_m', 'new_m_conv_dw': 'new_m', 'new_m_conv_dw_b': 'new_m', 'new_m_conv_ln_g': 'new_m', 'new_m_conv_ln_b': 'new_m', 'new_m_conv_w_out': 'new_m', 'new_m_fox_w_in': 'new_m', 'new_m_fox_b_f': 'new_m', 'new_m_fox_w_out': 'new_m', 'new_m_hgrn_w_in': 'new_m', 'new_m_hgrn_lb_logits': 'new_m', 'new_m_hgrn_norm': 'new_m', 'new_m_hgrn_w_out': 'new_m', 'new_m_pool_w': 'new_m', 'new_m_pool_scale': 'new_m', 'new_v_ffn_norm': 'new_v', 'new_v_ffn_w_gate': 'new_v', 'new_v_ffn_w_up': 'new_v', 'new_v_ffn_w_down': 'new_v', 'new_v_mix_norm': 'new_v', 'new_v_final_norm': 'new_v', 'new_v_conv_w_in': 'new_v', 'new_v_conv_b_in': 'new_v', 'new_v_conv_dw': 'new_v', 'new_v_conv_dw_b': 'new_v', 'new_v_conv_ln_g': 'new_v', 'new_v_conv_ln_b': 'new_v', 'new_v_conv_w_out': 'new_v', 'new_v_fox_w_in': 'new_v', 'new_v_fox_b_f': 'new_v', 'new_v_fox_w_out': 'new_v', 'new_v_hgrn_w_in': 'new_v', 'new_v_hgrn_lb_logits': 'new_v', 'new_v_hgrn_norm': 'new_v', 'new_v_hgrn_w_out': 'new_v', 'new_v_pool_w': 'new_v', 'new_v_pool_scale': 'new_v'}


def _forward(args):
    return _fwd_reference(*[args[k] for k in FWD_PARAMS])


def _output_shape():
    out = _jax.eval_shape(lambda: _forward(_fwd_setup_inputs(0)))
    return out.shape, out.dtype

N_MICROBATCH = 1
ADAM_LR = 0.001
ADAM_B1 = 0.9
ADAM_B2 = 0.999
ADAM_EPS = 1e-08
ADAM_WD = 0.01
ADAM_STEP = 10
PER_EXAMPLE_BATCH_AXIS = {'x': 0, 'loss_target': 0}
SHARED_INPUTS = []
_WEIGHT_DTYPES = {'ffn_norm': _jnp.float32, 'ffn_w_gate': _jnp.float32, 'ffn_w_up': _jnp.float32, 'ffn_w_down': _jnp.float32, 'mix_norm': _jnp.float32, 'final_norm': _jnp.float32, 'conv_w_in': _jnp.float32, 'conv_b_in': _jnp.float32, 'conv_dw': _jnp.float32, 'conv_dw_b': _jnp.float32, 'conv_ln_g': _jnp.float32, 'conv_ln_b': _jnp.float32, 'conv_w_out': _jnp.float32, 'fox_w_in': _jnp.float32, 'fox_b_f': _jnp.float32, 'fox_w_out': _jnp.float32, 'hgrn_w_in': _jnp.float32, 'hgrn_lb_logits': _jnp.float32, 'hgrn_norm': _jnp.float32, 'hgrn_w_out': _jnp.float32, 'pool_w': _jnp.float32, 'pool_scale': _jnp.float32}
MOMENT_SCALE = {'ffn_norm': 9.743060e-02, 'ffn_w_gate': 4.085129e-02, 'ffn_w_up': 3.961724e-02, 'ffn_w_down': 6.565627e-02, 'mix_norm': 1.522322e-01, 'final_norm': 6.431377e+01, 'conv_w_in': 1.289173e-01, 'conv_b_in': 1.801741e-01, 'conv_dw': 1.695003e-01, 'conv_dw_b': 4.578476e-01, 'conv_ln_g': 2.229762e-01, 'conv_ln_b': 2.167099e-01, 'conv_w_out': 1.716271e-01, 'fox_w_in': 7.315736e-02, 'fox_b_f': 4.498524e-01, 'fox_w_out': 9.318229e-02, 'hgrn_w_in': 7.651823e-02, 'hgrn_lb_logits': 4.834058e-03, 'hgrn_norm': 1.223750e-01, 'hgrn_w_out': 1.054983e-01, 'pool_w': 1.246652e-01, 'pool_scale': 3.483650e-01}


def _to_microbatches(a, axis):
    t = _jnp.moveaxis(a, axis, 0)
    t = t.reshape((N_MICROBATCH, t.shape[0] // N_MICROBATCH) + t.shape[1:])
    return _jnp.moveaxis(t, 1, axis + 1)


def setup_inputs(seed: int = 0) -> dict:
    inp = _fwd_setup_inputs(seed)
    key = _jax.random.fold_in(_jax.random.key(seed), 7919)
    shape, _ = _output_shape()
    out = dict(inp)
    out["loss_target"] = _jax.random.normal(_jax.random.fold_in(key, 0), shape, _jnp.float32)
    for i, name in enumerate(TWIN_WEIGHTS):
        w = inp[name].astype(_jnp.float32)
        if MOMENT_SCALE is None:
            s = _jnp.sqrt(_jnp.mean(_jnp.square(w)) + 1e-30)
        else:
            s = MOMENT_SCALE[name]
        km, kv = _jax.random.split(_jax.random.fold_in(key, i + 1))
        out[name] = w
        out["m_" + name] = s * _jax.random.normal(km, w.shape, _jnp.float32)
        out["v_" + name] = (s * s) * _jax.random.uniform(kv, w.shape, _jnp.float32, 0.5, 1.5)
    if N_MICROBATCH > 1:
        for name, axis in PER_EXAMPLE_BATCH_AXIS.items():
            out[name] = _to_microbatches(out[name], axis)
    return {'x': out['x'], 'ffn_norm': out['ffn_norm'], 'ffn_w_gate': out['ffn_w_gate'], 'ffn_w_up': out['ffn_w_up'], 'ffn_w_down': out['ffn_w_down'], 'mix_norm': out['mix_norm'], 'final_norm': out['final_norm'], 'conv_w_in': out['conv_w_in'], 'conv_b_in': out['conv_b_in'], 'conv_dw': out['conv_dw'], 'conv_dw_b': out['conv_dw_b'], 'conv_ln_g': out['conv_ln_g'], 'conv_ln_b': out['conv_ln_b'], 'conv_w_out': out['conv_w_out'], 'fox_w_in': out['fox_w_in'], 'fox_b_f': out['fox_b_f'], 'fox_w_out': out['fox_w_out'], 'hgrn_w_in': out['hgrn_w_in'], 'hgrn_lb_logits': out['hgrn_lb_logits'], 'hgrn_norm': out['hgrn_norm'], 'hgrn_w_out': out['hgrn_w_out'], 'pool_w': out['pool_w'], 'pool_scale': out['pool_scale'], 'loss_target': out['loss_target'], 'm_ffn_norm': out['m_ffn_norm'], 'm_ffn_w_gate': out['m_ffn_w_gate'], 'm_ffn_w_up': out['m_ffn_w_up'], 'm_ffn_w_down': out['m_ffn_w_down'], 'm_mix_norm': out['m_mix_norm'], 'm_final_norm': out['m_final_norm'], 'm_conv_w_in': out['m_conv_w_in'], 'm_conv_b_in': out['m_conv_b_in'], 'm_conv_dw': out['m_conv_dw'], 'm_conv_dw_b': out['m_conv_dw_b'], 'm_conv_ln_g': out['m_conv_ln_g'], 'm_conv_ln_b': out['m_conv_ln_b'], 'm_conv_w_out': out['m_conv_w_out'], 'm_fox_w_in': out['m_fox_w_in'], 'm_fox_b_f': out['m_fox_b_f'], 'm_fox_w_out': out['m_fox_w_out'], 'm_hgrn_w_in': out['m_hgrn_w_in'], 'm_hgrn_lb_logits': out['m_hgrn_lb_logits'], 'm_hgrn_norm': out['m_hgrn_norm'], 'm_hgrn_w_out': out['m_hgrn_w_out'], 'm_pool_w': out['m_pool_w'], 'm_pool_scale': out['m_pool_scale'], 'v_ffn_norm': out['v_ffn_norm'], 'v_ffn_w_gate': out['v_ffn_w_gate'], 'v_ffn_w_up': out['v_ffn_w_up'], 'v_ffn_w_down': out['v_ffn_w_down'], 'v_mix_norm': out['v_mix_norm'], 'v_final_norm': out['v_final_norm'], 'v_conv_w_in': out['v_conv_w_in'], 'v_conv_b_in': out['v_conv_b_in'], 'v_conv_dw': out['v_conv_dw'], 'v_conv_dw_b': out['v_conv_dw_b'], 'v_conv_ln_g': out['v_conv_ln_g'], 'v_conv_ln_b': out['v_conv_ln_b'], 'v_conv_w_out': out['v_conv_w_out'], 'v_fox_w_in': out['v_fox_w_in'], 'v_fox_b_f': out['v_fox_b_f'], 'v_fox_w_out': out['v_fox_w_out'], 'v_hgrn_w_in': out['v_hgrn_w_in'], 'v_hgrn_lb_logits': out['v_hgrn_lb_logits'], 'v_hgrn_norm': out['v_hgrn_norm'], 'v_hgrn_w_out': out['v_hgrn_w_out'], 'v_pool_w': out['v_pool_w'], 'v_pool_scale': out['v_pool_scale']}


def _loss(weights, diff, rest, loss_target):
    with _jax.named_scope("forward"):
        args = {**rest, TWIN_DIFF_INPUT: diff, **{k: w.astype(_WEIGHT_DTYPES[k]) for k, w in weights.items()}}
        y = _forward(args)
    with _jax.named_scope("loss_head"):
        err = _jnp.square(y.astype(_jnp.float32) - loss_target)
        return 0.5 * _jnp.sum(_jnp.mean(err, axis=-1)) if err.ndim else 0.5 * err


def _adamw(w, g, m, v):
    m = ADAM_B1 * m + (1.0 - ADAM_B1) * g
    v = ADAM_B2 * v + (1.0 - ADAM_B2) * _jnp.square(g)
    m_hat = m / (1.0 - ADAM_B1 ** ADAM_STEP)
    v_hat = v / (1.0 - ADAM_B2 ** ADAM_STEP)
    delta = -ADAM_LR * (m_hat / (_jnp.sqrt(v_hat) + ADAM_EPS) + ADAM_WD * w)
    return delta, m, v


def reference(x, ffn_norm, ffn_w_gate, ffn_w_up, ffn_w_down, mix_norm, final_norm, conv_w_in, conv_b_in, conv_dw, conv_dw_b, conv_ln_g, conv_ln_b, conv_w_out, fox_w_in, fox_b_f, fox_w_out, hgrn_w_in, hgrn_lb_logits, hgrn_norm, hgrn_w_out, pool_w, pool_scale, loss_target, m_ffn_norm, m_ffn_w_gate, m_ffn_w_up, m_ffn_w_down, m_mix_norm, m_final_norm, m_conv_w_in, m_conv_b_in, m_conv_dw, m_conv_dw_b, m_conv_ln_g, m_conv_ln_b, m_conv_w_out, m_fox_w_in, m_fox_b_f, m_fox_w_out, m_hgrn_w_in, m_hgrn_lb_logits, m_hgrn_norm, m_hgrn_w_out, m_pool_w, m_pool_scale, v_ffn_norm, v_ffn_w_gate, v_ffn_w_up, v_ffn_w_down, v_mix_norm, v_final_norm, v_conv_w_in, v_conv_b_in, v_conv_dw, v_conv_dw_b, v_conv_ln_g, v_conv_ln_b, v_conv_w_out, v_fox_w_in, v_fox_b_f, v_fox_w_out, v_hgrn_w_in, v_hgrn_lb_logits, v_hgrn_norm, v_hgrn_w_out, v_pool_w, v_pool_scale):
    given = dict(x=x, ffn_norm=ffn_norm, ffn_w_gate=ffn_w_gate, ffn_w_up=ffn_w_up, ffn_w_down=ffn_w_down, mix_norm=mix_norm, final_norm=final_norm, conv_w_in=conv_w_in, conv_b_in=conv_b_in, conv_dw=conv_dw, conv_dw_b=conv_dw_b, conv_ln_g=conv_ln_g, conv_ln_b=conv_ln_b, conv_w_out=conv_w_out, fox_w_in=fox_w_in, fox_b_f=fox_b_f, fox_w_out=fox_w_out, hgrn_w_in=hgrn_w_in, hgrn_lb_logits=hgrn_lb_logits, hgrn_norm=hgrn_norm, hgrn_w_out=hgrn_w_out, pool_w=pool_w, pool_scale=pool_scale, loss_target=loss_target, m_ffn_norm=m_ffn_norm, m_ffn_w_gate=m_ffn_w_gate, m_ffn_w_up=m_ffn_w_up, m_ffn_w_down=m_ffn_w_down, m_mix_norm=m_mix_norm, m_final_norm=m_final_norm, m_conv_w_in=m_conv_w_in, m_conv_b_in=m_conv_b_in, m_conv_dw=m_conv_dw, m_conv_dw_b=m_conv_dw_b, m_conv_ln_g=m_conv_ln_g, m_conv_ln_b=m_conv_ln_b, m_conv_w_out=m_conv_w_out, m_fox_w_in=m_fox_w_in, m_fox_b_f=m_fox_b_f, m_fox_w_out=m_fox_w_out, m_hgrn_w_in=m_hgrn_w_in, m_hgrn_lb_logits=m_hgrn_lb_logits, m_hgrn_norm=m_hgrn_norm, m_hgrn_w_out=m_hgrn_w_out, m_pool_w=m_pool_w, m_pool_scale=m_pool_scale, v_ffn_norm=v_ffn_norm, v_ffn_w_gate=v_ffn_w_gate, v_ffn_w_up=v_ffn_w_up, v_ffn_w_down=v_ffn_w_down, v_mix_norm=v_mix_norm, v_final_norm=v_final_norm, v_conv_w_in=v_conv_w_in, v_conv_b_in=v_conv_b_in, v_conv_dw=v_conv_dw, v_conv_dw_b=v_conv_dw_b, v_conv_ln_g=v_conv_ln_g, v_conv_ln_b=v_conv_ln_b, v_conv_w_out=v_conv_w_out, v_fox_w_in=v_fox_w_in, v_fox_b_f=v_fox_b_f, v_fox_w_out=v_fox_w_out, v_hgrn_w_in=v_hgrn_w_in, v_hgrn_lb_logits=v_hgrn_lb_logits, v_hgrn_norm=v_hgrn_norm, v_hgrn_w_out=v_hgrn_w_out, v_pool_w=v_pool_w, v_pool_scale=v_pool_scale)
    weights = {n: given[n] for n in TWIN_WEIGHTS}
    shared = {n: given[n] for n in SHARED_INPUTS}
    per_example = {n: given[n] for n in ['x']}
    grad_fn = _jax.value_and_grad(_loss, argnums=(0, 1))

    def one_microbatch(ex, loss_target):
        ex = dict(ex)
        diff = ex.pop(TWIN_DIFF_INPUT)
        return grad_fn(weights, diff, {**shared, **ex}, loss_target)

    if N_MICROBATCH == 1:
        loss, (grad_w, grad_x) = one_microbatch(per_example, given["loss_target"])
    else:
        def body(carry, xs):
            loss_sum, grad_sum = carry
            l_k, (gw_k, gx_k) = one_microbatch(xs[0], xs[1])
            with _jax.named_scope("update"):
                return (loss_sum + l_k, _jax.tree.map(_jnp.add, grad_sum, gw_k)), gx_k

        init = (_jnp.zeros((), _jnp.float32), _jax.tree.map(_jnp.zeros_like, weights))
        (loss, grad_w), grad_x = _jax.lax.scan(body, init, (per_example, given["loss_target"]))
    with _jax.named_scope("update"):
        delta_w, new_m, new_v = {}, {}, {}
        for n in TWIN_WEIGHTS:
            delta_w[n], new_m[n], new_v[n] = _adamw(weights[n], grad_w[n], given["m_" + n], given["v_" + n])
    return (loss, grad_x, *[grad_w[n] for n in TWIN_WEIGHTS], *[delta_w[n] for n in TWIN_WEIGHTS],
            *[new_m[n] for n in TWIN_WEIGHTS], *[new_v[n] for n in TWIN_WEIGHTS])
```

```python
import functools
import math

import jax
import jax.numpy as jnp
from jax import lax
from jax.experimental import pallas as pl
from jax.experimental.pallas import tpu as pltpu

F32 = jnp.float32
BF16 = jnp.bfloat16
MESH = pl.DeviceIdType.MESH

N_DEV = 8
RMS_EPS = 1e-6
LN_EPS = 1e-5
FOX_HEADS = 16
HGRN_EXPAND = 128
HGRN_CHUNK = 32
POOL_WINDOWS = (2, 4, 8, 16)
ADAM_LR, ADAM_B1, ADAM_B2, ADAM_EPS, ADAM_WD, ADAM_STEP = 0.001, 0.9, 0.999, 1e-08, 0.01, 10
LANES = 128
VMEM_LIMIT_MB = 48

SHARDED = dict(
    ffn_norm=2, ffn_w_gate=3, ffn_w_up=3, ffn_w_down=2, conv_w_in=2, conv_dw=2, conv_w_out=1, fox_w_in=2, fox_w_out=1,
    hgrn_w_in=2, hgrn_norm=1, hgrn_w_out=1, pool_w=2, pool_scale=1)
BIG = ("ffn_w_gate", "ffn_w_up", "ffn_w_down", "conv_w_in", "conv_w_out", "fox_w_in", "fox_w_out", "hgrn_w_in",
       "hgrn_w_out", "pool_w")
SMALL = ("ffn_norm", "conv_dw", "hgrn_norm", "pool_scale")
REPL = ("mix_norm", "final_norm", "conv_b_in", "conv_dw_b", "conv_ln_g", "conv_ln_b", "fox_b_f", "hgrn_lb_logits")
WEIGHTS = ("ffn_norm", "ffn_w_gate", "ffn_w_up", "ffn_w_down", "mix_norm", "final_norm", "conv_w_in", "conv_b_in",
           "conv_dw", "conv_dw_b", "conv_ln_g", "conv_ln_b", "conv_w_out", "fox_w_in", "fox_b_f", "fox_w_out",
           "hgrn_w_in", "hgrn_lb_logits", "hgrn_norm", "hgrn_w_out", "pool_w", "pool_scale")


def _tile(n, pref, mult=8):
    if n <= pref:
        return n
    for t in range(pref - pref % mult, 0, -mult):
        if n % t == 0:
            return t
    return n


def _sig(x):
    return 1.0 / (1.0 + jnp.exp(-x))


def _call(body, name, grid, in_specs, out_specs, out_shape, scratch=(), sem=None):
    params = dict(vmem_limit_bytes=VMEM_LIMIT_MB << 20)
    if sem is not None:
        params["dimension_semantics"] = sem
    return pl.pallas_call(body, name=name, grid=grid, in_specs=in_specs, out_specs=out_specs, out_shape=out_shape,
                          scratch_shapes=list(scratch), compiler_params=pltpu.CompilerParams(**params))


def _sds(shape, dtype):
    return jax.ShapeDtypeStruct(tuple(shape), dtype)


_DN = {"nn": (((1,), (0,)), ((), ())), "nt": (((1,), (1,)), ((), ())), "tn": (((0,), (0,)), ((), ()))}


def _dot(a, b, mode="nn"):
    return lax.dot_general(a.astype(BF16), b.astype(BF16), _DN[mode], preferred_element_type=F32)


def _roll(x, shift, axis=0):
    n = x.shape[axis]
    shift = shift % n
    return x if shift == 0 else pltpu.roll(x, shift, axis)


def _scan_rows(x, reverse=False):
    n = x.shape[0]
    row = lax.broadcasted_iota(jnp.int32, x.shape, 0)
    sh = 1
    while sh < n:
        if reverse:
            x = x + jnp.where(row < n - sh, _roll(x, n - sh), 0.0)
        else:
            x = x + jnp.where(row >= sh, _roll(x, sh), 0.0)
        sh *= 2
    return x


def _mm(name, mode, pairs, pair_specs, grid, k_axis, out_shapes, out_specs, acc_shape=None, extras=(), extra_specs=(),
        epilogue=None, alpha=1.0, sem=None):
    npair, nex, nout = len(pairs), len(extras), len(out_shapes)
    nk = grid[k_axis] if k_axis is not None else 1

    def body(*refs):
        prs = refs[:2 * npair]
        ex = refs[2 * npair:2 * npair + nex]
        outs = refs[2 * npair + nex:2 * npair + nex + nout]

        def partial():
            p = None
            for i in range(npair):
                d = _dot(prs[2 * i][...], prs[2 * i + 1][...], mode)
                p = d if p is None else p + d
            return p

        def finish(res):
            if alpha != 1.0:
                res = res * alpha
            if epilogue is None:
                outs[0][...] = res.astype(outs[0].dtype)
            else:
                epilogue(res, ex, outs)

        if k_axis is None:
            finish(partial())
        else:
            acc = refs[-1]
            k = pl.program_id(k_axis)

            @pl.when(k == 0)
            def _():
                acc[...] = partial()

            @pl.when(k > 0)
            def _():
                acc[...] += partial()

            @pl.when(k == nk - 1)
            def _():
                finish(acc[...])

    if sem is None:
        sem = tuple("arbitrary" if i == k_axis else "parallel" for i in range(len(grid)))
    scratch = [pltpu.VMEM(acc_shape, F32)] if k_axis is not None else []
    flat = [t for p in pairs for t in p]
    flat_specs = [s for p in pair_specs for s in p]
    return _call(body, name, grid, flat_specs + list(extra_specs), out_specs, out_shapes, scratch, sem)(*flat, *extras)


def _mm_nn(name, a, b, out_dtype, tm=512, tn=512):
    M, K = a.shape
    N = b.shape[1]
    tm, tn = _tile(M, tm), _tile(N, tn, LANES)
    return _mm(name, "nn", [(a, b)], [(pl.BlockSpec((tm, K), lambda i, j: (i, 0)), pl.BlockSpec((K, tn), lambda i, j: (0, j)))],
               (M // tm, N // tn), None, [_sds((M, N), out_dtype)], [pl.BlockSpec((tm, tn), lambda i, j: (i, j))])[0]


def _mm_res(name, u, w, x, alpha, tm=512):
    T, K = u.shape
    D = w.shape[1]
    tm = _tile(T, tm)

    def epi(res, ex, outs):
        outs[0][...] = ex[0][...] + res

    return _mm(name, "nn", [(u, w)], [(pl.BlockSpec((tm, K), lambda i: (i, 0)), pl.BlockSpec((K, D), lambda i: (0, 0)))],
               (T // tm,), None, [_sds((T, D), F32)], [pl.BlockSpec((tm, D), lambda i: (i, 0))],
               extras=[x], extra_specs=[pl.BlockSpec((tm, D), lambda i: (i, 0))], epilogue=epi, alpha=alpha)[0]


def _mm_nt(name, pairs, out_dtype, tm=512, tn=512, alpha=1.0):
    M = pairs[0][0].shape[0]
    N = pairs[0][1].shape[0]
    tm, tn = _tile(M, tm), _tile(N, tn, LANES)
    specs = [(pl.BlockSpec((tm, a.shape[1]), lambda i, j: (i, 0)), pl.BlockSpec((tn, b.shape[1]), lambda i, j: (j, 0)))
             for a, b in pairs]
    return _mm(name, "nt", pairs, specs, (M // tm, N // tn), None, [_sds((M, N), out_dtype)],
               [pl.BlockSpec((tm, tn), lambda i, j: (i, j))], alpha=alpha)[0]


def _mm_nt3(name, a3, b3, tm=512, tn=512):
    S, M, K = a3.shape
    N = b3.shape[1]
    tm, tn = _tile(M, tm), _tile(N, tn, LANES)
    return _mm(name, "nt", [(a3, b3)],
               [(pl.BlockSpec((None, tm, K), lambda i, j, s: (s, i, 0)), pl.BlockSpec((None, tn, K), lambda i, j, s: (s, j, 0)))],
               (M // tm, N // tn, S), 2, [_sds((M, N), F32)], [pl.BlockSpec((tm, tn), lambda i, j, s: (i, j))],
               acc_shape=(tm, tn))[0]


def _mm_tn(name, a, b, alpha=1.0, tm=512, tn=512, tk=512):
    T, M = a.shape
    N = b.shape[1]
    tm, tn, tk = _tile(M, tm, LANES), _tile(N, tn, LANES), _tile(T, tk)
    return _mm(name, "tn", [(a, b)],
               [(pl.BlockSpec((tk, tm), lambda i, j, k: (k, i)), pl.BlockSpec((tk, tn), lambda i, j, k: (k, j)))],
               (M // tm, N // tn, T // tk), 2, [_sds((M, N), F32)], [pl.BlockSpec((tm, tn), lambda i, j, k: (i, j))],
               acc_shape=(tm, tn), alpha=alpha)[0]


def _mm_tn3(name, a, b3, tm=512, tn=512, tk=512):
    T, M = a.shape
    S, _, N = b3.shape
    tm, tn, tk = _tile(M, tm, LANES), _tile(N, tn, LANES), _tile(T, tk)
    return _mm(name, "tn", [(a, b3)],
               [(pl.BlockSpec((tk, tm), lambda s, i, j, k: (k, i)), pl.BlockSpec((None, tk, tn), lambda s, i, j, k: (s, k, j)))],
               (S, M // tm, N // tn, T // tk), 3, [_sds((S, M, N), F32)],
               [pl.BlockSpec((None, tm, tn), lambda s, i, j, k: (s, i, j))], acc_shape=(tm, tn))[0]


def _rms_fwd(name, x, g):
    T, D = x.shape
    tt = _tile(T, 512)

    def body(x_ref, g_ref, h_ref):
        xv = x_ref[...]
        r = lax.rsqrt(jnp.mean(xv * xv, axis=-1, keepdims=True) + RMS_EPS)
        h_ref[...] = (xv * r * g_ref[...]).astype(h_ref.dtype)

    row = pl.BlockSpec((tt, D), lambda i: (i, 0))
    return _call(body, name, (T // tt,), [row, pl.BlockSpec((1, D), lambda i: (0, 0))], row, _sds((T, D), BF16))(x, g)


def _rms_bwd(name, dh, x, g, dx_in):
    T, D = x.shape
    tt = _tile(T, 512)

    def body(dh_ref, x_ref, g_ref, dxi_ref, dx_ref, dg_ref):
        xv = x_ref[...]
        r = lax.rsqrt(jnp.mean(xv * xv, axis=-1, keepdims=True) + RMS_EPS)
        xh = xv * r
        dhv = dh_ref[...]
        dxh = dhv * g_ref[...]
        dx_ref[...] = dxi_ref[...] + r * (dxh - xh * jnp.mean(dxh * xh, axis=-1, keepdims=True))
        part = jnp.sum(dhv * xh, axis=0, keepdims=True)

        @pl.when(pl.program_id(0) == 0)
        def _():
            dg_ref[...] = part

        @pl.when(pl.program_id(0) > 0)
        def _():
            dg_ref[...] += part

    row = pl.BlockSpec((tt, D), lambda i: (i, 0))
    vec = pl.BlockSpec((1, D), lambda i: (0, 0))
    return _call(body, name, (T // tt,), [row, row, vec, row], [row, vec], [_sds((T, D), F32), _sds((1, D), F32)],
                 sem=("arbitrary",))(dh, x, g, dx_in)


def _loss_head(x, g, tgt):
    T, D = x.shape
    tt = _tile(T, 512)

    def body(x_ref, g_ref, t_ref, loss_ref, dx_ref, dg_ref):
        xv = x_ref[...]
        r = lax.rsqrt(jnp.mean(xv * xv, axis=-1, keepdims=True) + RMS_EPS)
        xh = xv * r
        e = xh * g_ref[...] - t_ref[...]
        lp = 0.5 * jnp.sum(jnp.sum(e * e, axis=-1, keepdims=True), axis=0, keepdims=True) / D
        dy = e / D
        dxh = dy * g_ref[...]
        dx_ref[...] = r * (dxh - xh * jnp.mean(dxh * xh, axis=-1, keepdims=True))
        part = jnp.sum(dy * xh, axis=0, keepdims=True)

        @pl.when(pl.program_id(0) == 0)
        def _():
            dg_ref[...] = part
            loss_ref[...] = lp

        @pl.when(pl.program_id(0) > 0)
        def _():
            dg_ref[...] += part
            loss_ref[...] += lp

    row = pl.BlockSpec((tt, D), lambda i: (i, 0))
    vec = pl.BlockSpec((1, D), lambda i: (0, 0))
    one = pl.BlockSpec((1, 1), lambda i: (0, 0))
    return _call(body, "loss_head", (T // tt,), [row, vec, row], [one, row, vec],
                 [_sds((1, 1), F32), _sds((T, D), F32), _sds((1, D), F32)], sem=("arbitrary",))(x, g, tgt)


def _colsum3(name, a3):
    S, T, N = a3.shape
    tt = _tile(T, 512)

    def body(a_ref, o_ref):
        part = jnp.sum(a_ref[...].astype(F32), axis=0, keepdims=True)

        @pl.when(pl.program_id(1) == 0)
        def _():
            o_ref[...] = part

        @pl.when(pl.program_id(1) > 0)
        def _():
            o_ref[...] += part

    return _call(body, name, (S, T // tt), [pl.BlockSpec((None, tt, N), lambda s, i: (s, i, 0))],
                 pl.BlockSpec((None, 1, N), lambda s, i: (s, 0, 0)), _sds((S, 1, N), F32), sem=("parallel", "arbitrary"))(a3)


def _glu_mm(name, h, w2, bias2, mode, u_dtype, tm=1024, tn=256):
    T, K = h.shape
    N = w2.shape[2]
    tm, tn = _tile(T, tm), _tile(N, tn, LANES)
    has_bias = bias2 is not None

    def body(*refs):
        h_ref, w_ref = refs[0], refs[1]
        ab_ref, u_ref = refs[-2], refs[-1]
        hv = h_ref[...]
        a = _dot(hv, w_ref[0])
        b = _dot(hv, w_ref[1])
        if has_bias:
            a = a + refs[2][0]
            b = b + refs[2][1]
        u = a * _sig(a) * b if mode == "swiglu" else a * _sig(b)
        ab_ref[0] = a.astype(BF16)
        ab_ref[1] = b.astype(BF16)
        u_ref[...] = u.astype(u_ref.dtype)

    in_specs = [pl.BlockSpec((tm, K), lambda i, j: (i, 0)), pl.BlockSpec((2, K, tn), lambda i, j: (0, 0, j))]
    args = [h, w2]
    if has_bias:
        in_specs.append(pl.BlockSpec((2, 1, tn), lambda i, j: (0, 0, j)))
        args.append(bias2)
    return _call(body, name, (T // tm, N // tn), in_specs,
                 [pl.BlockSpec((2, tm, tn), lambda i, j: (0, i, j)), pl.BlockSpec((tm, tn), lambda i, j: (i, j))],
                 [_sds((2, T, N), BF16), _sds((T, N), u_dtype)], sem=("parallel", "parallel"))(*args)


def _swiglu_bwd_mm(name, dx, wd, ab, alpha, tm=1024, tn=256):
    T, D = dx.shape
    N = wd.shape[0]
    tm, tn = _tile(T, tm), _tile(N, tn, LANES)

    def epi(du, ex, outs):
        a = ex[0][0].astype(F32)
        b = ex[0][1].astype(F32)
        sg = _sig(a)
        sa = a * sg
        outs[0][0] = (du * b * (sg * (1.0 + a * (1.0 - sg)))).astype(BF16)
        outs[0][1] = (du * sa).astype(BF16)
        outs[1][...] = (sa * b).astype(BF16)

    return _mm(name, "nt", [(dx, wd)], [(pl.BlockSpec((tm, D), lambda i, j: (i, 0)), pl.BlockSpec((tn, D), lambda i, j: (j, 0)))],
               (T // tm, N // tn), None, [_sds((2, T, N), BF16), _sds((T, N), BF16)],
               [pl.BlockSpec((2, tm, tn), lambda i, j: (0, i, j)), pl.BlockSpec((tm, tn), lambda i, j: (i, j))],
               extras=[ab], extra_specs=[pl.BlockSpec((2, tm, tn), lambda i, j: (0, i, j))], epilogue=epi, alpha=alpha)


def _ffn_fwd(tag, x, g, wgu, wd):
    h = _rms_fwd(f"ffn_rms_{tag}", x, g)
    ab, u = _glu_mm(f"ffn_gu_{tag}", h, wgu, None, "swiglu", BF16)
    return _mm_res(f"ffn_down_{tag}", u, wd, x, 0.5), (x, h, ab)


def _ffn_bwd(tag, dx, saved, g, wgu, wd):
    x, h, ab = saved
    dab, u = _swiglu_bwd_mm(f"ffn_dgu_{tag}", dx, wd, ab, 0.5)
    dwd = _mm_tn(f"ffn_dwd_{tag}", u, dx, alpha=0.5)
    dwgu = _mm_tn3(f"ffn_dwgu_{tag}", h, dab)
    dh = _mm_nt3(f"ffn_dh_{tag}", dab, wgu)
    dx2, dg = _rms_bwd(f"ffn_drms_{tag}", dh, x, g, dx)
    return dx2, dg, dwgu, dwd


HALO = 32


def _conv_fwd(u3, dw32, dwb, lng, lnb):
    B, S, D = u3.shape
    W = dw32.shape[0]
    taps = CONV_TAPS
    tt = _tile(S, 256, HALO)
    hb = tt // HALO

    def body(u_ref, halo_ref, dw_ref, dwb_ref, g_ref, b_ref, v_ref, s_ref):
        i = pl.program_id(1)
        halo = jnp.where(i > 0, halo_ref[...], 0.0)
        ext = jnp.concatenate([halo, u_ref[...]], axis=0)
        acc = jnp.zeros((tt, D), F32) + dwb_ref[...]
        for j in range(taps):
            acc = acc + dw_ref[pl.ds(j, 1), :] * _roll(ext, taps - 1 - j)[HALO:]
        v_ref[...] = acc
        mu = jnp.mean(acc, axis=-1, keepdims=True)
        xc = acc - mu
        ln = xc * lax.rsqrt(jnp.mean(xc * xc, axis=-1, keepdims=True) + LN_EPS) * g_ref[...] + b_ref[...]
        s_ref[...] = (ln * _sig(ln)).astype(BF16)

    main = pl.BlockSpec((None, tt, D), lambda b, i: (b, i, 0))
    halo = pl.BlockSpec((None, HALO, D), lambda b, i: (b, jnp.maximum(i * hb - 1, 0), 0))
    vec = pl.BlockSpec((1, D), lambda b, i: (0, 0))
    return _call(body, "conv_fwd", (B, S // tt), [main, halo, pl.BlockSpec((W, D), lambda b, i: (0, 0)), vec, vec, vec],
                 [main, main], [_sds((B, S, D), F32), _sds((B, S, D), BF16)], sem=("parallel", "parallel"))(
        u3, u3, dw32, dwb, lng, lnb)


def _conv_bwd_ln(v, ds, lng, lnb):
    T, D = v.shape
    tt = _tile(T, 256)

    def body(v_ref, ds_ref, g_ref, b_ref, dv_ref, red_ref):
        vv = v_ref[...]
        mu = jnp.mean(vv, axis=-1, keepdims=True)
        xc = vv - mu
        rstd = lax.rsqrt(jnp.mean(xc * xc, axis=-1, keepdims=True) + LN_EPS)
        xh = xc * rstd
        ln = xh * g_ref[...] + b_ref[...]
        sg = _sig(ln)
        dln = ds_ref[...] * (sg * (1.0 + ln * (1.0 - sg)))
        dxh = dln * g_ref[...]
        dv = rstd * (dxh - jnp.mean(dxh, axis=-1, keepdims=True) - xh * jnp.mean(dxh * xh, axis=-1, keepdims=True))
        dv_ref[...] = dv
        parts = (jnp.sum(dln * xh, axis=0, keepdims=True), jnp.sum(dln, axis=0, keepdims=True),
                 jnp.sum(dv, axis=0, keepdims=True))

        @pl.when(pl.program_id(0) == 0)
        def _():
            for k in range(3):
                red_ref[k] = parts[k]

        @pl.when(pl.program_id(0) > 0)
        def _():
            for k in range(3):
                red_ref[k] += parts[k]

    row = pl.BlockSpec((tt, D), lambda i: (i, 0))
    vec = pl.BlockSpec((1, D), lambda i: (0, 0))
    return _call(body, "conv_bwd_ln", (T // tt,), [row, row, vec, vec], [row, pl.BlockSpec((3, 1, D), lambda i: (0, 0, 0))],
                 [_sds((T, D), F32), _sds((3, 1, D), F32)], sem=("arbitrary",))(v, ds, lng, lnb)


def _conv_bwd_dw(dv3, u3, ab, dw32):
    B, S, D = u3.shape
    W = dw32.shape[0]
    taps = CONV_TAPS
    tt = _tile(S, 256, HALO)
    hb = tt // HALO
    nt = S // tt
    L = tt + HALO

    def body(dv_ref, dvn_ref, u_ref, up_ref, ab_ref, dw_ref, dab_ref, ddw_ref):
        b, i = pl.program_id(0), pl.program_id(1)
        dv = dv_ref[...]
        ext_dv = jnp.concatenate([dv, jnp.where(i < nt - 1, dvn_ref[...], 0.0)], axis=0)
        ext_u = jnp.concatenate([jnp.where(i > 0, up_ref[...], 0.0), u_ref[...]], axis=0)
        du = jnp.zeros((tt, D), F32)
        first = jnp.logical_and(b == 0, i == 0)

        @pl.when(first)
        def _():
            ddw_ref[...] = jnp.zeros((W, D), F32)

        for j in range(taps):
            sh = taps - 1 - j
            du = du + dw_ref[pl.ds(j, 1), :] * _roll(ext_dv, L - sh)[:tt]
            ddw_ref[pl.ds(j, 1), :] += jnp.sum(dv * _roll(ext_u, sh)[HALO:], axis=0, keepdims=True)
        a = ab_ref[0].astype(F32)
        sb = _sig(ab_ref[1].astype(F32))
        dab_ref[0] = (du * sb).astype(BF16)
        dab_ref[1] = (du * a * sb * (1.0 - sb)).astype(BF16)

    main = pl.BlockSpec((None, tt, D), lambda b, i: (b, i, 0))
    prev = pl.BlockSpec((None, HALO, D), lambda b, i: (b, jnp.maximum(i * hb - 1, 0), 0))
    nxt = pl.BlockSpec((None, HALO, D), lambda b, i: (b, jnp.minimum((i + 1) * hb, S // HALO - 1), 0))
    abs_ = pl.BlockSpec((2, tt, D), lambda b, i: (0, b * nt + i, 0))
    wsp = pl.BlockSpec((W, D), lambda b, i: (0, 0))
    return _call(body, "conv_bwd_dw", (B, nt), [main, nxt, main, prev, abs_, wsp], [abs_, wsp],
                 [_sds((2, B * S, D), BF16), _sds((W, D), F32)], sem=("arbitrary", "arbitrary"))(dv3, dv3, u3, u3, ab, dw32)


CONV_TAPS = 31


def _conv_mixer_fwd(x, B, p):
    T, D = x.shape
    h = _rms_fwd("conv_rms", x, p["mix_norm"][0:1])
    ab, u = _glu_mm("conv_in", h, p["conv_w_in2"], p["conv_b_in2"], "glu", F32)
    v, s = _conv_fwd(u.reshape(B, T // B, D), p["conv_dw32"], p["conv_dw_b"], p["conv_ln_g"], p["conv_ln_b"])
    v, s = v.reshape(T, D), s.reshape(T, D)
    return _mm_res("conv_out", s, p["conv_w_out"], x, 1.0), (x, h, ab, u, v, s)


def _conv_mixer_bwd(dx, saved, B, p):
    x, h, ab, u, v, s = saved
    T, D = x.shape
    g = {}
    ds = _mm_nt("conv_ds", [(dx, p["conv_w_out"])], F32)
    g["conv_w_out"] = _mm_tn("conv_dwout", s, dx)
    dv, red = _conv_bwd_ln(v, ds, p["conv_ln_g"], p["conv_ln_b"])
    g["conv_ln_g"], g["conv_ln_b"], g["conv_dw_b"] = red[0], red[1], red[2]
    dab, ddw = _conv_bwd_dw(dv.reshape(B, T // B, D), u.reshape(B, T // B, D), ab, p["conv_dw32"])
    g["conv_dw"] = ddw[:CONV_TAPS][None]
    g["conv_b_in"] = _colsum3("conv_dbin", dab).reshape(1, 2 * D)
    dwin = _mm_tn3("conv_dwin", h, dab)
    g["conv_w_in"] = jnp.moveaxis(dwin, 0, 1).reshape(1, D, 2 * D)
    dh = _mm_nt3("conv_dh", dab, p["conv_w_in2"])
    dx2, dg = _rms_bwd("conv_drms", dh, x, p["mix_norm"][0:1], dx)
    return dx2, dg, g


def _log_sigmoid(z):
    return jnp.minimum(z, 0.0) - jnp.log(1.0 + jnp.exp(-jnp.abs(z)))


def _fox_gate_fwd(fl3, bf):
    B, S, N = fl3.shape
    tt = _tile(S, 512)

    def body(fl_ref, bf_ref, c_ref, carry):
        @pl.when(pl.program_id(1) == 0)
        def _():
            carry[...] = jnp.zeros((1, N), F32)

        c = _scan_rows(_log_sigmoid(fl_ref[...] + bf_ref[...])) + carry[...]
        c_ref[...] = c
        carry[...] = c_ref[pl.ds(tt - 1, 1), :]

    row = pl.BlockSpec((None, tt, N), lambda b, i: (b, i, 0))
    return _call(body, "fox_gate_fwd", (B, S // tt), [row, pl.BlockSpec((1, N), lambda b, i: (0, 0))], row,
                 _sds((B, S, N), F32), [pltpu.VMEM((1, N), F32)], sem=("parallel", "arbitrary"))(fl3, bf)


def _fox_gate_bwd(dc3, fl3, bf):
    B, S, N = fl3.shape
    tt = _tile(S, 512)
    nt = S // tt

    def body(dc_ref, fl_ref, bf_ref, dfl_ref, dbf_ref, carry):
        b, i = pl.program_id(0), pl.program_id(1)

        @pl.when(i == 0)
        def _():
            carry[...] = jnp.zeros((1, N), F32)

        dc = dc_ref[0] - dc_ref[1]
        dlf = _scan_rows(dc, reverse=True) + carry[...]
        dfl = dlf * _sig(-(fl_ref[...] + bf_ref[...]))
        dfl_ref[...] = dfl
        carry[...] += jnp.sum(dc, axis=0, keepdims=True)
        part = jnp.sum(dfl, axis=0, keepdims=True)

        @pl.when(jnp.logical_and(b == 0, i == 0))
        def _():
            dbf_ref[...] = part

        @pl.when(jnp.logical_or(b > 0, i > 0))
        def _():
            dbf_ref[...] += part

    row = pl.BlockSpec((None, tt, N), lambda b, i: (b, nt - 1 - i, 0))
    vec = pl.BlockSpec((1, N), lambda b, i: (0, 0))
    row2 = pl.BlockSpec((2, None, tt, N), lambda b, i: (0, b, nt - 1 - i, 0))
    return _call(body, "fox_gate_bwd", (B, nt), [row2, row, vec], [row, vec], [_sds((B, S, N), F32), _sds((1, N), F32)],
                 [pltpu.VMEM((1, N), F32)], sem=("arbitrary", "arbitrary"))(dc3, fl3, bf)


NEG = -1e30


def _fox_attn_fwd(q, k, v, ccol, crow, scale):
    B, H, S, dh = v.shape
    dqk = q.shape[-1]
    tq = _tile(S, 512, LANES)
    tk = tq

    def body(q_ref, k_ref, v_ref, cc_ref, cr_ref, o_ref, lse_ref):
        i = pl.program_id(2)
        qv = q_ref[...]
        cq = cc_ref[...]
        rows = i * tq + lax.broadcasted_iota(jnp.int32, (tq, tk), 0)
        cols0 = lax.broadcasted_iota(jnp.int32, (tq, tk), 1)

        def step(j, carry):
            m, l, acc = carry
            off = pl.multiple_of(j * tk, tk)
            s = _dot(qv, k_ref[pl.ds(off, tk), :], "nt") * scale + cq - cr_ref[:, pl.ds(off, tk)]
            s = jnp.where(rows >= cols0 + j * tk, s, -jnp.inf)
            m2 = jnp.maximum(m, jnp.max(s, axis=-1, keepdims=True))
            p = jnp.exp(s - m2)
            al = jnp.exp(m - m2)
            return m2, al * l + jnp.sum(p, axis=-1, keepdims=True), al * acc + _dot(p, v_ref[pl.ds(off, tk), :])

        m, l, acc = lax.fori_loop(0, i + 1, step, (jnp.full((tq, 1), NEG, F32), jnp.zeros((tq, 1), F32),
                                                   jnp.zeros((tq, dh), F32)))
        o_ref[...] = acc / l
        lse_ref[...] = m + jnp.log(l)

    qs = pl.BlockSpec((None, None, tq, dqk), lambda b, h, i: (b, h, i, 0))
    os_ = pl.BlockSpec((None, None, tq, dh), lambda b, h, i: (b, h, i, 0))
    fullk = pl.BlockSpec((None, None, S, dqk), lambda b, h, i: (b, h, 0, 0))
    full = pl.BlockSpec((None, None, S, dh), lambda b, h, i: (b, h, 0, 0))
    col = pl.BlockSpec((None, None, tq, 1), lambda b, h, i: (b, h, i, 0))
    rowv = pl.BlockSpec((None, None, 1, S), lambda b, h, i: (b, h, 0, 0))
    return _call(body, "fox_attn_fwd", (B, H, S // tq), [qs, fullk, full, col, rowv], [os_, col],
                 [_sds((B, H, S, dh), F32), _sds((B, H, S, 1), F32)], sem=("parallel", "parallel", "parallel"))(
        q, k, v, ccol, crow)


def _fox_rowstats(do, o, ccol, lse):
    B, H, S, dh = o.shape
    tq = _tile(S, 512)

    def body(do_ref, o_ref, cc_ref, lse_ref, rb_ref, dl_ref):
        rb_ref[...] = cc_ref[...] - lse_ref[...]
        dl_ref[...] = jnp.sum(do_ref[...].astype(F32) * o_ref[...].astype(F32), axis=-1, keepdims=True)

    qs = pl.BlockSpec((None, None, tq, dh), lambda b, h, i: (b, h, i, 0))
    col = pl.BlockSpec((None, None, tq, 1), lambda b, h, i: (b, h, i, 0))
    return _call(body, "fox_rowstats", (B, H, S // tq), [qs, qs, col, col], [col, col],
                 [_sds((B, H, S, 1), F32), _sds((B, H, S, 1), F32)], sem=("parallel",) * 3)(do, o, ccol, lse)


def _fox_attn_bwd(q, k, v, do, rb, delta, crow, scale):
    B, H, S, dh = v.shape
    dqk = q.shape[-1]
    tk = _tile(S, 512, LANES)
    tq = tk
    nq = S // tq

    def body(q_ref, k_ref, v_ref, do_ref, rb_ref, dl_ref, cr_ref, dq_ref, dk_ref, dv_ref):
        j = pl.program_id(2)

        @pl.when(j == 0)
        def _():
            dq_ref[...] = jnp.zeros((S, dqk), F32)

        kj, vj, ck = k_ref[...], v_ref[...], cr_ref[...]
        cols = j * tk + lax.broadcasted_iota(jnp.int32, (tq, tk), 1)
        rows0 = lax.broadcasted_iota(jnp.int32, (tq, tk), 0)

        def step(i, carry):
            dk, dv = carry
            off = pl.multiple_of(i * tq, tq)
            qi, doi = q_ref[pl.ds(off, tq), :], do_ref[pl.ds(off, tq), :]
            s = _dot(qi, kj, "nt") * scale + rb_ref[pl.ds(off, tq), :] - ck
            p = jnp.where(rows0 + i * tq >= cols, jnp.exp(s), 0.0)
            ds = p * (_dot(doi, vj, "nt") - dl_ref[pl.ds(off, tq), :])
            dq_ref[pl.ds(off, tq), :] += _dot(ds, kj)
            return dk + _dot(ds, qi, "tn"), dv + _dot(p, doi, "tn")

        dk, dv = lax.fori_loop(j, nq, step, (jnp.zeros((tk, dqk), F32), jnp.zeros((tk, dh), F32)))
        dk_ref[...] = dk * jnp.where(lax.broadcasted_iota(jnp.int32, (tk, dqk), 1) < dh, scale, 1.0)
        dv_ref[...] = dv

        @pl.when(j == S // tk - 1)
        def _():
            lane = lax.broadcasted_iota(jnp.int32, (S, dqk), 1)
            dq_ref[...] = dq_ref[...] * jnp.where(lane < dh, scale, 1.0)

    ks = pl.BlockSpec((None, None, tk, dqk), lambda b, h, j: (b, h, j, 0))
    vs = pl.BlockSpec((None, None, tk, dh), lambda b, h, j: (b, h, j, 0))
    fullq = pl.BlockSpec((None, None, S, dqk), lambda b, h, j: (b, h, 0, 0))
    full = pl.BlockSpec((None, None, S, dh), lambda b, h, j: (b, h, 0, 0))
    colf = pl.BlockSpec((None, None, S, 1), lambda b, h, j: (b, h, 0, 0))
    rowt = pl.BlockSpec((None, None, 1, tk), lambda b, h, j: (b, h, 0, j))
    return _call(body, "fox_attn_bwd", (B, H, S // tk), [fullq, ks, vs, full, colf, colf, rowt], [fullq, ks, vs],
                 [_sds((B, H, S, dqk), F32), _sds((B, H, S, dqk), F32), _sds((B, H, S, dh), F32)],
                 sem=("parallel", "parallel", "arbitrary"))(q, k, v, do, rb, delta, crow)


def _heads(t, B, H):
    T, D = t.shape
    return t.reshape(B, T // B, H, D // H).transpose(0, 2, 1, 3)


def _unheads(t):
    B, H, S, dh = t.shape
    return t.transpose(0, 2, 1, 3).reshape(B * S, H * dh)


def _fox_mixer_fwd(x, B, p):
    T, D = x.shape
    H = FOX_HEADS
    S = T // B
    scale = (D // H) ** -0.5
    h = _rms_fwd("fox_rms", x, p["mix_norm"][1:2])
    qkv = _mm_nn("fox_qkv", h, p["fox_w_qkv"], BF16)
    fl = _mm_nn("fox_fl", h, p["fox_w_f"], F32)
    c = _fox_gate_fwd(fl.reshape(B, S, LANES), p["fox_b_f128"])
    ch = c[:, :, :H].transpose(0, 2, 1)
    ccol, crow = ch[..., None], ch[:, :, None, :]
    q, k, v = _heads(qkv[:, :D], B, H), _heads(qkv[:, D:2 * D], B, H), _heads(qkv[:, 2 * D:], B, H)
    dh = D // H
    dqk = -(-(dh + 2) // LANES) * LANES
    one, zero = jnp.ones((B, H, S, 1), q.dtype), jnp.zeros((B, H, S, 1), q.dtype)
    rest = jnp.zeros((B, H, S, dqk - dh - 2), q.dtype)
    q = jnp.concatenate([q, zero, one, rest], axis=-1)
    k = jnp.concatenate([k, one, zero, rest], axis=-1)
    o, lse = _fox_attn_fwd(q, k, v, ccol, crow, scale)
    of = _unheads(o)
    return _mm_res("fox_out", of, p["fox_w_out"], x, 1.0), (x, h, fl, q, k, v, ccol, crow, o, lse, of)


def _fox_mixer_bwd(dx, saved, B, p):
    x, h, fl, q, k, v, ccol, crow, o, lse, of = saved
    T, D = x.shape
    H = FOX_HEADS
    S = T // B
    scale = (D // H) ** -0.5
    g = {}
    do = _heads(_mm_nt("fox_do", [(dx, p["fox_w_out"])], F32), B, H)
    g["fox_w_out"] = _mm_tn("fox_dwout", of, dx)
    rb, delta = _fox_rowstats(do, o, ccol, lse)
    dq, dk, dv = _fox_attn_bwd(q, k, v, do, rb, delta, crow, scale)
    dh_ = D // H
    dqkv = jnp.concatenate([_unheads(dq[..., :dh_]), _unheads(dk[..., :dh_]), _unheads(dv)], axis=1).astype(BF16)
    dc = jnp.stack([dq[..., dh_], dk[..., dh_ + 1]])
    dc = jnp.pad(dc.transpose(0, 1, 3, 2), ((0, 0), (0, 0), (0, 0), (0, LANES - H)))
    dfl, dbf = _fox_gate_bwd(dc, fl.reshape(B, S, LANES), p["fox_b_f128"])
    dfl = dfl.reshape(T, LANES)
    g["fox_b_f"] = dbf[:, :H]
    dwqkv = _mm_tn("fox_dwqkv", h, dqkv)
    dwf = _mm_tn("fox_dwf", h, dfl)
    g["fox_w_in"] = jnp.concatenate([dwqkv, dwf[:, :H]], axis=1)[None]
    dh = _mm_nt("fox_dh", [(dqkv, p["fox_w_qkv"]), (dfl, p["fox_w_f"])], F32)
    dx2, dg = _rms_bwd("fox_drms", dh, x, p["mix_norm"][1:2], dx)
    return dx2, dg, g


def _lb_fwd(logits):
    L, D = logits.shape

    def body(l_ref, lb_ref):
        z = l_ref[...]
        e = jnp.exp(z - jnp.max(z, axis=0, keepdims=True))
        p = e / jnp.sum(e, axis=0, keepdims=True)
        lb_ref[...] = jnp.sum(jnp.where(_lb_rows(z.shape), p, 0.0), axis=0, keepdims=True)

    return _call(body, "hgrn_lb", (1,), [pl.BlockSpec((L, D), lambda i: (0, 0))], pl.BlockSpec((1, D), lambda i: (0, 0)),
                 _sds((1, D), F32))(logits)


def _lb_rows(shape):
    r = lax.broadcasted_iota(jnp.int32, shape, 0)
    return jnp.logical_and(r >= 1, r <= HGRN_LAYER)


HGRN_LAYER = 2


def _lb_bwd(logits, dlb):
    L, D = logits.shape

    def body(l_ref, d_ref, o_ref):
        z = l_ref[...]
        e = jnp.exp(z - jnp.max(z, axis=0, keepdims=True))
        p = e / jnp.sum(e, axis=0, keepdims=True)
        dp = jnp.where(_lb_rows(z.shape), d_ref[...], 0.0)
        o_ref[...] = p * (dp - jnp.sum(p * dp, axis=0, keepdims=True))

    full = pl.BlockSpec((L, D), lambda i: (0, 0))
    return _call(body, "hgrn_dlb", (1,), [full, pl.BlockSpec((1, D), lambda i: (0, 0))], full, _sds((L, D), F32))(logits, dlb)


def _hgrn_gates(qr, fr, lbv):
    e = jnp.exp(-jnp.abs(fr))
    big, small = 1.0 / (1.0 + e), e / (1.0 + e)
    sf = jnp.where(fr >= 0, big, small)
    snf = jnp.where(fr >= 0, small, big)
    f = lbv + (1.0 - lbv) * sf
    sq = _sig(qr)
    return qr * sq, (1.0 - lbv) * snf, jnp.log(f), sf, snf, f, sq


def _hgrn_intra(G, q, kk, g_scr, q_scr):
    C = HGRN_CHUNK
    g_scr[...] = G
    q_scr[...] = q
    srow = lax.broadcasted_iota(jnp.int32, (C, LANES), 0)
    lane = lax.broadcasted_iota(jnp.int32, (C, LANES), 1)
    at = jnp.zeros((C, LANES), F32)
    for t in range(C):
        e = jnp.where(srow <= t, jnp.exp(g_scr[pl.ds(t, 1), :] - G), 0.0)
        col = jnp.sum(e * kk * q_scr[pl.ds(t, 1), :], axis=-1, keepdims=True)
        at = jnp.where(lane == t, col, at)
    return at


def _hgrn_fwd(proj3, lb, ng):
    B, S, D4 = proj3.shape
    D = D4 // 4
    H = D // HGRN_EXPAND
    C = HGRN_CHUNK
    R = _tile(S, 256, C)
    ncb = R // C
    dk = HGRN_EXPAND

    def body(q_ref, f_ref, i_ref, go_ref, lb_ref, ng_ref, y_ref, o_ref, st_ref, st, g_scr, q_scr):
        @pl.when(pl.program_id(2) == 0)
        def _():
            st[...] = jnp.zeros((dk, dk), F32)

        lbv = lb_ref[...]

        def chunk(c, carry):
            r0 = pl.multiple_of(c * C, C)
            rows = pl.ds(r0, C)
            q, kk, lf, *_ = _hgrn_gates(q_ref[rows, :], f_ref[rows, :], lbv)
            vv = i_ref[rows, :]
            G = _scan_rows(lf)
            at = _hgrn_intra(G, q, kk, g_scr, q_scr)
            gl = g_scr[pl.ds(C - 1, 1), :]
            stv = st[...]
            st_ref[c] = stv
            o = _dot(q * jnp.exp(G), stv, "nt") + _dot(at, vv, "tn")[:C]
            st[...] = stv * jnp.exp(gl) + _dot(vv, kk * jnp.exp(gl - G), "tn")
            o_ref[rows, :] = o
            gv = go_ref[rows, :]
            y = o * lax.rsqrt(jnp.mean(o * o, axis=-1, keepdims=True) + RMS_EPS) * ng_ref[...] * (gv * _sig(gv))
            y_ref[rows, :] = y.astype(BF16)
            return carry

        lax.fori_loop(0, ncb, chunk, 0)

    def col(k):
        return pl.BlockSpec((None, R, dk), lambda b, h, i: (b, i, h + k * H))

    vec = pl.BlockSpec((1, dk), lambda b, h, i: (0, h))
    out = pl.BlockSpec((None, R, dk), lambda b, h, i: (b, i, h))
    return _call(body, "hgrn_fwd", (B, H, S // R), [col(0), col(1), col(2), col(3), vec, vec],
                 [out, out, pl.BlockSpec((None, None, ncb, dk, dk), lambda b, h, i: (b, h, i, 0, 0))],
                 [_sds((B, S, D), BF16), _sds((B, S, D), F32), _sds((B, H, S // C, dk, dk), F32)],
                 [pltpu.VMEM((dk, dk), F32), pltpu.VMEM((C, dk), F32), pltpu.VMEM((C, dk), F32)],
                 sem=("parallel", "parallel", "arbitrary"))(proj3, proj3, proj3, proj3, lb, ng)


def _hgrn_bwd(proj3, o3, dy3, states, lb, ng):
    B, S, D4 = proj3.shape
    D = D4 // 4
    H = D // HGRN_EXPAND
    C = HGRN_CHUNK
    R = _tile(S, 256, C)
    ncb = R // C
    nb = S // R
    dk = HGRN_EXPAND

    def body(q_ref, f_ref, i_ref, go_ref, o_ref, dy_ref, st_ref, lb_ref, ng_ref,
             dp_ref, red_ref, dst, g_scr, q_scr, dq_scr, acc):
        b, i = pl.program_id(1), pl.program_id(2)

        @pl.when(i == 0)
        def _():
            dst[...] = jnp.zeros((dk, dk), F32)

        @pl.when(jnp.logical_and(b == 0, i == 0))
        def _():
            acc[...] = jnp.zeros((2, dk), F32)

        lbv = lb_ref[...]
        ngv = ng_ref[...]
        srow = lax.broadcasted_iota(jnp.int32, (C, LANES), 0)
        lane = lax.broadcasted_iota(jnp.int32, (C, LANES), 1)

        def chunk(cc, carry):
            c = ncb - 1 - cc
            r0 = pl.multiple_of(c * C, C)
            rows = pl.ds(r0, C)
            qr, fr, vv, gv = q_ref[rows, :], f_ref[rows, :], i_ref[rows, :], go_ref[rows, :]
            q, kk, lf, sf, snf, f, sq = _hgrn_gates(qr, fr, lbv)
            o = o_ref[rows, :]
            dy = dy_ref[rows, :]
            rinv = lax.rsqrt(jnp.mean(o * o, axis=-1, keepdims=True) + RMS_EPS)
            on = o * rinv
            sgv = _sig(gv)
            dz = dy * (gv * sgv)
            dp_ref[3, rows, :] = (dy * on * ngv * (sgv * (1.0 + gv * (1.0 - sgv)))).astype(BF16)
            acc[pl.ds(1, 1), :] += jnp.sum(dz * on, axis=0, keepdims=True)
            don = dz * ngv
            do = rinv * (don - on * jnp.mean(don * on, axis=-1, keepdims=True))
            G = _scan_rows(lf)
            g_scr[...] = G
            q_scr[...] = q
            gl = g_scr[pl.ds(C - 1, 1), :]
            egl = jnp.exp(gl)
            eG = jnp.exp(G)
            eK = jnp.exp(gl - G)
            qg, kg = q * eG, kk * eK
            stv = st_ref[c]
            dsv = dst[...]
            dqg = _dot(do, stv)
            do_pad = jnp.concatenate([do, jnp.zeros((LANES - C, dk), F32)], axis=0)
            dat = _dot(vv, do_pad, "nt")
            dkg = _dot(vv, dsv)
            dgl = egl * jnp.sum(stv * dsv, axis=0, keepdims=True) + jnp.sum(dkg * kg, axis=0, keepdims=True)
            at = jnp.zeros((C, LANES), F32)
            dki = jnp.zeros((C, dk), F32)
            for t in range(C):
                e = jnp.where(srow <= t, jnp.exp(g_scr[pl.ds(t, 1), :] - G), 0.0)
                qt = q_scr[pl.ds(t, 1), :]
                at = jnp.where(lane == t, jnp.sum(e * kk * qt, axis=-1, keepdims=True), at)
                z = e * jnp.sum(jnp.where(lane == t, dat, 0.0), axis=-1, keepdims=True)
                dq_scr[pl.ds(t, 1), :] = jnp.sum(z * kk, axis=0, keepdims=True)
                dki = dki + z * qt
            dqi = dq_scr[...]
            dp_ref[2, rows, :] = (_dot(at, do_pad) + _dot(kg, dsv, "nt")).astype(BF16)
            dst[...] = dsv * egl + _dot(do, qg, "tn")
            dq = dqg * eG + dqi
            dkk = dkg * eK + dki
            dG = dqg * qg - dkg * kg + q * dqi - kk * dki
            dG = dG + jnp.where(srow == C - 1, dgl, 0.0)
            dlf = _scan_rows(dG, reverse=True)
            dsf = (1.0 - lbv) * sf * snf
            dp_ref[1, rows, :] = (dlf * dsf / f - dkk * dsf).astype(BF16)
            acc[pl.ds(0, 1), :] += jnp.sum(dlf * snf / f - dkk * snf, axis=0, keepdims=True)
            dp_ref[0, rows, :] = (dq * (sq * (1.0 + qr * (1.0 - sq)))).astype(BF16)
            return carry

        lax.fori_loop(0, ncb, chunk, 0)

        @pl.when(jnp.logical_and(b == B - 1, i == nb - 1))
        def _():
            red_ref[0] = acc[pl.ds(0, 1), :]
            red_ref[1] = acc[pl.ds(1, 1), :]

    def col(k):
        return pl.BlockSpec((None, R, dk), lambda h, b, i: (b, nb - 1 - i, h + k * H))

    vec = pl.BlockSpec((1, dk), lambda h, b, i: (0, h))
    row = pl.BlockSpec((None, R, dk), lambda h, b, i: (b, nb - 1 - i, h))
    stsp = pl.BlockSpec((None, None, ncb, dk, dk), lambda h, b, i: (b, h, nb - 1 - i, 0, 0))
    outs = _call(body, "hgrn_bwd", (H, B, nb), [col(0), col(1), col(2), col(3), row, row, stsp, vec, vec],
                 [pl.BlockSpec((4, None, R, dk), lambda h, b, i: (0, b, nb - 1 - i, h)),
                  pl.BlockSpec((2, 1, dk), lambda h, b, i: (0, 0, h))],
                 [_sds((4, B, S, D), BF16), _sds((2, 1, D), F32)],
                 [pltpu.VMEM((dk, dk), F32), pltpu.VMEM((C, dk), F32), pltpu.VMEM((C, dk), F32), pltpu.VMEM((C, dk), F32),
                  pltpu.VMEM((2, dk), F32)],
                 sem=("parallel", "arbitrary", "arbitrary"))(proj3, proj3, proj3, proj3, o3, dy3, states, lb, ng)
    return outs


def _hgrn_mixer_fwd(x, B, p):
    T, D = x.shape
    S = T // B
    h = _rms_fwd("hgrn_rms", x, p["mix_norm"][2:3])
    proj = _mm_nn("hgrn_in", h, p["hgrn_w_in"], F32)
    lb = _lb_fwd(p["hgrn_lb_logits"])
    y, o, states = _hgrn_fwd(proj.reshape(B, S, 4 * D), lb, p["hgrn_norm"])
    y = y.reshape(T, D)
    return _mm_res("hgrn_out", y, p["hgrn_w_out"], x, 1.0), (x, h, proj, lb, y, o, states)


def _hgrn_mixer_bwd(dx, saved, B, p):
    x, h, proj, lb, y, o, states = saved
    T, D = x.shape
    S = T // B
    g = {}
    dy = _mm_nt("hgrn_dy", [(dx, p["hgrn_w_out"])], F32)
    g["hgrn_w_out"] = _mm_tn("hgrn_dwout", y, dx)
    dp, red = _hgrn_bwd(proj.reshape(B, S, 4 * D), o, dy.reshape(B, S, D), states, lb, p["hgrn_norm"])
    dp = dp.reshape(4, T, D)
    g["hgrn_norm"] = red[1]
    g["hgrn_lb_logits"] = _lb_bwd(p["hgrn_lb_logits"], red[0])
    dwin = _mm_tn3("hgrn_dwin", h, dp)
    g["hgrn_w_in"] = jnp.moveaxis(dwin, 0, 1).reshape(1, D, 4 * D)
    dh = _mm_nt3("hgrn_dh", dp, p["hgrn_w_in4"])
    dx2, dg = _rms_bwd("hgrn_drms", dh, x, p["mix_norm"][2:3], dx)
    return dx2, dg, g


POOL_HALO = 16


def _pool_fwd(x3, g):
    B, S, D = x3.shape
    tt = _tile(S, 256, POOL_HALO)
    hb = tt // POOL_HALO
    G = D // len(POOL_WINDOWS)

    def body(x_ref, halo_ref, g_ref, m_ref):
        i = pl.program_id(1)

        def norm(xv):
            return xv * lax.rsqrt(jnp.mean(xv * xv, axis=-1, keepdims=True) + RMS_EPS) * g_ref[...]

        hm = norm(x_ref[...])
        ext = jnp.concatenate([jnp.where(i > 0, norm(halo_ref[...]), 0.0), hm], axis=0)
        pos = (i * tt + lax.broadcasted_iota(jnp.int32, (tt, 1), 0) + 1).astype(F32)
        for gi, win in enumerate(POOL_WINDOWS):
            s = ext[:, gi * G:(gi + 1) * G]
            w = 1
            while w < win:
                s = s + _roll(s, w)
                w *= 2
            m_ref[:, gi * G:(gi + 1) * G] = (s[POOL_HALO:] / jnp.minimum(pos, float(win)) - hm[:, gi * G:(gi + 1) * G]).astype(BF16)

    main = pl.BlockSpec((None, tt, D), lambda b, i: (b, i, 0))
    halo = pl.BlockSpec((None, POOL_HALO, D), lambda b, i: (b, jnp.maximum(i * hb - 1, 0), 0))
    return _call(body, "pool_fwd", (B, S // tt), [main, halo, pl.BlockSpec((1, D), lambda b, i: (0, 0))], main,
                 _sds((B, S, D), BF16), sem=("parallel", "parallel"))(x3, x3, g)


def _pool_bwd(dm3):
    B, S, D = dm3.shape
    tt = _tile(S, 256, POOL_HALO)
    hb = tt // POOL_HALO
    nt = S // tt
    G = D // len(POOL_WINDOWS)
    L = tt + POOL_HALO

    def body(dm_ref, nxt_ref, dh_ref):
        i = pl.program_id(1)
        posm = (i * tt + lax.broadcasted_iota(jnp.int32, (tt, 1), 0) + 1).astype(F32)
        posn = ((i + 1) * tt + lax.broadcasted_iota(jnp.int32, (POOL_HALO, 1), 0) + 1).astype(F32)
        for gi, win in enumerate(POOL_WINDOWS):
            sl = slice(gi * G, (gi + 1) * G)
            dm = dm_ref[:, sl]
            s = jnp.concatenate([dm / jnp.minimum(posm, float(win)),
                                 jnp.where(i < nt - 1, nxt_ref[:, sl] / jnp.minimum(posn, float(win)), 0.0)], axis=0)
            w = 1
            while w < win:
                s = s + _roll(s, L - w)
                w *= 2
            dh_ref[:, sl] = s[:tt] - dm

    main = pl.BlockSpec((None, tt, D), lambda b, i: (b, i, 0))
    nxt = pl.BlockSpec((None, POOL_HALO, D), lambda b, i: (b, jnp.minimum((i + 1) * hb, S // POOL_HALO - 1), 0))
    return _call(body, "pool_bwd", (B, nt), [main, nxt], main, _sds((B, S, D), F32), sem=("parallel", "parallel"))(dm3, dm3)


def _pool_mixer_fwd(x, B, p):
    T, D = x.shape
    NG = len(POOL_WINDOWS)
    G = D // NG
    tm = _tile(T, 512)
    m = _pool_fwd(x.reshape(B, T // B, D), p["mix_norm"][3:4]).reshape(T, D)

    def epi(res, ex, outs):
        outs[0][...] = ex[1][...] + res * ex[0][...]

    blk = pl.BlockSpec((tm, G), lambda i, g: (i, g))
    x2 = _mm("pool_out", "nn", [(m, p["pool_w4"])], [(blk, pl.BlockSpec((None, G, G), lambda i, g: (g, 0, 0)))],
             (T // tm, NG), None, [_sds((T, D), F32)], [blk], extras=[p["pool_scale"], x],
             extra_specs=[pl.BlockSpec((1, G), lambda i, g: (0, g)), blk], epilogue=epi)[0]
    return x2, (x, m)


def _pool_mixer_bwd(dx, saved, B, p):
    x, m = saved
    T, D = x.shape
    NG = len(POOL_WINDOWS)
    G = D // NG
    tm = _tile(T, 512)
    g = {}

    def epi(zz, ex, outs):
        dy = ex[0][...]
        outs[0][...] = (dy * ex[1][...]).astype(BF16)
        part = jnp.sum(dy * zz, axis=0, keepdims=True)

        @pl.when(pl.program_id(1) == 0)
        def _():
            outs[1][...] = part

        @pl.when(pl.program_id(1) > 0)
        def _():
            outs[1][...] += part

    blk = pl.BlockSpec((tm, G), lambda g_, i: (i, g_))
    wsp = pl.BlockSpec((None, G, G), lambda g_, i: (g_, 0, 0))
    vec = pl.BlockSpec((1, G), lambda g_, i: (0, g_))
    dz, dsc = _mm("pool_dz", "nn", [(m, p["pool_w4"])], [(blk, wsp)], (NG, T // tm), None,
                  [_sds((T, D), BF16), _sds((1, D), F32)], [blk, vec], extras=[dx, p["pool_scale"]], extra_specs=[blk, vec],
                  epilogue=epi, sem=("parallel", "arbitrary"))
    g["pool_scale"] = dsc
    tk = _tile(T, 512)
    kb = pl.BlockSpec((tk, G), lambda g_, k: (k, g_))
    g["pool_w"] = _mm("pool_dw", "tn", [(m, dz)], [(kb, kb)], (NG, T // tk), 1, [_sds((NG, G, G), F32)],
                      [pl.BlockSpec((None, G, G), lambda g_, k: (g_, 0, 0))], acc_shape=(G, G))[0][None]
    blk2 = pl.BlockSpec((tm, G), lambda i, g_: (i, g_))
    dm = _mm("pool_dm", "nt", [(dz, p["pool_w4"])], [(blk2, pl.BlockSpec((None, G, G), lambda i, g_: (g_, 0, 0)))],
             (T // tm, NG), None, [_sds((T, D), F32)], [blk2])[0]
    dh = _pool_bwd(dm.reshape(B, T // B, D)).reshape(T, D)
    dx2, dg = _rms_bwd("pool_drms", dh, x, p["mix_norm"][3:4], dx)
    return dx2, dg, g


_MIXERS = ((_conv_mixer_fwd, _conv_mixer_bwd), (_fox_mixer_fwd, _fox_mixer_bwd), (_hgrn_mixer_fwd, _hgrn_mixer_bwd),
           (_pool_mixer_fwd, _pool_mixer_bwd))


def _local_step(x3, tgt3, w):
    B, S, D = x3.shape
    T = B * S
    depth = w["ffn_norm"].shape[0]
    F = w["ffn_w_gate"].shape[-1]
    H = FOX_HEADS
    p = dict(w)
    p["wgu"] = jnp.stack((w["ffn_w_gate"], w["ffn_w_up"]), axis=2)
    p["conv_w_in2"] = w["conv_w_in"][0].reshape(D, 2, D).transpose(1, 0, 2)
    p["conv_b_in2"] = w["conv_b_in"].reshape(2, 1, D)
    p["conv_dw32"] = jnp.pad(w["conv_dw"][0], ((0, HALO - CONV_TAPS), (0, 0)))
    p["conv_w_out"] = w["conv_w_out"][0]
    p["fox_w_qkv"] = w["fox_w_in"][0][:, :3 * D]
    p["fox_w_f"] = jnp.pad(w["fox_w_in"][0][:, 3 * D:], ((0, 0), (0, LANES - H)))
    p["fox_b_f128"] = jnp.pad(w["fox_b_f"], ((0, 0), (0, LANES - H)))
    p["fox_w_out"] = w["fox_w_out"][0]
    p["hgrn_w_in"] = w["hgrn_w_in"][0]
    p["hgrn_w_in4"] = w["hgrn_w_in"][0].reshape(D, 4, D).transpose(1, 0, 2)
    p["hgrn_w_out"] = w["hgrn_w_out"][0]
    p["pool_w4"] = w["pool_w"][0]

    x = x3.reshape(T, D)
    saved = []
    for i in range(depth):
        x, s0 = _ffn_fwd(f"{i}a", x, w["ffn_norm"][i, 0:1], p["wgu"][i, 0], w["ffn_w_down"][i, 0])
        x, s1 = _MIXERS[i % 4][0](x, B, p)
        x, s2 = _ffn_fwd(f"{i}b", x, w["ffn_norm"][i, 1:2], p["wgu"][i, 1], w["ffn_w_down"][i, 1])
        saved.append((s0, s1, s2))
    loss, dx, dfinal = _loss_head(x, w["final_norm"].reshape(1, D), tgt3.reshape(T, D))

    g = {"final_norm": dfinal}
    dffn_norm = [[None, None] for _ in range(depth)]
    dwgu = [[None, None] for _ in range(depth)]
    dwd = [[None, None] for _ in range(depth)]
    dmix = [None] * depth
    for i in reversed(range(depth)):
        s0, s1, s2 = saved[i]
        dx, dffn_norm[i][1], dwgu[i][1], dwd[i][1] = _ffn_bwd(f"{i}b", dx, s2, w["ffn_norm"][i, 1:2], p["wgu"][i, 1],
                                                            w["ffn_w_down"][i, 1])
        dx, dmix[i], gm = _MIXERS[i % 4][1](dx, s1, B, p)
        g.update(gm)
        dx, dffn_norm[i][0], dwgu[i][0], dwd[i][0] = _ffn_bwd(f"{i}a", dx, s0, w["ffn_norm"][i, 0:1], p["wgu"][i, 0],
                                                            w["ffn_w_down"][i, 0])
    g["ffn_norm"] = jnp.stack([jnp.stack([a[0], b[0]]) for a, b in dffn_norm])
    g["ffn_w_gate"] = jnp.stack([jnp.stack([a[0], b[0]]) for a, b in dwgu])
    g["ffn_w_up"] = jnp.stack([jnp.stack([a[1], b[1]]) for a, b in dwgu])
    g["ffn_w_down"] = jnp.stack([jnp.stack([a, b]) for a, b in dwd])
    g["mix_norm"] = jnp.concatenate(dmix, axis=0)
    return loss, dx.reshape(B, S, D), {n: g[n].reshape(w[n].shape) for n in WEIGHTS}


def _adamw(name, w, m, v, parts):
    shape = w.shape
    cols = shape[-1]
    rows = w.size // cols
    tr = _tile(rows, max(8, (1 << 19) // cols))
    n = len(parts)
    c1 = 1.0 - ADAM_B1 ** ADAM_STEP
    c2 = 1.0 - ADAM_B2 ** ADAM_STEP

    def body(*refs):
        w_ref, m_ref, v_ref = refs[:3]
        g_ref, d_ref, m2_ref, v2_ref = refs[3 + n:]
        g = refs[3][...].astype(F32)
        for k in range(1, n):
            g = g + refs[3 + k][...].astype(F32)
        m2 = ADAM_B1 * m_ref[...] + (1.0 - ADAM_B1) * g
        v2 = ADAM_B2 * v_ref[...] + (1.0 - ADAM_B2) * (g * g)
        g_ref[...] = g
        m2_ref[...] = m2
        v2_ref[...] = v2
        d_ref[...] = -ADAM_LR * ((m2 / c1) / (jnp.sqrt(v2 / c2) + ADAM_EPS) + ADAM_WD * w_ref[...])

    blk = pl.BlockSpec((tr, cols), lambda i: (i, 0))
    outs = _call(body, name, (rows // tr,), [blk] * (3 + n), [blk] * 4, [_sds((rows, cols), F32)] * 4, sem=("parallel",))(
        *[t.reshape(rows, cols) for t in (w, m, v, *parts)])
    return [o.reshape(shape) for o in outs]


ANY = pl.BlockSpec(memory_space=pl.ANY)
FLAT_COLS = 1024


def _place():
    return lax.axis_index("x"), lax.axis_index("y"), lax.axis_index("c")


def _all_gather(name, x):
    def body(x_ref, out_ref, send_sems, recv_sems, local_sem):
        xi, yi, ci = _place()
        me, sibling = (xi, yi, ci), (xi, yi, 1 - ci)
        chips = [(1 - xi, yi), (xi, 1 - yi), (1 - xi, 1 - yi)]

        def slot(px, py, pc):
            return out_ref.at[4 * px + 2 * py + pc]

        def copy(k, block, to, src=None):
            return pltpu.make_async_remote_copy(src_ref=slot(*block) if src is None else src, dst_ref=slot(*block),
                                                send_sem=send_sems.at[k], recv_sem=recv_sems.at[k], device_id=to,
                                                device_id_type=MESH)

        mine = pltpu.make_async_copy(x_ref, slot(*me), local_sem)
        mine.start()
        first = [copy(0, me, sibling, src=x_ref)] + [copy(1 + j, me, (*chip, ci), src=x_ref) for j, chip in enumerate(chips)]
        for cp in first:
            cp.start()
        passed = [copy(4 + j, (*chip, ci), sibling) for j, chip in enumerate(chips)]
        for j, chip in enumerate(chips):
            copy(1 + j, (*chip, ci), me).wait_recv()
            passed[j].start()
        copy(0, sibling, me).wait_recv()
        for j, chip in enumerate(chips):
            copy(4 + j, (*chip, 1 - ci), me).wait_recv()
        for cp in first + passed:
            cp.wait_send()
        mine.wait()

    return pl.pallas_call(body, name=name, out_shape=_sds((N_DEV,) + x.shape, x.dtype), in_specs=[ANY], out_specs=ANY,
                          scratch_shapes=[pltpu.SemaphoreType.DMA((7,)), pltpu.SemaphoreType.DMA((7,)),
                                          pltpu.SemaphoreType.DMA(())])(x)


def _swap_sibling(name, t):
    def body(t_ref, out_ref, send_sem, recv_sem):
        xi, yi, ci = _place()
        cp = pltpu.make_async_remote_copy(src_ref=t_ref, dst_ref=out_ref, send_sem=send_sem, recv_sem=recv_sem,
                                          device_id=(xi, yi, 1 - ci), device_id_type=MESH)
        cp.start()
        cp.wait()

    return pl.pallas_call(body, name=name, out_shape=_sds(t.shape, t.dtype), in_specs=[ANY], out_specs=ANY,
                          scratch_shapes=[pltpu.SemaphoreType.DMA(()), pltpu.SemaphoreType.DMA(())])(t)


def _scatter_chips(name, t):
    def body(t_ref, out_ref, send_sems, recv_sems):
        xi, yi, ci = _place()
        chips = [(1 - xi, yi), (xi, 1 - yi), (1 - xi, 1 - yi)]
        cps = [pltpu.make_async_remote_copy(src_ref=t_ref.at[2 * cx + cy], dst_ref=out_ref.at[j], send_sem=send_sems.at[j],
                                            recv_sem=recv_sems.at[j], device_id=(cx, cy, ci), device_id_type=MESH)
               for j, (cx, cy) in enumerate(chips)]
        for cp in cps:
            cp.start()
        for cp in cps:
            cp.wait()

    return pl.pallas_call(body, name=name, out_shape=_sds((3,) + t.shape[1:], t.dtype), in_specs=[ANY], out_specs=ANY,
                          scratch_shapes=[pltpu.SemaphoreType.DMA((3,)), pltpu.SemaphoreType.DMA((3,))])(t)


def _add_bf16(a, b):
    N, R, C = a.shape
    tr = _tile(R, 512)

    def body(a_ref, b_ref, o_ref):
        o_ref[...] = (a_ref[...].astype(F32) + b_ref[...].astype(F32)).astype(BF16)

    blk = pl.BlockSpec((None, tr, C), lambda n, i: (n, i, 0))
    return _call(body, "grad_pair_sum", (N, R // tr), [blk, blk], blk, _sds(a.shape, BF16), sem=("parallel", "parallel"))(a, b)


def _flat(parts, dtype, lead=()):
    flat = jnp.concatenate([t.reshape(lead + (-1,)).astype(dtype) for t in parts], axis=-1)
    n = flat.shape[-1]
    unit = 16 * FLAT_COLS
    padded = -(-n // unit) * unit
    flat = jnp.pad(flat, [(0, 0)] * len(lead) + [(0, padded - n)])
    return flat.reshape(lead + (padded // FLAT_COLS, FLAT_COLS))


def _unflat(flat, shapes, lead=()):
    flat = flat.reshape(lead + (-1,))
    out, off = [], 0
    for shp in shapes:
        n = math.prod(shp)
        out.append(flat[..., off:off + n].reshape(lead + tuple(shp)))
        off += n
    return out


def _dev_major(full, ax):
    shp = full.shape
    return jnp.moveaxis(full.reshape(shp[:ax] + (N_DEV, shp[ax] // N_DEV) + shp[ax + 1:]), ax, 0)


def _from_dev_major(blocks, ax):
    t = jnp.moveaxis(blocks, 0, ax)
    shp = t.shape
    return t.reshape(shp[:ax] + (shp[ax] * shp[ax + 1],) + shp[ax + 2:])


def kernel(x, *rest):
    nw = len(WEIGHTS)
    w = dict(zip(WEIGHTS, rest[:nw]))
    tgt = rest[nw]
    m = dict(zip(WEIGHTS, rest[nw + 1:2 * nw + 1]))
    v = dict(zip(WEIGHTS, rest[2 * nw + 1:3 * nw + 1]))
    xi, yi, ci = _place()
    dev = 4 * xi + 2 * yi + ci

    big = _all_gather("gather_matrices", _flat([w[n] for n in BIG], BF16))
    small = _all_gather("gather_vectors", _flat([w[n] for n in SMALL], F32))
    full = {n: w[n] for n in REPL}
    for names, got in ((BIG, big), (SMALL, small)):
        for n, blocks in zip(names, _unflat(got, [w[n].shape for n in names], (N_DEV,))):
            full[n] = _from_dev_major(blocks, SHARDED[n])

    loss, gx, g = _local_step(x, tgt, full)
    loss = lax.psum(loss[0, 0], ("x", "y", "c"))

    vec_names = SMALL + REPL
    vec = _all_gather("gather_vector_grads", _flat([g[n] for n in vec_names], F32))
    vec_parts = dict(zip(vec_names, _unflat(vec, [g[n].shape for n in vec_names], (N_DEV,))))

    gm = _flat([_dev_major(g[n], SHARDED[n]) for n in BIG], BF16, (N_DEV,))
    gm = gm.reshape((4, 2) + gm.shape[1:])
    keep = lax.dynamic_index_in_dim(gm, ci, 1, keepdims=False)
    got_sib = _swap_sibling("grads_to_sibling", lax.dynamic_index_in_dim(gm, 1 - ci, 1, keepdims=False))
    got_chips = _scatter_chips("grads_to_chips", _add_bf16(keep, got_sib))
    chip = 2 * xi + yi
    mat_parts = [lax.dynamic_index_in_dim(keep, chip, 0, keepdims=False), lax.dynamic_index_in_dim(got_sib, chip, 0, keepdims=False),
                 got_chips[0], got_chips[1], got_chips[2]]
    mat_parts = [_unflat(t, [w[n].shape for n in BIG]) for t in mat_parts]

    res = {}
    for k, n in enumerate(BIG):
        res[n] = _adamw(f"adamw_{n}", w[n], m[n], v[n], [t[k] for t in mat_parts])
    for n in vec_names:
        parts = vec_parts[n]
        if n in SHARDED:
            ax = SHARDED[n]
            parts = lax.dynamic_slice_in_dim(parts, dev * w[n].shape[ax], w[n].shape[ax], ax + 1)
        res[n] = _adamw(f"adamw_{n}", w[n], m[n], v[n], [parts[d] for d in range(N_DEV)])
    return (loss, gx, *[res[n][0] for n in WEIGHTS], *[res[n][1] for n in WEIGHTS], *[res[n][2] for n in WEIGHTS],
            *[res[n][3] for n in WEIGHTS])
```

```python
import functools
import math

import jax
import jax.numpy as jnp
from jax import lax
from jax.experimental import pallas as pl
from jax.experimental.pallas import tpu as pltpu

F32 = jnp.float32
BF16 = jnp.bfloat16
MESH = pl.DeviceIdType.MESH

N_DEV = 8
RMS_EPS = 1e-6
LN_EPS = 1e-5
FOX_HEADS = 16
HGRN_EXPAND = 128
HGRN_CHUNK = 32
POOL_WINDOWS = (2, 4, 8, 16)
ADAM_LR, ADAM_B1, ADAM_B2, ADAM_EPS, ADAM_WD, ADAM_STEP = 0.001, 0.9, 0.999, 1e-08, 0.01, 10
LANES = 128
VMEM_LIMIT_MB = 48

SHARDED = dict(
    ffn_norm=2, ffn_w_gate=3, ffn_w_up=3, ffn_w_down=2, conv_w_in=2, conv_dw=2, conv_w_out=1, fox_w_in=2, fox_w_out=1,
    hgrn_w_in=2, hgrn_norm=1, hgrn_w_out=1, pool_w=2, pool_scale=1)
BIG = ("ffn_w_gate", "ffn_w_up", "ffn_w_down", "conv_w_in", "conv_w_out", "fox_w_in", "fox_w_out", "hgrn_w_in",
       "hgrn_w_out", "pool_w")
SMALL = ("ffn_norm", "conv_dw", "hgrn_norm", "pool_scale")
REPL = ("mix_norm", "final_norm", "conv_b_in", "conv_dw_b", "conv_ln_g", "conv_ln_b", "fox_b_f", "hgrn_lb_logits")
WEIGHTS = ("ffn_norm", "ffn_w_gate", "ffn_w_up", "ffn_w_down", "mix_norm", "final_norm", "conv_w_in", "conv_b_in",
           "conv_dw", "conv_dw_b", "conv_ln_g", "conv_ln_b", "conv_w_out", "fox_w_in", "fox_b_f", "fox_w_out",
           "hgrn_w_in", "hgrn_lb_logits", "hgrn_norm", "hgrn_w_out", "pool_w", "pool_scale")


def _tile(n, pref, mult=8):
    if n <= pref:
        return n
    for t in range(pref - pref % mult, 0, -mult):
        if n % t == 0:
            return t
    return n


def _sig(x):
    return 1.0 / (1.0 + jnp.exp(-x))


def _call(body, name, grid, in_specs, out_specs, out_shape, scratch=(), sem=None):
    params = dict(vmem_limit_bytes=VMEM_LIMIT_MB << 20)
    if sem is not None:
        params["dimension_semantics"] = sem
    return pl.pallas_call(body, name=name, grid=grid, in_specs=in_specs, out_specs=out_specs, out_shape=out_shape,
                          scratch_shapes=list(scratch), compiler_params=pltpu.CompilerParams(**params))


def _sds(shape, dtype):
    return jax.ShapeDtypeStruct(tuple(shape), dtype)


_DN = {"nn": (((1,), (0,)), ((), ())), "nt": (((1,), (1,)), ((), ())), "tn": (((0,), (0,)), ((), ()))}


def _dot(a, b, mode="nn"):
    return lax.dot_general(a.astype(BF16), b.astype(BF16), _DN[mode], preferred_element_type=F32)


def _roll(x, shift, axis=0):
    n = x.shape[axis]
    shift = shift % n
    return x if shift == 0 else pltpu.roll(x, shift, axis)


def _scan_rows(x, reverse=False):
    n = x.shape[0]
    row = lax.broadcasted_iota(jnp.int32, x.shape, 0)
    sh = 1
    while sh < n:
        if reverse:
            x = x + jnp.where(row < n - sh, _roll(x, n - sh), 0.0)
        else:
            x = x + jnp.where(row >= sh, _roll(x, sh), 0.0)
        sh *= 2
    return x


def _mm(name, mode, pairs, pair_specs, grid, k_axis, out_shapes, out_specs, acc_shape=None, extras=(), extra_specs=(),
        epilogue=None, alpha=1.0, sem=None):
    npair, nex, nout = len(pairs), len(extras), len(out_shapes)
    nk = grid[k_axis] if k_axis is not None else 1

    def body(*refs):
        prs = refs[:2 * npair]
        ex = refs[2 * npair:2 * npair + nex]
        outs = refs[2 * npair + nex:2 * npair + nex + nout]

        def partial():
            p = None
            for i in range(npair):
                d = _dot(prs[2 * i][...], prs[2 * i + 1][...], mode)
                p = d if p is None else p + d
            return p

        def finish(res):
            if alpha != 1.0:
                res = res * alpha
            if epilogue is None:
                outs[0][...] = res.astype(outs[0].dtype)
            else:
                epilogue(res, ex, outs)

        if k_axis is None:
            finish(partial())
        else:
            acc = refs[-1]
            k = pl.program_id(k_axis)

            @pl.when(k == 0)
            def _():
                acc[...] = partial()

            @pl.when(k > 0)
            def _():
                acc[...] += partial()

            @pl.when(k == nk - 1)
            def _():
                finish(acc[...])

    if sem is None:
        sem = tuple("arbitrary" if i == k_axis else "parallel" for i in range(len(grid)))
    scratch = [pltpu.VMEM(acc_shape, F32)] if k_axis is not None else []
    flat = [t for p in pairs for t in p]
    flat_specs = [s for p in pair_specs for s in p]
    return _call(body, name, grid, flat_specs + list(extra_specs), out_specs, out_shapes, scratch, sem)(*flat, *extras)


def _mm_nn(name, a, b, out_dtype, tm=512, tn=512):
    M, K = a.shape
    N = b.shape[1]
    tm, tn = _tile(M, tm), _tile(N, tn, LANES)
    return _mm(name, "nn", [(a, b)], [(pl.BlockSpec((tm, K), lambda i, j: (i, 0)), pl.BlockSpec((K, tn), lambda i, j: (0, j)))],
               (M // tm, N // tn), None, [_sds((M, N), out_dtype)], [pl.BlockSpec((tm, tn), lambda i, j: (i, j))])[0]


def _mm_res(name, u, w, x, alpha, tm=512):
    T, K = u.shape
    D = w.shape[1]
    tm = _tile(T, tm)

    def epi(res, ex, outs):
        outs[0][...] = ex[0][...] + res

    return _mm(name, "nn", [(u, w)], [(pl.BlockSpec((tm, K), lambda i: (i, 0)), pl.BlockSpec((K, D), lambda i: (0, 0)))],
               (T // tm,), None, [_sds((T, D), F32)], [pl.BlockSpec((tm, D), lambda i: (i, 0))],
               extras=[x], extra_specs=[pl.BlockSpec((tm, D), lambda i: (i, 0))], epilogue=epi, alpha=alpha)[0]


def _mm_nt(name, pairs, out_dtype, tm=512, tn=512, alpha=1.0):
    M = pairs[0][0].shape[0]
    N = pairs[0][1].shape[0]
    tm, tn = _tile(M, tm), _tile(N, tn, LANES)
    specs = [(pl.BlockSpec((tm, a.shape[1]), lambda i, j: (i, 0)), pl.BlockSpec((tn, b.shape[1]), lambda i, j: (j, 0)))
             for a, b in pairs]
    return _mm(name, "nt", pairs, specs, (M // tm, N // tn), None, [_sds((M, N), out_dtype)],
               [pl.BlockSpec((tm, tn), lambda i, j: (i, j))], alpha=alpha)[0]


def _mm_nt3(name, a3, b3, tm=512, tn=512):
    S, M, K = a3.shape
    N = b3.shape[1]
    tm, tn = _tile(M, tm), _tile(N, tn, LANES)
    return _mm(name, "nt", [(a3, b3)],
               [(pl.BlockSpec((None, tm, K), lambda i, j, s: (s, i, 0)), pl.BlockSpec((None, tn, K), lambda i, j, s: (s, j, 0)))],
               (M // tm, N // tn, S), 2, [_sds((M, N), F32)], [pl.BlockSpec((tm, tn), lambda i, j, s: (i, j))],
               acc_shape=(tm, tn))[0]


def _mm_tn(name, a, b, alpha=1.0, tm=1408, tn=1408, tk=512):
    T, M = a.shape
    N = b.shape[1]
    tm, tn, tk = _tile(M, tm, LANES), _tile(N, tn, LANES), _tile(T, tk)
    return _mm(name, "tn", [(a, b)],
               [(pl.BlockSpec((tk, tm), lambda i, j, k: (k, i)), pl.BlockSpec((tk, tn), lambda i, j, k: (k, j)))],
               (M // tm, N // tn, T // tk), 2, [_sds((M, N), F32)], [pl.BlockSpec((tm, tn), lambda i, j, k: (i, j))],
               acc_shape=(tm, tn), alpha=alpha)[0]


def _mm_tn3(name, a, b3, tm=1408, tn=1408, tk=512):
    T, M = a.shape
    S, _, N = b3.shape
    tm, tn, tk = _tile(M, tm, LANES), _tile(N, tn, LANES), _tile(T, tk)
    return _mm(name, "tn", [(a, b3)],
               [(pl.BlockSpec((tk, tm), lambda s, i, j, k: (k, i)), pl.BlockSpec((None, tk, tn), lambda s, i, j, k: (s, k, j)))],
               (S, M // tm, N // tn, T // tk), 3, [_sds((S, M, N), F32)],
               [pl.BlockSpec((None, tm, tn), lambda s, i, j, k: (s, i, j))], acc_shape=(tm, tn))[0]


def _rms_fwd(name, x, g):
    T, D = x.shape
    tt = _tile(T, 512)

    def body(x_ref, g_ref, h_ref):
        xv = x_ref[...]
        r = lax.rsqrt(jnp.mean(xv * xv, axis=-1, keepdims=True) + RMS_EPS)
        h_ref[...] = (xv * r * g_ref[...]).astype(h_ref.dtype)

    row = pl.BlockSpec((tt, D), lambda i: (i, 0))
    return _call(body, name, (T // tt,), [row, pl.BlockSpec((1, D), lambda i: (0, 0))], row, _sds((T, D), BF16))(x, g)


def _rms_bwd(name, dh, x, g, dx_in):
    T, D = x.shape
    tt = _tile(T, 512)

    def body(dh_ref, x_ref, g_ref, dxi_ref, dx_ref, dg_ref):
        xv = x_ref[...]
        r = lax.rsqrt(jnp.mean(xv * xv, axis=-1, keepdims=True) + RMS_EPS)
        xh = xv * r
        dhv = dh_ref[...]
        dxh = dhv * g_ref[...]
        dx_ref[...] = dxi_ref[...] + r * (dxh - xh * jnp.mean(dxh * xh, axis=-1, keepdims=True))
        part = jnp.sum(dhv * xh, axis=0, keepdims=True)

        @pl.when(pl.program_id(0) == 0)
        def _():
            dg_ref[...] = part

        @pl.when(pl.program_id(0) > 0)
        def _():
            dg_ref[...] += part

    row = pl.BlockSpec((tt, D), lambda i: (i, 0))
    vec = pl.BlockSpec((1, D), lambda i: (0, 0))
    return _call(body, name, (T // tt,), [row, row, vec, row], [row, vec], [_sds((T, D), F32), _sds((1, D), F32)],
                 sem=("arbitrary",))(dh, x, g, dx_in)


def _loss_head(x, g, tgt):
    T, D = x.shape
    tt = _tile(T, 512)

    def body(x_ref, g_ref, t_ref, loss_ref, dx_ref, dg_ref):
        xv = x_ref[...]
        r = lax.rsqrt(jnp.mean(xv * xv, axis=-1, keepdims=True) + RMS_EPS)
        xh = xv * r
        e = xh * g_ref[...] - t_ref[...]
        lp = 0.5 * jnp.sum(jnp.sum(e * e, axis=-1, keepdims=True), axis=0, keepdims=True) / D
        dy = e / D
        dxh = dy * g_ref[...]
        dx_ref[...] = r * (dxh - xh * jnp.mean(dxh * xh, axis=-1, keepdims=True))
        part = jnp.sum(dy * xh, axis=0, keepdims=True)

        @pl.when(pl.program_id(0) == 0)
        def _():
            dg_ref[...] = part
            loss_ref[...] = lp

        @pl.when(pl.program_id(0) > 0)
        def _():
            dg_ref[...] += part
            loss_ref[...] += lp

    row = pl.BlockSpec((tt, D), lambda i: (i, 0))
    vec = pl.BlockSpec((1, D), lambda i: (0, 0))
    one = pl.BlockSpec((1, 1), lambda i: (0, 0))
    return _call(body, "loss_head", (T // tt,), [row, vec, row], [one, row, vec],
                 [_sds((1, 1), F32), _sds((T, D), F32), _sds((1, D), F32)], sem=("arbitrary",))(x, g, tgt)


def _colsum3(name, a3):
    S, T, N = a3.shape
    tt = _tile(T, 512)

    def body(a_ref, o_ref):
        part = jnp.sum(a_ref[...].astype(F32), axis=0, keepdims=True)

        @pl.when(pl.program_id(1) == 0)
        def _():
            o_ref[...] = part

        @pl.when(pl.program_id(1) > 0)
        def _():
            o_ref[...] += part

    return _call(body, name, (S, T // tt), [pl.BlockSpec((None, tt, N), lambda s, i: (s, i, 0))],
                 pl.BlockSpec((None, 1, N), lambda s, i: (s, 0, 0)), _sds((S, 1, N), F32), sem=("parallel", "arbitrary"))(a3)


def _glu_mm(name, h, w2, bias2, mode, u_dtype, tm=1024, tn=256):
    T, K = h.shape
    N = w2.shape[2]
    tm, tn = _tile(T, tm), _tile(N, tn, LANES)
    has_bias = bias2 is not None

    def body(*refs):
        h_ref, w_ref = refs[0], refs[1]
        ab_ref, u_ref = refs[-2], refs[-1]
        hv = h_ref[...]
        a = _dot(hv, w_ref[0])
        b = _dot(hv, w_ref[1])
        if has_bias:
            a = a + refs[2][0]
            b = b + refs[2][1]
        u = a * _sig(a) * b if mode == "swiglu" else a * _sig(b)
        ab_ref[0] = a.astype(BF16)
        ab_ref[1] = b.astype(BF16)
        u_ref[...] = u.astype(u_ref.dtype)

    in_specs = [pl.BlockSpec((tm, K), lambda i, j: (i, 0)), pl.BlockSpec((2, K, tn), lambda i, j: (0, 0, j))]
    args = [h, w2]
    if has_bias:
        in_specs.append(pl.BlockSpec((2, 1, tn), lambda i, j: (0, 0, j)))
        args.append(bias2)
    return _call(body, name, (T // tm, N // tn), in_specs,
                 [pl.BlockSpec((2, tm, tn), lambda i, j: (0, i, j)), pl.BlockSpec((tm, tn), lambda i, j: (i, j))],
                 [_sds((2, T, N), BF16), _sds((T, N), u_dtype)], sem=("parallel", "parallel"))(*args)


def _swiglu_bwd_mm(name, dx, wd, ab, alpha, tm=1024, tn=256):
    T, D = dx.shape
    N = wd.shape[0]
    tm, tn = _tile(T, tm), _tile(N, tn, LANES)

    def epi(du, ex, outs):
        a = ex[0][0].astype(F32)
        b = ex[0][1].astype(F32)
        sg = _sig(a)
        sa = a * sg
        outs[0][0] = (du * b * (sg * (1.0 + a * (1.0 - sg)))).astype(BF16)
        outs[0][1] = (du * sa).astype(BF16)
        outs[1][...] = (sa * b).astype(BF16)

    return _mm(name, "nt", [(dx, wd)], [(pl.BlockSpec((tm, D), lambda i, j: (i, 0)), pl.BlockSpec((tn, D), lambda i, j: (j, 0)))],
               (T // tm, N // tn), None, [_sds((2, T, N), BF16), _sds((T, N), BF16)],
               [pl.BlockSpec((2, tm, tn), lambda i, j: (0, i, j)), pl.BlockSpec((tm, tn), lambda i, j: (i, j))],
               extras=[ab], extra_specs=[pl.BlockSpec((2, tm, tn), lambda i, j: (0, i, j))], epilogue=epi, alpha=alpha)


def _ffn_fwd(tag, x, g, wgu, wd):
    h = _rms_fwd(f"ffn_rms_{tag}", x, g)
    ab, u = _glu_mm(f"ffn_gu_{tag}", h, wgu, None, "swiglu", BF16)
    return _mm_res(f"ffn_down_{tag}", u, wd, x, 0.5), (x, h, ab)


def _ffn_bwd(tag, dx, saved, g, wgu, wd):
    x, h, ab = saved
    dab, u = _swiglu_bwd_mm(f"ffn_dgu_{tag}", dx, wd, ab, 0.5)
    dwd = _mm_tn(f"ffn_dwd_{tag}", u, dx, alpha=0.5)
    dwgu = _mm_tn3(f"ffn_dwgu_{tag}", h, dab)
    dh = _mm_nt3(f"ffn_dh_{tag}", dab, wgu)
    dx2, dg = _rms_bwd(f"ffn_drms_{tag}", dh, x, g, dx)
    return dx2, dg, dwgu, dwd


HALO = 32


def _conv_fwd(u3, dw32, dwb, lng, lnb):
    B, S, D = u3.shape
    W = dw32.shape[0]
    taps = CONV_TAPS
    tt = _tile(S, 256, HALO)
    hb = tt // HALO

    def body(u_ref, halo_ref, dw_ref, dwb_ref, g_ref, b_ref, v_ref, s_ref):
        i = pl.program_id(1)
        halo = jnp.where(i > 0, halo_ref[...], 0.0)
        ext = jnp.concatenate([halo, u_ref[...]], axis=0)
        acc = jnp.zeros((tt, D), F32) + dwb_ref[...]
        for j in range(taps):
            acc = acc + dw_ref[pl.ds(j, 1), :] * _roll(ext, taps - 1 - j)[HALO:]
        v_ref[...] = acc
        mu = jnp.mean(acc, axis=-1, keepdims=True)
        xc = acc - mu
        ln = xc * lax.rsqrt(jnp.mean(xc * xc, axis=-1, keepdims=True) + LN_EPS) * g_ref[...] + b_ref[...]
        s_ref[...] = (ln * _sig(ln)).astype(BF16)

    main = pl.BlockSpec((None, tt, D), lambda b, i: (b, i, 0))
    halo = pl.BlockSpec((None, HALO, D), lambda b, i: (b, jnp.maximum(i * hb - 1, 0), 0))
    vec = pl.BlockSpec((1, D), lambda b, i: (0, 0))
    return _call(body, "conv_fwd", (B, S // tt), [main, halo, pl.BlockSpec((W, D), lambda b, i: (0, 0)), vec, vec, vec],
                 [main, main], [_sds((B, S, D), F32), _sds((B, S, D), BF16)], sem=("parallel", "parallel"))(
        u3, u3, dw32, dwb, lng, lnb)


def _conv_bwd_ln(v, ds, lng, lnb):
    T, D = v.shape
    tt = _tile(T, 256)

    def body(v_ref, ds_ref, g_ref, b_ref, dv_ref, red_ref):
        vv = v_ref[...]
        mu = jnp.mean(vv, axis=-1, keepdims=True)
        xc = vv - mu
        rstd = lax.rsqrt(jnp.mean(xc * xc, axis=-1, keepdims=True) + LN_EPS)
        xh = xc * rstd
        ln = xh * g_ref[...] + b_ref[...]
        sg = _sig(ln)
        dln = ds_ref[...] * (sg * (1.0 + ln * (1.0 - sg)))
        dxh = dln * g_ref[...]
        dv = rstd * (dxh - jnp.mean(dxh, axis=-1, keepdims=True) - xh * jnp.mean(dxh * xh, axis=-1, keepdims=True))
        dv_ref[...] = dv
        parts = (jnp.sum(dln * xh, axis=0, keepdims=True), jnp.sum(dln, axis=0, keepdims=True),
                 jnp.sum(dv, axis=0, keepdims=True))

        @pl.when(pl.program_id(0) == 0)
        def _():
            for k in range(3):
                red_ref[k] = parts[k]

        @pl.when(pl.program_id(0) > 0)
        def _():
            for k in range(3):
                red_ref[k] += parts[k]

    row = pl.BlockSpec((tt, D), lambda i: (i, 0))
    vec = pl.BlockSpec((1, D), lambda i: (0, 0))
    return _call(body, "conv_bwd_ln", (T // tt,), [row, row, vec, vec], [row, pl.BlockSpec((3, 1, D), lambda i: (0, 0, 0))],
                 [_sds((T, D), F32), _sds((3, 1, D), F32)], sem=("arbitrary",))(v, ds, lng, lnb)


def _conv_bwd_dw(dv3, u3, ab, dw32):
    B, S, D = u3.shape
    W = dw32.shape[0]
    taps = CONV_TAPS
    tt = _tile(S, 256, HALO)
    hb = tt // HALO
    nt = S // tt
    L = tt + HALO

    def body(dv_ref, dvn_ref, u_ref, up_ref, ab_ref, dw_ref, dab_ref, ddw_ref):
        b, i = pl.program_id(0), pl.program_id(1)
        dv = dv_ref[...]
        ext_dv = jnp.concatenate([dv, jnp.where(i < nt - 1, dvn_ref[...], 0.0)], axis=0)
        ext_u = jnp.concatenate([jnp.where(i > 0, up_ref[...], 0.0), u_ref[...]], axis=0)
        du = jnp.zeros((tt, D), F32)
        first = jnp.logical_and(b == 0, i == 0)

        @pl.when(first)
        def _():
            ddw_ref[...] = jnp.zeros((W, D), F32)

        for j in range(taps):
            sh = taps - 1 - j
            du = du + dw_ref[pl.ds(j, 1), :] * _roll(ext_dv, L - sh)[:tt]
            ddw_ref[pl.ds(j, 1), :] += jnp.sum(dv * _roll(ext_u, sh)[HALO:], axis=0, keepdims=True)
        a = ab_ref[0].astype(F32)
        sb = _sig(ab_ref[1].astype(F32))
        dab_ref[0] = (du * sb).astype(BF16)
        dab_ref[1] = (du * a * sb * (1.0 - sb)).astype(BF16)

    main = pl.BlockSpec((None, tt, D), lambda b, i: (b, i, 0))
    prev = pl.BlockSpec((None, HALO, D), lambda b, i: (b, jnp.maximum(i * hb - 1, 0), 0))
    nxt = pl.BlockSpec((None, HALO, D), lambda b, i: (b, jnp.minimum((i + 1) * hb, S // HALO - 1), 0))
    abs_ = pl.BlockSpec((2, tt, D), lambda b, i: (0, b * nt + i, 0))
    wsp = pl.BlockSpec((W, D), lambda b, i: (0, 0))
    return _call(body, "conv_bwd_dw", (B, nt), [main, nxt, main, prev, abs_, wsp], [abs_, wsp],
                 [_sds((2, B * S, D), BF16), _sds((W, D), F32)], sem=("arbitrary", "arbitrary"))(dv3, dv3, u3, u3, ab, dw32)


CONV_TAPS = 31


def _conv_mixer_fwd(x, B, p):
    T, D = x.shape
    h = _rms_fwd("conv_rms", x, p["mix_norm"][0:1])
    ab, u = _glu_mm("conv_in", h, p["conv_w_in2"], p["conv_b_in2"], "glu", F32)
    v, s = _conv_fwd(u.reshape(B, T // B, D), p["conv_dw32"], p["conv_dw_b"], p["conv_ln_g"], p["conv_ln_b"])
    v, s = v.reshape(T, D), s.reshape(T, D)
    return _mm_res("conv_out", s, p["conv_w_out"], x, 1.0), (x, h, ab, u, v, s)


def _conv_mixer_bwd(dx, saved, B, p):
    x, h, ab, u, v, s = saved
    T, D = x.shape
    g = {}
    ds = _mm_nt("conv_ds", [(dx, p["conv_w_out"])], F32)
    g["conv_w_out"] = _mm_tn("conv_dwout", s, dx)
    dv, red = _conv_bwd_ln(v, ds, p["conv_ln_g"], p["conv_ln_b"])
    g["conv_ln_g"], g["conv_ln_b"], g["conv_dw_b"] = red[0], red[1], red[2]
    dab, ddw = _conv_bwd_dw(dv.reshape(B, T // B, D), u.reshape(B, T // B, D), ab, p["conv_dw32"])
    g["conv_dw"] = ddw[:CONV_TAPS][None]
    g["conv_b_in"] = _colsum3("conv_dbin", dab).reshape(1, 2 * D)
    dwin = _mm_tn3("conv_dwin", h, dab)
    g["conv_w_in"] = jnp.moveaxis(dwin, 0, 1).reshape(1, D, 2 * D)
    dh = _mm_nt3("conv_dh", dab, p["conv_w_in2"])
    dx2, dg = _rms_bwd("conv_drms", dh, x, p["mix_norm"][0:1], dx)
    return dx2, dg, g


def _log_sigmoid(z):
    return jnp.minimum(z, 0.0) - jnp.log(1.0 + jnp.exp(-jnp.abs(z)))


def _fox_gate_fwd(fl3, bf):
    B, S, N = fl3.shape
    tt = _tile(S, 512)

    def body(fl_ref, bf_ref, c_ref, carry):
        @pl.when(pl.program_id(1) == 0)
        def _():
            carry[...] = jnp.zeros((1, N), F32)

        c = _scan_rows(_log_sigmoid(fl_ref[...] + bf_ref[...])) + carry[...]
        c_ref[...] = c
        carry[...] = c_ref[pl.ds(tt - 1, 1), :]

    row = pl.BlockSpec((None, tt, N), lambda b, i: (b, i, 0))
    return _call(body, "fox_gate_fwd", (B, S // tt), [row, pl.BlockSpec((1, N), lambda b, i: (0, 0))], row,
                 _sds((B, S, N), F32), [pltpu.VMEM((1, N), F32)], sem=("parallel", "arbitrary"))(fl3, bf)


def _fox_gate_bwd(dc3, fl3, bf):
    B, S, N = fl3.shape
    tt = _tile(S, 512)
    nt = S // tt

    def body(dc_ref, fl_ref, bf_ref, dfl_ref, dbf_ref, carry):
        b, i = pl.program_id(0), pl.program_id(1)

        @pl.when(i == 0)
        def _():
            carry[...] = jnp.zeros((1, N), F32)

        dc = dc_ref[0] - dc_ref[1]
        dlf = _scan_rows(dc, reverse=True) + carry[...]
        dfl = dlf * _sig(-(fl_ref[...] + bf_ref[...]))
        dfl_ref[...] = dfl
        carry[...] += jnp.sum(dc, axis=0, keepdims=True)
        part = jnp.sum(dfl, axis=0, keepdims=True)

        @pl.when(jnp.logical_and(b == 0, i == 0))
        def _():
            dbf_ref[...] = part

        @pl.when(jnp.logical_or(b > 0, i > 0))
        def _():
            dbf_ref[...] += part

    row = pl.BlockSpec((None, tt, N), lambda b, i: (b, nt - 1 - i, 0))
    vec = pl.BlockSpec((1, N), lambda b, i: (0, 0))
    row2 = pl.BlockSpec((2, None, tt, N), lambda b, i: (0, b, nt - 1 - i, 0))
    return _call(body, "fox_gate_bwd", (B, nt), [row2, row, vec], [row, vec], [_sds((B, S, N), F32), _sds((1, N), F32)],
                 [pltpu.VMEM((1, N), F32)], sem=("arbitrary", "arbitrary"))(dc3, fl3, bf)


NEG = -1e30


def _fox_attn_fwd(q, k, v, ccol, crow, scale):
    B, H, S, dh = v.shape
    dqk = q.shape[-1]
    tq = _tile(S, 512, LANES)
    tk = tq

    def body(q_ref, k_ref, v_ref, cc_ref, cr_ref, o_ref, lse_ref):
        i = pl.program_id(2)
        qv = q_ref[...]
        cq = cc_ref[...]
        rows = i * tq + lax.broadcasted_iota(jnp.int32, (tq, tk), 0)
        cols0 = lax.broadcasted_iota(jnp.int32, (tq, tk), 1)

        def step(j, carry):
            m, l, acc = carry
            off = pl.multiple_of(j * tk, tk)
            s = _dot(qv, k_ref[pl.ds(off, tk), :], "nt") * scale + cq - cr_ref[:, pl.ds(off, tk)]
            s = jnp.where(rows >= cols0 + j * tk, s, -jnp.inf)
            m2 = jnp.maximum(m, jnp.max(s, axis=-1, keepdims=True))
            p = jnp.exp(s - m2)
            al = jnp.exp(m - m2)
            return m2, al * l + jnp.sum(p, axis=-1, keepdims=True), al * acc + _dot(p, v_ref[pl.ds(off, tk), :])

        m, l, acc = lax.fori_loop(0, i + 1, step, (jnp.full((tq, 1), NEG, F32), jnp.zeros((tq, 1), F32),
                                                   jnp.zeros((tq, dh), F32)))
        o_ref[...] = acc / l
        lse_ref[...] = m + jnp.log(l)

    qs = pl.BlockSpec((None, None, tq, dqk), lambda b, h, i: (b, h, i, 0))
    os_ = pl.BlockSpec((None, None, tq, dh), lambda b, h, i: (b, h, i, 0))
    fullk = pl.BlockSpec((None, None, S, dqk), lambda b, h, i: (b, h, 0, 0))
    full = pl.BlockSpec((None, None, S, dh), lambda b, h, i: (b, h, 0, 0))
    col = pl.BlockSpec((None, None, tq, 1), lambda b, h, i: (b, h, i, 0))
    rowv = pl.BlockSpec((None, None, 1, S), lambda b, h, i: (b, h, 0, 0))
    return _call(body, "fox_attn_fwd", (B, H, S // tq), [qs, fullk, full, col, rowv], [os_, col],
                 [_sds((B, H, S, dh), F32), _sds((B, H, S, 1), F32)], sem=("parallel", "parallel", "parallel"))(
        q, k, v, ccol, crow)


def _fox_rowstats(do, o, ccol, lse):
    B, H, S, dh = o.shape
    tq = _tile(S, 4096)

    def body(do_ref, o_ref, cc_ref, lse_ref, rb_ref, dl_ref):
        rb_ref[...] = cc_ref[...] - lse_ref[...]
        dl_ref[...] = jnp.sum(do_ref[...].astype(F32) * o_ref[...].astype(F32), axis=-1, keepdims=True)

    qs = pl.BlockSpec((None, None, tq, dh), lambda b, h, i: (b, h, i, 0))
    col = pl.BlockSpec((None, None, tq, 1), lambda b, h, i: (b, h, i, 0))
    return _call(body, "fox_rowstats", (B, H, S // tq), [qs, qs, col, col], [col, col],
                 [_sds((B, H, S, 1), F32), _sds((B, H, S, 1), F32)], sem=("parallel",) * 3)(do, o, ccol, lse)


def _fox_attn_bwd(q, k, v, do, rb, delta, crow, scale):
    B, H, S, dh = v.shape
    dqk = q.shape[-1]
    tk = _tile(S, 512, LANES)
    tq = tk
    nq = S // tq

    def body(q_ref, k_ref, v_ref, do_ref, rb_ref, dl_ref, cr_ref, dq_ref, dk_ref, dv_ref):
        j = pl.program_id(2)

        @pl.when(j == 0)
        def _():
            dq_ref[...] = jnp.zeros((S, dqk), F32)

        kj, vj, ck = k_ref[...], v_ref[...], cr_ref[...]
        cols = j * tk + lax.broadcasted_iota(jnp.int32, (tq, tk), 1)
        rows0 = lax.broadcasted_iota(jnp.int32, (tq, tk), 0)

        def step(i, carry):
            dk, dv = carry
            off = pl.multiple_of(i * tq, tq)
            qi, doi = q_ref[pl.ds(off, tq), :], do_ref[pl.ds(off, tq), :]
            s = _dot(qi, kj, "nt") * scale + rb_ref[pl.ds(off, tq), :] - ck
            p = jnp.where(rows0 + i * tq >= cols, jnp.exp(s), 0.0)
            ds = p * (_dot(doi, vj, "nt") - dl_ref[pl.ds(off, tq), :])
            dq_ref[pl.ds(off, tq), :] += _dot(ds, kj)
            return dk + _dot(ds, qi, "tn"), dv + _dot(p, doi, "tn")

        dk, dv = lax.fori_loop(j, nq, step, (jnp.zeros((tk, dqk), F32), jnp.zeros((tk, dh), F32)))
        dk_ref[...] = dk * jnp.where(lax.broadcasted_iota(jnp.int32, (tk, dqk), 1) < dh, scale, 1.0)
        dv_ref[...] = dv

        @pl.when(j == S // tk - 1)
        def _():
            lane = lax.broadcasted_iota(jnp.int32, (S, dqk), 1)
            dq_ref[...] = dq_ref[...] * jnp.where(lane < dh, scale, 1.0)

    ks = pl.BlockSpec((None, None, tk, dqk), lambda b, h, j: (b, h, j, 0))
    vs = pl.BlockSpec((None, None, tk, dh), lambda b, h, j: (b, h, j, 0))
    fullq = pl.BlockSpec((None, None, S, dqk), lambda b, h, j: (b, h, 0, 0))
    full = pl.BlockSpec((None, None, S, dh), lambda b, h, j: (b, h, 0, 0))
    colf = pl.BlockSpec((None, None, S, 1), lambda b, h, j: (b, h, 0, 0))
    rowt = pl.BlockSpec((None, None, 1, tk), lambda b, h, j: (b, h, 0, j))
    return _call(body, "fox_attn_bwd", (B, H, S // tk), [fullq, ks, vs, full, colf, colf, rowt], [fullq, ks, vs],
                 [_sds((B, H, S, dqk), F32), _sds((B, H, S, dqk), F32), _sds((B, H, S, dh), F32)],
                 sem=("parallel", "parallel", "arbitrary"))(q, k, v, do, rb, delta, crow)


def _heads(t, B, H):
    T, D = t.shape
    return t.reshape(B, T // B, H, D // H).transpose(0, 2, 1, 3)


def _unheads(t):
    B, H, S, dh = t.shape
    return t.transpose(0, 2, 1, 3).reshape(B * S, H * dh)


def _fox_mixer_fwd(x, B, p):
    T, D = x.shape
    H = FOX_HEADS
    S = T // B
    scale = (D // H) ** -0.5
    h = _rms_fwd("fox_rms", x, p["mix_norm"][1:2])
    qkv = _mm_nn("fox_qkv", h, p["fox_w_qkv"], BF16)
    fl = _mm_nn("fox_fl", h, p["fox_w_f"], F32)
    c = _fox_gate_fwd(fl.reshape(B, S, LANES), p["fox_b_f128"])
    ch = c[:, :, :H].transpose(0, 2, 1)
    ccol, crow = ch[..., None], ch[:, :, None, :]
    q, k, v = _heads(qkv[:, :D], B, H), _heads(qkv[:, D:2 * D], B, H), _heads(qkv[:, 2 * D:], B, H)
    dh = D // H
    dqk = -(-(dh + 2) // LANES) * LANES
    one, zero = jnp.ones((B, H, S, 1), q.dtype), jnp.zeros((B, H, S, 1), q.dtype)
    rest = jnp.zeros((B, H, S, dqk - dh - 2), q.dtype)
    q = jnp.concatenate([q, zero, one, rest], axis=-1)
    k = jnp.concatenate([k, one, zero, rest], axis=-1)
    o, lse = _fox_attn_fwd(q, k, v, ccol, crow, scale)
    of = _unheads(o)
    return _mm_res("fox_out", of, p["fox_w_out"], x, 1.0), (x, h, fl, q, k, v, ccol, crow, o, lse, of)


def _fox_mixer_bwd(dx, saved, B, p):
    x, h, fl, q, k, v, ccol, crow, o, lse, of = saved
    T, D = x.shape
    H = FOX_HEADS
    S = T // B
    scale = (D // H) ** -0.5
    g = {}
    do = _heads(_mm_nt("fox_do", [(dx, p["fox_w_out"])], F32), B, H)
    g["fox_w_out"] = _mm_tn("fox_dwout", of, dx)
    rb, delta = _fox_rowstats(do, o, ccol, lse)
    dq, dk, dv = _fox_attn_bwd(q, k, v, do, rb, delta, crow, scale)
    dh_ = D // H
    dqkv = jnp.concatenate([_unheads(dq[..., :dh_]), _unheads(dk[..., :dh_]), _unheads(dv)], axis=1).astype(BF16)
    dc = jnp.stack([dq[..., dh_], dk[..., dh_ + 1]])
    dc = jnp.pad(dc.transpose(0, 1, 3, 2), ((0, 0), (0, 0), (0, 0), (0, LANES - H)))
    dfl, dbf = _fox_gate_bwd(dc, fl.reshape(B, S, LANES), p["fox_b_f128"])
    dfl = dfl.reshape(T, LANES)
    g["fox_b_f"] = dbf[:, :H]
    dwqkv = _mm_tn("fox_dwqkv", h, dqkv)
    dwf = _mm_tn("fox_dwf", h, dfl)
    g["fox_w_in"] = jnp.concatenate([dwqkv, dwf[:, :H]], axis=1)[None]
    dh = _mm_nt("fox_dh", [(dqkv, p["fox_w_qkv"]), (dfl, p["fox_w_f"])], F32)
    dx2, dg = _rms_bwd("fox_drms", dh, x, p["mix_norm"][1:2], dx)
    return dx2, dg, g


def _lb_fwd(logits):
    L, D = logits.shape

    def body(l_ref, lb_ref):
        z = l_ref[...]
        e = jnp.exp(z - jnp.max(z, axis=0, keepdims=True))
        p = e / jnp.sum(e, axis=0, keepdims=True)
        lb_ref[...] = jnp.sum(jnp.where(_lb_rows(z.shape), p, 0.0), axis=0, keepdims=True)

    return _call(body, "hgrn_lb", (1,), [pl.BlockSpec((L, D), lambda i: (0, 0))], pl.BlockSpec((1, D), lambda i: (0, 0)),
                 _sds((1, D), F32))(logits)


def _lb_rows(shape):
    r = lax.broadcasted_iota(jnp.int32, shape, 0)
    return jnp.logical_and(r >= 1, r <= HGRN_LAYER)


HGRN_LAYER = 2


def _lb_bwd(logits, dlb):
    L, D = logits.shape

    def body(l_ref, d_ref, o_ref):
        z = l_ref[...]
        e = jnp.exp(z - jnp.max(z, axis=0, keepdims=True))
        p = e / jnp.sum(e, axis=0, keepdims=True)
        dp = jnp.where(_lb_rows(z.shape), d_ref[...], 0.0)
        o_ref[...] = p * (dp - jnp.sum(p * dp, axis=0, keepdims=True))

    full = pl.BlockSpec((L, D), lambda i: (0, 0))
    return _call(body, "hgrn_dlb", (1,), [full, pl.BlockSpec((1, D), lambda i: (0, 0))], full, _sds((L, D), F32))(logits, dlb)


def _hgrn_gates(qr, fr, lbv):
    e = jnp.exp(-jnp.abs(fr))
    big, small = 1.0 / (1.0 + e), e / (1.0 + e)
    sf = jnp.where(fr >= 0, big, small)
    snf = jnp.where(fr >= 0, small, big)
    f = lbv + (1.0 - lbv) * sf
    sq = _sig(qr)
    return qr * sq, (1.0 - lbv) * snf, jnp.log(f), sf, snf, f, sq


def _hgrn_intra(G, q, kk, g_scr, q_scr):
    C = HGRN_CHUNK
    g_scr[...] = G
    q_scr[...] = q
    srow = lax.broadcasted_iota(jnp.int32, (C, LANES), 0)
    lane = lax.broadcasted_iota(jnp.int32, (C, LANES), 1)
    at = jnp.zeros((C, LANES), F32)
    for t in range(C):
        e = jnp.where(srow <= t, jnp.exp(g_scr[pl.ds(t, 1), :] - G), 0.0)
        col = jnp.sum(e * kk * q_scr[pl.ds(t, 1), :], axis=-1, keepdims=True)
        at = jnp.where(lane == t, col, at)
    return at


def _hgrn_fwd(proj3, lb, ng):
    B, S, D4 = proj3.shape
    D = D4 // 4
    H = D // HGRN_EXPAND
    C = HGRN_CHUNK
    R = _tile(S, 256, C)
    ncb = R // C
    dk = HGRN_EXPAND

    def body(q_ref, f_ref, i_ref, go_ref, lb_ref, ng_ref, y_ref, o_ref, st_ref, st, g_scr, q_scr):
        @pl.when(pl.program_id(2) == 0)
        def _():
            st[...] = jnp.zeros((dk, dk), F32)

        lbv = lb_ref[...]

        def chunk(c, carry):
            r0 = pl.multiple_of(c * C, C)
            rows = pl.ds(r0, C)
            q, kk, lf, *_ = _hgrn_gates(q_ref[rows, :], f_ref[rows, :], lbv)
            vv = i_ref[rows, :]
            G = _scan_rows(lf)
            at = _hgrn_intra(G, q, kk, g_scr, q_scr)
            gl = g_scr[pl.ds(C - 1, 1), :]
            stv = st[...]
            st_ref[c] = stv
            o = _dot(q * jnp.exp(G), stv, "nt") + _dot(at, vv, "tn")[:C]
            st[...] = stv * jnp.exp(gl) + _dot(vv, kk * jnp.exp(gl - G), "tn")
            o_ref[rows, :] = o
            gv = go_ref[rows, :]
            y = o * lax.rsqrt(jnp.mean(o * o, axis=-1, keepdims=True) + RMS_EPS) * ng_ref[...] * (gv * _sig(gv))
            y_ref[rows, :] = y.astype(BF16)
            return carry

        lax.fori_loop(0, ncb, chunk, 0)

    def col(k):
        return pl.BlockSpec((None, R, dk), lambda b, h, i: (b, i, h + k * H))

    vec = pl.BlockSpec((1, dk), lambda b, h, i: (0, h))
    out = pl.BlockSpec((None, R, dk), lambda b, h, i: (b, i, h))
    return _call(body, "hgrn_fwd", (B, H, S // R), [col(0), col(1), col(2), col(3), vec, vec],
                 [out, out, pl.BlockSpec((None, None, ncb, dk, dk), lambda b, h, i: (b, h, i, 0, 0))],
                 [_sds((B, S, D), BF16), _sds((B, S, D), F32), _sds((B, H, S // C, dk, dk), F32)],
                 [pltpu.VMEM((dk, dk), F32), pltpu.VMEM((C, dk), F32), pltpu.VMEM((C, dk), F32)],
                 sem=("parallel", "parallel", "arbitrary"))(proj3, proj3, proj3, proj3, lb, ng)


def _hgrn_bwd(proj3, o3, dy3, states, lb, ng):
    B, S, D4 = proj3.shape
    D = D4 // 4
    H = D // HGRN_EXPAND
    C = HGRN_CHUNK
    R = _tile(S, 256, C)
    ncb = R // C
    nb = S // R
    dk = HGRN_EXPAND

    def body(q_ref, f_ref, i_ref, go_ref, o_ref, dy_ref, st_ref, lb_ref, ng_ref,
             dp_ref, red_ref, dst, g_scr, q_scr, dq_scr, acc):
        b, i = pl.program_id(1), pl.program_id(2)

        @pl.when(i == 0)
        def _():
            dst[...] = jnp.zeros((dk, dk), F32)

        @pl.when(jnp.logical_and(b == 0, i == 0))
        def _():
            acc[...] = jnp.zeros((2, dk), F32)

        lbv = lb_ref[...]
        ngv = ng_ref[...]
        srow = lax.broadcasted_iota(jnp.int32, (C, LANES), 0)
        lane = lax.broadcasted_iota(jnp.int32, (C, LANES), 1)

        def chunk(cc, carry):
            c = ncb - 1 - cc
            r0 = pl.multiple_of(c * C, C)
            rows = pl.ds(r0, C)
            qr, fr, vv, gv = q_ref[rows, :], f_ref[rows, :], i_ref[rows, :], go_ref[rows, :]
            q, kk, lf, sf, snf, f, sq = _hgrn_gates(qr, fr, lbv)
            o = o_ref[rows, :]
            dy = dy_ref[rows, :]
            rinv = lax.rsqrt(jnp.mean(o * o, axis=-1, keepdims=True) + RMS_EPS)
            on = o * rinv
            sgv = _sig(gv)
            dz = dy * (gv * sgv)
            dp_ref[3, rows, :] = (dy * on * ngv * (sgv * (1.0 + gv * (1.0 - sgv)))).astype(BF16)
            acc[pl.ds(1, 1), :] += jnp.sum(dz * on, axis=0, keepdims=True)
            don = dz * ngv
            do = rinv * (don - on * jnp.mean(don * on, axis=-1, keepdims=True))
            G = _scan_rows(lf)
            g_scr[...] = G
            q_scr[...] = q
            gl = g_scr[pl.ds(C - 1, 1), :]
            egl = jnp.exp(gl)
            eG = jnp.exp(G)
            eK = jnp.exp(gl - G)
            qg, kg = q * eG, kk * eK
            stv = st_ref[c]
            dsv = dst[...]
            dqg = _dot(do, stv)
            do_pad = jnp.concatenate([do, jnp.zeros((LANES - C, dk), F32)], axis=0)
            dat = _dot(vv, do_pad, "nt")
            dkg = _dot(vv, dsv)
            dgl = egl * jnp.sum(stv * dsv, axis=0, keepdims=True) + jnp.sum(dkg * kg, axis=0, keepdims=True)
            at = jnp.zeros((C, LANES), F32)
            dki = jnp.zeros((C, dk), F32)
            for t in range(C):
                e = jnp.where(srow <= t, jnp.exp(g_scr[pl.ds(t, 1), :] - G), 0.0)
                qt = q_scr[pl.ds(t, 1), :]
                at = jnp.where(lane == t, jnp.sum(e * kk * qt, axis=-1, keepdims=True), at)
                z = e * jnp.sum(jnp.where(lane == t, dat, 0.0), axis=-1, keepdims=True)
                dq_scr[pl.ds(t, 1), :] = jnp.sum(z * kk, axis=0, keepdims=True)
                dki = dki + z * qt
            dqi = dq_scr[...]
            dp_ref[2, rows, :] = (_dot(at, do_pad) + _dot(kg, dsv, "nt")).astype(BF16)
            dst[...] = dsv * egl + _dot(do, qg, "tn")
            dq = dqg * eG + dqi
            dkk = dkg * eK + dki
            dG = dqg * qg - dkg * kg + q * dqi - kk * dki
            dG = dG + jnp.where(srow == C - 1, dgl, 0.0)
            dlf = _scan_rows(dG, reverse=True)
            dsf = (1.0 - lbv) * sf * snf
            dp_ref[1, rows, :] = (dlf * dsf / f - dkk * dsf).astype(BF16)
            acc[pl.ds(0, 1), :] += jnp.sum(dlf * snf / f - dkk * snf, axis=0, keepdims=True)
            dp_ref[0, rows, :] = (dq * (sq * (1.0 + qr * (1.0 - sq)))).astype(BF16)
            return carry

        lax.fori_loop(0, ncb, chunk, 0)

        @pl.when(jnp.logical_and(b == B - 1, i == nb - 1))
        def _():
            red_ref[0] = acc[pl.ds(0, 1), :]
            red_ref[1] = acc[pl.ds(1, 1), :]

    def col(k):
        return pl.BlockSpec((None, R, dk), lambda h, b, i: (b, nb - 1 - i, h + k * H))

    vec = pl.BlockSpec((1, dk), lambda h, b, i: (0, h))
    row = pl.BlockSpec((None, R, dk), lambda h, b, i: (b, nb - 1 - i, h))
    stsp = pl.BlockSpec((None, None, ncb, dk, dk), lambda h, b, i: (b, h, nb - 1 - i, 0, 0))
    outs = _call(body, "hgrn_bwd", (H, B, nb), [col(0), col(1), col(2), col(3), row, row, stsp, vec, vec],
                 [pl.BlockSpec((4, None, R, dk), lambda h, b, i: (0, b, nb - 1 - i, h)),
                  pl.BlockSpec((2, 1, dk), lambda h, b, i: (0, 0, h))],
                 [_sds((4, B, S, D), BF16), _sds((2, 1, D), F32)],
                 [pltpu.VMEM((dk, dk), F32), pltpu.VMEM((C, dk), F32), pltpu.VMEM((C, dk), F32), pltpu.VMEM((C, dk), F32),
                  pltpu.VMEM((2, dk), F32)],
                 sem=("parallel", "arbitrary", "arbitrary"))(proj3, proj3, proj3, proj3, o3, dy3, states, lb, ng)
    return outs


def _hgrn_mixer_fwd(x, B, p):
    T, D = x.shape
    S = T // B
    h = _rms_fwd("hgrn_rms", x, p["mix_norm"][2:3])
    proj = _mm_nn("hgrn_in", h, p["hgrn_w_in"], F32)
    lb = _lb_fwd(p["hgrn_lb_logits"])
    y, o, states = _hgrn_fwd(proj.reshape(B, S, 4 * D), lb, p["hgrn_norm"])
    y = y.reshape(T, D)
    return _mm_res("hgrn_out", y, p["hgrn_w_out"], x, 1.0), (x, h, proj, lb, y, o, states)


def _hgrn_mixer_bwd(dx, saved, B, p):
    x, h, proj, lb, y, o, states = saved
    T, D = x.shape
    S = T // B
    g = {}
    dy = _mm_nt("hgrn_dy", [(dx, p["hgrn_w_out"])], F32)
    g["hgrn_w_out"] = _mm_tn("hgrn_dwout", y, dx)
    dp, red = _hgrn_bwd(proj.reshape(B, S, 4 * D), o, dy.reshape(B, S, D), states, lb, p["hgrn_norm"])
    dp = dp.reshape(4, T, D)
    g["hgrn_norm"] = red[1]
    g["hgrn_lb_logits"] = _lb_bwd(p["hgrn_lb_logits"], red[0])
    dwin = _mm_tn3("hgrn_dwin", h, dp)
    g["hgrn_w_in"] = jnp.moveaxis(dwin, 0, 1).reshape(1, D, 4 * D)
    dh = _mm_nt3("hgrn_dh", dp, p["hgrn_w_in4"])
    dx2, dg = _rms_bwd("hgrn_drms", dh, x, p["mix_norm"][2:3], dx)
    return dx2, dg, g


POOL_HALO = 16


def _pool_fwd(x3, g):
    B, S, D = x3.shape
    tt = _tile(S, 256, POOL_HALO)
    hb = tt // POOL_HALO
    G = D // len(POOL_WINDOWS)

    def body(x_ref, halo_ref, g_ref, m_ref):
        i = pl.program_id(1)

        def norm(xv):
            return xv * lax.rsqrt(jnp.mean(xv * xv, axis=-1, keepdims=True) + RMS_EPS) * g_ref[...]

        hm = norm(x_ref[...])
        ext = jnp.concatenate([jnp.where(i > 0, norm(halo_ref[...]), 0.0), hm], axis=0)
        pos = (i * tt + lax.broadcasted_iota(jnp.int32, (tt, 1), 0) + 1).astype(F32)
        for gi, win in enumerate(POOL_WINDOWS):
            s = ext[:, gi * G:(gi + 1) * G]
            w = 1
            while w < win:
                s = s + _roll(s, w)
                w *= 2
            m_ref[:, gi * G:(gi + 1) * G] = (s[POOL_HALO:] / jnp.minimum(pos, float(win)) - hm[:, gi * G:(gi + 1) * G]).astype(BF16)

    main = pl.BlockSpec((None, tt, D), lambda b, i: (b, i, 0))
    halo = pl.BlockSpec((None, POOL_HALO, D), lambda b, i: (b, jnp.maximum(i * hb - 1, 0), 0))
    return _call(body, "pool_fwd", (B, S // tt), [main, halo, pl.BlockSpec((1, D), lambda b, i: (0, 0))], main,
                 _sds((B, S, D), BF16), sem=("parallel", "parallel"))(x3, x3, g)


def _pool_bwd(dm3):
    B, S, D = dm3.shape
    tt = _tile(S, 256, POOL_HALO)
    hb = tt // POOL_HALO
    nt = S // tt
    G = D // len(POOL_WINDOWS)
    L = tt + POOL_HALO

    def body(dm_ref, nxt_ref, dh_ref):
        i = pl.program_id(1)
        posm = (i * tt + lax.broadcasted_iota(jnp.int32, (tt, 1), 0) + 1).astype(F32)
        posn = ((i + 1) * tt + lax.broadcasted_iota(jnp.int32, (POOL_HALO, 1), 0) + 1).astype(F32)
        for gi, win in enumerate(POOL_WINDOWS):
            sl = slice(gi * G, (gi + 1) * G)
            dm = dm_ref[:, sl]
            s = jnp.concatenate([dm / jnp.minimum(posm, float(win)),
                                 jnp.where(i < nt - 1, nxt_ref[:, sl] / jnp.minimum(posn, float(win)), 0.0)], axis=0)
            w = 1
            while w < win:
                s = s + _roll(s, L - w)
                w *= 2
            dh_ref[:, sl] = s[:tt] - dm

    main = pl.BlockSpec((None, tt, D), lambda b, i: (b, i, 0))
    nxt = pl.BlockSpec((None, POOL_HALO, D), lambda b, i: (b, jnp.minimum((i + 1) * hb, S // POOL_HALO - 1), 0))
    return _call(body, "pool_bwd", (B, nt), [main, nxt], main, _sds((B, S, D), F32), sem=("parallel", "parallel"))(dm3, dm3)


def _pool_mixer_fwd(x, B, p):
    T, D = x.shape
    NG = len(POOL_WINDOWS)
    G = D // NG
    tm = _tile(T, 512)
    m = _pool_fwd(x.reshape(B, T // B, D), p["mix_norm"][3:4]).reshape(T, D)

    def epi(res, ex, outs):
        outs[0][...] = ex[1][...] + res * ex[0][...]

    blk = pl.BlockSpec((tm, G), lambda i, g: (i, g))
    x2 = _mm("pool_out", "nn", [(m, p["pool_w4"])], [(blk, pl.BlockSpec((None, G, G), lambda i, g: (g, 0, 0)))],
             (T // tm, NG), None, [_sds((T, D), F32)], [blk], extras=[p["pool_scale"], x],
             extra_specs=[pl.BlockSpec((1, G), lambda i, g: (0, g)), blk], epilogue=epi)[0]
    return x2, (x, m)


def _pool_mixer_bwd(dx, saved, B, p):
    x, m = saved
    T, D = x.shape
    NG = len(POOL_WINDOWS)
    G = D // NG
    tm = _tile(T, 512)
    g = {}

    def epi(zz, ex, outs):
        dy = ex[0][...]
        outs[0][...] = (dy * ex[1][...]).astype(BF16)
        part = jnp.sum(dy * zz, axis=0, keepdims=True)

        @pl.when(pl.program_id(1) == 0)
        def _():
            outs[1][...] = part

        @pl.when(pl.program_id(1) > 0)
        def _():
            outs[1][...] += part

    blk = pl.BlockSpec((tm, G), lambda g_, i: (i, g_))
    wsp = pl.BlockSpec((None, G, G), lambda g_, i: (g_, 0, 0))
    vec = pl.BlockSpec((1, G), lambda g_, i: (0, g_))
    dz, dsc = _mm("pool_dz", "nn", [(m, p["pool_w4"])], [(blk, wsp)], (NG, T // tm), None,
                  [_sds((T, D), BF16), _sds((1, D), F32)], [blk, vec], extras=[dx, p["pool_scale"]], extra_specs=[blk, vec],
                  epilogue=epi, sem=("parallel", "arbitrary"))
    g["pool_scale"] = dsc
    tk = _tile(T, 512)
    kb = pl.BlockSpec((tk, G), lambda g_, k: (k, g_))
    g["pool_w"] = _mm("pool_dw", "tn", [(m, dz)], [(kb, kb)], (NG, T // tk), 1, [_sds((NG, G, G), F32)],
                      [pl.BlockSpec((None, G, G), lambda g_, k: (g_, 0, 0))], acc_shape=(G, G))[0][None]
    blk2 = pl.BlockSpec((tm, G), lambda i, g_: (i, g_))
    dm = _mm("pool_dm", "nt", [(dz, p["pool_w4"])], [(blk2, pl.BlockSpec((None, G, G), lambda i, g_: (g_, 0, 0)))],
             (T // tm, NG), None, [_sds((T, D), F32)], [blk2])[0]
    dh = _pool_bwd(dm.reshape(B, T // B, D)).reshape(T, D)
    dx2, dg = _rms_bwd("pool_drms", dh, x, p["mix_norm"][3:4], dx)
    return dx2, dg, g


_MIXERS = ((_conv_mixer_fwd, _conv_mixer_bwd), (_fox_mixer_fwd, _fox_mixer_bwd), (_hgrn_mixer_fwd, _hgrn_mixer_bwd),
           (_pool_mixer_fwd, _pool_mixer_bwd))


def _local_step(x3, tgt3, w):
    B, S, D = x3.shape
    T = B * S
    depth = w["ffn_norm"].shape[0]
    F = w["ffn_w_gate"].shape[-1]
    H = FOX_HEADS
    p = dict(w)
    p["wgu"] = jnp.stack((w["ffn_w_gate"], w["ffn_w_up"]), axis=2)
    p["conv_w_in2"] = w["conv_w_in"][0].reshape(D, 2, D).transpose(1, 0, 2)
    p["conv_b_in2"] = w["conv_b_in"].reshape(2, 1, D)
    p["conv_dw32"] = jnp.pad(w["conv_dw"][0], ((0, HALO - CONV_TAPS), (0, 0)))
    p["conv_w_out"] = w["conv_w_out"][0]
    p["fox_w_qkv"] = w["fox_w_in"][0][:, :3 * D]
    p["fox_w_f"] = jnp.pad(w["fox_w_in"][0][:, 3 * D:], ((0, 0), (0, LANES - H)))
    p["fox_b_f128"] = jnp.pad(w["fox_b_f"], ((0, 0), (0, LANES - H)))
    p["fox_w_out"] = w["fox_w_out"][0]
    p["hgrn_w_in"] = w["hgrn_w_in"][0]
    p["hgrn_w_in4"] = w["hgrn_w_in"][0].reshape(D, 4, D).transpose(1, 0, 2)
    p["hgrn_w_out"] = w["hgrn_w_out"][0]
    p["pool_w4"] = w["pool_w"][0]

    x = x3.reshape(T, D)
    saved = []
    for i in range(depth):
        x, s0 = _ffn_fwd(f"{i}a", x, w["ffn_norm"][i, 0:1], p["wgu"][i, 0], w["ffn_w_down"][i, 0])
        x, s1 = _MIXERS[i % 4][0](x, B, p)
        x, s2 = _ffn_fwd(f"{i}b", x, w["ffn_norm"][i, 1:2], p["wgu"][i, 1], w["ffn_w_down"][i, 1])
        saved.append((s0, s1, s2))
    loss, dx, dfinal = _loss_head(x, w["final_norm"].reshape(1, D), tgt3.reshape(T, D))

    g = {"final_norm": dfinal}
    dffn_norm = [[None, None] for _ in range(depth)]
    dwgu = [[None, None] for _ in range(depth)]
    dwd = [[None, None] for _ in range(depth)]
    dmix = [None] * depth
    for i in reversed(range(depth)):
        s0, s1, s2 = saved[i]
        dx, dffn_norm[i][1], dwgu[i][1], dwd[i][1] = _ffn_bwd(f"{i}b", dx, s2, w["ffn_norm"][i, 1:2], p["wgu"][i, 1],
                                                            w["ffn_w_down"][i, 1])
        dx, dmix[i], gm = _MIXERS[i % 4][1](dx, s1, B, p)
        g.update(gm)
        dx, dffn_norm[i][0], dwgu[i][0], dwd[i][0] = _ffn_bwd(f"{i}a", dx, s0, w["ffn_norm"][i, 0:1], p["wgu"][i, 0],
                                                            w["ffn_w_down"][i, 0])
    g["ffn_norm"] = jnp.stack([jnp.stack([a[0], b[0]]) for a, b in dffn_norm])
    g["ffn_w_gate"] = jnp.stack([jnp.stack([a[0], b[0]]) for a, b in dwgu])
    g["ffn_w_up"] = jnp.stack([jnp.stack([a[1], b[1]]) for a, b in dwgu])
    g["ffn_w_down"] = jnp.stack([jnp.stack([a, b]) for a, b in dwd])
    g["mix_norm"] = jnp.concatenate(dmix, axis=0)
    return loss, dx.reshape(B, S, D), {n: g[n].reshape(w[n].shape) for n in WEIGHTS}


def _adamw(name, w, m, v, parts):
    shape = w.shape
    cols = shape[-1]
    rows = w.size // cols
    tr = _tile(rows, max(8, (1 << 19) // cols))
    n = len(parts)
    c1 = 1.0 - ADAM_B1 ** ADAM_STEP
    c2 = 1.0 - ADAM_B2 ** ADAM_STEP

    def body(*refs):
        w_ref, m_ref, v_ref = refs[:3]
        g_ref, d_ref, m2_ref, v2_ref = refs[3 + n:]
        g = refs[3][...].astype(F32)
        for k in range(1, n):
            g = g + refs[3 + k][...].astype(F32)
        m2 = ADAM_B1 * m_ref[...] + (1.0 - ADAM_B1) * g
        v2 = ADAM_B2 * v_ref[...] + (1.0 - ADAM_B2) * (g * g)
        g_ref[...] = g
        m2_ref[...] = m2
        v2_ref[...] = v2
        d_ref[...] = -ADAM_LR * ((m2 / c1) / (jnp.sqrt(v2 / c2) + ADAM_EPS) + ADAM_WD * w_ref[...])

    blk = pl.BlockSpec((tr, cols), lambda i: (i, 0))
    outs = _call(body, name, (rows // tr,), [blk] * (3 + n), [blk] * 4, [_sds((rows, cols), F32)] * 4, sem=("parallel",))(
        *[t.reshape(rows, cols) for t in (w, m, v, *parts)])
    return [o.reshape(shape) for o in outs]


ANY = pl.BlockSpec(memory_space=pl.ANY)
FLAT_COLS = 1024


def _place():
    return lax.axis_index("x"), lax.axis_index("y"), lax.axis_index("c")


def _all_gather(name, xs):
    K = len(xs)

    def body(*refs):
        x_refs, out_refs = refs[:K], refs[K:2 * K]
        send_sems, recv_sems, local_sems = refs[2 * K:]
        xi, yi, ci = _place()
        me, sibling = (xi, yi, ci), (xi, yi, 1 - ci)
        chips = [(1 - xi, yi), (xi, 1 - yi), (1 - xi, 1 - yi)]

        def slot(a, px, py, pc):
            return out_refs[a].at[4 * px + 2 * py + pc]

        def copy(a, k, block, to, own=False):
            return pltpu.make_async_remote_copy(src_ref=x_refs[a] if own else slot(a, *block), dst_ref=slot(a, *block),
                                                send_sem=send_sems.at[7 * a + k], recv_sem=recv_sems.at[7 * a + k],
                                                device_id=to, device_id_type=MESH)

        mine = [pltpu.make_async_copy(x_refs[a], slot(a, *me), local_sems.at[a]) for a in range(K)]
        first = [copy(a, 1 + j, me, (*chip, ci), own=True) for j, chip in enumerate(chips) for a in range(K)]
        first += [copy(a, 0, me, sibling, own=True) for a in range(K)]
        for cp in mine + first:
            cp.start()
        passed = []
        for j, chip in enumerate(chips):
            for a in range(K):
                copy(a, 1 + j, (*chip, ci), me).wait_recv()
                passed.append(copy(a, 4 + j, (*chip, ci), sibling))
                passed[-1].start()
        for a in range(K):
            copy(a, 0, sibling, me).wait_recv()
            for j, chip in enumerate(chips):
                copy(a, 4 + j, (*chip, 1 - ci), me).wait_recv()
        for cp in first + passed:
            cp.wait_send()
        for cp in mine:
            cp.wait()

    return pl.pallas_call(body, name=name, out_shape=[_sds((N_DEV,) + x.shape, x.dtype) for x in xs], in_specs=[ANY] * K,
                          out_specs=[ANY] * K,
                          scratch_shapes=[pltpu.SemaphoreType.DMA((7 * K,)), pltpu.SemaphoreType.DMA((7 * K,)),
                                          pltpu.SemaphoreType.DMA((K,))])(*xs)


def _swap_sibling(name, ts):
    K = len(ts)

    def body(*refs):
        t_refs, out_refs, send_sems, recv_sems = refs[:K], refs[K:2 * K], refs[2 * K], refs[2 * K + 1]
        xi, yi, ci = _place()
        cps = [pltpu.make_async_remote_copy(src_ref=t_refs[a], dst_ref=out_refs[a], send_sem=send_sems.at[a],
                                            recv_sem=recv_sems.at[a], device_id=(xi, yi, 1 - ci), device_id_type=MESH)
               for a in range(K)]
        for cp in cps:
            cp.start()
        for cp in cps:
            cp.wait()

    return pl.pallas_call(body, name=name, out_shape=[_sds(t.shape, t.dtype) for t in ts], in_specs=[ANY] * K,
                          out_specs=[ANY] * K,
                          scratch_shapes=[pltpu.SemaphoreType.DMA((K,)), pltpu.SemaphoreType.DMA((K,))])(*ts)


def _scatter_chips(name, ts):
    K = len(ts)

    def body(*refs):
        t_refs, out_refs, send_sems, recv_sems = refs[:K], refs[K:2 * K], refs[2 * K], refs[2 * K + 1]
        xi, yi, ci = _place()
        chips = [(1 - xi, yi), (xi, 1 - yi), (1 - xi, 1 - yi)]
        cps = [pltpu.make_async_remote_copy(src_ref=t_refs[a].at[2 * cx + cy], dst_ref=out_refs[a].at[j],
                                            send_sem=send_sems.at[3 * a + j], recv_sem=recv_sems.at[3 * a + j],
                                            device_id=(cx, cy, ci), device_id_type=MESH)
               for j, (cx, cy) in enumerate(chips) for a in range(K)]
        for cp in cps:
            cp.start()
        for cp in cps:
            cp.wait()

    return pl.pallas_call(body, name=name, out_shape=[_sds((3,) + t.shape[1:], t.dtype) for t in ts], in_specs=[ANY] * K,
                          out_specs=[ANY] * K,
                          scratch_shapes=[pltpu.SemaphoreType.DMA((3 * K,)), pltpu.SemaphoreType.DMA((3 * K,))])(*ts)


def _add_bf16(name, a, b):
    shape = a.shape
    N, C = shape[0], shape[-1]
    R = a.size // (N * C)
    tr = _tile(R, max(8, (1 << 19) // C))

    def body(a_ref, b_ref, o_ref):
        o_ref[...] = (a_ref[...].astype(F32) + b_ref[...].astype(F32)).astype(BF16)

    blk = pl.BlockSpec((None, tr, C), lambda n, i: (n, i, 0))
    return _call(body, name, (N, R // tr), [blk, blk], blk, _sds((N, R, C), BF16), sem=("parallel", "parallel"))(
        a.reshape(N, R, C), b.reshape(N, R, C)).reshape(shape)


def _flat(parts, dtype, lead=()):
    flat = jnp.concatenate([t.reshape(lead + (-1,)).astype(dtype) for t in parts], axis=-1)
    n = flat.shape[-1]
    unit = 16 * FLAT_COLS
    padded = -(-n // unit) * unit
    flat = jnp.pad(flat, [(0, 0)] * len(lead) + [(0, padded - n)])
    return flat.reshape(lead + (padded // FLAT_COLS, FLAT_COLS))


def _unflat(flat, shapes, lead=()):
    flat = flat.reshape(lead + (-1,))
    out, off = [], 0
    for shp in shapes:
        n = math.prod(shp)
        out.append(flat[..., off:off + n].reshape(lead + tuple(shp)))
        off += n
    return out


def _dev_major(full, ax):
    shp = full.shape
    return jnp.moveaxis(full.reshape(shp[:ax] + (N_DEV, shp[ax] // N_DEV) + shp[ax + 1:]), ax, 0)


def _from_dev_major(blocks, ax):
    t = jnp.moveaxis(blocks, 0, ax)
    shp = t.shape
    return t.reshape(shp[:ax] + (shp[ax] * shp[ax + 1],) + shp[ax + 2:])


def kernel(x, *rest):
    nw = len(WEIGHTS)
    w = dict(zip(WEIGHTS, rest[:nw]))
    tgt = rest[nw]
    m = dict(zip(WEIGHTS, rest[nw + 1:2 * nw + 1]))
    v = dict(zip(WEIGHTS, rest[2 * nw + 1:3 * nw + 1]))
    xi, yi, ci = _place()
    dev = 4 * xi + 2 * yi + ci

    big = _all_gather("gather_matrices", [w[n].astype(BF16) for n in BIG])
    small = _all_gather("gather_vectors", [_flat([w[n] for n in SMALL], F32)])[0]
    full = {n: w[n] for n in REPL}
    for n, blocks in zip(BIG, big):
        full[n] = _from_dev_major(blocks, SHARDED[n])
    for n, blocks in zip(SMALL, _unflat(small, [w[n].shape for n in SMALL], (N_DEV,))):
        full[n] = _from_dev_major(blocks, SHARDED[n])

    loss, gx, g = _local_step(x, tgt, full)
    loss = lax.psum(loss[0, 0], ("x", "y", "c"))

    vec_names = SMALL + REPL
    vec = _all_gather("gather_vector_grads", [_flat([g[n] for n in vec_names], F32)])[0]
    vec_parts = dict(zip(vec_names, _unflat(vec, [g[n].shape for n in vec_names], (N_DEV,))))

    keep, send = [], []
    for n in BIG:
        gd = _dev_major(g[n], SHARDED[n]).astype(BF16)
        gd = gd.reshape((4, 2) + gd.shape[1:])
        keep.append(lax.dynamic_index_in_dim(gd, ci, 1, keepdims=False))
        send.append(lax.dynamic_index_in_dim(gd, 1 - ci, 1, keepdims=False))
    got_sib = _swap_sibling("grads_to_sibling", send)
    pair = [_add_bf16(f"grad_pair_sum_{n}", a, b) for n, a, b in zip(BIG, keep, got_sib)]
    got_chips = _scatter_chips("grads_to_chips", pair)
    chip = 2 * xi + yi

    res = {}
    for k, n in enumerate(BIG):
        parts = [lax.dynamic_index_in_dim(keep[k], chip, 0, keepdims=False),
                 lax.dynamic_index_in_dim(got_sib[k], chip, 0, keepdims=False), got_chips[k][0], got_chips[k][1], got_chips[k][2]]
        res[n] = _adamw(f"adamw_{n}", w[n], m[n], v[n], parts)
    for n in vec_names:
        parts = vec_parts[n]
        if n in SHARDED:
            ax = SHARDED[n]
            parts = lax.dynamic_slice_in_dim(parts, dev * w[n].shape[ax], w[n].shape[ax], ax + 1)
        res[n] = _adamw(f"adamw_{n}", w[n], m[n], v[n], [parts[d] for d in range(N_DEV)])
    return (loss, gx, *[res[n][0] for n in WEIGHTS], *[res[n][1] for n in WEIGHTS], *[res[n][2] for n in WEIGHTS],
            *[res[n][3] for n in WEIGHTS])
```

```python
import functools
import math

import jax
import jax.numpy as jnp
from jax import lax
from jax.experimental import pallas as pl
from jax.experimental.pallas import tpu as pltpu

F32 = jnp.float32
BF16 = jnp.bfloat16
MESH = pl.DeviceIdType.MESH

N_DEV = 8
RMS_EPS = 1e-6
LN_EPS = 1e-5
FOX_HEADS = 16
HGRN_EXPAND = 128
HGRN_CHUNK = 32
POOL_WINDOWS = (2, 4, 8, 16)
ADAM_LR, ADAM_B1, ADAM_B2, ADAM_EPS, ADAM_WD, ADAM_STEP = 0.001, 0.9, 0.999, 1e-08, 0.01, 10
LANES = 128
VMEM_LIMIT_MB = 48

SHARDED = dict(
    ffn_norm=2, ffn_w_gate=3, ffn_w_up=3, ffn_w_down=2, conv_w_in=2, conv_dw=2, conv_w_out=1, fox_w_in=2, fox_w_out=1,
    hgrn_w_in=2, hgrn_norm=1, hgrn_w_out=1, pool_w=2, pool_scale=1)
BIG = ("ffn_w_gate", "ffn_w_up", "ffn_w_down", "conv_w_in", "conv_w_out", "fox_w_in", "fox_w_out", "hgrn_w_in",
       "hgrn_w_out", "pool_w")
SMALL = ("ffn_norm", "conv_dw", "hgrn_norm", "pool_scale")
REPL = ("mix_norm", "final_norm", "conv_b_in", "conv_dw_b", "conv_ln_g", "conv_ln_b", "fox_b_f", "hgrn_lb_logits")
WEIGHTS = ("ffn_norm", "ffn_w_gate", "ffn_w_up", "ffn_w_down", "mix_norm", "final_norm", "conv_w_in", "conv_b_in",
           "conv_dw", "conv_dw_b", "conv_ln_g", "conv_ln_b", "conv_w_out", "fox_w_in", "fox_b_f", "fox_w_out",
           "hgrn_w_in", "hgrn_lb_logits", "hgrn_norm", "hgrn_w_out", "pool_w", "pool_scale")


def _tile(n, pref, mult=8):
    if n <= pref:
        return n
    for t in range(pref - pref % mult, 0, -mult):
        if n % t == 0:
            return t
    return n


def _sig(x):
    return 1.0 / (1.0 + jnp.exp(-x))


def _call(body, name, grid, in_specs, out_specs, out_shape, scratch=(), sem=None):
    params = dict(vmem_limit_bytes=VMEM_LIMIT_MB << 20)
    if sem is not None:
        params["dimension_semantics"] = sem
    return pl.pallas_call(body, name=name, grid=grid, in_specs=in_specs, out_specs=out_specs, out_shape=out_shape,
                          scratch_shapes=list(scratch), compiler_params=pltpu.CompilerParams(**params))


def _sds(shape, dtype):
    return jax.ShapeDtypeStruct(tuple(shape), dtype)


_DN = {"nn": (((1,), (0,)), ((), ())), "nt": (((1,), (1,)), ((), ())), "tn": (((0,), (0,)), ((), ()))}


def _dot(a, b, mode="nn"):
    return lax.dot_general(a.astype(BF16), b.astype(BF16), _DN[mode], preferred_element_type=F32)


def _roll(x, shift, axis=0):
    n = x.shape[axis]
    shift = shift % n
    return x if shift == 0 else pltpu.roll(x, shift, axis)


def _scan_rows(x, reverse=False):
    n = x.shape[0]
    row = lax.broadcasted_iota(jnp.int32, x.shape, 0)
    sh = 1
    while sh < n:
        if reverse:
            x = x + jnp.where(row < n - sh, _roll(x, n - sh), 0.0)
        else:
            x = x + jnp.where(row >= sh, _roll(x, sh), 0.0)
        sh *= 2
    return x


def _mm(name, mode, pairs, pair_specs, grid, k_axis, out_shapes, out_specs, acc_shape=None, extras=(), extra_specs=(),
        epilogue=None, alpha=1.0, sem=None):
    npair, nex, nout = len(pairs), len(extras), len(out_shapes)
    nk = grid[k_axis] if k_axis is not None else 1

    def body(*refs):
        prs = refs[:2 * npair]
        ex = refs[2 * npair:2 * npair + nex]
        outs = refs[2 * npair + nex:2 * npair + nex + nout]

        def partial():
            p = None
            for i in range(npair):
                d = _dot(prs[2 * i][...], prs[2 * i + 1][...], mode)
                p = d if p is None else p + d
            return p

        def finish(res):
            if alpha != 1.0:
                res = res * alpha
            if epilogue is None:
                outs[0][...] = res.astype(outs[0].dtype)
            else:
                epilogue(res, ex, outs)

        if k_axis is None:
            finish(partial())
        else:
            acc = refs[-1]
            k = pl.program_id(k_axis)

            @pl.when(k == 0)
            def _():
                acc[...] = partial()

            @pl.when(k > 0)
            def _():
                acc[...] += partial()

            @pl.when(k == nk - 1)
            def _():
                finish(acc[...])

    if sem is None:
        sem = tuple("arbitrary" if i == k_axis else "parallel" for i in range(len(grid)))
    scratch = [pltpu.VMEM(acc_shape, F32)] if k_axis is not None else []
    flat = [t for p in pairs for t in p]
    flat_specs = [s for p in pair_specs for s in p]
    return _call(body, name, grid, flat_specs + list(extra_specs), out_specs, out_shapes, scratch, sem)(*flat, *extras)


def _mm_nn(name, a, b, out_dtype, tm=512, tn=512):
    M, K = a.shape
    N = b.shape[1]
    tm, tn = _tile(M, tm), _tile(N, tn, LANES)
    return _mm(name, "nn", [(a, b)], [(pl.BlockSpec((tm, K), lambda i, j: (i, 0)), pl.BlockSpec((K, tn), lambda i, j: (0, j)))],
               (M // tm, N // tn), None, [_sds((M, N), out_dtype)], [pl.BlockSpec((tm, tn), lambda i, j: (i, j))])[0]


def _mm_res(name, u, w, x, alpha, tm=512):
    T, K = u.shape
    D = w.shape[1]
    tm = _tile(T, tm)

    def epi(res, ex, outs):
        outs[0][...] = ex[0][...] + res

    return _mm(name, "nn", [(u, w)], [(pl.BlockSpec((tm, K), lambda i: (i, 0)), pl.BlockSpec((K, D), lambda i: (0, 0)))],
               (T // tm,), None, [_sds((T, D), F32)], [pl.BlockSpec((tm, D), lambda i: (i, 0))],
               extras=[x], extra_specs=[pl.BlockSpec((tm, D), lambda i: (i, 0))], epilogue=epi, alpha=alpha)[0]


def _mm_nt(name, pairs, out_dtype, tm=512, tn=512, alpha=1.0):
    M = pairs[0][0].shape[0]
    N = pairs[0][1].shape[0]
    tm, tn = _tile(M, tm), _tile(N, tn, LANES)
    specs = [(pl.BlockSpec((tm, a.shape[1]), lambda i, j: (i, 0)), pl.BlockSpec((tn, b.shape[1]), lambda i, j: (j, 0)))
             for a, b in pairs]
    return _mm(name, "nt", pairs, specs, (M // tm, N // tn), None, [_sds((M, N), out_dtype)],
               [pl.BlockSpec((tm, tn), lambda i, j: (i, j))], alpha=alpha)[0]


def _mm_nt3(name, a3, b3, tm=512, tn=512):
    S, M, K = a3.shape
    N = b3.shape[1]
    tm, tn = _tile(M, tm), _tile(N, tn, LANES)
    return _mm(name, "nt", [(a3, b3)],
               [(pl.BlockSpec((None, tm, K), lambda i, j, s: (s, i, 0)), pl.BlockSpec((None, tn, K), lambda i, j, s: (s, j, 0)))],
               (M // tm, N // tn, S), 2, [_sds((M, N), F32)], [pl.BlockSpec((tm, tn), lambda i, j, s: (i, j))],
               acc_shape=(tm, tn))[0]


def _mm_tn(name, a, b, alpha=1.0, tm=1408, tn=1408, tk=512):
    T, M = a.shape
    N = b.shape[1]
    tm, tn, tk = _tile(M, tm, LANES), _tile(N, tn, LANES), _tile(T, tk)
    return _mm(name, "tn", [(a, b)],
               [(pl.BlockSpec((tk, tm), lambda i, j, k: (k, i)), pl.BlockSpec((tk, tn), lambda i, j, k: (k, j)))],
               (M // tm, N // tn, T // tk), 2, [_sds((M, N), F32)], [pl.BlockSpec((tm, tn), lambda i, j, k: (i, j))],
               acc_shape=(tm, tn), alpha=alpha)[0]


def _mm_tn3(name, a, b3, tm=1408, tn=1408, tk=512):
    T, M = a.shape
    S, _, N = b3.shape
    tm, tn, tk = _tile(M, tm, LANES), _tile(N, tn, LANES), _tile(T, tk)
    return _mm(name, "tn", [(a, b3)],
               [(pl.BlockSpec((tk, tm), lambda s, i, j, k: (k, i)), pl.BlockSpec((None, tk, tn), lambda s, i, j, k: (s, k, j)))],
               (S, M // tm, N // tn, T // tk), 3, [_sds((S, M, N), F32)],
               [pl.BlockSpec((None, tm, tn), lambda s, i, j, k: (s, i, j))], acc_shape=(tm, tn))[0]


def _rms_fwd(name, x, g):
    T, D = x.shape
    tt = _tile(T, 512)

    def body(x_ref, g_ref, h_ref):
        xv = x_ref[...]
        r = lax.rsqrt(jnp.mean(xv * xv, axis=-1, keepdims=True) + RMS_EPS)
        h_ref[...] = (xv * r * g_ref[...]).astype(h_ref.dtype)

    row = pl.BlockSpec((tt, D), lambda i: (i, 0))
    return _call(body, name, (T // tt,), [row, pl.BlockSpec((1, D), lambda i: (0, 0))], row, _sds((T, D), BF16))(x, g)


def _rms_bwd(name, dh, x, g, dx_in):
    T, D = x.shape
    tt = _tile(T, 512)

    def body(dh_ref, x_ref, g_ref, dxi_ref, dx_ref, dxb_ref, dg_ref):
        xv = x_ref[...]
        r = lax.rsqrt(jnp.mean(xv * xv, axis=-1, keepdims=True) + RMS_EPS)
        xh = xv * r
        dhv = dh_ref[...]
        dxh = dhv * g_ref[...]
        dx = dxi_ref[...] + r * (dxh - xh * jnp.mean(dxh * xh, axis=-1, keepdims=True))
        dx_ref[...] = dx
        dxb_ref[...] = dx.astype(BF16)
        part = jnp.sum(dhv * xh, axis=0, keepdims=True)

        @pl.when(pl.program_id(0) == 0)
        def _():
            dg_ref[...] = part

        @pl.when(pl.program_id(0) > 0)
        def _():
            dg_ref[...] += part

    row = pl.BlockSpec((tt, D), lambda i: (i, 0))
    vec = pl.BlockSpec((1, D), lambda i: (0, 0))
    dx, dxb, dg = _call(body, name, (T // tt,), [row, row, vec, row], [row, row, vec],
                        [_sds((T, D), F32), _sds((T, D), BF16), _sds((1, D), F32)], sem=("arbitrary",))(dh, x, g, dx_in)
    return (dx, dxb), dg


def _loss_head(x, g, tgt):
    T, D = x.shape
    tt = _tile(T, 512)

    def body(x_ref, g_ref, t_ref, loss_ref, dx_ref, dxb_ref, dg_ref):
        xv = x_ref[...]
        r = lax.rsqrt(jnp.mean(xv * xv, axis=-1, keepdims=True) + RMS_EPS)
        xh = xv * r
        e = xh * g_ref[...] - t_ref[...]
        lp = 0.5 * jnp.sum(jnp.sum(e * e, axis=-1, keepdims=True), axis=0, keepdims=True) / D
        dy = e / D
        dxh = dy * g_ref[...]
        dx = r * (dxh - xh * jnp.mean(dxh * xh, axis=-1, keepdims=True))
        dx_ref[...] = dx
        dxb_ref[...] = dx.astype(BF16)
        part = jnp.sum(dy * xh, axis=0, keepdims=True)

        @pl.when(pl.program_id(0) == 0)
        def _():
            dg_ref[...] = part
            loss_ref[...] = lp

        @pl.when(pl.program_id(0) > 0)
        def _():
            dg_ref[...] += part
            loss_ref[...] += lp

    row = pl.BlockSpec((tt, D), lambda i: (i, 0))
    vec = pl.BlockSpec((1, D), lambda i: (0, 0))
    one = pl.BlockSpec((1, 1), lambda i: (0, 0))
    loss, dx, dxb, dg = _call(body, "loss_head", (T // tt,), [row, vec, row], [one, row, row, vec],
                              [_sds((1, 1), F32), _sds((T, D), F32), _sds((T, D), BF16), _sds((1, D), F32)],
                              sem=("arbitrary",))(x, g, tgt)
    return loss, (dx, dxb), dg


def _colsum3(name, a3):
    S, T, N = a3.shape
    tt = _tile(T, 512)

    def body(a_ref, o_ref):
        part = jnp.sum(a_ref[...].astype(F32), axis=0, keepdims=True)

        @pl.when(pl.program_id(1) == 0)
        def _():
            o_ref[...] = part

        @pl.when(pl.program_id(1) > 0)
        def _():
            o_ref[...] += part

    return _call(body, name, (S, T // tt), [pl.BlockSpec((None, tt, N), lambda s, i: (s, i, 0))],
                 pl.BlockSpec((None, 1, N), lambda s, i: (s, 0, 0)), _sds((S, 1, N), F32), sem=("parallel", "arbitrary"))(a3)


def _glu_mm(name, h, w2, bias2, mode, u_dtype, tm=1024, tn=256):
    T, K = h.shape
    N = w2.shape[2]
    tm, tn = _tile(T, tm), _tile(N, tn, LANES)
    has_bias = bias2 is not None

    rc = _tile(tm, ROW_CHUNK, 16)

    def body(*refs):
        h_ref, w_ref = refs[0], refs[1]
        ab_ref, u_ref, a_scr, b_scr = refs[-4:]
        hv = h_ref[...]
        a_scr[...] = _dot(hv, w_ref[0])
        b_scr[...] = _dot(hv, w_ref[1])

        def chunk(r, carry):
            rows = pl.ds(pl.multiple_of(r * rc, rc), rc)
            a, b = a_scr[rows, :], b_scr[rows, :]
            if has_bias:
                a = a + refs[2][0]
                b = b + refs[2][1]
            u = a * _sig(a) * b if mode == "swiglu" else a * _sig(b)
            ab_ref[0, rows, :] = a.astype(BF16)
            ab_ref[1, rows, :] = b.astype(BF16)
            u_ref[rows, :] = u.astype(u_ref.dtype)
            return carry

        lax.fori_loop(0, tm // rc, chunk, 0, unroll=CHUNK_UNROLL)

    in_specs = [pl.BlockSpec((tm, K), lambda i, j: (i, 0)), pl.BlockSpec((2, K, tn), lambda i, j: (0, 0, j))]
    args = [h, w2]
    if has_bias:
        in_specs.append(pl.BlockSpec((2, 1, tn), lambda i, j: (0, 0, j)))
        args.append(bias2)
    return _call(body, name, (T // tm, N // tn), in_specs,
                 [pl.BlockSpec((2, tm, tn), lambda i, j: (0, i, j)), pl.BlockSpec((tm, tn), lambda i, j: (i, j))],
                 [_sds((2, T, N), BF16), _sds((T, N), u_dtype)], [pltpu.VMEM((tm, tn), F32), pltpu.VMEM((tm, tn), F32)],
                 sem=("parallel", "parallel"))(*args)


def _swiglu_bwd_mm(name, dxb, wd, ab, alpha, tm=1024, tn=256):
    T, D = dxb.shape
    N = wd.shape[0]
    tm, tn = _tile(T, tm), _tile(N, tn, LANES)
    rc = _tile(tm, ROW_CHUNK, 16)

    def body(dx_ref, wd_ref, ab_ref, dab_ref, u_ref, acc):
        acc[...] = _dot(dx_ref[...], wd_ref[...], "nt")

        def chunk(r, carry):
            rows = pl.ds(pl.multiple_of(r * rc, rc), rc)
            du = acc[rows, :] * alpha
            a = ab_ref[0, rows, :].astype(F32)
            b = ab_ref[1, rows, :].astype(F32)
            sg = _sig(a)
            sa = a * sg
            dab_ref[0, rows, :] = (du * b * (sg * (1.0 + a * (1.0 - sg)))).astype(BF16)
            dab_ref[1, rows, :] = (du * sa).astype(BF16)
            u_ref[rows, :] = (sa * b).astype(BF16)
            return carry

        lax.fori_loop(0, tm // rc, chunk, 0, unroll=CHUNK_UNROLL)

    ab_spec = pl.BlockSpec((2, tm, tn), lambda i, j: (0, i, j))
    return _call(body, name, (T // tm, N // tn),
                 [pl.BlockSpec((tm, D), lambda i, j: (i, 0)), pl.BlockSpec((tn, D), lambda i, j: (j, 0)), ab_spec],
                 [ab_spec, pl.BlockSpec((tm, tn), lambda i, j: (i, j))], [_sds((2, T, N), BF16), _sds((T, N), BF16)],
                 [pltpu.VMEM((tm, tn), F32)], sem=("parallel", "parallel"))(dxb, wd, ab)


def _ffn_fwd(tag, x, g, wgu, wd):
    h = _rms_fwd(f"ffn_rms_{tag}", x, g)
    ab, u = _glu_mm(f"ffn_gu_{tag}", h, wgu, None, "swiglu", BF16)
    return _mm_res(f"ffn_down_{tag}", u, wd, x, 0.5), (x, h, ab)


def _ffn_bwd(tag, dxp, saved, g, wgu, wd):
    x, h, ab = saved
    dx, dxb = dxp
    dab, u = _swiglu_bwd_mm(f"ffn_dgu_{tag}", dxb, wd, ab, 0.5)
    dwd = _mm_tn(f"ffn_dwd_{tag}", u, dxb, alpha=0.5)
    dwgu = _mm_tn3(f"ffn_dwgu_{tag}", h, dab)
    dh = _mm_nt3(f"ffn_dh_{tag}", dab, wgu)
    dx2, dg = _rms_bwd(f"ffn_drms_{tag}", dh, x, g, dx)
    return dx2, dg, dwgu, dwd


HALO = 32


def _conv_fwd(u3, dw32, dwb, lng, lnb):
    B, S, D = u3.shape
    W = dw32.shape[0]
    taps = CONV_TAPS
    tt = _tile(S, 256, HALO)
    hb = tt // HALO

    def body(u_ref, halo_ref, dw_ref, dwb_ref, g_ref, b_ref, v_ref, s_ref):
        i = pl.program_id(1)
        halo = jnp.where(i > 0, halo_ref[...], 0.0)
        ext = jnp.concatenate([halo, u_ref[...]], axis=0)
        acc = jnp.zeros((tt, D), F32) + dwb_ref[...]
        for j in range(taps):
            acc = acc + dw_ref[pl.ds(j, 1), :] * _roll(ext, taps - 1 - j)[HALO:]
        v_ref[...] = acc
        mu = jnp.mean(acc, axis=-1, keepdims=True)
        xc = acc - mu
        ln = xc * lax.rsqrt(jnp.mean(xc * xc, axis=-1, keepdims=True) + LN_EPS) * g_ref[...] + b_ref[...]
        s_ref[...] = (ln * _sig(ln)).astype(BF16)

    main = pl.BlockSpec((None, tt, D), lambda b, i: (b, i, 0))
    halo = pl.BlockSpec((None, HALO, D), lambda b, i: (b, jnp.maximum(i * hb - 1, 0), 0))
    vec = pl.BlockSpec((1, D), lambda b, i: (0, 0))
    return _call(body, "conv_fwd", (B, S // tt), [main, halo, pl.BlockSpec((W, D), lambda b, i: (0, 0)), vec, vec, vec],
                 [main, main], [_sds((B, S, D), F32), _sds((B, S, D), BF16)], sem=("parallel", "parallel"))(
        u3, u3, dw32, dwb, lng, lnb)


def _conv_bwd_ln(v, ds, lng, lnb):
    T, D = v.shape
    tt = _tile(T, 256)

    def body(v_ref, ds_ref, g_ref, b_ref, dv_ref, red_ref):
        vv = v_ref[...]
        mu = jnp.mean(vv, axis=-1, keepdims=True)
        xc = vv - mu
        rstd = lax.rsqrt(jnp.mean(xc * xc, axis=-1, keepdims=True) + LN_EPS)
        xh = xc * rstd
        ln = xh * g_ref[...] + b_ref[...]
        sg = _sig(ln)
        dln = ds_ref[...] * (sg * (1.0 + ln * (1.0 - sg)))
        dxh = dln * g_ref[...]
        dv = rstd * (dxh - jnp.mean(dxh, axis=-1, keepdims=True) - xh * jnp.mean(dxh * xh, axis=-1, keepdims=True))
        dv_ref[...] = dv
        parts = (jnp.sum(dln * xh, axis=0, keepdims=True), jnp.sum(dln, axis=0, keepdims=True),
                 jnp.sum(dv, axis=0, keepdims=True))

        @pl.when(pl.program_id(0) == 0)
        def _():
            for k in range(3):
                red_ref[k] = parts[k]

        @pl.when(pl.program_id(0) > 0)
        def _():
            for k in range(3):
                red_ref[k] += parts[k]

    row = pl.BlockSpec((tt, D), lambda i: (i, 0))
    vec = pl.BlockSpec((1, D), lambda i: (0, 0))
    return _call(body, "conv_bwd_ln", (T // tt,), [row, row, vec, vec], [row, pl.BlockSpec((3, 1, D), lambda i: (0, 0, 0))],
                 [_sds((T, D), F32), _sds((3, 1, D), F32)], sem=("arbitrary",))(v, ds, lng, lnb)


def _conv_bwd_dw(dv3, u3, ab, dw32):
    B, S, D = u3.shape
    W = dw32.shape[0]
    taps = CONV_TAPS
    tt = _tile(S, 256, HALO)
    hb = tt // HALO
    nt = S // tt
    L = tt + HALO

    def body(dv_ref, dvn_ref, u_ref, up_ref, ab_ref, dw_ref, dab_ref, ddw_ref):
        b, i = pl.program_id(0), pl.program_id(1)
        dv = dv_ref[...]
        ext_dv = jnp.concatenate([dv, jnp.where(i < nt - 1, dvn_ref[...], 0.0)], axis=0)
        ext_u = jnp.concatenate([jnp.where(i > 0, up_ref[...], 0.0), u_ref[...]], axis=0)
        du = jnp.zeros((tt, D), F32)
        first = jnp.logical_and(b == 0, i == 0)

        @pl.when(first)
        def _():
            ddw_ref[...] = jnp.zeros((W, D), F32)

        for j in range(taps):
            sh = taps - 1 - j
            du = du + dw_ref[pl.ds(j, 1), :] * _roll(ext_dv, L - sh)[:tt]
            ddw_ref[pl.ds(j, 1), :] += jnp.sum(dv * _roll(ext_u, sh)[HALO:], axis=0, keepdims=True)
        a = ab_ref[0].astype(F32)
        sb = _sig(ab_ref[1].astype(F32))
        dab_ref[0] = (du * sb).astype(BF16)
        dab_ref[1] = (du * a * sb * (1.0 - sb)).astype(BF16)

    main = pl.BlockSpec((None, tt, D), lambda b, i: (b, i, 0))
    prev = pl.BlockSpec((None, HALO, D), lambda b, i: (b, jnp.maximum(i * hb - 1, 0), 0))
    nxt = pl.BlockSpec((None, HALO, D), lambda b, i: (b, jnp.minimum((i + 1) * hb, S // HALO - 1), 0))
    abs_ = pl.BlockSpec((2, tt, D), lambda b, i: (0, b * nt + i, 0))
    wsp = pl.BlockSpec((W, D), lambda b, i: (0, 0))
    return _call(body, "conv_bwd_dw", (B, nt), [main, nxt, main, prev, abs_, wsp], [abs_, wsp],
                 [_sds((2, B * S, D), BF16), _sds((W, D), F32)], sem=("arbitrary", "arbitrary"))(dv3, dv3, u3, u3, ab, dw32)


CONV_TAPS = 31


def _conv_mixer_fwd(x, B, p):
    T, D = x.shape
    h = _rms_fwd("conv_rms", x, p["mix_norm"][0:1])
    ab, u = _glu_mm("conv_in", h, p["conv_w_in2"], p["conv_b_in2"], "glu", F32)
    v, s = _conv_fwd(u.reshape(B, T // B, D), p["conv_dw32"], p["conv_dw_b"], p["conv_ln_g"], p["conv_ln_b"])
    v, s = v.reshape(T, D), s.reshape(T, D)
    return _mm_res("conv_out", s, p["conv_w_out"], x, 1.0), (x, h, ab, u, v, s)


def _conv_mixer_bwd(dxp, saved, B, p):
    x, h, ab, u, v, s = saved
    T, D = x.shape
    dx, dxb = dxp
    g = {}
    ds = _mm_nt("conv_ds", [(dxb, p["conv_w_out"])], F32)
    g["conv_w_out"] = _mm_tn("conv_dwout", s, dxb)
    dv, red = _conv_bwd_ln(v, ds, p["conv_ln_g"], p["conv_ln_b"])
    g["conv_ln_g"], g["conv_ln_b"], g["conv_dw_b"] = red[0], red[1], red[2]
    dab, ddw = _conv_bwd_dw(dv.reshape(B, T // B, D), u.reshape(B, T // B, D), ab, p["conv_dw32"])
    g["conv_dw"] = ddw[:CONV_TAPS][None]
    g["conv_b_in"] = _colsum3("conv_dbin", dab).reshape(1, 2 * D)
    dwin = _mm_tn3("conv_dwin", h, dab)
    g["conv_w_in"] = jnp.moveaxis(dwin, 0, 1).reshape(1, D, 2 * D)
    dh = _mm_nt3("conv_dh", dab, p["conv_w_in2"])
    dx2, dg = _rms_bwd("conv_drms", dh, x, p["mix_norm"][0:1], dx)
    return dx2, dg, g


def _log_sigmoid(z):
    return jnp.minimum(z, 0.0) - jnp.log(1.0 + jnp.exp(-jnp.abs(z)))


def _fox_gate_fwd(fl3, bf):
    B, S, N = fl3.shape
    tt = _tile(S, 512)

    def body(fl_ref, bf_ref, c_ref, carry):
        @pl.when(pl.program_id(1) == 0)
        def _():
            carry[...] = jnp.zeros((1, N), F32)

        c = _scan_rows(_log_sigmoid(fl_ref[...] + bf_ref[...])) + carry[...]
        c_ref[...] = c
        carry[...] = c_ref[pl.ds(tt - 1, 1), :]

    row = pl.BlockSpec((None, tt, N), lambda b, i: (b, i, 0))
    return _call(body, "fox_gate_fwd", (B, S // tt), [row, pl.BlockSpec((1, N), lambda b, i: (0, 0))], row,
                 _sds((B, S, N), F32), [pltpu.VMEM((1, N), F32)], sem=("parallel", "arbitrary"))(fl3, bf)


def _fox_gate_bwd(dc3, fl3, bf):
    B, S, N = fl3.shape
    tt = _tile(S, 512)
    nt = S // tt

    def body(dc_ref, fl_ref, bf_ref, dfl_ref, dbf_ref, carry):
        b, i = pl.program_id(0), pl.program_id(1)

        @pl.when(i == 0)
        def _():
            carry[...] = jnp.zeros((1, N), F32)

        dc = dc_ref[0] - dc_ref[1]
        dlf = _scan_rows(dc, reverse=True) + carry[...]
        dfl = dlf * _sig(-(fl_ref[...] + bf_ref[...]))
        dfl_ref[...] = dfl
        carry[...] += jnp.sum(dc, axis=0, keepdims=True)
        part = jnp.sum(dfl, axis=0, keepdims=True)

        @pl.when(jnp.logical_and(b == 0, i == 0))
        def _():
            dbf_ref[...] = part

        @pl.when(jnp.logical_or(b > 0, i > 0))
        def _():
            dbf_ref[...] += part

    row = pl.BlockSpec((None, tt, N), lambda b, i: (b, nt - 1 - i, 0))
    vec = pl.BlockSpec((1, N), lambda b, i: (0, 0))
    row2 = pl.BlockSpec((2, None, tt, N), lambda b, i: (0, b, nt - 1 - i, 0))
    return _call(body, "fox_gate_bwd", (B, nt), [row2, row, vec], [row, vec], [_sds((B, S, N), F32), _sds((1, N), F32)],
                 [pltpu.VMEM((1, N), F32)], sem=("arbitrary", "arbitrary"))(dc3, fl3, bf)


NEG = -1e30
ROW_CHUNK = 32
CHUNK_UNROLL = 4


def _fox_attn_fwd(qa, ka, vat, crow, ckb, dh, scale):
    B, H, S, P = qa.shape
    tq = _tile(S, 512, LANES)
    tk = tq
    nl = tq // LANES

    def body(q_ref, k_ref, vt_ref, cr_ref, ck_ref, o_ref, lse_ref, s_scr, p_scr, m_scr, al_scr, acc_scr):
        i = pl.program_id(2)
        qv = q_ref[...]
        m_scr[...] = jnp.full((1, tq), NEG, F32)
        acc_scr[...] = jnp.zeros((P, tq), F32)
        key = lax.broadcasted_iota(jnp.int32, (tk, LANES), 0)
        qry = lax.broadcasted_iota(jnp.int32, (tk, LANES), 1)

        def kv_step(j, diagonal):
            off = pl.multiple_of(j * tk, tk)
            s_scr[...] = _dot(k_ref[pl.ds(off, tk), :], qv, "nt")
            ck = ck_ref[pl.ds(off, tk), :]
            for lt in range(nl):
                ls = slice(lt * LANES, (lt + 1) * LANES)
                s = s_scr[:, ls] * scale + cr_ref[:, ls] - ck
                if diagonal:
                    s = jnp.where(key <= qry + lt * LANES, s, -jnp.inf)
                m1 = m_scr[:, ls]
                m2 = jnp.maximum(m1, jnp.max(s, axis=0, keepdims=True))
                p_scr[:, ls] = jnp.exp(s - m2).astype(BF16)
                al_scr[:, ls] = jnp.exp(m1 - m2)
                m_scr[:, ls] = m2
            acc_scr[...] = al_scr[...] * acc_scr[...] + _dot(vt_ref[:, pl.ds(off, tk)], p_scr[...])

        def before(j, carry):
            kv_step(j, False)
            return carry

        lax.fori_loop(0, i, before, 0)
        kv_step(i, True)
        l = acc_scr[pl.ds(dh, 1), :]
        o_ref[...] = (acc_scr[...] / l).T
        lse_ref[...] = m_scr[...] + jnp.log(l)

    qs = pl.BlockSpec((None, None, tq, P), lambda b, h, i: (b, h, i, 0))
    fullk = pl.BlockSpec((None, None, S, P), lambda b, h, i: (b, h, 0, 0))
    fullt = pl.BlockSpec((None, None, P, S), lambda b, h, i: (b, h, 0, 0))
    fullc = pl.BlockSpec((None, None, S, LANES), lambda b, h, i: (b, h, 0, 0))
    rowt = pl.BlockSpec((None, None, 1, tq), lambda b, h, i: (b, h, 0, i))
    scratch = [pltpu.VMEM((tk, tq), F32), pltpu.VMEM((tk, tq), BF16), pltpu.VMEM((1, tq), F32), pltpu.VMEM((1, tq), F32),
               pltpu.VMEM((P, tq), F32)]
    return _call(body, "fox_attn_fwd", (B, H, S // tq), [qs, fullk, fullt, rowt, fullc], [qs, rowt],
                 [_sds((B, H, S, P), F32), _sds((B, H, 1, S), F32)], scratch, sem=("parallel", "parallel", "parallel"))(
        qa, ka, vat, crow, ckb)


def _fox_rowstats(do, o):
    B, H, S, P = o.shape
    tq = _tile(S, 4096)

    def body(do_ref, o_ref, dl_ref):
        dl_ref[...] = jnp.sum(do_ref[...] * o_ref[...], axis=-1, keepdims=True)

    qs = pl.BlockSpec((None, None, tq, P), lambda b, h, i: (b, h, i, 0))
    col = pl.BlockSpec((None, None, tq, 1), lambda b, h, i: (b, h, i, 0))
    return _call(body, "fox_rowstats", (B, H, S // tq), [qs, qs], col, _sds((B, H, S, 1), F32), sem=("parallel",) * 3)(do, o)


def _fox_attn_bwd(qa, ka, kat, va, do, crow, lse, delta, ckb, dh, scale):
    B, H, S, P = qa.shape
    tk = _tile(S, 512, LANES)
    tq = tk
    nq = S // tq
    nl = tq // LANES

    def body(q_ref, k_ref, kt_ref, v_ref, do_ref, cr_ref, lse_ref, dl_ref, ck_ref, dqt_ref, dk_ref, dv_ref,
             s_scr, dp_scr, p_scr, ds_scr, dk_acc, dv_acc):
        j = pl.program_id(2)

        @pl.when(j == 0)
        def _():
            dqt_ref[...] = jnp.zeros((P, S), F32)

        kj, vj, ck = k_ref[...], v_ref[...], ck_ref[...]
        dk_acc[...] = jnp.zeros((tk, P), F32)
        dv_acc[...] = jnp.zeros((tk, P), F32)
        key = lax.broadcasted_iota(jnp.int32, (tk, LANES), 0)
        qry = lax.broadcasted_iota(jnp.int32, (tk, LANES), 1)

        def q_step(i, diagonal):
            off = pl.multiple_of(i * tq, tq)
            qi, doi = q_ref[pl.ds(off, tq), :], do_ref[pl.ds(off, tq), :].astype(BF16)
            s_scr[...] = _dot(kj, qi, "nt")
            dp_scr[...] = _dot(vj, doi, "nt")
            for lt in range(nl):
                ls = slice(lt * LANES, (lt + 1) * LANES)
                gl = pl.ds(pl.multiple_of(off + lt * LANES, LANES), LANES)
                p = jnp.exp(s_scr[:, ls] * scale + (cr_ref[:, gl] - lse_ref[:, gl]) - ck)
                if diagonal:
                    p = jnp.where(key <= qry + lt * LANES, p, 0.0)
                p_scr[:, ls] = p.astype(BF16)
                ds_scr[:, ls] = (p * (dp_scr[:, ls] - dl_ref[:, gl])).astype(BF16)
            ds = ds_scr[...]
            dqt_ref[:, pl.ds(off, tq)] += _dot(kt_ref[...], ds)
            dk_acc[...] += _dot(ds, qi)
            dv_acc[...] += _dot(p_scr[...], doi)

        def after(i, carry):
            q_step(i, False)
            return carry

        q_step(j, True)
        lax.fori_loop(j + 1, nq, after, 0)
        dk_ref[...] = dk_acc[...] * jnp.where(lax.broadcasted_iota(jnp.int32, (tk, P), 1) < dh, scale, 1.0)
        dv_ref[...] = dv_acc[...]

        @pl.when(j == S // tk - 1)
        def _():
            dqt_ref[...] = dqt_ref[...] * jnp.where(lax.broadcasted_iota(jnp.int32, (P, S), 0) < dh, scale, 1.0)

    ks = pl.BlockSpec((None, None, tk, P), lambda b, h, j: (b, h, j, 0))
    kts = pl.BlockSpec((None, None, P, tk), lambda b, h, j: (b, h, 0, j))
    cks = pl.BlockSpec((None, None, tk, LANES), lambda b, h, j: (b, h, j, 0))
    full = pl.BlockSpec((None, None, S, P), lambda b, h, j: (b, h, 0, 0))
    fullt = pl.BlockSpec((None, None, P, S), lambda b, h, j: (b, h, 0, 0))
    rowf = pl.BlockSpec((None, None, 1, S), lambda b, h, j: (b, h, 0, 0))
    scratch = [pltpu.VMEM((tk, tq), F32), pltpu.VMEM((tk, tq), F32), pltpu.VMEM((tk, tq), BF16), pltpu.VMEM((tk, tq), BF16),
               pltpu.VMEM((tk, P), F32), pltpu.VMEM((tk, P), F32)]
    return _call(body, "fox_attn_bwd", (B, H, S // tk), [full, ks, kts, ks, full, rowf, rowf, rowf, cks], [fullt, ks, ks],
                 [_sds((B, H, P, S), F32), _sds((B, H, S, P), F32), _sds((B, H, S, P), F32)], scratch,
                 sem=("parallel", "parallel", "arbitrary"))(qa, ka, kat, va, do, crow, lse, delta, ckb)


def _heads(t, B, H):
    T, D = t.shape
    return t.reshape(B, T // B, H, D // H).transpose(0, 2, 1, 3)


def _unheads(t):
    B, H, S, dh = t.shape
    return t.transpose(0, 2, 1, 3).reshape(B * S, H * dh)


def _fox_mixer_fwd(x, B, p):
    T, D = x.shape
    H = FOX_HEADS
    S = T // B
    scale = (D // H) ** -0.5
    h = _rms_fwd("fox_rms", x, p["mix_norm"][1:2])
    qkv = _mm_nn("fox_qkv", h, p["fox_w_qkv"], BF16)
    fl = _mm_nn("fox_fl", h, p["fox_w_f"], F32)
    c = _fox_gate_fwd(fl.reshape(B, S, LANES), p["fox_b_f128"])
    ch = c[:, :, :H].transpose(0, 2, 1)
    crow = ch[:, :, None, :]
    ckb = jnp.broadcast_to(ch[..., None], (B, H, S, LANES))
    q, k, v = _heads(qkv[:, :D], B, H), _heads(qkv[:, D:2 * D], B, H), _heads(qkv[:, 2 * D:], B, H)
    dh = D // H
    P = -(-(dh + 2) // LANES) * LANES
    one, zero = jnp.ones((B, H, S, 1), q.dtype), jnp.zeros((B, H, S, 1), q.dtype)
    rest = jnp.zeros((B, H, S, P - dh - 2), q.dtype)
    qa = jnp.concatenate([q, zero, one, rest], axis=-1)
    ka = jnp.concatenate([k, one, zero, rest], axis=-1)
    va = jnp.concatenate([v, one, zero, rest], axis=-1)
    o, lse = _fox_attn_fwd(qa, ka, va.transpose(0, 1, 3, 2), crow, ckb, dh, scale)
    of = _unheads(o[..., :dh])
    return _mm_res("fox_out", of, p["fox_w_out"], x, 1.0), (x, h, fl, qa, ka, va, crow, ckb, o, lse, of)


def _fox_mixer_bwd(dxp, saved, B, p):
    x, h, fl, qa, ka, va, crow, ckb, o, lse, of = saved
    T, D = x.shape
    H = FOX_HEADS
    S = T // B
    dh_ = D // H
    P = qa.shape[-1]
    scale = dh_ ** -0.5
    dx, dxb = dxp
    g = {}
    do = _heads(_mm_nt("fox_do", [(dxb, p["fox_w_out"])], F32), B, H)
    do = jnp.pad(do, ((0, 0), (0, 0), (0, 0), (0, P - dh_)))
    g["fox_w_out"] = _mm_tn("fox_dwout", of, dxb)
    delta = _fox_rowstats(do, o).reshape(B, H, 1, S)
    dqt, dk, dv = _fox_attn_bwd(qa, ka, ka.transpose(0, 1, 3, 2), va, do, crow, lse, delta, ckb, dh_, scale)
    dq = dqt[:, :, :dh_, :].transpose(0, 3, 1, 2).reshape(T, D)
    dqkv = jnp.concatenate([dq, _unheads(dk[..., :dh_]), _unheads(dv[..., :dh_])], axis=1).astype(BF16)
    dc = jnp.stack([dqt[:, :, dh_, :], dk[..., dh_ + 1]])
    dc = jnp.pad(dc.transpose(0, 1, 3, 2), ((0, 0), (0, 0), (0, 0), (0, LANES - H)))
    dfl, dbf = _fox_gate_bwd(dc, fl.reshape(B, S, LANES), p["fox_b_f128"])
    dfl = dfl.reshape(T, LANES)
    g["fox_b_f"] = dbf[:, :H]
    dwqkv = _mm_tn("fox_dwqkv", h, dqkv)
    dwf = _mm_tn("fox_dwf", h, dfl)
    g["fox_w_in"] = jnp.concatenate([dwqkv, dwf[:, :H]], axis=1)[None]
    dh = _mm_nt("fox_dh", [(dqkv, p["fox_w_qkv"]), (dfl, p["fox_w_f"])], F32)
    dx2, dg = _rms_bwd("fox_drms", dh, x, p["mix_norm"][1:2], dx)
    return dx2, dg, g


def _lb_fwd(logits):
    L, D = logits.shape

    def body(l_ref, lb_ref):
        z = l_ref[...]
        e = jnp.exp(z - jnp.max(z, axis=0, keepdims=True))
        p = e / jnp.sum(e, axis=0, keepdims=True)
        lb_ref[...] = jnp.sum(jnp.where(_lb_rows(z.shape), p, 0.0), axis=0, keepdims=True)

    return _call(body, "hgrn_lb", (1,), [pl.BlockSpec((L, D), lambda i: (0, 0))], pl.BlockSpec((1, D), lambda i: (0, 0)),
                 _sds((1, D), F32))(logits)


def _lb_rows(shape):
    r = lax.broadcasted_iota(jnp.int32, shape, 0)
    return jnp.logical_and(r >= 1, r <= HGRN_LAYER)


HGRN_LAYER = 2


def _lb_bwd(logits, dlb):
    L, D = logits.shape

    def body(l_ref, d_ref, o_ref):
        z = l_ref[...]
        e = jnp.exp(z - jnp.max(z, axis=0, keepdims=True))
        p = e / jnp.sum(e, axis=0, keepdims=True)
        dp = jnp.where(_lb_rows(z.shape), d_ref[...], 0.0)
        o_ref[...] = p * (dp - jnp.sum(p * dp, axis=0, keepdims=True))

    full = pl.BlockSpec((L, D), lambda i: (0, 0))
    return _call(body, "hgrn_dlb", (1,), [full, pl.BlockSpec((1, D), lambda i: (0, 0))], full, _sds((L, D), F32))(logits, dlb)


def _hgrn_gates(qr, fr, lbv):
    e = jnp.exp(-jnp.abs(fr))
    big, small = 1.0 / (1.0 + e), e / (1.0 + e)
    sf = jnp.where(fr >= 0, big, small)
    snf = jnp.where(fr >= 0, small, big)
    f = lbv + (1.0 - lbv) * sf
    sq = _sig(qr)
    return qr * sq, (1.0 - lbv) * snf, jnp.log(f), sf, snf, f, sq


def _hgrn_intra(G, q, kk, g_scr, q_scr):
    C = HGRN_CHUNK
    g_scr[...] = G
    q_scr[...] = q
    srow = lax.broadcasted_iota(jnp.int32, (C, LANES), 0)
    lane = lax.broadcasted_iota(jnp.int32, (C, LANES), 1)
    at = jnp.zeros((C, LANES), F32)
    for t in range(C):
        e = jnp.where(srow <= t, jnp.exp(g_scr[pl.ds(t, 1), :] - G), 0.0)
        col = jnp.sum(e * kk * q_scr[pl.ds(t, 1), :], axis=-1, keepdims=True)
        at = jnp.where(lane == t, col, at)
    return at


def _hgrn_fwd(proj3, lb, ng):
    B, S, D4 = proj3.shape
    D = D4 // 4
    H = D // HGRN_EXPAND
    C = HGRN_CHUNK
    R = _tile(S, 256, C)
    ncb = R // C
    dk = HGRN_EXPAND

    def body(q_ref, f_ref, i_ref, go_ref, lb_ref, ng_ref, y_ref, o_ref, st_ref, st, g_scr, q_scr):
        @pl.when(pl.program_id(2) == 0)
        def _():
            st[...] = jnp.zeros((dk, dk), F32)

        lbv = lb_ref[...]

        def chunk(c, carry):
            r0 = pl.multiple_of(c * C, C)
            rows = pl.ds(r0, C)
            q, kk, lf, *_ = _hgrn_gates(q_ref[rows, :], f_ref[rows, :], lbv)
            vv = i_ref[rows, :]
            G = _scan_rows(lf)
            at = _hgrn_intra(G, q, kk, g_scr, q_scr)
            gl = g_scr[pl.ds(C - 1, 1), :]
            stv = st[...]
            st_ref[c] = stv
            o = _dot(q * jnp.exp(G), stv, "nt") + _dot(at, vv, "tn")[:C]
            st[...] = stv * jnp.exp(gl) + _dot(vv, kk * jnp.exp(gl - G), "tn")
            o_ref[rows, :] = o
            gv = go_ref[rows, :]
            y = o * lax.rsqrt(jnp.mean(o * o, axis=-1, keepdims=True) + RMS_EPS) * ng_ref[...] * (gv * _sig(gv))
            y_ref[rows, :] = y.astype(BF16)
            return carry

        lax.fori_loop(0, ncb, chunk, 0)

    def col(k):
        return pl.BlockSpec((None, R, dk), lambda b, h, i: (b, i, h + k * H))

    vec = pl.BlockSpec((1, dk), lambda b, h, i: (0, h))
    out = pl.BlockSpec((None, R, dk), lambda b, h, i: (b, i, h))
    return _call(body, "hgrn_fwd", (B, H, S // R), [col(0), col(1), col(2), col(3), vec, vec],
                 [out, out, pl.BlockSpec((None, None, ncb, dk, dk), lambda b, h, i: (b, h, i, 0, 0))],
                 [_sds((B, S, D), BF16), _sds((B, S, D), F32), _sds((B, H, S // C, dk, dk), F32)],
                 [pltpu.VMEM((dk, dk), F32), pltpu.VMEM((C, dk), F32), pltpu.VMEM((C, dk), F32)],
                 sem=("parallel", "parallel", "arbitrary"))(proj3, proj3, proj3, proj3, lb, ng)


def _hgrn_bwd(proj3, o3, dy3, states, lb, ng):
    B, S, D4 = proj3.shape
    D = D4 // 4
    H = D // HGRN_EXPAND
    C = HGRN_CHUNK
    R = _tile(S, 256, C)
    ncb = R // C
    nb = S // R
    dk = HGRN_EXPAND

    def body(q_ref, f_ref, i_ref, go_ref, o_ref, dy_ref, st_ref, lb_ref, ng_ref,
             dp_ref, red_ref, dst, g_scr, q_scr, dq_scr, acc):
        b, i = pl.program_id(1), pl.program_id(2)

        @pl.when(i == 0)
        def _():
            dst[...] = jnp.zeros((dk, dk), F32)

        @pl.when(jnp.logical_and(b == 0, i == 0))
        def _():
            acc[...] = jnp.zeros((2, dk), F32)

        lbv = lb_ref[...]
        ngv = ng_ref[...]
        srow = lax.broadcasted_iota(jnp.int32, (C, LANES), 0)
        lane = lax.broadcasted_iota(jnp.int32, (C, LANES), 1)

        def chunk(cc, carry):
            c = ncb - 1 - cc
            r0 = pl.multiple_of(c * C, C)
            rows = pl.ds(r0, C)
            qr, fr, vv, gv = q_ref[rows, :], f_ref[rows, :], i_ref[rows, :], go_ref[rows, :]
            q, kk, lf, sf, snf, f, sq = _hgrn_gates(qr, fr, lbv)
            o = o_ref[rows, :]
            dy = dy_ref[rows, :]
            rinv = lax.rsqrt(jnp.mean(o * o, axis=-1, keepdims=True) + RMS_EPS)
            on = o * rinv
            sgv = _sig(gv)
            dz = dy * (gv * sgv)
            dp_ref[3, rows, :] = (dy * on * ngv * (sgv * (1.0 + gv * (1.0 - sgv)))).astype(BF16)
            acc[pl.ds(1, 1), :] += jnp.sum(dz * on, axis=0, keepdims=True)
            don = dz * ngv
            do = rinv * (don - on * jnp.mean(don * on, axis=-1, keepdims=True))
            G = _scan_rows(lf)
            g_scr[...] = G
            q_scr[...] = q
            gl = g_scr[pl.ds(C - 1, 1), :]
            egl = jnp.exp(gl)
            eG = jnp.exp(G)
            eK = jnp.exp(gl - G)
            qg, kg = q * eG, kk * eK
            stv = st_ref[c]
            dsv = dst[...]
            dqg = _dot(do, stv)
            do_pad = jnp.concatenate([do, jnp.zeros((LANES - C, dk), F32)], axis=0)
            dat = _dot(vv, do_pad, "nt")
            dkg = _dot(vv, dsv)
            dgl = egl * jnp.sum(stv * dsv, axis=0, keepdims=True) + jnp.sum(dkg * kg, axis=0, keepdims=True)
            at = jnp.zeros((C, LANES), F32)
            dki = jnp.zeros((C, dk), F32)
            for t in range(C):
                e = jnp.where(srow <= t, jnp.exp(g_scr[pl.ds(t, 1), :] - G), 0.0)
                qt = q_scr[pl.ds(t, 1), :]
                at = jnp.where(lane == t, jnp.sum(e * kk * qt, axis=-1, keepdims=True), at)
                z = e * jnp.sum(jnp.where(lane == t, dat, 0.0), axis=-1, keepdims=True)
                dq_scr[pl.ds(t, 1), :] = jnp.sum(z * kk, axis=0, keepdims=True)
                dki = dki + z * qt
            dqi = dq_scr[...]
            dp_ref[2, rows, :] = (_dot(at, do_pad) + _dot(kg, dsv, "nt")).astype(BF16)
            dst[...] = dsv * egl + _dot(do, qg, "tn")
            dq = dqg * eG + dqi
            dkk = dkg * eK + dki
            dG = dqg * qg - dkg * kg + q * dqi - kk * dki
            dG = dG + jnp.where(srow == C - 1, dgl, 0.0)
            dlf = _scan_rows(dG, reverse=True)
            dsf = (1.0 - lbv) * sf * snf
            dp_ref[1, rows, :] = (dlf * dsf / f - dkk * dsf).astype(BF16)
            acc[pl.ds(0, 1), :] += jnp.sum(dlf * snf / f - dkk * snf, axis=0, keepdims=True)
            dp_ref[0, rows, :] = (dq * (sq * (1.0 + qr * (1.0 - sq)))).astype(BF16)
            return carry

        lax.fori_loop(0, ncb, chunk, 0)

        @pl.when(jnp.logical_and(b == B - 1, i == nb - 1))
        def _():
            red_ref[0] = acc[pl.ds(0, 1), :]
            red_ref[1] = acc[pl.ds(1, 1), :]

    def col(k):
        return pl.BlockSpec((None, R, dk), lambda h, b, i: (b, nb - 1 - i, h + k * H))

    vec = pl.BlockSpec((1, dk), lambda h, b, i: (0, h))
    row = pl.BlockSpec((None, R, dk), lambda h, b, i: (b, nb - 1 - i, h))
    stsp = pl.BlockSpec((None, None, ncb, dk, dk), lambda h, b, i: (b, h, nb - 1 - i, 0, 0))
    outs = _call(body, "hgrn_bwd", (H, B, nb), [col(0), col(1), col(2), col(3), row, row, stsp, vec, vec],
                 [pl.BlockSpec((4, None, R, dk), lambda h, b, i: (0, b, nb - 1 - i, h)),
                  pl.BlockSpec((2, 1, dk), lambda h, b, i: (0, 0, h))],
                 [_sds((4, B, S, D), BF16), _sds((2, 1, D), F32)],
                 [pltpu.VMEM((dk, dk), F32), pltpu.VMEM((C, dk), F32), pltpu.VMEM((C, dk), F32), pltpu.VMEM((C, dk), F32),
                  pltpu.VMEM((2, dk), F32)],
                 sem=("parallel", "arbitrary", "arbitrary"))(proj3, proj3, proj3, proj3, o3, dy3, states, lb, ng)
    return outs


def _hgrn_mixer_fwd(x, B, p):
    T, D = x.shape
    S = T // B
    h = _rms_fwd("hgrn_rms", x, p["mix_norm"][2:3])
    proj = _mm_nn("hgrn_in", h, p["hgrn_w_in"], F32)
    lb = _lb_fwd(p["hgrn_lb_logits"])
    y, o, states = _hgrn_fwd(proj.reshape(B, S, 4 * D), lb, p["hgrn_norm"])
    y = y.reshape(T, D)
    return _mm_res("hgrn_out", y, p["hgrn_w_out"], x, 1.0), (x, h, proj, lb, y, o, states)


def _hgrn_mixer_bwd(dxp, saved, B, p):
    x, h, proj, lb, y, o, states = saved
    T, D = x.shape
    S = T // B
    dx, dxb = dxp
    g = {}
    dy = _mm_nt("hgrn_dy", [(dxb, p["hgrn_w_out"])], F32)
    g["hgrn_w_out"] = _mm_tn("hgrn_dwout", y, dxb)
    dp, red = _hgrn_bwd(proj.reshape(B, S, 4 * D), o, dy.reshape(B, S, D), states, lb, p["hgrn_norm"])
    dp = dp.reshape(4, T, D)
    g["hgrn_norm"] = red[1]
    g["hgrn_lb_logits"] = _lb_bwd(p["hgrn_lb_logits"], red[0])
    dwin = _mm_tn3("hgrn_dwin", h, dp)
    g["hgrn_w_in"] = jnp.moveaxis(dwin, 0, 1).reshape(1, D, 4 * D)
    dh = _mm_nt3("hgrn_dh", dp, p["hgrn_w_in4"])
    dx2, dg = _rms_bwd("hgrn_drms", dh, x, p["mix_norm"][2:3], dx)
    return dx2, dg, g


POOL_HALO = 16


def _pool_fwd(x3, g):
    B, S, D = x3.shape
    tt = _tile(S, 256, POOL_HALO)
    hb = tt // POOL_HALO
    G = D // len(POOL_WINDOWS)

    def body(x_ref, halo_ref, g_ref, m_ref):
        i = pl.program_id(1)

        def norm(xv):
            return xv * lax.rsqrt(jnp.mean(xv * xv, axis=-1, keepdims=True) + RMS_EPS) * g_ref[...]

        hm = norm(x_ref[...])
        ext = jnp.concatenate([jnp.where(i > 0, norm(halo_ref[...]), 0.0), hm], axis=0)
        pos = (i * tt + lax.broadcasted_iota(jnp.int32, (tt, 1), 0) + 1).astype(F32)
        for gi, win in enumerate(POOL_WINDOWS):
            s = ext[:, gi * G:(gi + 1) * G]
            w = 1
            while w < win:
                s = s + _roll(s, w)
                w *= 2
            m_ref[:, gi * G:(gi + 1) * G] = (s[POOL_HALO:] / jnp.minimum(pos, float(win)) - hm[:, gi * G:(gi + 1) * G]).astype(BF16)

    main = pl.BlockSpec((None, tt, D), lambda b, i: (b, i, 0))
    halo = pl.BlockSpec((None, POOL_HALO, D), lambda b, i: (b, jnp.maximum(i * hb - 1, 0), 0))
    return _call(body, "pool_fwd", (B, S // tt), [main, halo, pl.BlockSpec((1, D), lambda b, i: (0, 0))], main,
                 _sds((B, S, D), BF16), sem=("parallel", "parallel"))(x3, x3, g)


def _pool_bwd(dm3):
    B, S, D = dm3.shape
    tt = _tile(S, 256, POOL_HALO)
    hb = tt // POOL_HALO
    nt = S // tt
    G = D // len(POOL_WINDOWS)
    L = tt + POOL_HALO

    def body(dm_ref, nxt_ref, dh_ref):
        i = pl.program_id(1)
        posm = (i * tt + lax.broadcasted_iota(jnp.int32, (tt, 1), 0) + 1).astype(F32)
        posn = ((i + 1) * tt + lax.broadcasted_iota(jnp.int32, (POOL_HALO, 1), 0) + 1).astype(F32)
        for gi, win in enumerate(POOL_WINDOWS):
            sl = slice(gi * G, (gi + 1) * G)
            dm = dm_ref[:, sl]
            s = jnp.concatenate([dm / jnp.minimum(posm, float(win)),
                                 jnp.where(i < nt - 1, nxt_ref[:, sl] / jnp.minimum(posn, float(win)), 0.0)], axis=0)
            w = 1
            while w < win:
                s = s + _roll(s, L - w)
                w *= 2
            dh_ref[:, sl] = s[:tt] - dm

    main = pl.BlockSpec((None, tt, D), lambda b, i: (b, i, 0))
    nxt = pl.BlockSpec((None, POOL_HALO, D), lambda b, i: (b, jnp.minimum((i + 1) * hb, S // POOL_HALO - 1), 0))
    return _call(body, "pool_bwd", (B, nt), [main, nxt], main, _sds((B, S, D), F32), sem=("parallel", "parallel"))(dm3, dm3)


def _pool_mixer_fwd(x, B, p):
    T, D = x.shape
    NG = len(POOL_WINDOWS)
    G = D // NG
    tm = _tile(T, 512)
    m = _pool_fwd(x.reshape(B, T // B, D), p["mix_norm"][3:4]).reshape(T, D)

    def epi(res, ex, outs):
        outs[0][...] = ex[1][...] + res * ex[0][...]

    blk = pl.BlockSpec((tm, G), lambda i, g: (i, g))
    x2 = _mm("pool_out", "nn", [(m, p["pool_w4"])], [(blk, pl.BlockSpec((None, G, G), lambda i, g: (g, 0, 0)))],
             (T // tm, NG), None, [_sds((T, D), F32)], [blk], extras=[p["pool_scale"], x],
             extra_specs=[pl.BlockSpec((1, G), lambda i, g: (0, g)), blk], epilogue=epi)[0]
    return x2, (x, m)


def _pool_mixer_bwd(dxp, saved, B, p):
    x, m = saved
    dx, dxb = dxp
    T, D = x.shape
    NG = len(POOL_WINDOWS)
    G = D // NG
    tm = _tile(T, 512)
    g = {}

    def epi(zz, ex, outs):
        dy = ex[0][...]
        outs[0][...] = (dy * ex[1][...]).astype(BF16)
        part = jnp.sum(dy * zz, axis=0, keepdims=True)

        @pl.when(pl.program_id(1) == 0)
        def _():
            outs[1][...] = part

        @pl.when(pl.program_id(1) > 0)
        def _():
            outs[1][...] += part

    blk = pl.BlockSpec((tm, G), lambda g_, i: (i, g_))
    wsp = pl.BlockSpec((None, G, G), lambda g_, i: (g_, 0, 0))
    vec = pl.BlockSpec((1, G), lambda g_, i: (0, g_))
    dz, dsc = _mm("pool_dz", "nn", [(m, p["pool_w4"])], [(blk, wsp)], (NG, T // tm), None,
                  [_sds((T, D), BF16), _sds((1, D), F32)], [blk, vec], extras=[dx, p["pool_scale"]], extra_specs=[blk, vec],
                  epilogue=epi, sem=("parallel", "arbitrary"))
    g["pool_scale"] = dsc
    tk = _tile(T, 512)
    kb = pl.BlockSpec((tk, G), lambda g_, k: (k, g_))
    g["pool_w"] = _mm("pool_dw", "tn", [(m, dz)], [(kb, kb)], (NG, T // tk), 1, [_sds((NG, G, G), F32)],
                      [pl.BlockSpec((None, G, G), lambda g_, k: (g_, 0, 0))], acc_shape=(G, G))[0][None]
    blk2 = pl.BlockSpec((tm, G), lambda i, g_: (i, g_))
    dm = _mm("pool_dm", "nt", [(dz, p["pool_w4"])], [(blk2, pl.BlockSpec((None, G, G), lambda i, g_: (g_, 0, 0)))],
             (T // tm, NG), None, [_sds((T, D), F32)], [blk2])[0]
    dh = _pool_bwd(dm.reshape(B, T // B, D)).reshape(T, D)
    dx2, dg = _rms_bwd("pool_drms", dh, x, p["mix_norm"][3:4], dx)
    return dx2, dg, g


_MIXERS = ((_conv_mixer_fwd, _conv_mixer_bwd), (_fox_mixer_fwd, _fox_mixer_bwd), (_hgrn_mixer_fwd, _hgrn_mixer_bwd),
           (_pool_mixer_fwd, _pool_mixer_bwd))


def _local_step(x3, tgt3, w):
    B, S, D = x3.shape
    T = B * S
    depth = w["ffn_norm"].shape[0]
    F = w["ffn_w_gate"].shape[-1]
    H = FOX_HEADS
    p = dict(w)
    p["wgu"] = jnp.stack((w["ffn_w_gate"], w["ffn_w_up"]), axis=2)
    p["conv_w_in2"] = w["conv_w_in"][0].reshape(D, 2, D).transpose(1, 0, 2)
    p["conv_b_in2"] = w["conv_b_in"].reshape(2, 1, D)
    p["conv_dw32"] = jnp.pad(w["conv_dw"][0], ((0, HALO - CONV_TAPS), (0, 0)))
    p["conv_w_out"] = w["conv_w_out"][0]
    p["fox_w_qkv"] = w["fox_w_in"][0][:, :3 * D]
    p["fox_w_f"] = jnp.pad(w["fox_w_in"][0][:, 3 * D:], ((0, 0), (0, LANES - H)))
    p["fox_b_f128"] = jnp.pad(w["fox_b_f"], ((0, 0), (0, LANES - H)))
    p["fox_w_out"] = w["fox_w_out"][0]
    p["hgrn_w_in"] = w["hgrn_w_in"][0]
    p["hgrn_w_in4"] = w["hgrn_w_in"][0].reshape(D, 4, D).transpose(1, 0, 2)
    p["hgrn_w_out"] = w["hgrn_w_out"][0]
    p["pool_w4"] = w["pool_w"][0]

    x = x3.reshape(T, D)
    saved = []
    for i in range(depth):
        x, s0 = _ffn_fwd(f"{i}a", x, w["ffn_norm"][i, 0:1], p["wgu"][i, 0], w["ffn_w_down"][i, 0])
        x, s1 = _MIXERS[i % 4][0](x, B, p)
        x, s2 = _ffn_fwd(f"{i}b", x, w["ffn_norm"][i, 1:2], p["wgu"][i, 1], w["ffn_w_down"][i, 1])
        saved.append((s0, s1, s2))
    loss, dx, dfinal = _loss_head(x, w["final_norm"].reshape(1, D), tgt3.reshape(T, D))

    g = {"final_norm": dfinal}
    dffn_norm = [[None, None] for _ in range(depth)]
    dwgu = [[None, None] for _ in range(depth)]
    dwd = [[None, None] for _ in range(depth)]
    dmix = [None] * depth
    for i in reversed(range(depth)):
        s0, s1, s2 = saved[i]
        dx, dffn_norm[i][1], dwgu[i][1], dwd[i][1] = _ffn_bwd(f"{i}b", dx, s2, w["ffn_norm"][i, 1:2], p["wgu"][i, 1],
                                                            w["ffn_w_down"][i, 1])
        dx, dmix[i], gm = _MIXERS[i % 4][1](dx, s1, B, p)
        g.update(gm)
        dx, dffn_norm[i][0], dwgu[i][0], dwd[i][0] = _ffn_bwd(f"{i}a", dx, s0, w["ffn_norm"][i, 0:1], p["wgu"][i, 0],
                                                            w["ffn_w_down"][i, 0])
    g["ffn_norm"] = jnp.stack([jnp.stack([a[0], b[0]]) for a, b in dffn_norm])
    g["ffn_w_gate"] = jnp.stack([jnp.stack([a[0], b[0]]) for a, b in dwgu])
    g["ffn_w_up"] = jnp.stack([jnp.stack([a[1], b[1]]) for a, b in dwgu])
    g["ffn_w_down"] = jnp.stack([jnp.stack([a, b]) for a, b in dwd])
    g["mix_norm"] = jnp.concatenate(dmix, axis=0)
    return loss, dx[0].reshape(B, S, D), {n: g[n].reshape(w[n].shape) for n in WEIGHTS}


def _adamw(name, w, m, v, parts):
    shape = w.shape
    cols = shape[-1]
    rows = w.size // cols
    tr = _tile(rows, max(8, (1 << 19) // cols))
    n = len(parts)
    c1 = 1.0 - ADAM_B1 ** ADAM_STEP
    c2 = 1.0 - ADAM_B2 ** ADAM_STEP

    def body(*refs):
        w_ref, m_ref, v_ref = refs[:3]
        g_ref, d_ref, m2_ref, v2_ref = refs[3 + n:]
        g = refs[3][...].astype(F32)
        for k in range(1, n):
            g = g + refs[3 + k][...].astype(F32)
        m2 = ADAM_B1 * m_ref[...] + (1.0 - ADAM_B1) * g
        v2 = ADAM_B2 * v_ref[...] + (1.0 - ADAM_B2) * (g * g)
        g_ref[...] = g
        m2_ref[...] = m2
        v2_ref[...] = v2
        d_ref[...] = -ADAM_LR * ((m2 / c1) / (jnp.sqrt(v2 / c2) + ADAM_EPS) + ADAM_WD * w_ref[...])

    blk = pl.BlockSpec((tr, cols), lambda i: (i, 0))
    outs = _call(body, name, (rows // tr,), [blk] * (3 + n), [blk] * 4, [_sds((rows, cols), F32)] * 4, sem=("parallel",))(
        *[t.reshape(rows, cols) for t in (w, m, v, *parts)])
    return [o.reshape(shape) for o in outs]


ANY = pl.BlockSpec(memory_space=pl.ANY)
FLAT_COLS = 1024


def _place():
    return lax.axis_index("x"), lax.axis_index("y"), lax.axis_index("c")


def _all_gather(name, xs):
    K = len(xs)

    def body(*refs):
        x_refs, out_refs = refs[:K], refs[K:2 * K]
        send_sems, recv_sems, local_sems = refs[2 * K:]
        xi, yi, ci = _place()
        me, sibling = (xi, yi, ci), (xi, yi, 1 - ci)
        chips = [(1 - xi, yi), (xi, 1 - yi), (1 - xi, 1 - yi)]

        def slot(a, px, py, pc):
            return out_refs[a].at[4 * px + 2 * py + pc]

        def copy(a, k, block, to, own=False):
            return pltpu.make_async_remote_copy(src_ref=x_refs[a] if own else slot(a, *block), dst_ref=slot(a, *block),
                                                send_sem=send_sems.at[7 * a + k], recv_sem=recv_sems.at[7 * a + k],
                                                device_id=to, device_id_type=MESH)

        mine = [pltpu.make_async_copy(x_refs[a], slot(a, *me), local_sems.at[a]) for a in range(K)]
        first = [copy(a, 1 + j, me, (*chip, ci), own=True) for j, chip in enumerate(chips) for a in range(K)]
        first += [copy(a, 0, me, sibling, own=True) for a in range(K)]
        for cp in mine + first:
            cp.start()
        passed = []
        for j, chip in enumerate(chips):
            for a in range(K):
                copy(a, 1 + j, (*chip, ci), me).wait_recv()
                passed.append(copy(a, 4 + j, (*chip, ci), sibling))
                passed[-1].start()
        for a in range(K):
            copy(a, 0, sibling, me).wait_recv()
            for j, chip in enumerate(chips):
                copy(a, 4 + j, (*chip, 1 - ci), me).wait_recv()
        for cp in first + passed:
            cp.wait_send()
        for cp in mine:
            cp.wait()

    return pl.pallas_call(body, name=name, out_shape=[_sds((N_DEV,) + x.shape, x.dtype) for x in xs], in_specs=[ANY] * K,
                          out_specs=[ANY] * K,
                          scratch_shapes=[pltpu.SemaphoreType.DMA((7 * K,)), pltpu.SemaphoreType.DMA((7 * K,)),
                                          pltpu.SemaphoreType.DMA((K,))])(*xs)


def _swap_sibling(name, ts):
    K = len(ts)

    def body(*refs):
        t_refs, out_refs, send_sems, recv_sems = refs[:K], refs[K:2 * K], refs[2 * K], refs[2 * K + 1]
        xi, yi, ci = _place()
        cps = [pltpu.make_async_remote_copy(src_ref=t_refs[a], dst_ref=out_refs[a], send_sem=send_sems.at[a],
                                            recv_sem=recv_sems.at[a], device_id=(xi, yi, 1 - ci), device_id_type=MESH)
               for a in range(K)]
        for cp in cps:
            cp.start()
        for cp in cps:
            cp.wait()

    return pl.pallas_call(body, name=name, out_shape=[_sds(t.shape, t.dtype) for t in ts], in_specs=[ANY] * K,
                          out_specs=[ANY] * K,
                          scratch_shapes=[pltpu.SemaphoreType.DMA((K,)), pltpu.SemaphoreType.DMA((K,))])(*ts)


def _scatter_chips(name, ts):
    K = len(ts)

    def body(*refs):
        t_refs, out_refs, send_sems, recv_sems = refs[:K], refs[K:2 * K], refs[2 * K], refs[2 * K + 1]
        xi, yi, ci = _place()
        chips = [(1 - xi, yi), (xi, 1 - yi), (1 - xi, 1 - yi)]
        cps = [pltpu.make_async_remote_copy(src_ref=t_refs[a].at[2 * cx + cy], dst_ref=out_refs[a].at[j],
                                            send_sem=send_sems.at[3 * a + j], recv_sem=recv_sems.at[3 * a + j],
                                            device_id=(cx, cy, ci), device_id_type=MESH)
               for j, (cx, cy) in enumerate(chips) for a in range(K)]
        for cp in cps:
            cp.start()
        for cp in cps:
            cp.wait()

    return pl.pallas_call(body, name=name, out_shape=[_sds((3,) + t.shape[1:], t.dtype) for t in ts], in_specs=[ANY] * K,
                          out_specs=[ANY] * K,
                          scratch_shapes=[pltpu.SemaphoreType.DMA((3 * K,)), pltpu.SemaphoreType.DMA((3 * K,))])(*ts)


def _add_bf16(name, a, b):
    shape = a.shape
    N, C = shape[0], shape[-1]
    R = a.size // (N * C)
    tr = _tile(R, max(8, (1 << 19) // C))

    def body(a_ref, b_ref, o_ref):
        o_ref[...] = (a_ref[...].astype(F32) + b_ref[...].astype(F32)).astype(BF16)

    blk = pl.BlockSpec((None, tr, C), lambda n, i: (n, i, 0))
    return _call(body, name, (N, R // tr), [blk, blk], blk, _sds((N, R, C), BF16), sem=("parallel", "parallel"))(
        a.reshape(N, R, C), b.reshape(N, R, C)).reshape(shape)


def _sum_parts(name, parts):
    N, C = parts.shape

    def body(p_ref, o_ref):
        s = p_ref[pl.ds(0, 1), :]
        for d in range(1, N):
            s = s + p_ref[pl.ds(d, 1), :]
        o_ref[...] = s

    return _call(body, name, (1,), [pl.BlockSpec((N, C), lambda i: (0, 0))], pl.BlockSpec((1, C), lambda i: (0, 0)),
                 _sds((1, C), F32))(parts)


def _flat(parts, dtype, lead=()):
    flat = jnp.concatenate([t.reshape(lead + (-1,)).astype(dtype) for t in parts], axis=-1)
    n = flat.shape[-1]
    unit = 16 * FLAT_COLS
    padded = -(-n // unit) * unit
    flat = jnp.pad(flat, [(0, 0)] * len(lead) + [(0, padded - n)])
    return flat.reshape(lead + (padded // FLAT_COLS, FLAT_COLS))


def _unflat(flat, shapes, lead=()):
    flat = flat.reshape(lead + (-1,))
    out, off = [], 0
    for shp in shapes:
        n = math.prod(shp)
        out.append(flat[..., off:off + n].reshape(lead + tuple(shp)))
        off += n
    return out


def _dev_major(full, ax):
    shp = full.shape
    return jnp.moveaxis(full.reshape(shp[:ax] + (N_DEV, shp[ax] // N_DEV) + shp[ax + 1:]), ax, 0)


def _from_dev_major(blocks, ax):
    t = jnp.moveaxis(blocks, 0, ax)
    shp = t.shape
    return t.reshape(shp[:ax] + (shp[ax] * shp[ax + 1],) + shp[ax + 2:])


def kernel(x, *rest):
    nw = len(WEIGHTS)
    w = dict(zip(WEIGHTS, rest[:nw]))
    tgt = rest[nw]
    m = dict(zip(WEIGHTS, rest[nw + 1:2 * nw + 1]))
    v = dict(zip(WEIGHTS, rest[2 * nw + 1:3 * nw + 1]))
    xi, yi, ci = _place()
    dev = 4 * xi + 2 * yi + ci

    big = _all_gather("gather_matrices", [w[n].astype(BF16) for n in BIG])
    small = _all_gather("gather_vectors", [_flat([w[n] for n in SMALL], F32)])[0]
    full = {n: w[n] for n in REPL}
    for n, blocks in zip(BIG, big):
        full[n] = _from_dev_major(blocks, SHARDED[n])
    for n, blocks in zip(SMALL, _unflat(small, [w[n].shape for n in SMALL], (N_DEV,))):
        full[n] = _from_dev_major(blocks, SHARDED[n])

    loss, gx, g = _local_step(x, tgt, full)

    vec_names = SMALL + REPL
    vec = _all_gather("gather_vector_grads", [_flat([g[n] for n in vec_names] + [loss], F32)])[0]
    *vec_list, losses = _unflat(vec, [g[n].shape for n in vec_names] + [(1, 1)], (N_DEV,))
    vec_parts = dict(zip(vec_names, vec_list))
    loss = _sum_parts("loss_sum", losses.reshape(N_DEV, 1))[0, 0]

    keep, send = [], []
    for n in BIG:
        gd = _dev_major(g[n], SHARDED[n]).astype(BF16)
        gd = gd.reshape((4, 2) + gd.shape[1:])
        keep.append(lax.dynamic_index_in_dim(gd, ci, 1, keepdims=False))
        send.append(lax.dynamic_index_in_dim(gd, 1 - ci, 1, keepdims=False))
    got_sib = _swap_sibling("grads_to_sibling", send)
    pair = [_add_bf16(f"grad_pair_sum_{n}", a, b) for n, a, b in zip(BIG, keep, got_sib)]
    got_chips = _scatter_chips("grads_to_chips", pair)
    chip = 2 * xi + yi

    res = {}
    for k, n in enumerate(BIG):
        parts = [lax.dynamic_index_in_dim(keep[k], chip, 0, keepdims=False),
                 lax.dynamic_index_in_dim(got_sib[k], chip, 0, keepdims=False), got_chips[k][0], got_chips[k][1], got_chips[k][2]]
        res[n] = _adamw(f"adamw_{n}", w[n], m[n], v[n], parts)
    for n in vec_names:
        parts = vec_parts[n]
        if n in SHARDED:
            ax = SHARDED[n]
            parts = lax.dynamic_slice_in_dim(parts, dev * w[n].shape[ax], w[n].shape[ax], ax + 1)
        res[n] = _adamw(f"adamw_{n}", w[n], m[n], v[n], [parts[d] for d in range(N_DEV)])
    return (loss, gx, *[res[n][0] for n in WEIGHTS], *[res[n][1] for n in WEIGHTS], *[res[n][2] for n in WEIGHTS],
            *[res[n][3] for n in WEIGHTS])
```

```python
import functools
import math

import jax
import jax.numpy as jnp
from jax import lax
from jax.experimental import pallas as pl
from jax.experimental.pallas import tpu as pltpu

F32 = jnp.float32
BF16 = jnp.bfloat16
MESH = pl.DeviceIdType.MESH

N_DEV = 8
RMS_EPS = 1e-6
LN_EPS = 1e-5
FOX_HEADS = 16
HGRN_EXPAND = 128
HGRN_CHUNK = 32
POOL_WINDOWS = (2, 4, 8, 16)
ADAM_LR, ADAM_B1, ADAM_B2, ADAM_EPS, ADAM_WD, ADAM_STEP = 0.001, 0.9, 0.999, 1e-08, 0.01, 10
LANES = 128
VMEM_LIMIT_MB = 48

SHARDED = dict(
    ffn_norm=2, ffn_w_gate=3, ffn_w_up=3, ffn_w_down=2, conv_w_in=2, conv_dw=2, conv_w_out=1, fox_w_in=2, fox_w_out=1,
    hgrn_w_in=2, hgrn_norm=1, hgrn_w_out=1, pool_w=2, pool_scale=1)
BIG = ("ffn_w_gate", "ffn_w_up", "ffn_w_down", "conv_w_in", "conv_w_out", "fox_w_in", "fox_w_out", "hgrn_w_in",
       "hgrn_w_out", "pool_w")
SMALL = ("ffn_norm", "conv_dw", "hgrn_norm", "pool_scale")
REPL = ("mix_norm", "final_norm", "conv_b_in", "conv_dw_b", "conv_ln_g", "conv_ln_b", "fox_b_f", "hgrn_lb_logits")
WEIGHTS = ("ffn_norm", "ffn_w_gate", "ffn_w_up", "ffn_w_down", "mix_norm", "final_norm", "conv_w_in", "conv_b_in",
           "conv_dw", "conv_dw_b", "conv_ln_g", "conv_ln_b", "conv_w_out", "fox_w_in", "fox_b_f", "fox_w_out",
           "hgrn_w_in", "hgrn_lb_logits", "hgrn_norm", "hgrn_w_out", "pool_w", "pool_scale")


def _tile(n, pref, mult=8):
    if n <= pref:
        return n
    for t in range(pref - pref % mult, 0, -mult):
        if n % t == 0:
            return t
    return n


def _sig(x):
    return 1.0 / (1.0 + jnp.exp(-x))


def _call(body, name, grid, in_specs, out_specs, out_shape, scratch=(), sem=None):
    params = dict(vmem_limit_bytes=VMEM_LIMIT_MB << 20)
    if sem is not None:
        params["dimension_semantics"] = sem
    return pl.pallas_call(body, name=name, grid=grid, in_specs=in_specs, out_specs=out_specs, out_shape=out_shape,
                          scratch_shapes=list(scratch), compiler_params=pltpu.CompilerParams(**params))


def _sds(shape, dtype):
    return jax.ShapeDtypeStruct(tuple(shape), dtype)


_DN = {"nn": (((1,), (0,)), ((), ())), "nt": (((1,), (1,)), ((), ())), "tn": (((0,), (0,)), ((), ()))}


def _dot(a, b, mode="nn"):
    return lax.dot_general(a.astype(BF16), b.astype(BF16), _DN[mode], preferred_element_type=F32)


def _roll(x, shift, axis=0):
    n = x.shape[axis]
    shift = shift % n
    return x if shift == 0 else pltpu.roll(x, shift, axis)


def _scan_rows(x, reverse=False):
    n = x.shape[0]
    row = lax.broadcasted_iota(jnp.int32, x.shape, 0)
    sh = 1
    while sh < n:
        if reverse:
            x = x + jnp.where(row < n - sh, _roll(x, n - sh), 0.0)
        else:
            x = x + jnp.where(row >= sh, _roll(x, sh), 0.0)
        sh *= 2
    return x


def _mm(name, mode, pairs, pair_specs, grid, k_axis, out_shapes, out_specs, acc_shape=None, extras=(), extra_specs=(),
        epilogue=None, alpha=1.0, sem=None):
    npair, nex, nout = len(pairs), len(extras), len(out_shapes)
    nk = grid[k_axis] if k_axis is not None else 1

    def body(*refs):
        prs = refs[:2 * npair]
        ex = refs[2 * npair:2 * npair + nex]
        outs = refs[2 * npair + nex:2 * npair + nex + nout]

        def partial():
            p = None
            for i in range(npair):
                d = _dot(prs[2 * i][...], prs[2 * i + 1][...], mode)
                p = d if p is None else p + d
            return p

        def finish(res):
            if alpha != 1.0:
                res = res * alpha
            if epilogue is None:
                outs[0][...] = res.astype(outs[0].dtype)
            else:
                epilogue(res, ex, outs)

        if k_axis is None:
            finish(partial())
        else:
            acc = refs[-1]
            k = pl.program_id(k_axis)

            @pl.when(k == 0)
            def _():
                acc[...] = partial()

            @pl.when(k > 0)
            def _():
                acc[...] += partial()

            @pl.when(k == nk - 1)
            def _():
                finish(acc[...])

    if sem is None:
        sem = tuple("arbitrary" if i == k_axis else "parallel" for i in range(len(grid)))
    scratch = [pltpu.VMEM(acc_shape, F32)] if k_axis is not None else []
    flat = [t for p in pairs for t in p]
    flat_specs = [s for p in pair_specs for s in p]
    return _call(body, name, grid, flat_specs + list(extra_specs), out_specs, out_shapes, scratch, sem)(*flat, *extras)


def _mm_nn(name, a, b, out_dtype, tm=512, tn=512):
    M, K = a.shape
    N = b.shape[1]
    tm, tn = _tile(M, tm), _tile(N, tn, LANES)
    return _mm(name, "nn", [(a, b)], [(pl.BlockSpec((tm, K), lambda i, j: (i, 0)), pl.BlockSpec((K, tn), lambda i, j: (0, j)))],
               (M // tm, N // tn), None, [_sds((M, N), out_dtype)], [pl.BlockSpec((tm, tn), lambda i, j: (i, j))])[0]


def _mm_res(name, u, w, x, alpha, tm=512):
    T, K = u.shape
    D = w.shape[1]
    tm = _tile(T, tm)

    def epi(res, ex, outs):
        outs[0][...] = ex[0][...] + res

    return _mm(name, "nn", [(u, w)], [(pl.BlockSpec((tm, K), lambda i: (i, 0)), pl.BlockSpec((K, D), lambda i: (0, 0)))],
               (T // tm,), None, [_sds((T, D), F32)], [pl.BlockSpec((tm, D), lambda i: (i, 0))],
               extras=[x], extra_specs=[pl.BlockSpec((tm, D), lambda i: (i, 0))], epilogue=epi, alpha=alpha)[0]


def _mm_nt(name, pairs, out_dtype, tm=512, tn=512, alpha=1.0):
    M = pairs[0][0].shape[0]
    N = pairs[0][1].shape[0]
    tm, tn = _tile(M, tm), _tile(N, tn, LANES)
    specs = [(pl.BlockSpec((tm, a.shape[1]), lambda i, j: (i, 0)), pl.BlockSpec((tn, b.shape[1]), lambda i, j: (j, 0)))
             for a, b in pairs]
    return _mm(name, "nt", pairs, specs, (M // tm, N // tn), None, [_sds((M, N), out_dtype)],
               [pl.BlockSpec((tm, tn), lambda i, j: (i, j))], alpha=alpha)[0]


def _mm_nt3(name, a3, b3, tm=512, tn=512):
    S, M, K = a3.shape
    N = b3.shape[1]
    tm, tn = _tile(M, tm), _tile(N, tn, LANES)
    return _mm(name, "nt", [(a3, b3)],
               [(pl.BlockSpec((None, tm, K), lambda i, j, s: (s, i, 0)), pl.BlockSpec((None, tn, K), lambda i, j, s: (s, j, 0)))],
               (M // tm, N // tn, S), 2, [_sds((M, N), F32)], [pl.BlockSpec((tm, tn), lambda i, j, s: (i, j))],
               acc_shape=(tm, tn))[0]


def _mm_tn(name, a, b, alpha=1.0, tm=1408, tn=1408, tk=512):
    T, M = a.shape
    N = b.shape[1]
    tm, tn, tk = _tile(M, tm, LANES), _tile(N, tn, LANES), _tile(T, tk)
    return _mm(name, "tn", [(a, b)],
               [(pl.BlockSpec((tk, tm), lambda i, j, k: (k, i)), pl.BlockSpec((tk, tn), lambda i, j, k: (k, j)))],
               (M // tm, N // tn, T // tk), 2, [_sds((M, N), F32)], [pl.BlockSpec((tm, tn), lambda i, j, k: (i, j))],
               acc_shape=(tm, tn), alpha=alpha)[0]


def _mm_tn3(name, a, b3, tm=1408, tn=1408, tk=512):
    T, M = a.shape
    S, _, N = b3.shape
    tm, tn, tk = _tile(M, tm, LANES), _tile(N, tn, LANES), _tile(T, tk)
    return _mm(name, "tn", [(a, b3)],
               [(pl.BlockSpec((tk, tm), lambda s, i, j, k: (k, i)), pl.BlockSpec((None, tk, tn), lambda s, i, j, k: (s, k, j)))],
               (S, M // tm, N // tn, T // tk), 3, [_sds((S, M, N), F32)],
               [pl.BlockSpec((None, tm, tn), lambda s, i, j, k: (s, i, j))], acc_shape=(tm, tn))[0]


def _rms_fwd(name, x, g):
    T, D = x.shape
    tt = _tile(T, 512)

    def body(x_ref, g_ref, h_ref):
        xv = x_ref[...]
        r = lax.rsqrt(jnp.mean(xv * xv, axis=-1, keepdims=True) + RMS_EPS)
        h_ref[...] = (xv * r * g_ref[...]).astype(h_ref.dtype)

    row = pl.BlockSpec((tt, D), lambda i: (i, 0))
    return _call(body, name, (T // tt,), [row, pl.BlockSpec((1, D), lambda i: (0, 0))], row, _sds((T, D), BF16))(x, g)


def _rms_bwd(name, dh, x, g, dx_in):
    T, D = x.shape
    tt = _tile(T, 512)

    def body(dh_ref, x_ref, g_ref, dxi_ref, dx_ref, dxb_ref, dg_ref):
        xv = x_ref[...]
        r = lax.rsqrt(jnp.mean(xv * xv, axis=-1, keepdims=True) + RMS_EPS)
        xh = xv * r
        dhv = dh_ref[...]
        dxh = dhv * g_ref[...]
        dx = dxi_ref[...] + r * (dxh - xh * jnp.mean(dxh * xh, axis=-1, keepdims=True))
        dx_ref[...] = dx
        dxb_ref[...] = dx.astype(BF16)
        part = jnp.sum(dhv * xh, axis=0, keepdims=True)

        @pl.when(pl.program_id(0) == 0)
        def _():
            dg_ref[...] = part

        @pl.when(pl.program_id(0) > 0)
        def _():
            dg_ref[...] += part

    row = pl.BlockSpec((tt, D), lambda i: (i, 0))
    vec = pl.BlockSpec((1, D), lambda i: (0, 0))
    dx, dxb, dg = _call(body, name, (T // tt,), [row, row, vec, row], [row, row, vec],
                        [_sds((T, D), F32), _sds((T, D), BF16), _sds((1, D), F32)], sem=("arbitrary",))(dh, x, g, dx_in)
    return (dx, dxb), dg


def _loss_head(x, g, tgt):
    T, D = x.shape
    tt = _tile(T, 512)

    def body(x_ref, g_ref, t_ref, loss_ref, dx_ref, dxb_ref, dg_ref):
        xv = x_ref[...]
        r = lax.rsqrt(jnp.mean(xv * xv, axis=-1, keepdims=True) + RMS_EPS)
        xh = xv * r
        e = xh * g_ref[...] - t_ref[...]
        lp = 0.5 * jnp.sum(jnp.sum(e * e, axis=-1, keepdims=True), axis=0, keepdims=True) / D
        dy = e / D
        dxh = dy * g_ref[...]
        dx = r * (dxh - xh * jnp.mean(dxh * xh, axis=-1, keepdims=True))
        dx_ref[...] = dx
        dxb_ref[...] = dx.astype(BF16)
        part = jnp.sum(dy * xh, axis=0, keepdims=True)

        @pl.when(pl.program_id(0) == 0)
        def _():
            dg_ref[...] = part
            loss_ref[...] = lp

        @pl.when(pl.program_id(0) > 0)
        def _():
            dg_ref[...] += part
            loss_ref[...] += lp

    row = pl.BlockSpec((tt, D), lambda i: (i, 0))
    vec = pl.BlockSpec((1, D), lambda i: (0, 0))
    one = pl.BlockSpec((1, 1), lambda i: (0, 0))
    loss, dx, dxb, dg = _call(body, "loss_head", (T // tt,), [row, vec, row], [one, row, row, vec],
                              [_sds((1, 1), F32), _sds((T, D), F32), _sds((T, D), BF16), _sds((1, D), F32)],
                              sem=("arbitrary",))(x, g, tgt)
    return loss, (dx, dxb), dg


def _colsum3(name, a3):
    S, T, N = a3.shape
    tt = _tile(T, 512)

    def body(a_ref, o_ref):
        part = jnp.sum(a_ref[...].astype(F32), axis=0, keepdims=True)

        @pl.when(pl.program_id(1) == 0)
        def _():
            o_ref[...] = part

        @pl.when(pl.program_id(1) > 0)
        def _():
            o_ref[...] += part

    return _call(body, name, (S, T // tt), [pl.BlockSpec((None, tt, N), lambda s, i: (s, i, 0))],
                 pl.BlockSpec((None, 1, N), lambda s, i: (s, 0, 0)), _sds((S, 1, N), F32), sem=("parallel", "arbitrary"))(a3)


def _glu_mm(name, h, w2, bias2, mode, u_dtype, tm=1024, tn=256):
    T, K = h.shape
    N = w2.shape[2]
    tm, tn = _tile(T, tm), _tile(N, tn, LANES)
    has_bias = bias2 is not None

    def body(*refs):
        h_ref, w_ref = refs[0], refs[1]
        ab_ref, u_ref = refs[-2], refs[-1]
        hv = h_ref[...]
        a = _dot(hv, w_ref[0])
        b = _dot(hv, w_ref[1])
        if has_bias:
            a = a + refs[2][0]
            b = b + refs[2][1]
        u = a * _sig(a) * b if mode == "swiglu" else a * _sig(b)
        ab_ref[0] = a.astype(BF16)
        ab_ref[1] = b.astype(BF16)
        u_ref[...] = u.astype(u_ref.dtype)

    in_specs = [pl.BlockSpec((tm, K), lambda i, j: (i, 0)), pl.BlockSpec((2, K, tn), lambda i, j: (0, 0, j))]
    args = [h, w2]
    if has_bias:
        in_specs.append(pl.BlockSpec((2, 1, tn), lambda i, j: (0, 0, j)))
        args.append(bias2)
    return _call(body, name, (T // tm, N // tn), in_specs,
                 [pl.BlockSpec((2, tm, tn), lambda i, j: (0, i, j)), pl.BlockSpec((tm, tn), lambda i, j: (i, j))],
                 [_sds((2, T, N), BF16), _sds((T, N), u_dtype)], sem=("parallel", "parallel"))(*args)


def _swiglu_bwd_mm(name, dxb, wd, ab, alpha, tm=1024, tn=256):
    T, D = dxb.shape
    N = wd.shape[0]
    tm, tn = _tile(T, tm), _tile(N, tn, LANES)
    def body(dx_ref, wd_ref, ab_ref, dab_ref, u_ref):
        du = _dot(dx_ref[...], wd_ref[...], "nt") * alpha
        a = ab_ref[0].astype(F32)
        b = ab_ref[1].astype(F32)
        sg = _sig(a)
        sa = a * sg
        dab_ref[0] = (du * b * (sg * (1.0 + a * (1.0 - sg)))).astype(BF16)
        dab_ref[1] = (du * sa).astype(BF16)
        u_ref[...] = (sa * b).astype(BF16)

    ab_spec = pl.BlockSpec((2, tm, tn), lambda i, j: (0, i, j))
    return _call(body, name, (T // tm, N // tn),
                 [pl.BlockSpec((tm, D), lambda i, j: (i, 0)), pl.BlockSpec((tn, D), lambda i, j: (j, 0)), ab_spec],
                 [ab_spec, pl.BlockSpec((tm, tn), lambda i, j: (i, j))], [_sds((2, T, N), BF16), _sds((T, N), BF16)],
                 sem=("parallel", "parallel"))(dxb, wd, ab)


def _ffn_fwd(tag, x, g, wgu, wd):
    h = _rms_fwd(f"ffn_rms_{tag}", x, g)
    ab, u = _glu_mm(f"ffn_gu_{tag}", h, wgu, None, "swiglu", BF16)
    return _mm_res(f"ffn_down_{tag}", u, wd, x, 0.5), (x, h, ab)


def _ffn_bwd(tag, dxp, saved, g, wgu, wd):
    x, h, ab = saved
    dx, dxb = dxp
    dab, u = _swiglu_bwd_mm(f"ffn_dgu_{tag}", dxb, wd, ab, 0.5)
    dwd = _mm_tn(f"ffn_dwd_{tag}", u, dxb, alpha=0.5)
    dwgu = _mm_tn3(f"ffn_dwgu_{tag}", h, dab)
    dh = _mm_nt3(f"ffn_dh_{tag}", dab, wgu)
    dx2, dg = _rms_bwd(f"ffn_drms_{tag}", dh, x, g, dx)
    return dx2, dg, dwgu, dwd


HALO = 32


def _conv_fwd(u3, dw32, dwb, lng, lnb):
    B, S, D = u3.shape
    W = dw32.shape[0]
    taps = CONV_TAPS
    tt = _tile(S, 256, HALO)
    hb = tt // HALO

    def body(u_ref, halo_ref, dw_ref, dwb_ref, g_ref, b_ref, v_ref, s_ref):
        i = pl.program_id(1)
        halo = jnp.where(i > 0, halo_ref[...], 0.0)
        ext = jnp.concatenate([halo, u_ref[...]], axis=0)
        acc = jnp.zeros((tt, D), F32) + dwb_ref[...]
        for j in range(taps):
            acc = acc + dw_ref[pl.ds(j, 1), :] * _roll(ext, taps - 1 - j)[HALO:]
        v_ref[...] = acc
        mu = jnp.mean(acc, axis=-1, keepdims=True)
        xc = acc - mu
        ln = xc * lax.rsqrt(jnp.mean(xc * xc, axis=-1, keepdims=True) + LN_EPS) * g_ref[...] + b_ref[...]
        s_ref[...] = (ln * _sig(ln)).astype(BF16)

    main = pl.BlockSpec((None, tt, D), lambda b, i: (b, i, 0))
    halo = pl.BlockSpec((None, HALO, D), lambda b, i: (b, jnp.maximum(i * hb - 1, 0), 0))
    vec = pl.BlockSpec((1, D), lambda b, i: (0, 0))
    return _call(body, "conv_fwd", (B, S // tt), [main, halo, pl.BlockSpec((W, D), lambda b, i: (0, 0)), vec, vec, vec],
                 [main, main], [_sds((B, S, D), F32), _sds((B, S, D), BF16)], sem=("parallel", "parallel"))(
        u3, u3, dw32, dwb, lng, lnb)


def _conv_bwd_ln(v, ds, lng, lnb):
    T, D = v.shape
    tt = _tile(T, 256)

    def body(v_ref, ds_ref, g_ref, b_ref, dv_ref, red_ref):
        vv = v_ref[...]
        mu = jnp.mean(vv, axis=-1, keepdims=True)
        xc = vv - mu
        rstd = lax.rsqrt(jnp.mean(xc * xc, axis=-1, keepdims=True) + LN_EPS)
        xh = xc * rstd
        ln = xh * g_ref[...] + b_ref[...]
        sg = _sig(ln)
        dln = ds_ref[...] * (sg * (1.0 + ln * (1.0 - sg)))
        dxh = dln * g_ref[...]
        dv = rstd * (dxh - jnp.mean(dxh, axis=-1, keepdims=True) - xh * jnp.mean(dxh * xh, axis=-1, keepdims=True))
        dv_ref[...] = dv
        parts = (jnp.sum(dln * xh, axis=0, keepdims=True), jnp.sum(dln, axis=0, keepdims=True),
                 jnp.sum(dv, axis=0, keepdims=True))

        @pl.when(pl.program_id(0) == 0)
        def _():
            for k in range(3):
                red_ref[k] = parts[k]

        @pl.when(pl.program_id(0) > 0)
        def _():
            for k in range(3):
                red_ref[k] += parts[k]

    row = pl.BlockSpec((tt, D), lambda i: (i, 0))
    vec = pl.BlockSpec((1, D), lambda i: (0, 0))
    return _call(body, "conv_bwd_ln", (T // tt,), [row, row, vec, vec], [row, pl.BlockSpec((3, 1, D), lambda i: (0, 0, 0))],
                 [_sds((T, D), F32), _sds((3, 1, D), F32)], sem=("arbitrary",))(v, ds, lng, lnb)


def _conv_bwd_dw(dv3, u3, ab, dw32):
    B, S, D = u3.shape
    W = dw32.shape[0]
    taps = CONV_TAPS
    tt = _tile(S, 256, HALO)
    hb = tt // HALO
    nt = S // tt
    L = tt + HALO

    def body(dv_ref, dvn_ref, u_ref, up_ref, ab_ref, dw_ref, dab_ref, ddw_ref):
        b, i = pl.program_id(0), pl.program_id(1)
        dv = dv_ref[...]
        ext_dv = jnp.concatenate([dv, jnp.where(i < nt - 1, dvn_ref[...], 0.0)], axis=0)
        ext_u = jnp.concatenate([jnp.where(i > 0, up_ref[...], 0.0), u_ref[...]], axis=0)
        du = jnp.zeros((tt, D), F32)
        first = jnp.logical_and(b == 0, i == 0)

        @pl.when(first)
        def _():
            ddw_ref[...] = jnp.zeros((W, D), F32)

        for j in range(taps):
            sh = taps - 1 - j
            du = du + dw_ref[pl.ds(j, 1), :] * _roll(ext_dv, L - sh)[:tt]
            ddw_ref[pl.ds(j, 1), :] += jnp.sum(dv * _roll(ext_u, sh)[HALO:], axis=0, keepdims=True)
        a = ab_ref[0].astype(F32)
        sb = _sig(ab_ref[1].astype(F32))
        dab_ref[0] = (du * sb).astype(BF16)
        dab_ref[1] = (du * a * sb * (1.0 - sb)).astype(BF16)

    main = pl.BlockSpec((None, tt, D), lambda b, i: (b, i, 0))
    prev = pl.BlockSpec((None, HALO, D), lambda b, i: (b, jnp.maximum(i * hb - 1, 0), 0))
    nxt = pl.BlockSpec((None, HALO, D), lambda b, i: (b, jnp.minimum((i + 1) * hb, S // HALO - 1), 0))
    abs_ = pl.BlockSpec((2, tt, D), lambda b, i: (0, b * nt + i, 0))
    wsp = pl.BlockSpec((W, D), lambda b, i: (0, 0))
    return _call(body, "conv_bwd_dw", (B, nt), [main, nxt, main, prev, abs_, wsp], [abs_, wsp],
                 [_sds((2, B * S, D), BF16), _sds((W, D), F32)], sem=("arbitrary", "arbitrary"))(dv3, dv3, u3, u3, ab, dw32)


CONV_TAPS = 31


def _conv_mixer_fwd(x, B, p):
    T, D = x.shape
    h = _rms_fwd("conv_rms", x, p["mix_norm"][0:1])
    ab, u = _glu_mm("conv_in", h, p["conv_w_in2"], p["conv_b_in2"], "glu", F32)
    v, s = _conv_fwd(u.reshape(B, T // B, D), p["conv_dw32"], p["conv_dw_b"], p["conv_ln_g"], p["conv_ln_b"])
    v, s = v.reshape(T, D), s.reshape(T, D)
    return _mm_res("conv_out", s, p["conv_w_out"], x, 1.0), (x, h, ab, u, v, s)


def _conv_mixer_bwd(dxp, saved, B, p):
    x, h, ab, u, v, s = saved
    T, D = x.shape
    dx, dxb = dxp
    g = {}
    ds = _mm_nt("conv_ds", [(dxb, p["conv_w_out"])], F32)
    g["conv_w_out"] = _mm_tn("conv_dwout", s, dxb)
    dv, red = _conv_bwd_ln(v, ds, p["conv_ln_g"], p["conv_ln_b"])
    g["conv_ln_g"], g["conv_ln_b"], g["conv_dw_b"] = red[0], red[1], red[2]
    dab, ddw = _conv_bwd_dw(dv.reshape(B, T // B, D), u.reshape(B, T // B, D), ab, p["conv_dw32"])
    g["conv_dw"] = ddw[:CONV_TAPS][None]
    g["conv_b_in"] = _colsum3("conv_dbin", dab).reshape(1, 2 * D)
    dwin = _mm_tn3("conv_dwin", h, dab)
    g["conv_w_in"] = jnp.moveaxis(dwin, 0, 1).reshape(1, D, 2 * D)
    dh = _mm_nt3("conv_dh", dab, p["conv_w_in2"])
    dx2, dg = _rms_bwd("conv_drms", dh, x, p["mix_norm"][0:1], dx)
    return dx2, dg, g


def _log_sigmoid(z):
    return jnp.minimum(z, 0.0) - jnp.log(1.0 + jnp.exp(-jnp.abs(z)))


def _fox_gate_fwd(fl3, bf):
    B, S, N = fl3.shape
    tt = _tile(S, 512)

    def body(fl_ref, bf_ref, c_ref, carry):
        @pl.when(pl.program_id(1) == 0)
        def _():
            carry[...] = jnp.zeros((1, N), F32)

        c = _scan_rows(_log_sigmoid(fl_ref[...] + bf_ref[...])) + carry[...]
        c_ref[...] = c
        carry[...] = c_ref[pl.ds(tt - 1, 1), :]

    row = pl.BlockSpec((None, tt, N), lambda b, i: (b, i, 0))
    return _call(body, "fox_gate_fwd", (B, S // tt), [row, pl.BlockSpec((1, N), lambda b, i: (0, 0))], row,
                 _sds((B, S, N), F32), [pltpu.VMEM((1, N), F32)], sem=("parallel", "arbitrary"))(fl3, bf)


def _fox_gate_bwd(dc3, fl3, bf):
    B, S, N = fl3.shape
    tt = _tile(S, 512)
    nt = S // tt

    def body(dc_ref, fl_ref, bf_ref, dfl_ref, dbf_ref, carry):
        b, i = pl.program_id(0), pl.program_id(1)

        @pl.when(i == 0)
        def _():
            carry[...] = jnp.zeros((1, N), F32)

        dc = dc_ref[0] - dc_ref[1]
        dlf = _scan_rows(dc, reverse=True) + carry[...]
        dfl = dlf * _sig(-(fl_ref[...] + bf_ref[...]))
        dfl_ref[...] = dfl
        carry[...] += jnp.sum(dc, axis=0, keepdims=True)
        part = jnp.sum(dfl, axis=0, keepdims=True)

        @pl.when(jnp.logical_and(b == 0, i == 0))
        def _():
            dbf_ref[...] = part

        @pl.when(jnp.logical_or(b > 0, i > 0))
        def _():
            dbf_ref[...] += part

    row = pl.BlockSpec((None, tt, N), lambda b, i: (b, nt - 1 - i, 0))
    vec = pl.BlockSpec((1, N), lambda b, i: (0, 0))
    row2 = pl.BlockSpec((2, None, tt, N), lambda b, i: (0, b, nt - 1 - i, 0))
    return _call(body, "fox_gate_bwd", (B, nt), [row2, row, vec], [row, vec], [_sds((B, S, N), F32), _sds((1, N), F32)],
                 [pltpu.VMEM((1, N), F32)], sem=("arbitrary", "arbitrary"))(dc3, fl3, bf)


NEG = -1e30


def _fox_attn_fwd(qa, ka, vat, crow, ckb, dh, scale):
    B, H, S, P = qa.shape
    tq = _tile(S, 512, LANES)
    tk = tq
    nl = tq // LANES

    def body(q_ref, k_ref, vt_ref, cr_ref, ck_ref, o_ref, lse_ref, s_scr, p_scr, m_scr, al_scr, acc_scr):
        i = pl.program_id(2)
        qv = q_ref[...]
        m_scr[...] = jnp.full((1, tq), NEG, F32)
        acc_scr[...] = jnp.zeros((P, tq), F32)
        key = lax.broadcasted_iota(jnp.int32, (tk, LANES), 0)
        qry = lax.broadcasted_iota(jnp.int32, (tk, LANES), 1)

        def kv_step(j, diagonal):
            off = pl.multiple_of(j * tk, tk)
            s_scr[...] = _dot(k_ref[pl.ds(off, tk), :], qv, "nt")
            ck = ck_ref[pl.ds(off, tk), :]
            for lt in range(nl):
                ls = slice(lt * LANES, (lt + 1) * LANES)
                s = s_scr[:, ls] * scale + cr_ref[:, ls] - ck
                if diagonal:
                    s = jnp.where(key <= qry + lt * LANES, s, -jnp.inf)
                m1 = m_scr[:, ls]
                m2 = jnp.maximum(m1, jnp.max(s, axis=0, keepdims=True))
                p_scr[:, ls] = jnp.exp(s - m2).astype(BF16)
                al_scr[:, ls] = jnp.exp(m1 - m2)
                m_scr[:, ls] = m2
            acc_scr[...] = al_scr[...] * acc_scr[...] + _dot(vt_ref[:, pl.ds(off, tk)], p_scr[...])

        def before(j, carry):
            kv_step(j, False)
            return carry

        lax.fori_loop(0, i, before, 0)
        kv_step(i, True)
        l = acc_scr[pl.ds(dh, 1), :]
        o_ref[...] = (acc_scr[...] / l).T
        lse_ref[...] = m_scr[...] + jnp.log(l)

    qs = pl.BlockSpec((None, None, tq, P), lambda b, h, i: (b, h, i, 0))
    fullk = pl.BlockSpec((None, None, S, P), lambda b, h, i: (b, h, 0, 0))
    fullt = pl.BlockSpec((None, None, P, S), lambda b, h, i: (b, h, 0, 0))
    fullc = pl.BlockSpec((None, None, S, LANES), lambda b, h, i: (b, h, 0, 0))
    rowt = pl.BlockSpec((None, None, 1, tq), lambda b, h, i: (b, h, 0, i))
    scratch = [pltpu.VMEM((tk, tq), F32), pltpu.VMEM((tk, tq), BF16), pltpu.VMEM((1, tq), F32), pltpu.VMEM((1, tq), F32),
               pltpu.VMEM((P, tq), F32)]
    return _call(body, "fox_attn_fwd", (B, H, S // tq), [qs, fullk, fullt, rowt, fullc], [qs, rowt],
                 [_sds((B, H, S, P), F32), _sds((B, H, 1, S), F32)], scratch, sem=("parallel", "parallel", "parallel"))(
        qa, ka, vat, crow, ckb)


def _fox_rowstats(do, o):
    B, H, S, P = o.shape
    tq = _tile(S, 4096)

    def body(do_ref, o_ref, dl_ref):
        dl_ref[...] = jnp.sum(do_ref[...] * o_ref[...], axis=-1, keepdims=True)

    qs = pl.BlockSpec((None, None, tq, P), lambda b, h, i: (b, h, i, 0))
    col = pl.BlockSpec((None, None, tq, 1), lambda b, h, i: (b, h, i, 0))
    return _call(body, "fox_rowstats", (B, H, S // tq), [qs, qs], col, _sds((B, H, S, 1), F32), sem=("parallel",) * 3)(do, o)


def _fox_attn_bwd(qa, ka, kat, va, do, crow, lse, delta, ckb, dh, scale):
    B, H, S, P = qa.shape
    tk = _tile(S, 512, LANES)
    tq = tk
    nq = S // tq
    nl = tq // LANES

    def body(q_ref, k_ref, kt_ref, v_ref, do_ref, cr_ref, lse_ref, dl_ref, ck_ref, dqt_ref, dk_ref, dv_ref, rs_ref,
             s_scr, dp_scr, p_scr, ds_scr, dk_acc, dv_acc):
        j = pl.program_id(2)

        @pl.when(j == 0)
        def _():
            dqt_ref[...] = jnp.zeros((P, S), F32)

        kj, vj, ck = k_ref[...], v_ref[...], ck_ref[...]
        dk_acc[...] = jnp.zeros((tk, P), F32)
        dv_acc[...] = jnp.zeros((tk, P), F32)
        key = lax.broadcasted_iota(jnp.int32, (tk, LANES), 0)
        qry = lax.broadcasted_iota(jnp.int32, (tk, LANES), 1)

        def q_step(i, diagonal):
            off = pl.multiple_of(i * tq, tq)
            qi, doi = q_ref[pl.ds(off, tq), :], do_ref[pl.ds(off, tq), :].astype(BF16)
            s_scr[...] = _dot(kj, qi, "nt")
            dp_scr[...] = _dot(vj, doi, "nt")
            for lt in range(nl):
                ls = slice(lt * LANES, (lt + 1) * LANES)
                gl = pl.ds(pl.multiple_of(off + lt * LANES, LANES), LANES)
                p = jnp.exp(s_scr[:, ls] * scale + (cr_ref[:, gl] - lse_ref[:, gl]) - ck)
                if diagonal:
                    p = jnp.where(key <= qry + lt * LANES, p, 0.0)
                p_scr[:, ls] = p.astype(BF16)
                ds_scr[:, ls] = (p * (dp_scr[:, ls] - dl_ref[:, gl])).astype(BF16)
            ds = ds_scr[...]
            dqt_ref[:, pl.ds(off, tq)] += _dot(kt_ref[...], ds)
            dk_acc[...] += _dot(ds, qi)
            dv_acc[...] += _dot(p_scr[...], doi)

        def after(i, carry):
            q_step(i, False)
            return carry

        q_step(j, True)
        lax.fori_loop(j + 1, nq, after, 0)
        dk_ref[...] = dk_acc[...] * jnp.where(lax.broadcasted_iota(jnp.int32, (tk, P), 1) < dh, scale, 1.0)
        dv_ref[...] = dv_acc[...]

        @pl.when(j == S // tk - 1)
        def _():
            rs_ref[...] = dqt_ref[pl.ds(dh, 1), :]
            dqt_ref[...] = dqt_ref[...] * jnp.where(lax.broadcasted_iota(jnp.int32, (P, S), 0) < dh, scale, 1.0)

    ks = pl.BlockSpec((None, None, tk, P), lambda b, h, j: (b, h, j, 0))
    kts = pl.BlockSpec((None, None, P, tk), lambda b, h, j: (b, h, 0, j))
    cks = pl.BlockSpec((None, None, tk, LANES), lambda b, h, j: (b, h, j, 0))
    full = pl.BlockSpec((None, None, S, P), lambda b, h, j: (b, h, 0, 0))
    fullt = pl.BlockSpec((None, None, P, S), lambda b, h, j: (b, h, 0, 0))
    rowf = pl.BlockSpec((None, None, 1, S), lambda b, h, j: (b, h, 0, 0))
    scratch = [pltpu.VMEM((tk, tq), F32), pltpu.VMEM((tk, tq), F32), pltpu.VMEM((tk, tq), BF16), pltpu.VMEM((tk, tq), BF16),
               pltpu.VMEM((tk, P), F32), pltpu.VMEM((tk, P), F32)]
    return _call(body, "fox_attn_bwd", (B, H, S // tk), [full, ks, kts, ks, full, rowf, rowf, rowf, cks],
                 [fullt, ks, ks, rowf],
                 [_sds((B, H, P, S), F32), _sds((B, H, S, P), F32), _sds((B, H, S, P), F32), _sds((B, H, 1, S), F32)], scratch,
                 sem=("parallel", "parallel", "arbitrary"))(qa, ka, kat, va, do, crow, lse, delta, ckb)


def _heads(t, B, H):
    T, D = t.shape
    return t.reshape(B, T // B, H, D // H).transpose(0, 2, 1, 3)


def _unheads(t):
    B, H, S, dh = t.shape
    return t.transpose(0, 2, 1, 3).reshape(B * S, H * dh)


def _fox_mixer_fwd(x, B, p):
    T, D = x.shape
    H = FOX_HEADS
    S = T // B
    scale = (D // H) ** -0.5
    h = _rms_fwd("fox_rms", x, p["mix_norm"][1:2])
    qkv = _mm_nn("fox_qkv", h, p["fox_w_qkv"], BF16)
    fl = _mm_nn("fox_fl", h, p["fox_w_f"], F32)
    c = _fox_gate_fwd(fl.reshape(B, S, LANES), p["fox_b_f128"])
    ch = c[:, :, :H].transpose(0, 2, 1)
    crow = ch[:, :, None, :]
    ckb = jnp.broadcast_to(ch[..., None], (B, H, S, LANES))
    q, k, v = _heads(qkv[:, :D], B, H), _heads(qkv[:, D:2 * D], B, H), _heads(qkv[:, 2 * D:], B, H)
    dh = D // H
    P = -(-(dh + 2) // LANES) * LANES
    one, zero = jnp.ones((B, H, S, 1), q.dtype), jnp.zeros((B, H, S, 1), q.dtype)
    rest = jnp.zeros((B, H, S, P - dh - 2), q.dtype)
    qa = jnp.concatenate([q, zero, one, rest], axis=-1)
    ka = jnp.concatenate([k, one, zero, rest], axis=-1)
    va = jnp.concatenate([v, one, zero, rest], axis=-1)
    o, lse = _fox_attn_fwd(qa, ka, va.transpose(0, 1, 3, 2), crow, ckb, dh, scale)
    of = _unheads(o[..., :dh])
    return _mm_res("fox_out", of, p["fox_w_out"], x, 1.0), (x, h, fl, qa, ka, va, crow, ckb, o, lse, of)


def _fox_mixer_bwd(dxp, saved, B, p):
    x, h, fl, qa, ka, va, crow, ckb, o, lse, of = saved
    T, D = x.shape
    H = FOX_HEADS
    S = T // B
    dh_ = D // H
    P = qa.shape[-1]
    scale = dh_ ** -0.5
    dx, dxb = dxp
    g = {}
    do = _heads(_mm_nt("fox_do", [(dxb, p["fox_w_out"])], F32), B, H)
    do = jnp.pad(do, ((0, 0), (0, 0), (0, 0), (0, P - dh_)))
    g["fox_w_out"] = _mm_tn("fox_dwout", of, dxb)
    delta = _fox_rowstats(do, o).reshape(B, H, 1, S)
    dqt, dk, dv, rowsum = _fox_attn_bwd(qa, ka, ka.transpose(0, 1, 3, 2), va, do, crow, lse, delta, ckb, dh_, scale)
    dq = dqt[:, :, :dh_, :].transpose(0, 3, 1, 2).reshape(T, D)
    dqkv = jnp.concatenate([dq, _unheads(dk[..., :dh_]), _unheads(dv[..., :dh_])], axis=1).astype(BF16)
    dc = jnp.stack([rowsum[:, :, 0, :], dk[..., dh_ + 1]])
    dc = jnp.pad(dc.transpose(0, 1, 3, 2), ((0, 0), (0, 0), (0, 0), (0, LANES - H)))
    dfl, dbf = _fox_gate_bwd(dc, fl.reshape(B, S, LANES), p["fox_b_f128"])
    dfl = dfl.reshape(T, LANES)
    g["fox_b_f"] = dbf[:, :H]
    dwqkv = _mm_tn("fox_dwqkv", h, dqkv)
    dwf = _mm_tn("fox_dwf", h, dfl)
    g["fox_w_in"] = jnp.concatenate([dwqkv, dwf[:, :H]], axis=1)[None]
    dh = _mm_nt("fox_dh", [(dqkv, p["fox_w_qkv"]), (dfl, p["fox_w_f"])], F32)
    dx2, dg = _rms_bwd("fox_drms", dh, x, p["mix_norm"][1:2], dx)
    return dx2, dg, g


def _lb_fwd(logits):
    L, D = logits.shape

    def body(l_ref, lb_ref):
        z = l_ref[...]
        e = jnp.exp(z - jnp.max(z, axis=0, keepdims=True))
        p = e / jnp.sum(e, axis=0, keepdims=True)
        lb_ref[...] = jnp.sum(jnp.where(_lb_rows(z.shape), p, 0.0), axis=0, keepdims=True)

    return _call(body, "hgrn_lb", (1,), [pl.BlockSpec((L, D), lambda i: (0, 0))], pl.BlockSpec((1, D), lambda i: (0, 0)),
                 _sds((1, D), F32))(logits)


def _lb_rows(shape):
    r = lax.broadcasted_iota(jnp.int32, shape, 0)
    return jnp.logical_and(r >= 1, r <= HGRN_LAYER)


HGRN_LAYER = 2


def _lb_bwd(logits, dlb):
    L, D = logits.shape

    def body(l_ref, d_ref, o_ref):
        z = l_ref[...]
        e = jnp.exp(z - jnp.max(z, axis=0, keepdims=True))
        p = e / jnp.sum(e, axis=0, keepdims=True)
        dp = jnp.where(_lb_rows(z.shape), d_ref[...], 0.0)
        o_ref[...] = p * (dp - jnp.sum(p * dp, axis=0, keepdims=True))

    full = pl.BlockSpec((L, D), lambda i: (0, 0))
    return _call(body, "hgrn_dlb", (1,), [full, pl.BlockSpec((1, D), lambda i: (0, 0))], full, _sds((L, D), F32))(logits, dlb)


def _hgrn_gates(qr, fr, lbv):
    e = jnp.exp(-jnp.abs(fr))
    big, small = 1.0 / (1.0 + e), e / (1.0 + e)
    sf = jnp.where(fr >= 0, big, small)
    snf = jnp.where(fr >= 0, small, big)
    f = lbv + (1.0 - lbv) * sf
    sq = _sig(qr)
    return qr * sq, (1.0 - lbv) * snf, jnp.log(f), sf, snf, f, sq


def _hgrn_intra(G, q, kk, g_scr, q_scr):
    C = HGRN_CHUNK
    g_scr[...] = G
    q_scr[...] = q
    srow = lax.broadcasted_iota(jnp.int32, (C, LANES), 0)
    lane = lax.broadcasted_iota(jnp.int32, (C, LANES), 1)
    at = jnp.zeros((C, LANES), F32)
    for t in range(C):
        e = jnp.where(srow <= t, jnp.exp(g_scr[pl.ds(t, 1), :] - G), 0.0)
        col = jnp.sum(e * kk * q_scr[pl.ds(t, 1), :], axis=-1, keepdims=True)
        at = jnp.where(lane == t, col, at)
    return at


def _hgrn_fwd(proj3, lb, ng):
    B, S, D4 = proj3.shape
    D = D4 // 4
    H = D // HGRN_EXPAND
    C = HGRN_CHUNK
    R = _tile(S, 256, 2 * C)
    ncb = R // C
    dk = HGRN_EXPAND

    def body(q_ref, f_ref, i_ref, go_ref, lb_ref, ng_ref, y_ref, o_ref, st_ref, st, g_scr, q_scr):
        @pl.when(pl.program_id(2) == 0)
        def _():
            st[...] = jnp.zeros((dk, dk), F32)

        lbv = lb_ref[...]

        def chunk(c, slot):
            r0 = pl.multiple_of(c * C, C)
            rows = pl.ds(r0, C)
            q, kk, lf, *_ = _hgrn_gates(q_ref[rows, :], f_ref[rows, :], lbv)
            vv = i_ref[rows, :]
            G = _scan_rows(lf)
            at = _hgrn_intra(G, q, kk, g_scr.at[slot], q_scr.at[slot])
            gl = g_scr[slot, pl.ds(C - 1, 1), :]
            stv = st[...]
            st_ref[c] = stv
            o = _dot(q * jnp.exp(G), stv, "nt") + _dot(at, vv, "tn")[:C]
            st[...] = stv * jnp.exp(gl) + _dot(vv, kk * jnp.exp(gl - G), "tn")
            o_ref[rows, :] = o
            gv = go_ref[rows, :]
            y = o * lax.rsqrt(jnp.mean(o * o, axis=-1, keepdims=True) + RMS_EPS) * ng_ref[...] * (gv * _sig(gv))
            y_ref[rows, :] = y.astype(BF16)

        def pair(c2, carry):
            chunk(2 * c2, 0)
            chunk(2 * c2 + 1, 1)
            return carry

        lax.fori_loop(0, ncb // 2, pair, 0)

    def col(k):
        return pl.BlockSpec((None, R, dk), lambda b, h, i: (b, i, h + k * H))

    vec = pl.BlockSpec((1, dk), lambda b, h, i: (0, h))
    out = pl.BlockSpec((None, R, dk), lambda b, h, i: (b, i, h))
    return _call(body, "hgrn_fwd", (B, H, S // R), [col(0), col(1), col(2), col(3), vec, vec],
                 [out, out, pl.BlockSpec((None, None, ncb, dk, dk), lambda b, h, i: (b, h, i, 0, 0))],
                 [_sds((B, S, D), BF16), _sds((B, S, D), F32), _sds((B, H, S // C, dk, dk), F32)],
                 [pltpu.VMEM((dk, dk), F32), pltpu.VMEM((2, C, dk), F32), pltpu.VMEM((2, C, dk), F32)],
                 sem=("parallel", "parallel", "arbitrary"))(proj3, proj3, proj3, proj3, lb, ng)


def _hgrn_bwd(proj3, o3, dy3, states, lb, ng):
    B, S, D4 = proj3.shape
    D = D4 // 4
    H = D // HGRN_EXPAND
    C = HGRN_CHUNK
    R = _tile(S, 256, 2 * C)
    ncb = R // C
    nb = S // R
    dk = HGRN_EXPAND

    def body(q_ref, f_ref, i_ref, go_ref, o_ref, dy_ref, st_ref, lb_ref, ng_ref,
             dp_ref, red_ref, dst, g_scr, q_scr, dq_scr, acc):
        b, i = pl.program_id(1), pl.program_id(2)

        @pl.when(i == 0)
        def _():
            dst[...] = jnp.zeros((dk, dk), F32)

        @pl.when(jnp.logical_and(b == 0, i == 0))
        def _():
            acc[...] = jnp.zeros((2, dk), F32)

        lbv = lb_ref[...]
        ngv = ng_ref[...]
        srow = lax.broadcasted_iota(jnp.int32, (C, LANES), 0)
        lane = lax.broadcasted_iota(jnp.int32, (C, LANES), 1)

        def chunk(c, slot):
            r0 = pl.multiple_of(c * C, C)
            rows = pl.ds(r0, C)
            qr, fr, vv, gv = q_ref[rows, :], f_ref[rows, :], i_ref[rows, :], go_ref[rows, :]
            q, kk, lf, sf, snf, f, sq = _hgrn_gates(qr, fr, lbv)
            o = o_ref[rows, :]
            dy = dy_ref[rows, :]
            rinv = lax.rsqrt(jnp.mean(o * o, axis=-1, keepdims=True) + RMS_EPS)
            on = o * rinv
            sgv = _sig(gv)
            dz = dy * (gv * sgv)
            dp_ref[3, rows, :] = (dy * on * ngv * (sgv * (1.0 + gv * (1.0 - sgv)))).astype(BF16)
            acc[pl.ds(1, 1), :] += jnp.sum(dz * on, axis=0, keepdims=True)
            don = dz * ngv
            do = rinv * (don - on * jnp.mean(don * on, axis=-1, keepdims=True))
            G = _scan_rows(lf)
            g_scr[slot] = G
            q_scr[slot] = q
            gl = g_scr[slot, pl.ds(C - 1, 1), :]
            egl = jnp.exp(gl)
            eG = jnp.exp(G)
            eK = jnp.exp(gl - G)
            qg, kg = q * eG, kk * eK
            stv = st_ref[c]
            dsv = dst[...]
            dqg = _dot(do, stv)
            do_pad = jnp.concatenate([do, jnp.zeros((LANES - C, dk), F32)], axis=0)
            dat = _dot(vv, do_pad, "nt")
            dkg = _dot(vv, dsv)
            dgl = egl * jnp.sum(stv * dsv, axis=0, keepdims=True) + jnp.sum(dkg * kg, axis=0, keepdims=True)
            at = jnp.zeros((C, LANES), F32)
            dki = jnp.zeros((C, dk), F32)
            for t in range(C):
                e = jnp.where(srow <= t, jnp.exp(g_scr[slot, pl.ds(t, 1), :] - G), 0.0)
                qt = q_scr[slot, pl.ds(t, 1), :]
                at = jnp.where(lane == t, jnp.sum(e * kk * qt, axis=-1, keepdims=True), at)
                z = e * jnp.sum(jnp.where(lane == t, dat, 0.0), axis=-1, keepdims=True)
                dq_scr[slot, pl.ds(t, 1), :] = jnp.sum(z * kk, axis=0, keepdims=True)
                dki = dki + z * qt
            dqi = dq_scr[slot]
            dp_ref[2, rows, :] = (_dot(at, do_pad) + _dot(kg, dsv, "nt")).astype(BF16)
            dst[...] = dsv * egl + _dot(do, qg, "tn")
            dq = dqg * eG + dqi
            dkk = dkg * eK + dki
            dG = dqg * qg - dkg * kg + q * dqi - kk * dki
            dG = dG + jnp.where(srow == C - 1, dgl, 0.0)
            dlf = _scan_rows(dG, reverse=True)
            dsf = (1.0 - lbv) * sf * snf
            dp_ref[1, rows, :] = (dlf * dsf / f - dkk * dsf).astype(BF16)
            acc[pl.ds(0, 1), :] += jnp.sum(dlf * snf / f - dkk * snf, axis=0, keepdims=True)
            dp_ref[0, rows, :] = (dq * (sq * (1.0 + qr * (1.0 - sq)))).astype(BF16)

        def pair(c2, carry):
            chunk(ncb - 1 - 2 * c2, 0)
            chunk(ncb - 2 - 2 * c2, 1)
            return carry

        lax.fori_loop(0, ncb // 2, pair, 0)

        @pl.when(jnp.logical_and(b == B - 1, i == nb - 1))
        def _():
            red_ref[0] = acc[pl.ds(0, 1), :]
            red_ref[1] = acc[pl.ds(1, 1), :]

    def col(k):
        return pl.BlockSpec((None, R, dk), lambda h, b, i: (b, nb - 1 - i, h + k * H))

    vec = pl.BlockSpec((1, dk), lambda h, b, i: (0, h))
    row = pl.BlockSpec((None, R, dk), lambda h, b, i: (b, nb - 1 - i, h))
    stsp = pl.BlockSpec((None, None, ncb, dk, dk), lambda h, b, i: (b, h, nb - 1 - i, 0, 0))
    outs = _call(body, "hgrn_bwd", (H, B, nb), [col(0), col(1), col(2), col(3), row, row, stsp, vec, vec],
                 [pl.BlockSpec((4, None, R, dk), lambda h, b, i: (0, b, nb - 1 - i, h)),
                  pl.BlockSpec((2, 1, dk), lambda h, b, i: (0, 0, h))],
                 [_sds((4, B, S, D), BF16), _sds((2, 1, D), F32)],
                 [pltpu.VMEM((dk, dk), F32), pltpu.VMEM((2, C, dk), F32), pltpu.VMEM((2, C, dk), F32),
                  pltpu.VMEM((2, C, dk), F32), pltpu.VMEM((2, dk), F32)],
                 sem=("parallel", "arbitrary", "arbitrary"))(proj3, proj3, proj3, proj3, o3, dy3, states, lb, ng)
    return outs


def _hgrn_mixer_fwd(x, B, p):
    T, D = x.shape
    S = T // B
    h = _rms_fwd("hgrn_rms", x, p["mix_norm"][2:3])
    proj = _mm_nn("hgrn_in", h, p["hgrn_w_in"], F32)
    lb = _lb_fwd(p["hgrn_lb_logits"])
    y, o, states = _hgrn_fwd(proj.reshape(B, S, 4 * D), lb, p["hgrn_norm"])
    y = y.reshape(T, D)
    return _mm_res("hgrn_out", y, p["hgrn_w_out"], x, 1.0), (x, h, proj, lb, y, o, states)


def _hgrn_mixer_bwd(dxp, saved, B, p):
    x, h, proj, lb, y, o, states = saved
    T, D = x.shape
    S = T // B
    dx, dxb = dxp
    g = {}
    dy = _mm_nt("hgrn_dy", [(dxb, p["hgrn_w_out"])], F32)
    g["hgrn_w_out"] = _mm_tn("hgrn_dwout", y, dxb)
    dp, red = _hgrn_bwd(proj.reshape(B, S, 4 * D), o, dy.reshape(B, S, D), states, lb, p["hgrn_norm"])
    dp = dp.reshape(4, T, D)
    g["hgrn_norm"] = red[1]
    g["hgrn_lb_logits"] = _lb_bwd(p["hgrn_lb_logits"], red[0])
    dwin = _mm_tn3("hgrn_dwin", h, dp)
    g["hgrn_w_in"] = jnp.moveaxis(dwin, 0, 1).reshape(1, D, 4 * D)
    dh = _mm_nt3("hgrn_dh", dp, p["hgrn_w_in4"])
    dx2, dg = _rms_bwd("hgrn_drms", dh, x, p["mix_norm"][2:3], dx)
    return dx2, dg, g


POOL_HALO = 16


def _pool_fwd(x3, g):
    B, S, D = x3.shape
    tt = _tile(S, 256, POOL_HALO)
    hb = tt // POOL_HALO
    G = D // len(POOL_WINDOWS)

    def body(x_ref, halo_ref, g_ref, m_ref):
        i = pl.program_id(1)

        def norm(xv):
            return xv * lax.rsqrt(jnp.mean(xv * xv, axis=-1, keepdims=True) + RMS_EPS) * g_ref[...]

        hm = norm(x_ref[...])
        ext = jnp.concatenate([jnp.where(i > 0, norm(halo_ref[...]), 0.0), hm], axis=0)
        pos = (i * tt + lax.broadcasted_iota(jnp.int32, (tt, 1), 0) + 1).astype(F32)
        for gi, win in enumerate(POOL_WINDOWS):
            s = ext[:, gi * G:(gi + 1) * G]
            w = 1
            while w < win:
                s = s + _roll(s, w)
                w *= 2
            m_ref[:, gi * G:(gi + 1) * G] = (s[POOL_HALO:] / jnp.minimum(pos, float(win)) - hm[:, gi * G:(gi + 1) * G]).astype(BF16)

    main = pl.BlockSpec((None, tt, D), lambda b, i: (b, i, 0))
    halo = pl.BlockSpec((None, POOL_HALO, D), lambda b, i: (b, jnp.maximum(i * hb - 1, 0), 0))
    return _call(body, "pool_fwd", (B, S // tt), [main, halo, pl.BlockSpec((1, D), lambda b, i: (0, 0))], main,
                 _sds((B, S, D), BF16), sem=("parallel", "parallel"))(x3, x3, g)


def _pool_bwd(dm3):
    B, S, D = dm3.shape
    tt = _tile(S, 256, POOL_HALO)
    hb = tt // POOL_HALO
    nt = S // tt
    G = D // len(POOL_WINDOWS)
    L = tt + POOL_HALO

    def body(dm_ref, nxt_ref, dh_ref):
        i = pl.program_id(1)
        posm = (i * tt + lax.broadcasted_iota(jnp.int32, (tt, 1), 0) + 1).astype(F32)
        posn = ((i + 1) * tt + lax.broadcasted_iota(jnp.int32, (POOL_HALO, 1), 0) + 1).astype(F32)
        for gi, win in enumerate(POOL_WINDOWS):
            sl = slice(gi * G, (gi + 1) * G)
            dm = dm_ref[:, sl]
            s = jnp.concatenate([dm / jnp.minimum(posm, float(win)),
                                 jnp.where(i < nt - 1, nxt_ref[:, sl] / jnp.minimum(posn, float(win)), 0.0)], axis=0)
            w = 1
            while w < win:
                s = s + _roll(s, L - w)
                w *= 2
            dh_ref[:, sl] = s[:tt] - dm

    main = pl.BlockSpec((None, tt, D), lambda b, i: (b, i, 0))
    nxt = pl.BlockSpec((None, POOL_HALO, D), lambda b, i: (b, jnp.minimum((i + 1) * hb, S // POOL_HALO - 1), 0))
    return _call(body, "pool_bwd", (B, nt), [main, nxt], main, _sds((B, S, D), F32), sem=("parallel", "parallel"))(dm3, dm3)


def _pool_mixer_fwd(x, B, p):
    T, D = x.shape
    NG = len(POOL_WINDOWS)
    G = D // NG
    tm = _tile(T, 512)
    m = _pool_fwd(x.reshape(B, T // B, D), p["mix_norm"][3:4]).reshape(T, D)

    def epi(res, ex, outs):
        outs[0][...] = ex[1][...] + res * ex[0][...]

    blk = pl.BlockSpec((tm, G), lambda i, g: (i, g))
    x2 = _mm("pool_out", "nn", [(m, p["pool_w4"])], [(blk, pl.BlockSpec((None, G, G), lambda i, g: (g, 0, 0)))],
             (T // tm, NG), None, [_sds((T, D), F32)], [blk], extras=[p["pool_scale"], x],
             extra_specs=[pl.BlockSpec((1, G), lambda i, g: (0, g)), blk], epilogue=epi)[0]
    return x2, (x, m)


def _pool_mixer_bwd(dxp, saved, B, p):
    x, m = saved
    dx, dxb = dxp
    T, D = x.shape
    NG = len(POOL_WINDOWS)
    G = D // NG
    tm = _tile(T, 512)
    g = {}

    def epi(zz, ex, outs):
        dy = ex[0][...]
        outs[0][...] = (dy * ex[1][...]).astype(BF16)
        part = jnp.sum(dy * zz, axis=0, keepdims=True)

        @pl.when(pl.program_id(1) == 0)
        def _():
            outs[1][...] = part

        @pl.when(pl.program_id(1) > 0)
        def _():
            outs[1][...] += part

    blk = pl.BlockSpec((tm, G), lambda g_, i: (i, g_))
    wsp = pl.BlockSpec((None, G, G), lambda g_, i: (g_, 0, 0))
    vec = pl.BlockSpec((1, G), lambda g_, i: (0, g_))
    dz, dsc = _mm("pool_dz", "nn", [(m, p["pool_w4"])], [(blk, wsp)], (NG, T // tm), None,
                  [_sds((T, D), BF16), _sds((1, D), F32)], [blk, vec], extras=[dx, p["pool_scale"]], extra_specs=[blk, vec],
                  epilogue=epi, sem=("parallel", "arbitrary"))
    g["pool_scale"] = dsc
    tk = _tile(T, 512)
    kb = pl.BlockSpec((tk, G), lambda g_, k: (k, g_))
    g["pool_w"] = _mm("pool_dw", "tn", [(m, dz)], [(kb, kb)], (NG, T // tk), 1, [_sds((NG, G, G), F32)],
                      [pl.BlockSpec((None, G, G), lambda g_, k: (g_, 0, 0))], acc_shape=(G, G))[0][None]
    blk2 = pl.BlockSpec((tm, G), lambda i, g_: (i, g_))
    dm = _mm("pool_dm", "nt", [(dz, p["pool_w4"])], [(blk2, pl.BlockSpec((None, G, G), lambda i, g_: (g_, 0, 0)))],
             (T // tm, NG), None, [_sds((T, D), F32)], [blk2])[0]
    dh = _pool_bwd(dm.reshape(B, T // B, D)).reshape(T, D)
    dx2, dg = _rms_bwd("pool_drms", dh, x, p["mix_norm"][3:4], dx)
    return dx2, dg, g


_MIXERS = ((_conv_mixer_fwd, _conv_mixer_bwd), (_fox_mixer_fwd, _fox_mixer_bwd), (_hgrn_mixer_fwd, _hgrn_mixer_bwd),
           (_pool_mixer_fwd, _pool_mixer_bwd))


def _local_step(x3, tgt3, w):
    B, S, D = x3.shape
    T = B * S
    depth = w["ffn_norm"].shape[0]
    F = w["ffn_w_gate"].shape[-1]
    H = FOX_HEADS
    p = dict(w)
    p["wgu"] = jnp.stack((w["ffn_w_gate"], w["ffn_w_up"]), axis=2)
    p["conv_w_in2"] = w["conv_w_in"][0].reshape(D, 2, D).transpose(1, 0, 2)
    p["conv_b_in2"] = w["conv_b_in"].reshape(2, 1, D)
    p["conv_dw32"] = jnp.pad(w["conv_dw"][0], ((0, HALO - CONV_TAPS), (0, 0)))
    p["conv_w_out"] = w["conv_w_out"][0]
    p["fox_w_qkv"] = w["fox_w_in"][0][:, :3 * D]
    p["fox_w_f"] = jnp.pad(w["fox_w_in"][0][:, 3 * D:], ((0, 0), (0, LANES - H)))
    p["fox_b_f128"] = jnp.pad(w["fox_b_f"], ((0, 0), (0, LANES - H)))
    p["fox_w_out"] = w["fox_w_out"][0]
    p["hgrn_w_in"] = w["hgrn_w_in"][0]
    p["hgrn_w_in4"] = w["hgrn_w_in"][0].reshape(D, 4, D).transpose(1, 0, 2)
    p["hgrn_w_out"] = w["hgrn_w_out"][0]
    p["pool_w4"] = w["pool_w"][0]

    x = x3.reshape(T, D)
    saved = []
    for i in range(depth):
        x, s0 = _ffn_fwd(f"{i}a", x, w["ffn_norm"][i, 0:1], p["wgu"][i, 0], w["ffn_w_down"][i, 0])
        x, s1 = _MIXERS[i % 4][0](x, B, p)
        x, s2 = _ffn_fwd(f"{i}b", x, w["ffn_norm"][i, 1:2], p["wgu"][i, 1], w["ffn_w_down"][i, 1])
        saved.append((s0, s1, s2))
    loss, dx, dfinal = _loss_head(x, w["final_norm"].reshape(1, D), tgt3.reshape(T, D))

    g = {"final_norm": dfinal}
    dffn_norm = [[None, None] for _ in range(depth)]
    dwgu = [[None, None] for _ in range(depth)]
    dwd = [[None, None] for _ in range(depth)]
    dmix = [None] * depth
    for i in reversed(range(depth)):
        s0, s1, s2 = saved[i]
        dx, dffn_norm[i][1], dwgu[i][1], dwd[i][1] = _ffn_bwd(f"{i}b", dx, s2, w["ffn_norm"][i, 1:2], p["wgu"][i, 1],
                                                            w["ffn_w_down"][i, 1])
        dx, dmix[i], gm = _MIXERS[i % 4][1](dx, s1, B, p)
        g.update(gm)
        dx, dffn_norm[i][0], dwgu[i][0], dwd[i][0] = _ffn_bwd(f"{i}a", dx, s0, w["ffn_norm"][i, 0:1], p["wgu"][i, 0],
                                                            w["ffn_w_down"][i, 0])
    g["ffn_norm"] = jnp.stack([jnp.stack([a[0], b[0]]) for a, b in dffn_norm])
    g["ffn_w_gate"] = jnp.stack([jnp.stack([a[0], b[0]]) for a, b in dwgu])
    g["ffn_w_up"] = jnp.stack([jnp.stack([a[1], b[1]]) for a, b in dwgu])
    g["ffn_w_down"] = jnp.stack([jnp.stack([a, b]) for a, b in dwd])
    g["mix_norm"] = jnp.concatenate(dmix, axis=0)
    return loss, dx[0].reshape(B, S, D), {n: g[n].reshape(w[n].shape) for n in WEIGHTS}


def _adamw(name, w, m, v, parts):
    shape = w.shape
    cols = shape[-1]
    rows = w.size // cols
    tr = _tile(rows, max(8, (1 << 19) // cols))
    n = len(parts)
    c1 = 1.0 - ADAM_B1 ** ADAM_STEP
    c2 = 1.0 - ADAM_B2 ** ADAM_STEP

    def body(*refs):
        w_ref, m_ref, v_ref = refs[:3]
        g_ref, d_ref, m2_ref, v2_ref = refs[3 + n:]
        g = refs[3][...].astype(F32)
        for k in range(1, n):
            g = g + refs[3 + k][...].astype(F32)
        m2 = ADAM_B1 * m_ref[...] + (1.0 - ADAM_B1) * g
        v2 = ADAM_B2 * v_ref[...] + (1.0 - ADAM_B2) * (g * g)
        g_ref[...] = g
        m2_ref[...] = m2
        v2_ref[...] = v2
        d_ref[...] = -ADAM_LR * ((m2 / c1) / (jnp.sqrt(v2 / c2) + ADAM_EPS) + ADAM_WD * w_ref[...])

    blk = pl.BlockSpec((tr, cols), lambda i: (i, 0))
    outs = _call(body, name, (rows // tr,), [blk] * (3 + n), [blk] * 4, [_sds((rows, cols), F32)] * 4, sem=("parallel",))(
        *[t.reshape(rows, cols) for t in (w, m, v, *parts)])
    return [o.reshape(shape) for o in outs]


ANY = pl.BlockSpec(memory_space=pl.ANY)
FLAT_COLS = 1024


def _place():
    return lax.axis_index("x"), lax.axis_index("y"), lax.axis_index("c")


def _all_gather(name, xs):
    K = len(xs)

    def body(*refs):
        x_refs, out_refs = refs[:K], refs[K:2 * K]
        send_sems, recv_sems, local_sems = refs[2 * K:]
        xi, yi, ci = _place()
        me, sibling = (xi, yi, ci), (xi, yi, 1 - ci)
        chips = [(1 - xi, yi), (xi, 1 - yi), (1 - xi, 1 - yi)]

        def slot(a, px, py, pc):
            return out_refs[a].at[4 * px + 2 * py + pc]

        def copy(a, k, block, to, own=False):
            return pltpu.make_async_remote_copy(src_ref=x_refs[a] if own else slot(a, *block), dst_ref=slot(a, *block),
                                                send_sem=send_sems.at[7 * a + k], recv_sem=recv_sems.at[7 * a + k],
                                                device_id=to, device_id_type=MESH)

        mine = [pltpu.make_async_copy(x_refs[a], slot(a, *me), local_sems.at[a]) for a in range(K)]
        first = [copy(a, 1 + j, me, (*chip, ci), own=True) for j, chip in enumerate(chips) for a in range(K)]
        first += [copy(a, 0, me, sibling, own=True) for a in range(K)]
        for cp in mine + first:
            cp.start()
        passed = []
        for j, chip in enumerate(chips):
            for a in range(K):
                copy(a, 1 + j, (*chip, ci), me).wait_recv()
                passed.append(copy(a, 4 + j, (*chip, ci), sibling))
                passed[-1].start()
        for a in range(K):
            copy(a, 0, sibling, me).wait_recv()
            for j, chip in enumerate(chips):
                copy(a, 4 + j, (*chip, 1 - ci), me).wait_recv()
        for cp in first + passed:
            cp.wait_send()
        for cp in mine:
            cp.wait()

    return pl.pallas_call(body, name=name, out_shape=[_sds((N_DEV,) + x.shape, x.dtype) for x in xs], in_specs=[ANY] * K,
                          out_specs=[ANY] * K,
                          scratch_shapes=[pltpu.SemaphoreType.DMA((7 * K,)), pltpu.SemaphoreType.DMA((7 * K,)),
                                          pltpu.SemaphoreType.DMA((K,))])(*xs)


def _swap_sibling(name, ts):
    K = len(ts)

    def body(*refs):
        t_refs, out_refs, send_sems, recv_sems = refs[:K], refs[K:2 * K], refs[2 * K], refs[2 * K + 1]
        xi, yi, ci = _place()
        cps = [pltpu.make_async_remote_copy(src_ref=t_refs[a], dst_ref=out_refs[a], send_sem=send_sems.at[a],
                                            recv_sem=recv_sems.at[a], device_id=(xi, yi, 1 - ci), device_id_type=MESH)
               for a in range(K)]
        for cp in cps:
            cp.start()
        for cp in cps:
            cp.wait()

    return pl.pallas_call(body, name=name, out_shape=[_sds(t.shape, t.dtype) for t in ts], in_specs=[ANY] * K,
                          out_specs=[ANY] * K,
                          scratch_shapes=[pltpu.SemaphoreType.DMA((K,)), pltpu.SemaphoreType.DMA((K,))])(*ts)


def _scatter_chips(name, ts):
    K = len(ts)

    def body(*refs):
        t_refs, out_refs, send_sems, recv_sems = refs[:K], refs[K:2 * K], refs[2 * K], refs[2 * K + 1]
        xi, yi, ci = _place()
        chips = [(1 - xi, yi), (xi, 1 - yi), (1 - xi, 1 - yi)]
        cps = [pltpu.make_async_remote_copy(src_ref=t_refs[a].at[2 * cx + cy], dst_ref=out_refs[a].at[j],
                                            send_sem=send_sems.at[3 * a + j], recv_sem=recv_sems.at[3 * a + j],
                                            device_id=(cx, cy, ci), device_id_type=MESH)
               for j, (cx, cy) in enumerate(chips) for a in range(K)]
        for cp in cps:
            cp.start()
        for cp in cps:
            cp.wait()

    return pl.pallas_call(body, name=name, out_shape=[_sds((3,) + t.shape[1:], t.dtype) for t in ts], in_specs=[ANY] * K,
                          out_specs=[ANY] * K,
                          scratch_shapes=[pltpu.SemaphoreType.DMA((3 * K,)), pltpu.SemaphoreType.DMA((3 * K,))])(*ts)


def _add_bf16(name, a, b):
    shape = a.shape
    N, C = shape[0], shape[-1]
    R = a.size // (N * C)
    tr = _tile(R, max(8, (1 << 19) // C))

    def body(a_ref, b_ref, o_ref):
        o_ref[...] = (a_ref[...].astype(F32) + b_ref[...].astype(F32)).astype(BF16)

    blk = pl.BlockSpec((None, tr, C), lambda n, i: (n, i, 0))
    return _call(body, name, (N, R // tr), [blk, blk], blk, _sds((N, R, C), BF16), sem=("parallel", "parallel"))(
        a.reshape(N, R, C), b.reshape(N, R, C)).reshape(shape)


def _sum_parts(name, parts):
    N, C = parts.shape

    def body(p_ref, o_ref):
        s = p_ref[pl.ds(0, 1), :]
        for d in range(1, N):
            s = s + p_ref[pl.ds(d, 1), :]
        o_ref[...] = s

    return _call(body, name, (1,), [pl.BlockSpec((N, C), lambda i: (0, 0))], pl.BlockSpec((1, C), lambda i: (0, 0)),
                 _sds((1, C), F32))(parts)


def _flat(parts, dtype, lead=()):
    flat = jnp.concatenate([t.reshape(lead + (-1,)).astype(dtype) for t in parts], axis=-1)
    n = flat.shape[-1]
    unit = 16 * FLAT_COLS
    padded = -(-n // unit) * unit
    flat = jnp.pad(flat, [(0, 0)] * len(lead) + [(0, padded - n)])
    return flat.reshape(lead + (padded // FLAT_COLS, FLAT_COLS))


def _unflat(flat, shapes, lead=()):
    flat = flat.reshape(lead + (-1,))
    out, off = [], 0
    for shp in shapes:
        n = math.prod(shp)
        out.append(flat[..., off:off + n].reshape(lead + tuple(shp)))
        off += n
    return out


def _dev_major(full, ax):
    shp = full.shape
    return jnp.moveaxis(full.reshape(shp[:ax] + (N_DEV, shp[ax] // N_DEV) + shp[ax + 1:]), ax, 0)


def _from_dev_major(blocks, ax):
    t = jnp.moveaxis(blocks, 0, ax)
    shp = t.shape
    return t.reshape(shp[:ax] + (shp[ax] * shp[ax + 1],) + shp[ax + 2:])


def kernel(x, *rest):
    nw = len(WEIGHTS)
    w = dict(zip(WEIGHTS, rest[:nw]))
    tgt = rest[nw]
    m = dict(zip(WEIGHTS, rest[nw + 1:2 * nw + 1]))
    v = dict(zip(WEIGHTS, rest[2 * nw + 1:3 * nw + 1]))
    xi, yi, ci = _place()
    dev = 4 * xi + 2 * yi + ci

    big = _all_gather("gather_matrices", [w[n].astype(BF16) for n in BIG])
    small = _all_gather("gather_vectors", [_flat([w[n] for n in SMALL], F32)])[0]
    full = {n: w[n] for n in REPL}
    for n, blocks in zip(BIG, big):
        full[n] = _from_dev_major(blocks, SHARDED[n])
    for n, blocks in zip(SMALL, _unflat(small, [w[n].shape for n in SMALL], (N_DEV,))):
        full[n] = _from_dev_major(blocks, SHARDED[n])

    loss, gx, g = _local_step(x, tgt, full)

    vec_names = SMALL + REPL
    vec = _all_gather("gather_vector_grads", [_flat([g[n] for n in vec_names] + [loss], F32)])[0]
    *vec_list, losses = _unflat(vec, [g[n].shape for n in vec_names] + [(1, 1)], (N_DEV,))
    vec_parts = dict(zip(vec_names, vec_list))
    loss = _sum_parts("loss_sum", losses.reshape(N_DEV, 1))[0, 0]

    keep, send = [], []
    for n in BIG:
        gd = _dev_major(g[n], SHARDED[n]).astype(BF16)
        gd = gd.reshape((4, 2) + gd.shape[1:])
        keep.append(lax.dynamic_index_in_dim(gd, ci, 1, keepdims=False))
        send.append(lax.dynamic_index_in_dim(gd, 1 - ci, 1, keepdims=False))
    got_sib = _swap_sibling("grads_to_sibling", send)
    pair = [_add_bf16(f"grad_pair_sum_{n}", a, b) for n, a, b in zip(BIG, keep, got_sib)]
    got_chips = _scatter_chips("grads_to_chips", pair)
    chip = 2 * xi + yi

    res = {}
    for k, n in enumerate(BIG):
        parts = [lax.dynamic_index_in_dim(keep[k], chip, 0, keepdims=False),
                 lax.dynamic_index_in_dim(got_sib[k], chip, 0, keepdims=False), got_chips[k][0], got_chips[k][1], got_chips[k][2]]
        res[n] = _adamw(f"adamw_{n}", w[n], m[n], v[n], parts)
    for n in vec_names:
        parts = vec_parts[n]
        if n in SHARDED:
            ax = SHARDED[n]
            parts = lax.dynamic_slice_in_dim(parts, dev * w[n].shape[ax], w[n].shape[ax], ax + 1)
        res[n] = _adamw(f"adamw_{n}", w[n], m[n], v[n], [parts[d] for d in range(N_DEV)])
    return (loss, gx, *[res[n][0] for n in WEIGHTS], *[res[n][1] for n in WEIGHTS], *[res[n][2] for n in WEIGHTS],
            *[res[n][3] for n in WEIGHTS])
```

```python
import functools
import math

import jax
import jax.numpy as jnp
from jax import lax
from jax.experimental import pallas as pl
from jax.experimental.pallas import tpu as pltpu

F32 = jnp.float32
BF16 = jnp.bfloat16
MESH = pl.DeviceIdType.MESH

N_DEV = 8
RMS_EPS = 1e-6
LN_EPS = 1e-5
FOX_HEADS = 16
HGRN_EXPAND = 128
HGRN_CHUNK = 32
POOL_WINDOWS = (2, 4, 8, 16)
ADAM_LR, ADAM_B1, ADAM_B2, ADAM_EPS, ADAM_WD, ADAM_STEP = 0.001, 0.9, 0.999, 1e-08, 0.01, 10
LANES = 128
VMEM_LIMIT_MB = 48

SHARDED = dict(
    ffn_norm=2, ffn_w_gate=3, ffn_w_up=3, ffn_w_down=2, conv_w_in=2, conv_dw=2, conv_w_out=1, fox_w_in=2, fox_w_out=1,
    hgrn_w_in=2, hgrn_norm=1, hgrn_w_out=1, pool_w=2, pool_scale=1)
BIG = ("ffn_w_gate", "ffn_w_up", "ffn_w_down", "conv_w_in", "conv_w_out", "fox_w_in", "fox_w_out", "hgrn_w_in",
       "hgrn_w_out", "pool_w")
SMALL = ("ffn_norm", "conv_dw", "hgrn_norm", "pool_scale")
REPL = ("mix_norm", "final_norm", "conv_b_in", "conv_dw_b", "conv_ln_g", "conv_ln_b", "fox_b_f", "hgrn_lb_logits")
WEIGHTS = ("ffn_norm", "ffn_w_gate", "ffn_w_up", "ffn_w_down", "mix_norm", "final_norm", "conv_w_in", "conv_b_in",
           "conv_dw", "conv_dw_b", "conv_ln_g", "conv_ln_b", "conv_w_out", "fox_w_in", "fox_b_f", "fox_w_out",
           "hgrn_w_in", "hgrn_lb_logits", "hgrn_norm", "hgrn_w_out", "pool_w", "pool_scale")


def _tile(n, pref, mult=8):
    if n <= pref:
        return n
    for t in range(pref - pref % mult, 0, -mult):
        if n % t == 0:
            return t
    return n


def _sig(x):
    return 1.0 / (1.0 + jnp.exp(-x))


def _call(body, name, grid, in_specs, out_specs, out_shape, scratch=(), sem=None):
    params = dict(vmem_limit_bytes=VMEM_LIMIT_MB << 20)
    if sem is not None:
        params["dimension_semantics"] = sem
    return pl.pallas_call(body, name=name, grid=grid, in_specs=in_specs, out_specs=out_specs, out_shape=out_shape,
                          scratch_shapes=list(scratch), compiler_params=pltpu.CompilerParams(**params))


FLIPS = [(fx, fy, fc) for fx in (0, 1) for fy in (0, 1) for fc in (0, 1)][1:]


def _carry_call(body, name, grid, in_specs, out_specs, out_shape, scratch, kind, arrays):
    K = len(arrays)
    n_in, n_out, n_scr = len(in_specs), len(out_shape), len(scratch)
    n_peer = len(FLIPS)
    if kind == "gather":
        landed = [_sds((N_DEV,) + a.shape, a.dtype) for a in arrays]
    else:
        landed = [_sds((n_peer,) + a.shape[1:], a.dtype) for a in arrays]

    def wrapped(*refs):
        ins, sent = refs[:n_in], refs[n_in:n_in + K]
        outs, got = refs[n_in + K:n_in + K + n_out], refs[n_in + K + n_out:n_in + 2 * K + n_out]
        scr = refs[n_in + 2 * K + n_out:n_in + 2 * K + n_out + n_scr]
        send_sems, recv_sems, local_sems = refs[-3:]
        ids = [pl.program_id(d) for d in range(len(grid))]
        first = functools.reduce(jnp.logical_and, [i == 0 for i in ids])
        last = functools.reduce(jnp.logical_and, [i == n - 1 for i, n in zip(ids, grid)])
        xi, yi, ci = _place()
        me = 4 * xi + 2 * yi + ci

        def copies():
            cps = []
            for a in range(K):
                for k, (fx, fy, fc) in enumerate(FLIPS):
                    px, py, pc = (xi + fx) % 2, (yi + fy) % 2, (ci + fc) % 2
                    if kind == "gather":
                        src, dst = sent[a], got[a].at[me]
                    else:
                        src, dst = sent[a].at[4 * px + 2 * py + pc], got[a].at[k]
                    cps.append(pltpu.make_async_remote_copy(src_ref=src, dst_ref=dst, send_sem=send_sems.at[n_peer * a + k],
                                                            recv_sem=recv_sems.at[n_peer * a + k], device_id=(px, py, pc),
                                                            device_id_type=MESH))
            return cps

        def own():
            return [pltpu.make_async_copy(sent[a], got[a].at[me], local_sems.at[a]) for a in range(K)] if kind == "gather" else []

        @pl.when(first)
        def _():
            for cp in copies() + own():
                cp.start()

        body(*ins, *outs, *scr)

        @pl.when(last)
        def _():
            for cp in copies() + own():
                cp.wait()

    params = dict(vmem_limit_bytes=VMEM_LIMIT_MB << 20, dimension_semantics=("arbitrary",) * len(grid))
    res = pl.pallas_call(wrapped, name=name, grid=grid, in_specs=list(in_specs) + [ANY] * K,
                         out_specs=list(out_specs) + [ANY] * K, out_shape=list(out_shape) + landed,
                         scratch_shapes=list(scratch) + [pltpu.SemaphoreType.DMA((n_peer * K,)), pltpu.SemaphoreType.DMA((n_peer * K,)),
                                                         pltpu.SemaphoreType.DMA((K,))],
                         compiler_params=pltpu.CompilerParams(**params))
    return res


def _sds(shape, dtype):
    return jax.ShapeDtypeStruct(tuple(shape), dtype)


_DN = {"nn": (((1,), (0,)), ((), ())), "nt": (((1,), (1,)), ((), ())), "tn": (((0,), (0,)), ((), ()))}


def _dot(a, b, mode="nn"):
    return lax.dot_general(a.astype(BF16), b.astype(BF16), _DN[mode], preferred_element_type=F32)


def _roll(x, shift, axis=0):
    n = x.shape[axis]
    shift = shift % n
    return x if shift == 0 else pltpu.roll(x, shift, axis)


def _scan_rows(x, reverse=False):
    n = x.shape[0]
    row = lax.broadcasted_iota(jnp.int32, x.shape, 0)
    sh = 1
    while sh < n:
        if reverse:
            x = x + jnp.where(row < n - sh, _roll(x, n - sh), 0.0)
        else:
            x = x + jnp.where(row >= sh, _roll(x, sh), 0.0)
        sh *= 2
    return x


def _mm(name, mode, pairs, pair_specs, grid, k_axis, out_shapes, out_specs, acc_shape=None, extras=(), extra_specs=(),
        epilogue=None, alpha=1.0, sem=None):
    npair, nex, nout = len(pairs), len(extras), len(out_shapes)
    nk = grid[k_axis] if k_axis is not None else 1

    def body(*refs):
        prs = refs[:2 * npair]
        ex = refs[2 * npair:2 * npair + nex]
        outs = refs[2 * npair + nex:2 * npair + nex + nout]

        def partial():
            p = None
            for i in range(npair):
                d = _dot(prs[2 * i][...], prs[2 * i + 1][...], mode)
                p = d if p is None else p + d
            return p

        def finish(res):
            if alpha != 1.0:
                res = res * alpha
            if epilogue is None:
                outs[0][...] = res.astype(outs[0].dtype)
            else:
                epilogue(res, ex, outs)

        if k_axis is None:
            finish(partial())
        else:
            acc = refs[-1]
            k = pl.program_id(k_axis)

            @pl.when(k == 0)
            def _():
                acc[...] = partial()

            @pl.when(k > 0)
            def _():
                acc[...] += partial()

            @pl.when(k == nk - 1)
            def _():
                finish(acc[...])

    if sem is None:
        sem = tuple("arbitrary" if i == k_axis else "parallel" for i in range(len(grid)))
    scratch = [pltpu.VMEM(acc_shape, F32)] if k_axis is not None else []
    flat = [t for p in pairs for t in p]
    flat_specs = [s for p in pair_specs for s in p]
    return _call(body, name, grid, flat_specs + list(extra_specs), out_specs, out_shapes, scratch, sem)(*flat, *extras)


def _mm_nn(name, a, b, out_dtype, tm=512, tn=512):
    M, K = a.shape
    N = b.shape[1]
    tm, tn = _tile(M, tm), _tile(N, tn, LANES)
    return _mm(name, "nn", [(a, b)], [(pl.BlockSpec((tm, K), lambda i, j: (i, 0)), pl.BlockSpec((K, tn), lambda i, j: (0, j)))],
               (M // tm, N // tn), None, [_sds((M, N), out_dtype)], [pl.BlockSpec((tm, tn), lambda i, j: (i, j))])[0]


def _mm_res(name, u, w, x, alpha, tm=512):
    T, K = u.shape
    D = w.shape[1]
    tm = _tile(T, tm)

    def epi(res, ex, outs):
        outs[0][...] = ex[0][...] + res

    return _mm(name, "nn", [(u, w)], [(pl.BlockSpec((tm, K), lambda i: (i, 0)), pl.BlockSpec((K, D), lambda i: (0, 0)))],
               (T // tm,), None, [_sds((T, D), F32)], [pl.BlockSpec((tm, D), lambda i: (i, 0))],
               extras=[x], extra_specs=[pl.BlockSpec((tm, D), lambda i: (i, 0))], epilogue=epi, alpha=alpha)[0]


def _mm_nt(name, pairs, out_dtype, tm=512, tn=512, alpha=1.0):
    M = pairs[0][0].shape[0]
    N = pairs[0][1].shape[0]
    tm, tn = _tile(M, tm), _tile(N, tn, LANES)
    specs = [(pl.BlockSpec((tm, a.shape[1]), lambda i, j: (i, 0)), pl.BlockSpec((tn, b.shape[1]), lambda i, j: (j, 0)))
             for a, b in pairs]
    return _mm(name, "nt", pairs, specs, (M // tm, N // tn), None, [_sds((M, N), out_dtype)],
               [pl.BlockSpec((tm, tn), lambda i, j: (i, j))], alpha=alpha)[0]


def _mm_nt3(name, a3, b3, tm=512, tn=512):
    S, M, K = a3.shape
    N = b3.shape[1]
    tm, tn = _tile(M, tm), _tile(N, tn, LANES)
    return _mm(name, "nt", [(a3, b3)],
               [(pl.BlockSpec((None, tm, K), lambda i, j, s: (s, i, 0)), pl.BlockSpec((None, tn, K), lambda i, j, s: (s, j, 0)))],
               (M // tm, N // tn, S), 2, [_sds((M, N), F32)], [pl.BlockSpec((tm, tn), lambda i, j, s: (i, j))],
               acc_shape=(tm, tn))[0]


def _mm_tn(name, a, b, alpha=1.0, tm=1408, tn=1408, tk=512):
    T, M = a.shape
    N = b.shape[1]
    tm, tn, tk = _tile(M, tm, LANES), _tile(N, tn, LANES), _tile(T, tk)
    return _mm(name, "tn", [(a, b)],
               [(pl.BlockSpec((tk, tm), lambda i, j, k: (k, i)), pl.BlockSpec((tk, tn), lambda i, j, k: (k, j)))],
               (M // tm, N // tn, T // tk), 2, [_sds((M, N), F32)], [pl.BlockSpec((tm, tn), lambda i, j, k: (i, j))],
               acc_shape=(tm, tn), alpha=alpha)[0]


def _mm_tn3(name, a, b3, tm=1408, tn=1408, tk=512):
    T, M = a.shape
    S, _, N = b3.shape
    tm, tn, tk = _tile(M, tm, LANES), _tile(N, tn, LANES), _tile(T, tk)
    return _mm(name, "tn", [(a, b3)],
               [(pl.BlockSpec((tk, tm), lambda s, i, j, k: (k, i)), pl.BlockSpec((None, tk, tn), lambda s, i, j, k: (s, k, j)))],
               (S, M // tm, N // tn, T // tk), 3, [_sds((S, M, N), F32)],
               [pl.BlockSpec((None, tm, tn), lambda s, i, j, k: (s, i, j))], acc_shape=(tm, tn))[0]


def _rms_fwd(name, x, g):
    T, D = x.shape
    tt = _tile(T, 512)

    def body(x_ref, g_ref, h_ref):
        xv = x_ref[...]
        r = lax.rsqrt(jnp.mean(xv * xv, axis=-1, keepdims=True) + RMS_EPS)
        h_ref[...] = (xv * r * g_ref[...]).astype(h_ref.dtype)

    row = pl.BlockSpec((tt, D), lambda i: (i, 0))
    return _call(body, name, (T // tt,), [row, pl.BlockSpec((1, D), lambda i: (0, 0))], row, _sds((T, D), BF16))(x, g)


def _rms_bwd(name, dh, x, g, dx_in):
    T, D = x.shape
    tt = _tile(T, 512)

    def body(dh_ref, x_ref, g_ref, dxi_ref, dx_ref, dxb_ref, dg_ref):
        xv = x_ref[...]
        r = lax.rsqrt(jnp.mean(xv * xv, axis=-1, keepdims=True) + RMS_EPS)
        xh = xv * r
        dhv = dh_ref[...]
        dxh = dhv * g_ref[...]
        dx = dxi_ref[...] + r * (dxh - xh * jnp.mean(dxh * xh, axis=-1, keepdims=True))
        dx_ref[...] = dx
        dxb_ref[...] = dx.astype(BF16)
        part = jnp.sum(dhv * xh, axis=0, keepdims=True)

        @pl.when(pl.program_id(0) == 0)
        def _():
            dg_ref[...] = part

        @pl.when(pl.program_id(0) > 0)
        def _():
            dg_ref[...] += part

    row = pl.BlockSpec((tt, D), lambda i: (i, 0))
    vec = pl.BlockSpec((1, D), lambda i: (0, 0))
    dx, dxb, dg = _call(body, name, (T // tt,), [row, row, vec, row], [row, row, vec],
                        [_sds((T, D), F32), _sds((T, D), BF16), _sds((1, D), F32)], sem=("arbitrary",))(dh, x, g, dx_in)
    return (dx, dxb), dg


def _loss_head(x, g, tgt):
    T, D = x.shape
    tt = _tile(T, 512)

    def body(x_ref, g_ref, t_ref, loss_ref, dx_ref, dxb_ref, dg_ref):
        xv = x_ref[...]
        r = lax.rsqrt(jnp.mean(xv * xv, axis=-1, keepdims=True) + RMS_EPS)
        xh = xv * r
        e = xh * g_ref[...] - t_ref[...]
        lp = 0.5 * jnp.sum(jnp.sum(e * e, axis=-1, keepdims=True), axis=0, keepdims=True) / D
        dy = e / D
        dxh = dy * g_ref[...]
        dx = r * (dxh - xh * jnp.mean(dxh * xh, axis=-1, keepdims=True))
        dx_ref[...] = dx
        dxb_ref[...] = dx.astype(BF16)
        part = jnp.sum(dy * xh, axis=0, keepdims=True)

        @pl.when(pl.program_id(0) == 0)
        def _():
            dg_ref[...] = part
            loss_ref[...] = lp

        @pl.when(pl.program_id(0) > 0)
        def _():
            dg_ref[...] += part
            loss_ref[...] += lp

    row = pl.BlockSpec((tt, D), lambda i: (i, 0))
    vec = pl.BlockSpec((1, D), lambda i: (0, 0))
    one = pl.BlockSpec((1, 1), lambda i: (0, 0))
    loss, dx, dxb, dg = _call(body, "loss_head", (T // tt,), [row, vec, row], [one, row, row, vec],
                              [_sds((1, 1), F32), _sds((T, D), F32), _sds((T, D), BF16), _sds((1, D), F32)],
                              sem=("arbitrary",))(x, g, tgt)
    return loss, (dx, dxb), dg


def _colsum3(name, a3):
    S, T, N = a3.shape
    tt = _tile(T, 512)

    def body(a_ref, o_ref):
        part = jnp.sum(a_ref[...].astype(F32), axis=0, keepdims=True)

        @pl.when(pl.program_id(1) == 0)
        def _():
            o_ref[...] = part

        @pl.when(pl.program_id(1) > 0)
        def _():
            o_ref[...] += part

    return _call(body, name, (S, T // tt), [pl.BlockSpec((None, tt, N), lambda s, i: (s, i, 0))],
                 pl.BlockSpec((None, 1, N), lambda s, i: (s, 0, 0)), _sds((S, 1, N), F32), sem=("parallel", "arbitrary"))(a3)


def _glu_mm(name, h, w2, bias2, mode, u_dtype, tm=1024, tn=256):
    T, K = h.shape
    N = w2.shape[2]
    tm, tn = _tile(T, tm), _tile(N, tn, LANES)
    has_bias = bias2 is not None

    def body(*refs):
        h_ref, w_ref = refs[0], refs[1]
        ab_ref, u_ref = refs[-2], refs[-1]
        hv = h_ref[...]
        a = _dot(hv, w_ref[0])
        b = _dot(hv, w_ref[1])
        if has_bias:
            a = a + refs[2][0]
            b = b + refs[2][1]
        u = a * _sig(a) * b if mode == "swiglu" else a * _sig(b)
        ab_ref[0] = a.astype(BF16)
        ab_ref[1] = b.astype(BF16)
        u_ref[...] = u.astype(u_ref.dtype)

    in_specs = [pl.BlockSpec((tm, K), lambda i, j: (i, 0)), pl.BlockSpec((2, K, tn), lambda i, j: (0, 0, j))]
    args = [h, w2]
    if has_bias:
        in_specs.append(pl.BlockSpec((2, 1, tn), lambda i, j: (0, 0, j)))
        args.append(bias2)
    return _call(body, name, (T // tm, N // tn), in_specs,
                 [pl.BlockSpec((2, tm, tn), lambda i, j: (0, i, j)), pl.BlockSpec((tm, tn), lambda i, j: (i, j))],
                 [_sds((2, T, N), BF16), _sds((T, N), u_dtype)], sem=("parallel", "parallel"))(*args)


def _swiglu_bwd_mm(name, dxb, wd, ab, alpha, tm=1024, tn=256):
    T, D = dxb.shape
    N = wd.shape[0]
    tm, tn = _tile(T, tm), _tile(N, tn, LANES)
    def body(dx_ref, wd_ref, ab_ref, dab_ref, u_ref):
        du = _dot(dx_ref[...], wd_ref[...], "nt") * alpha
        a = ab_ref[0].astype(F32)
        b = ab_ref[1].astype(F32)
        sg = _sig(a)
        sa = a * sg
        dab_ref[0] = (du * b * (sg * (1.0 + a * (1.0 - sg)))).astype(BF16)
        dab_ref[1] = (du * sa).astype(BF16)
        u_ref[...] = (sa * b).astype(BF16)

    ab_spec = pl.BlockSpec((2, tm, tn), lambda i, j: (0, i, j))
    return _call(body, name, (T // tm, N // tn),
                 [pl.BlockSpec((tm, D), lambda i, j: (i, 0)), pl.BlockSpec((tn, D), lambda i, j: (j, 0)), ab_spec],
                 [ab_spec, pl.BlockSpec((tm, tn), lambda i, j: (i, j))], [_sds((2, T, N), BF16), _sds((T, N), BF16)],
                 sem=("parallel", "parallel"))(dxb, wd, ab)


def _ffn_fwd(tag, x, g, wgu, wd):
    h = _rms_fwd(f"ffn_rms_{tag}", x, g)
    ab, u = _glu_mm(f"ffn_gu_{tag}", h, wgu, None, "swiglu", BF16)
    return _mm_res(f"ffn_down_{tag}", u, wd, x, 0.5), (x, h, ab)


def _ffn_bwd(tag, dxp, saved, g, wgu, wd):
    x, h, ab = saved
    dx, dxb = dxp
    dab, u = _swiglu_bwd_mm(f"ffn_dgu_{tag}", dxb, wd, ab, 0.5)
    dwd = _mm_tn(f"ffn_dwd_{tag}", u, dxb, alpha=0.5)
    dwgu = _mm_tn3(f"ffn_dwgu_{tag}", h, dab)
    dh = _mm_nt3(f"ffn_dh_{tag}", dab, wgu)
    dx2, dg = _rms_bwd(f"ffn_drms_{tag}", dh, x, g, dx)
    return dx2, dg, dwgu, dwd


HALO = 32


def _conv_fwd(u3, dw32, dwb, lng, lnb):
    B, S, D = u3.shape
    W = dw32.shape[0]
    taps = CONV_TAPS
    tt = _tile(S, 256, HALO)
    hb = tt // HALO

    def body(u_ref, halo_ref, dw_ref, dwb_ref, g_ref, b_ref, v_ref, s_ref):
        i = pl.program_id(1)
        halo = jnp.where(i > 0, halo_ref[...], 0.0)
        ext = jnp.concatenate([halo, u_ref[...]], axis=0)
        acc = jnp.zeros((tt, D), F32) + dwb_ref[...]
        for j in range(taps):
            acc = acc + dw_ref[pl.ds(j, 1), :] * _roll(ext, taps - 1 - j)[HALO:]
        v_ref[...] = acc
        mu = jnp.mean(acc, axis=-1, keepdims=True)
        xc = acc - mu
        ln = xc * lax.rsqrt(jnp.mean(xc * xc, axis=-1, keepdims=True) + LN_EPS) * g_ref[...] + b_ref[...]
        s_ref[...] = (ln * _sig(ln)).astype(BF16)

    main = pl.BlockSpec((None, tt, D), lambda b, i: (b, i, 0))
    halo = pl.BlockSpec((None, HALO, D), lambda b, i: (b, jnp.maximum(i * hb - 1, 0), 0))
    vec = pl.BlockSpec((1, D), lambda b, i: (0, 0))
    return _call(body, "conv_fwd", (B, S // tt), [main, halo, pl.BlockSpec((W, D), lambda b, i: (0, 0)), vec, vec, vec],
                 [main, main], [_sds((B, S, D), F32), _sds((B, S, D), BF16)], sem=("parallel", "parallel"))(
        u3, u3, dw32, dwb, lng, lnb)


def _conv_bwd_ln(v, ds, lng, lnb):
    T, D = v.shape
    tt = _tile(T, 256)

    def body(v_ref, ds_ref, g_ref, b_ref, dv_ref, red_ref):
        vv = v_ref[...]
        mu = jnp.mean(vv, axis=-1, keepdims=True)
        xc = vv - mu
        rstd = lax.rsqrt(jnp.mean(xc * xc, axis=-1, keepdims=True) + LN_EPS)
        xh = xc * rstd
        ln = xh * g_ref[...] + b_ref[...]
        sg = _sig(ln)
        dln = ds_ref[...] * (sg * (1.0 + ln * (1.0 - sg)))
        dxh = dln * g_ref[...]
        dv = rstd * (dxh - jnp.mean(dxh, axis=-1, keepdims=True) - xh * jnp.mean(dxh * xh, axis=-1, keepdims=True))
        dv_ref[...] = dv
        parts = (jnp.sum(dln * xh, axis=0, keepdims=True), jnp.sum(dln, axis=0, keepdims=True),
                 jnp.sum(dv, axis=0, keepdims=True))

        @pl.when(pl.program_id(0) == 0)
        def _():
            for k in range(3):
                red_ref[k] = parts[k]

        @pl.when(pl.program_id(0) > 0)
        def _():
            for k in range(3):
                red_ref[k] += parts[k]

    row = pl.BlockSpec((tt, D), lambda i: (i, 0))
    vec = pl.BlockSpec((1, D), lambda i: (0, 0))
    return _call(body, "conv_bwd_ln", (T // tt,), [row, row, vec, vec], [row, pl.BlockSpec((3, 1, D), lambda i: (0, 0, 0))],
                 [_sds((T, D), F32), _sds((3, 1, D), F32)], sem=("arbitrary",))(v, ds, lng, lnb)


def _conv_bwd_dw(dv3, u3, ab, dw32):
    B, S, D = u3.shape
    W = dw32.shape[0]
    taps = CONV_TAPS
    tt = _tile(S, 256, HALO)
    hb = tt // HALO
    nt = S // tt
    L = tt + HALO

    def body(dv_ref, dvn_ref, u_ref, up_ref, ab_ref, dw_ref, dab_ref, ddw_ref):
        b, i = pl.program_id(0), pl.program_id(1)
        dv = dv_ref[...]
        ext_dv = jnp.concatenate([dv, jnp.where(i < nt - 1, dvn_ref[...], 0.0)], axis=0)
        ext_u = jnp.concatenate([jnp.where(i > 0, up_ref[...], 0.0), u_ref[...]], axis=0)
        du = jnp.zeros((tt, D), F32)
        first = jnp.logical_and(b == 0, i == 0)

        @pl.when(first)
        def _():
            ddw_ref[...] = jnp.zeros((W, D), F32)

        for j in range(taps):
            sh = taps - 1 - j
            du = du + dw_ref[pl.ds(j, 1), :] * _roll(ext_dv, L - sh)[:tt]
            ddw_ref[pl.ds(j, 1), :] += jnp.sum(dv * _roll(ext_u, sh)[HALO:], axis=0, keepdims=True)
        a = ab_ref[0].astype(F32)
        sb = _sig(ab_ref[1].astype(F32))
        dab_ref[0] = (du * sb).astype(BF16)
        dab_ref[1] = (du * a * sb * (1.0 - sb)).astype(BF16)

    main = pl.BlockSpec((None, tt, D), lambda b, i: (b, i, 0))
    prev = pl.BlockSpec((None, HALO, D), lambda b, i: (b, jnp.maximum(i * hb - 1, 0), 0))
    nxt = pl.BlockSpec((None, HALO, D), lambda b, i: (b, jnp.minimum((i + 1) * hb, S // HALO - 1), 0))
    abs_ = pl.BlockSpec((2, tt, D), lambda b, i: (0, b * nt + i, 0))
    wsp = pl.BlockSpec((W, D), lambda b, i: (0, 0))
    return _call(body, "conv_bwd_dw", (B, nt), [main, nxt, main, prev, abs_, wsp], [abs_, wsp],
                 [_sds((2, B * S, D), BF16), _sds((W, D), F32)], sem=("arbitrary", "arbitrary"))(dv3, dv3, u3, u3, ab, dw32)


CONV_TAPS = 31


def _conv_mixer_fwd(x, B, p):
    T, D = x.shape
    h = _rms_fwd("conv_rms", x, p["mix_norm"][0:1])
    ab, u = _glu_mm("conv_in", h, p["conv_w_in2"], p["conv_b_in2"], "glu", F32)
    v, s = _conv_fwd(u.reshape(B, T // B, D), p["conv_dw32"], p["conv_dw_b"], p["conv_ln_g"], p["conv_ln_b"])
    v, s = v.reshape(T, D), s.reshape(T, D)
    return _mm_res("conv_out", s, p["conv_w_out"], x, 1.0), (x, h, ab, u, v, s)


def _conv_mixer_bwd(dxp, saved, B, p):
    x, h, ab, u, v, s = saved
    T, D = x.shape
    dx, dxb = dxp
    g = {}
    ds = _mm_nt("conv_ds", [(dxb, p["conv_w_out"])], F32)
    g["conv_w_out"] = _mm_tn("conv_dwout", s, dxb)
    dv, red = _conv_bwd_ln(v, ds, p["conv_ln_g"], p["conv_ln_b"])
    g["conv_ln_g"], g["conv_ln_b"], g["conv_dw_b"] = red[0], red[1], red[2]
    dab, ddw = _conv_bwd_dw(dv.reshape(B, T // B, D), u.reshape(B, T // B, D), ab, p["conv_dw32"])
    g["conv_dw"] = ddw[:CONV_TAPS][None]
    g["conv_b_in"] = _colsum3("conv_dbin", dab).reshape(1, 2 * D)
    dwin = _mm_tn3("conv_dwin", h, dab)
    g["conv_w_in"] = jnp.moveaxis(dwin, 0, 1).reshape(1, D, 2 * D)
    dh = _mm_nt3("conv_dh", dab, p["conv_w_in2"])
    dx2, dg = _rms_bwd("conv_drms", dh, x, p["mix_norm"][0:1], dx)
    return dx2, dg, g


def _log_sigmoid(z):
    return jnp.minimum(z, 0.0) - jnp.log(1.0 + jnp.exp(-jnp.abs(z)))


def _fox_gate_fwd(fl3, bf):
    B, S, N = fl3.shape
    tt = _tile(S, 512)

    def body(fl_ref, bf_ref, c_ref, carry):
        @pl.when(pl.program_id(1) == 0)
        def _():
            carry[...] = jnp.zeros((1, N), F32)

        c = _scan_rows(_log_sigmoid(fl_ref[...] + bf_ref[...])) + carry[...]
        c_ref[...] = c
        carry[...] = c_ref[pl.ds(tt - 1, 1), :]

    row = pl.BlockSpec((None, tt, N), lambda b, i: (b, i, 0))
    return _call(body, "fox_gate_fwd", (B, S // tt), [row, pl.BlockSpec((1, N), lambda b, i: (0, 0))], row,
                 _sds((B, S, N), F32), [pltpu.VMEM((1, N), F32)], sem=("parallel", "arbitrary"))(fl3, bf)


def _fox_gate_bwd(dc3, fl3, bf):
    B, S, N = fl3.shape
    tt = _tile(S, 512)
    nt = S // tt

    def body(dc_ref, fl_ref, bf_ref, dfl_ref, dbf_ref, carry):
        b, i = pl.program_id(0), pl.program_id(1)

        @pl.when(i == 0)
        def _():
            carry[...] = jnp.zeros((1, N), F32)

        dc = dc_ref[0] - dc_ref[1]
        dlf = _scan_rows(dc, reverse=True) + carry[...]
        dfl = dlf * _sig(-(fl_ref[...] + bf_ref[...]))
        dfl_ref[...] = dfl
        carry[...] += jnp.sum(dc, axis=0, keepdims=True)
        part = jnp.sum(dfl, axis=0, keepdims=True)

        @pl.when(jnp.logical_and(b == 0, i == 0))
        def _():
            dbf_ref[...] = part

        @pl.when(jnp.logical_or(b > 0, i > 0))
        def _():
            dbf_ref[...] += part

    row = pl.BlockSpec((None, tt, N), lambda b, i: (b, nt - 1 - i, 0))
    vec = pl.BlockSpec((1, N), lambda b, i: (0, 0))
    row2 = pl.BlockSpec((2, None, tt, N), lambda b, i: (0, b, nt - 1 - i, 0))
    return _call(body, "fox_gate_bwd", (B, nt), [row2, row, vec], [row, vec], [_sds((B, S, N), F32), _sds((1, N), F32)],
                 [pltpu.VMEM((1, N), F32)], sem=("arbitrary", "arbitrary"))(dc3, fl3, bf)


NEG = -1e30


def _fox_attn_fwd(qa, ka, vat, crow, ckb, dh, scale, carried=None):
    B, H, S, P = qa.shape
    tq = _tile(S, 512, LANES)
    tk = tq
    nl = tq // LANES

    def body(q_ref, k_ref, vt_ref, cr_ref, ck_ref, o_ref, lse_ref, s_scr, p_scr, m_scr, al_scr, acc_scr):
        i = pl.program_id(2)
        qv = q_ref[...]
        m_scr[...] = jnp.full((1, tq), NEG, F32)
        acc_scr[...] = jnp.zeros((P, tq), F32)
        key = lax.broadcasted_iota(jnp.int32, (tk, LANES), 0)
        qry = lax.broadcasted_iota(jnp.int32, (tk, LANES), 1)

        def kv_step(j, diagonal):
            off = pl.multiple_of(j * tk, tk)
            s_scr[...] = _dot(k_ref[pl.ds(off, tk), :], qv, "nt")
            ck = ck_ref[pl.ds(off, tk), :]
            for lt in range(nl):
                ls = slice(lt * LANES, (lt + 1) * LANES)
                s = s_scr[:, ls] * scale + cr_ref[:, ls] - ck
                if diagonal:
                    s = jnp.where(key <= qry + lt * LANES, s, -jnp.inf)
                m1 = m_scr[:, ls]
                m2 = jnp.maximum(m1, jnp.max(s, axis=0, keepdims=True))
                p_scr[:, ls] = jnp.exp(s - m2).astype(BF16)
                al_scr[:, ls] = jnp.exp(m1 - m2)
                m_scr[:, ls] = m2
            acc_scr[...] = al_scr[...] * acc_scr[...] + _dot(vt_ref[:, pl.ds(off, tk)], p_scr[...])

        def before(j, carry):
            kv_step(j, False)
            return carry

        lax.fori_loop(0, i, before, 0)
        kv_step(i, True)
        l = acc_scr[pl.ds(dh, 1), :]
        o_ref[...] = (acc_scr[...] / l).T
        lse_ref[...] = m_scr[...] + jnp.log(l)

    qs = pl.BlockSpec((None, None, tq, P), lambda b, h, i: (b, h, i, 0))
    fullk = pl.BlockSpec((None, None, S, P), lambda b, h, i: (b, h, 0, 0))
    fullt = pl.BlockSpec((None, None, P, S), lambda b, h, i: (b, h, 0, 0))
    fullc = pl.BlockSpec((None, None, S, LANES), lambda b, h, i: (b, h, 0, 0))
    rowt = pl.BlockSpec((None, None, 1, tq), lambda b, h, i: (b, h, 0, i))
    scratch = [pltpu.VMEM((tk, tq), F32), pltpu.VMEM((tk, tq), BF16), pltpu.VMEM((1, tq), F32), pltpu.VMEM((1, tq), F32),
               pltpu.VMEM((P, tq), F32)]
    specs = ((B, H, S // tq), [qs, fullk, fullt, rowt, fullc], [qs, rowt], [_sds((B, H, S, P), F32), _sds((B, H, 1, S), F32)],
             scratch)
    if carried is None:
        return _call(body, "fox_attn_fwd", *specs, sem=("parallel", "parallel", "parallel"))(qa, ka, vat, crow, ckb)
    o, lse, *got = _carry_call(body, "fox_attn_fwd", *specs, "gather", carried)(qa, ka, vat, crow, ckb, *carried)
    return o, lse, got


def _fox_rowstats(do, o):
    B, H, S, P = o.shape
    tq = _tile(S, 4096)

    def body(do_ref, o_ref, dl_ref):
        dl_ref[...] = jnp.sum(do_ref[...] * o_ref[...], axis=-1, keepdims=True)

    qs = pl.BlockSpec((None, None, tq, P), lambda b, h, i: (b, h, i, 0))
    col = pl.BlockSpec((None, None, tq, 1), lambda b, h, i: (b, h, i, 0))
    return _call(body, "fox_rowstats", (B, H, S // tq), [qs, qs], col, _sds((B, H, S, 1), F32), sem=("parallel",) * 3)(do, o)


def _fox_attn_bwd(qa, ka, kat, va, do, crow, lse, delta, ckb, dh, scale, carried=None):
    B, H, S, P = qa.shape
    tk = _tile(S, 512, LANES)
    tq = tk
    nq = S // tq
    nl = tq // LANES

    def body(q_ref, k_ref, kt_ref, v_ref, do_ref, cr_ref, lse_ref, dl_ref, ck_ref, dqt_ref, dk_ref, dv_ref, rs_ref,
             s_scr, dp_scr, p_scr, ds_scr, dk_acc, dv_acc):
        j = pl.program_id(2)

        @pl.when(j == 0)
        def _():
            dqt_ref[...] = jnp.zeros((P, S), F32)

        kj, vj, ck = k_ref[...], v_ref[...], ck_ref[...]
        dk_acc[...] = jnp.zeros((tk, P), F32)
        dv_acc[...] = jnp.zeros((tk, P), F32)
        key = lax.broadcasted_iota(jnp.int32, (tk, LANES), 0)
        qry = lax.broadcasted_iota(jnp.int32, (tk, LANES), 1)

        def q_step(i, diagonal):
            off = pl.multiple_of(i * tq, tq)
            qi, doi = q_ref[pl.ds(off, tq), :], do_ref[pl.ds(off, tq), :].astype(BF16)
            s_scr[...] = _dot(kj, qi, "nt")
            dp_scr[...] = _dot(vj, doi, "nt")
            for lt in range(nl):
                ls = slice(lt * LANES, (lt + 1) * LANES)
                gl = pl.ds(pl.multiple_of(off + lt * LANES, LANES), LANES)
                p = jnp.exp(s_scr[:, ls] * scale + (cr_ref[:, gl] - lse_ref[:, gl]) - ck)
                if diagonal:
                    p = jnp.where(key <= qry + lt * LANES, p, 0.0)
                p_scr[:, ls] = p.astype(BF16)
                ds_scr[:, ls] = (p * (dp_scr[:, ls] - dl_ref[:, gl])).astype(BF16)
            ds = ds_scr[...]
            dqt_ref[:, pl.ds(off, tq)] += _dot(kt_ref[...], ds)
            dk_acc[...] += _dot(ds, qi)
            dv_acc[...] += _dot(p_scr[...], doi)

        def after(i, carry):
            q_step(i, False)
            return carry

        q_step(j, True)
        lax.fori_loop(j + 1, nq, after, 0)
        dk_ref[...] = dk_acc[...] * jnp.where(lax.broadcasted_iota(jnp.int32, (tk, P), 1) < dh, scale, 1.0)
        dv_ref[...] = dv_acc[...]

        @pl.when(j == S // tk - 1)
        def _():
            rs_ref[...] = dqt_ref[pl.ds(dh, 1), :]
            dqt_ref[...] = dqt_ref[...] * jnp.where(lax.broadcasted_iota(jnp.int32, (P, S), 0) < dh, scale, 1.0)

    ks = pl.BlockSpec((None, None, tk, P), lambda b, h, j: (b, h, j, 0))
    kts = pl.BlockSpec((None, None, P, tk), lambda b, h, j: (b, h, 0, j))
    cks = pl.BlockSpec((None, None, tk, LANES), lambda b, h, j: (b, h, j, 0))
    full = pl.BlockSpec((None, None, S, P), lambda b, h, j: (b, h, 0, 0))
    fullt = pl.BlockSpec((None, None, P, S), lambda b, h, j: (b, h, 0, 0))
    rowf = pl.BlockSpec((None, None, 1, S), lambda b, h, j: (b, h, 0, 0))
    scratch = [pltpu.VMEM((tk, tq), F32), pltpu.VMEM((tk, tq), F32), pltpu.VMEM((tk, tq), BF16), pltpu.VMEM((tk, tq), BF16),
               pltpu.VMEM((tk, P), F32), pltpu.VMEM((tk, P), F32)]
    specs = ((B, H, S // tk), [full, ks, kts, ks, full, rowf, rowf, rowf, cks], [fullt, ks, ks, rowf],
             [_sds((B, H, P, S), F32), _sds((B, H, S, P), F32), _sds((B, H, S, P), F32), _sds((B, H, 1, S), F32)], scratch)
    args = (qa, ka, kat, va, do, crow, lse, delta, ckb)
    if carried is None:
        return _call(body, "fox_attn_bwd", *specs, sem=("parallel", "parallel", "arbitrary"))(*args)
    dqt, dk, dv, rs, *got = _carry_call(body, "fox_attn_bwd", *specs, "scatter", carried)(*args, *carried)
    return dqt, dk, dv, rs, got


def _heads(t, B, H):
    T, D = t.shape
    return t.reshape(B, T // B, H, D // H).transpose(0, 2, 1, 3)


def _unheads(t):
    B, H, S, dh = t.shape
    return t.transpose(0, 2, 1, 3).reshape(B * S, H * dh)


def _fox_mixer_fwd(x, B, p):
    T, D = x.shape
    H = FOX_HEADS
    S = T // B
    scale = (D // H) ** -0.5
    h = _rms_fwd("fox_rms", x, p["mix_norm"][1:2])
    qkv = _mm_nn("fox_qkv", h, p["fox_w_qkv"], BF16)
    fl = _mm_nn("fox_fl", h, p["fox_w_f"], F32)
    c = _fox_gate_fwd(fl.reshape(B, S, LANES), p["fox_b_f128"])
    ch = c[:, :, :H].transpose(0, 2, 1)
    crow = ch[:, :, None, :]
    ckb = jnp.broadcast_to(ch[..., None], (B, H, S, LANES))
    q, k, v = _heads(qkv[:, :D], B, H), _heads(qkv[:, D:2 * D], B, H), _heads(qkv[:, 2 * D:], B, H)
    dh = D // H
    P = -(-(dh + 2) // LANES) * LANES
    one, zero = jnp.ones((B, H, S, 1), q.dtype), jnp.zeros((B, H, S, 1), q.dtype)
    rest = jnp.zeros((B, H, S, P - dh - 2), q.dtype)
    qa = jnp.concatenate([q, zero, one, rest], axis=-1)
    ka = jnp.concatenate([k, one, zero, rest], axis=-1)
    va = jnp.concatenate([v, one, zero, rest], axis=-1)
    if "late_gather" in p:
        o, lse, p["late_gathered"] = _fox_attn_fwd(qa, ka, va.transpose(0, 1, 3, 2), crow, ckb, dh, scale, p.pop("late_gather"))
    else:
        o, lse = _fox_attn_fwd(qa, ka, va.transpose(0, 1, 3, 2), crow, ckb, dh, scale)
    of = _unheads(o[..., :dh])
    return _mm_res("fox_out", of, p["fox_w_out"], x, 1.0), (x, h, fl, qa, ka, va, crow, ckb, o, lse, of)


def _fox_mixer_bwd(dxp, saved, B, p):
    x, h, fl, qa, ka, va, crow, ckb, o, lse, of = saved
    T, D = x.shape
    H = FOX_HEADS
    S = T // B
    dh_ = D // H
    P = qa.shape[-1]
    scale = dh_ ** -0.5
    dx, dxb = dxp
    g = {}
    do = _heads(_mm_nt("fox_do", [(dxb, p["fox_w_out"])], F32), B, H)
    do = jnp.pad(do, ((0, 0), (0, 0), (0, 0), (0, P - dh_)))
    g["fox_w_out"] = _mm_tn("fox_dwout", of, dxb)
    delta = _fox_rowstats(do, o).reshape(B, H, 1, S)
    if "scatter" in p:
        dqt, dk, dv, rowsum, p["scattered"] = _fox_attn_bwd(qa, ka, ka.transpose(0, 1, 3, 2), va, do, crow, lse, delta, ckb,
                                                          dh_, scale, p.pop("scatter"))
    else:
        dqt, dk, dv, rowsum = _fox_attn_bwd(qa, ka, ka.transpose(0, 1, 3, 2), va, do, crow, lse, delta, ckb, dh_, scale)
    dq = dqt[:, :, :dh_, :].transpose(0, 3, 1, 2).reshape(T, D)
    dqkv = jnp.concatenate([dq, _unheads(dk[..., :dh_]), _unheads(dv[..., :dh_])], axis=1).astype(BF16)
    dc = jnp.stack([rowsum[:, :, 0, :], dk[..., dh_ + 1]])
    dc = jnp.pad(dc.transpose(0, 1, 3, 2), ((0, 0), (0, 0), (0, 0), (0, LANES - H)))
    dfl, dbf = _fox_gate_bwd(dc, fl.reshape(B, S, LANES), p["fox_b_f128"])
    dfl = dfl.reshape(T, LANES)
    g["fox_b_f"] = dbf[:, :H]
    dwqkv = _mm_tn("fox_dwqkv", h, dqkv)
    dwf = _mm_tn("fox_dwf", h, dfl)
    g["fox_w_in"] = jnp.concatenate([dwqkv, dwf[:, :H]], axis=1)[None]
    dh = _mm_nt("fox_dh", [(dqkv, p["fox_w_qkv"]), (dfl, p["fox_w_f"])], F32)
    dx2, dg = _rms_bwd("fox_drms", dh, x, p["mix_norm"][1:2], dx)
    return dx2, dg, g


def _lb_fwd(logits):
    L, D = logits.shape

    def body(l_ref, lb_ref):
        z = l_ref[...]
        e = jnp.exp(z - jnp.max(z, axis=0, keepdims=True))
        p = e / jnp.sum(e, axis=0, keepdims=True)
        lb_ref[...] = jnp.sum(jnp.where(_lb_rows(z.shape), p, 0.0), axis=0, keepdims=True)

    return _call(body, "hgrn_lb", (1,), [pl.BlockSpec((L, D), lambda i: (0, 0))], pl.BlockSpec((1, D), lambda i: (0, 0)),
                 _sds((1, D), F32))(logits)


def _lb_rows(shape):
    r = lax.broadcasted_iota(jnp.int32, shape, 0)
    return jnp.logical_and(r >= 1, r <= HGRN_LAYER)


HGRN_LAYER = 2


def _lb_bwd(logits, dlb):
    L, D = logits.shape

    def body(l_ref, d_ref, o_ref):
        z = l_ref[...]
        e = jnp.exp(z - jnp.max(z, axis=0, keepdims=True))
        p = e / jnp.sum(e, axis=0, keepdims=True)
        dp = jnp.where(_lb_rows(z.shape), d_ref[...], 0.0)
        o_ref[...] = p * (dp - jnp.sum(p * dp, axis=0, keepdims=True))

    full = pl.BlockSpec((L, D), lambda i: (0, 0))
    return _call(body, "hgrn_dlb", (1,), [full, pl.BlockSpec((1, D), lambda i: (0, 0))], full, _sds((L, D), F32))(logits, dlb)


def _hgrn_gates(qr, fr, lbv):
    e = jnp.exp(-jnp.abs(fr))
    big, small = 1.0 / (1.0 + e), e / (1.0 + e)
    sf = jnp.where(fr >= 0, big, small)
    snf = jnp.where(fr >= 0, small, big)
    f = lbv + (1.0 - lbv) * sf
    sq = _sig(qr)
    return qr * sq, (1.0 - lbv) * snf, jnp.log(f), sf, snf, f, sq


def _hgrn_intra(G, q, kk, g_scr, q_scr):
    C = HGRN_CHUNK
    g_scr[...] = G
    q_scr[...] = q
    srow = lax.broadcasted_iota(jnp.int32, (C, LANES), 0)
    lane = lax.broadcasted_iota(jnp.int32, (C, LANES), 1)
    at = jnp.zeros((C, LANES), F32)
    for t in range(C):
        e = jnp.where(srow <= t, jnp.exp(g_scr[pl.ds(t, 1), :] - G), 0.0)
        col = jnp.sum(e * kk * q_scr[pl.ds(t, 1), :], axis=-1, keepdims=True)
        at = jnp.where(lane == t, col, at)
    return at


def _hgrn_fwd(proj3, lb, ng):
    B, S, D4 = proj3.shape
    D = D4 // 4
    H = D // HGRN_EXPAND
    C = HGRN_CHUNK
    R = _tile(S, 256, 2 * C)
    ncb = R // C
    dk = HGRN_EXPAND

    def body(q_ref, f_ref, i_ref, go_ref, lb_ref, ng_ref, y_ref, o_ref, st_ref, st, g_scr, q_scr):
        @pl.when(pl.program_id(2) == 0)
        def _():
            st[...] = jnp.zeros((dk, dk), F32)

        lbv = lb_ref[...]

        def chunk(c, slot):
            r0 = pl.multiple_of(c * C, C)
            rows = pl.ds(r0, C)
            q, kk, lf, *_ = _hgrn_gates(q_ref[rows, :], f_ref[rows, :], lbv)
            vv = i_ref[rows, :]
            G = _scan_rows(lf)
            at = _hgrn_intra(G, q, kk, g_scr.at[slot], q_scr.at[slot])
            gl = g_scr[slot, pl.ds(C - 1, 1), :]
            stv = st[...]
            st_ref[c] = stv
            o = _dot(q * jnp.exp(G), stv, "nt") + _dot(at, vv, "tn")[:C]
            st[...] = stv * jnp.exp(gl) + _dot(vv, kk * jnp.exp(gl - G), "tn")
            o_ref[rows, :] = o
            gv = go_ref[rows, :]
            y = o * lax.rsqrt(jnp.mean(o * o, axis=-1, keepdims=True) + RMS_EPS) * ng_ref[...] * (gv * _sig(gv))
            y_ref[rows, :] = y.astype(BF16)

        def pair(c2, carry):
            chunk(2 * c2, 0)
            chunk(2 * c2 + 1, 1)
            return carry

        lax.fori_loop(0, ncb // 2, pair, 0)

    def col(k):
        return pl.BlockSpec((None, R, dk), lambda b, h, i: (b, i, h + k * H))

    vec = pl.BlockSpec((1, dk), lambda b, h, i: (0, h))
    out = pl.BlockSpec((None, R, dk), lambda b, h, i: (b, i, h))
    return _call(body, "hgrn_fwd", (B, H, S // R), [col(0), col(1), col(2), col(3), vec, vec],
                 [out, out, pl.BlockSpec((None, None, ncb, dk, dk), lambda b, h, i: (b, h, i, 0, 0))],
                 [_sds((B, S, D), BF16), _sds((B, S, D), F32), _sds((B, H, S // C, dk, dk), F32)],
                 [pltpu.VMEM((dk, dk), F32), pltpu.VMEM((2, C, dk), F32), pltpu.VMEM((2, C, dk), F32)],
                 sem=("parallel", "parallel", "arbitrary"))(proj3, proj3, proj3, proj3, lb, ng)


def _hgrn_bwd(proj3, o3, dy3, states, lb, ng, carried=None):
    B, S, D4 = proj3.shape
    D = D4 // 4
    H = D // HGRN_EXPAND
    C = HGRN_CHUNK
    R = _tile(S, 256, 2 * C)
    ncb = R // C
    nb = S // R
    dk = HGRN_EXPAND

    def body(q_ref, f_ref, i_ref, go_ref, o_ref, dy_ref, st_ref, lb_ref, ng_ref,
             dp_ref, red_ref, dst, g_scr, q_scr, dq_scr, acc):
        b, i = pl.program_id(1), pl.program_id(2)

        @pl.when(i == 0)
        def _():
            dst[...] = jnp.zeros((dk, dk), F32)

        @pl.when(jnp.logical_and(b == 0, i == 0))
        def _():
            acc[...] = jnp.zeros((2, dk), F32)

        lbv = lb_ref[...]
        ngv = ng_ref[...]
        srow = lax.broadcasted_iota(jnp.int32, (C, LANES), 0)
        lane = lax.broadcasted_iota(jnp.int32, (C, LANES), 1)

        def chunk(c, slot):
            r0 = pl.multiple_of(c * C, C)
            rows = pl.ds(r0, C)
            qr, fr, vv, gv = q_ref[rows, :], f_ref[rows, :], i_ref[rows, :], go_ref[rows, :]
            q, kk, lf, sf, snf, f, sq = _hgrn_gates(qr, fr, lbv)
            o = o_ref[rows, :]
            dy = dy_ref[rows, :]
            rinv = lax.rsqrt(jnp.mean(o * o, axis=-1, keepdims=True) + RMS_EPS)
            on = o * rinv
            sgv = _sig(gv)
            dz = dy * (gv * sgv)
            dp_ref[3, rows, :] = (dy * on * ngv * (sgv * (1.0 + gv * (1.0 - sgv)))).astype(BF16)
            acc[pl.ds(1, 1), :] += jnp.sum(dz * on, axis=0, keepdims=True)
            don = dz * ngv
            do = rinv * (don - on * jnp.mean(don * on, axis=-1, keepdims=True))
            G = _scan_rows(lf)
            g_scr[slot] = G
            q_scr[slot] = q
            gl = g_scr[slot, pl.ds(C - 1, 1), :]
            egl = jnp.exp(gl)
            eG = jnp.exp(G)
            eK = jnp.exp(gl - G)
            qg, kg = q * eG, kk * eK
            stv = st_ref[c]
            dsv = dst[...]
            dqg = _dot(do, stv)
            do_pad = jnp.concatenate([do, jnp.zeros((LANES - C, dk), F32)], axis=0)
            dat = _dot(vv, do_pad, "nt")
            dkg = _dot(vv, dsv)
            dgl = egl * jnp.sum(stv * dsv, axis=0, keepdims=True) + jnp.sum(dkg * kg, axis=0, keepdims=True)
            at = jnp.zeros((C, LANES), F32)
            dki = jnp.zeros((C, dk), F32)
            for t in range(C):
                e = jnp.where(srow <= t, jnp.exp(g_scr[slot, pl.ds(t, 1), :] - G), 0.0)
                qt = q_scr[slot, pl.ds(t, 1), :]
                at = jnp.where(lane == t, jnp.sum(e * kk * qt, axis=-1, keepdims=True), at)
                z = e * jnp.sum(jnp.where(lane == t, dat, 0.0), axis=-1, keepdims=True)
                dq_scr[slot, pl.ds(t, 1), :] = jnp.sum(z * kk, axis=0, keepdims=True)
                dki = dki + z * qt
            dqi = dq_scr[slot]
            dp_ref[2, rows, :] = (_dot(at, do_pad) + _dot(kg, dsv, "nt")).astype(BF16)
            dst[...] = dsv * egl + _dot(do, qg, "tn")
            dq = dqg * eG + dqi
            dkk = dkg * eK + dki
            dG = dqg * qg - dkg * kg + q * dqi - kk * dki
            dG = dG + jnp.where(srow == C - 1, dgl, 0.0)
            dlf = _scan_rows(dG, reverse=True)
            dsf = (1.0 - lbv) * sf * snf
            dp_ref[1, rows, :] = (dlf * dsf / f - dkk * dsf).astype(BF16)
            acc[pl.ds(0, 1), :] += jnp.sum(dlf * snf / f - dkk * snf, axis=0, keepdims=True)
            dp_ref[0, rows, :] = (dq * (sq * (1.0 + qr * (1.0 - sq)))).astype(BF16)

        def pair(c2, carry):
            chunk(ncb - 1 - 2 * c2, 0)
            chunk(ncb - 2 - 2 * c2, 1)
            return carry

        lax.fori_loop(0, ncb // 2, pair, 0)

        @pl.when(jnp.logical_and(b == B - 1, i == nb - 1))
        def _():
            red_ref[0] = acc[pl.ds(0, 1), :]
            red_ref[1] = acc[pl.ds(1, 1), :]

    def col(k):
        return pl.BlockSpec((None, R, dk), lambda h, b, i: (b, nb - 1 - i, h + k * H))

    vec = pl.BlockSpec((1, dk), lambda h, b, i: (0, h))
    row = pl.BlockSpec((None, R, dk), lambda h, b, i: (b, nb - 1 - i, h))
    stsp = pl.BlockSpec((None, None, ncb, dk, dk), lambda h, b, i: (b, h, nb - 1 - i, 0, 0))
    specs = ((H, B, nb), [col(0), col(1), col(2), col(3), row, row, stsp, vec, vec],
             [pl.BlockSpec((4, None, R, dk), lambda h, b, i: (0, b, nb - 1 - i, h)),
              pl.BlockSpec((2, 1, dk), lambda h, b, i: (0, 0, h))],
             [_sds((4, B, S, D), BF16), _sds((2, 1, D), F32)],
             [pltpu.VMEM((dk, dk), F32), pltpu.VMEM((2, C, dk), F32), pltpu.VMEM((2, C, dk), F32),
              pltpu.VMEM((2, C, dk), F32), pltpu.VMEM((2, dk), F32)])
    args = (proj3, proj3, proj3, proj3, o3, dy3, states, lb, ng)
    if carried is None:
        return _call(body, "hgrn_bwd", *specs, sem=("parallel", "arbitrary", "arbitrary"))(*args)
    dp, red, *got = _carry_call(body, "hgrn_bwd", *specs, "scatter", carried)(*args, *carried)
    return dp, red, got


def _hgrn_mixer_fwd(x, B, p):
    T, D = x.shape
    S = T // B
    h = _rms_fwd("hgrn_rms", x, p["mix_norm"][2:3])
    proj = _mm_nn("hgrn_in", h, p["hgrn_w_in"], F32)
    lb = _lb_fwd(p["hgrn_lb_logits"])
    y, o, states = _hgrn_fwd(proj.reshape(B, S, 4 * D), lb, p["hgrn_norm"])
    y = y.reshape(T, D)
    return _mm_res("hgrn_out", y, p["hgrn_w_out"], x, 1.0), (x, h, proj, lb, y, o, states)


def _hgrn_mixer_bwd(dxp, saved, B, p):
    x, h, proj, lb, y, o, states = saved
    T, D = x.shape
    S = T // B
    dx, dxb = dxp
    g = {}
    dy = _mm_nt("hgrn_dy", [(dxb, p["hgrn_w_out"])], F32)
    g["hgrn_w_out"] = _mm_tn("hgrn_dwout", y, dxb)[None]
    if "scatter" in p:
        dp, red, p["scattered"] = _hgrn_bwd(proj.reshape(B, S, 4 * D), o, dy.reshape(B, S, D), states, lb, p["hgrn_norm"],
                                           p.pop("scatter"))
    else:
        dp, red = _hgrn_bwd(proj.reshape(B, S, 4 * D), o, dy.reshape(B, S, D), states, lb, p["hgrn_norm"])
    dp = dp.reshape(4, T, D)
    g["hgrn_norm"] = red[1]
    g["hgrn_lb_logits"] = _lb_bwd(p["hgrn_lb_logits"], red[0])
    dwin = _mm_tn3("hgrn_dwin", h, dp)
    g["hgrn_w_in"] = jnp.moveaxis(dwin, 0, 1).reshape(1, D, 4 * D)
    dh = _mm_nt3("hgrn_dh", dp, p["hgrn_w_in4"])
    dx2, dg = _rms_bwd("hgrn_drms", dh, x, p["mix_norm"][2:3], dx)
    return dx2, dg, g


POOL_HALO = 16


def _pool_fwd(x3, g):
    B, S, D = x3.shape
    tt = _tile(S, 256, POOL_HALO)
    hb = tt // POOL_HALO
    G = D // len(POOL_WINDOWS)

    def body(x_ref, halo_ref, g_ref, m_ref):
        i = pl.program_id(1)

        def norm(xv):
            return xv * lax.rsqrt(jnp.mean(xv * xv, axis=-1, keepdims=True) + RMS_EPS) * g_ref[...]

        hm = norm(x_ref[...])
        ext = jnp.concatenate([jnp.where(i > 0, norm(halo_ref[...]), 0.0), hm], axis=0)
        pos = (i * tt + lax.broadcasted_iota(jnp.int32, (tt, 1), 0) + 1).astype(F32)
        for gi, win in enumerate(POOL_WINDOWS):
            s = ext[:, gi * G:(gi + 1) * G]
            w = 1
            while w < win:
                s = s + _roll(s, w)
                w *= 2
            m_ref[:, gi * G:(gi + 1) * G] = (s[POOL_HALO:] / jnp.minimum(pos, float(win)) - hm[:, gi * G:(gi + 1) * G]).astype(BF16)

    main = pl.BlockSpec((None, tt, D), lambda b, i: (b, i, 0))
    halo = pl.BlockSpec((None, POOL_HALO, D), lambda b, i: (b, jnp.maximum(i * hb - 1, 0), 0))
    return _call(body, "pool_fwd", (B, S // tt), [main, halo, pl.BlockSpec((1, D), lambda b, i: (0, 0))], main,
                 _sds((B, S, D), BF16), sem=("parallel", "parallel"))(x3, x3, g)


def _pool_bwd(dm3):
    B, S, D = dm3.shape
    tt = _tile(S, 256, POOL_HALO)
    hb = tt // POOL_HALO
    nt = S // tt
    G = D // len(POOL_WINDOWS)
    L = tt + POOL_HALO

    def body(dm_ref, nxt_ref, dh_ref):
        i = pl.program_id(1)
        posm = (i * tt + lax.broadcasted_iota(jnp.int32, (tt, 1), 0) + 1).astype(F32)
        posn = ((i + 1) * tt + lax.broadcasted_iota(jnp.int32, (POOL_HALO, 1), 0) + 1).astype(F32)
        for gi, win in enumerate(POOL_WINDOWS):
            sl = slice(gi * G, (gi + 1) * G)
            dm = dm_ref[:, sl]
            s = jnp.concatenate([dm / jnp.minimum(posm, float(win)),
                                 jnp.where(i < nt - 1, nxt_ref[:, sl] / jnp.minimum(posn, float(win)), 0.0)], axis=0)
            w = 1
            while w < win:
                s = s + _roll(s, L - w)
                w *= 2
            dh_ref[:, sl] = s[:tt] - dm

    main = pl.BlockSpec((None, tt, D), lambda b, i: (b, i, 0))
    nxt = pl.BlockSpec((None, POOL_HALO, D), lambda b, i: (b, jnp.minimum((i + 1) * hb, S // POOL_HALO - 1), 0))
    return _call(body, "pool_bwd", (B, nt), [main, nxt], main, _sds((B, S, D), F32), sem=("parallel", "parallel"))(dm3, dm3)


def _pool_mixer_fwd(x, B, p):
    T, D = x.shape
    NG = len(POOL_WINDOWS)
    G = D // NG
    tm = _tile(T, 512)
    m = _pool_fwd(x.reshape(B, T // B, D), p["mix_norm"][3:4]).reshape(T, D)

    def epi(res, ex, outs):
        outs[0][...] = ex[1][...] + res * ex[0][...]

    blk = pl.BlockSpec((tm, G), lambda i, g: (i, g))
    x2 = _mm("pool_out", "nn", [(m, p["pool_w4"])], [(blk, pl.BlockSpec((None, G, G), lambda i, g: (g, 0, 0)))],
             (T // tm, NG), None, [_sds((T, D), F32)], [blk], extras=[p["pool_scale"], x],
             extra_specs=[pl.BlockSpec((1, G), lambda i, g: (0, g)), blk], epilogue=epi)[0]
    return x2, (x, m)


def _pool_mixer_bwd(dxp, saved, B, p):
    x, m = saved
    dx, dxb = dxp
    T, D = x.shape
    NG = len(POOL_WINDOWS)
    G = D // NG
    tm = _tile(T, 512)
    g = {}

    def epi(zz, ex, outs):
        dy = ex[0][...]
        outs[0][...] = (dy * ex[1][...]).astype(BF16)
        part = jnp.sum(dy * zz, axis=0, keepdims=True)

        @pl.when(pl.program_id(1) == 0)
        def _():
            outs[1][...] = part

        @pl.when(pl.program_id(1) > 0)
        def _():
            outs[1][...] += part

    blk = pl.BlockSpec((tm, G), lambda g_, i: (i, g_))
    wsp = pl.BlockSpec((None, G, G), lambda g_, i: (g_, 0, 0))
    vec = pl.BlockSpec((1, G), lambda g_, i: (0, g_))
    dz, dsc = _mm("pool_dz", "nn", [(m, p["pool_w4"])], [(blk, wsp)], (NG, T // tm), None,
                  [_sds((T, D), BF16), _sds((1, D), F32)], [blk, vec], extras=[dx, p["pool_scale"]], extra_specs=[blk, vec],
                  epilogue=epi, sem=("parallel", "arbitrary"))
    g["pool_scale"] = dsc
    tk = _tile(T, 512)
    kb = pl.BlockSpec((tk, G), lambda g_, k: (k, g_))
    g["pool_w"] = _mm("pool_dw", "tn", [(m, dz)], [(kb, kb)], (NG, T // tk), 1, [_sds((NG, G, G), F32)],
                      [pl.BlockSpec((None, G, G), lambda g_, k: (g_, 0, 0))], acc_shape=(G, G))[0][None]
    blk2 = pl.BlockSpec((tm, G), lambda i, g_: (i, g_))
    dm = _mm("pool_dm", "nt", [(dz, p["pool_w4"])], [(blk2, pl.BlockSpec((None, G, G), lambda i, g_: (g_, 0, 0)))],
             (T // tm, NG), None, [_sds((T, D), F32)], [blk2])[0]
    dh = _pool_bwd(dm.reshape(B, T // B, D)).reshape(T, D)
    dx2, dg = _rms_bwd("pool_drms", dh, x, p["mix_norm"][3:4], dx)
    return dx2, dg, g


_MIXERS = ((_conv_mixer_fwd, _conv_mixer_bwd), (_fox_mixer_fwd, _fox_mixer_bwd), (_hgrn_mixer_fwd, _hgrn_mixer_bwd),
           (_pool_mixer_fwd, _pool_mixer_bwd))


EARLY_LAYERS = 2
LATE = ("ffn_w_gate", "ffn_w_up", "ffn_w_down", "hgrn_w_in", "hgrn_w_out", "pool_w")


def _local_step(x3, tgt3, w, plan=None):
    B, S, D = x3.shape
    T = B * S
    depth = w["ffn_norm"].shape[0]
    H = FOX_HEADS
    p = dict(w)
    p["conv_w_in2"] = w["conv_w_in"][0].reshape(D, 2, D).transpose(1, 0, 2)
    p["conv_b_in2"] = w["conv_b_in"].reshape(2, 1, D)
    p["conv_dw32"] = jnp.pad(w["conv_dw"][0], ((0, HALO - CONV_TAPS), (0, 0)))
    p["conv_w_out"] = w["conv_w_out"][0]
    p["fox_w_qkv"] = w["fox_w_in"][0][:, :3 * D]
    p["fox_w_f"] = jnp.pad(w["fox_w_in"][0][:, 3 * D:], ((0, 0), (0, LANES - H)))
    p["fox_b_f128"] = jnp.pad(w["fox_b_f"], ((0, 0), (0, LANES - H)))
    p["fox_w_out"] = w["fox_w_out"][0]
    ffn = {}

    def add_layers(first, gate, up, down, rest):
        for k in range(gate.shape[0]):
            ffn[first + k] = (jnp.stack((gate[k], up[k]), axis=1), down[k])
        if rest is not None:
            p["hgrn_w_in"] = rest["hgrn_w_in"][0]
            p["hgrn_w_in4"] = rest["hgrn_w_in"][0].reshape(D, 4, D).transpose(1, 0, 2)
            p["hgrn_w_out"] = rest["hgrn_w_out"][0]
            p["pool_w4"] = rest["pool_w"][0]

    add_layers(0, w["ffn_w_gate"], w["ffn_w_up"], w["ffn_w_down"], w if plan is None else None)
    if plan is not None:
        p["late_gather"] = plan["late_shards"]

    x = x3.reshape(T, D)
    saved = []
    for i in range(depth):
        x, s0 = _ffn_fwd(f"{i}a", x, w["ffn_norm"][i, 0:1], ffn[i][0][0], ffn[i][1][0])
        x, s1 = _MIXERS[i % 4][0](x, B, p)
        if "late_gathered" in p:
            late = plan["assemble"](p.pop("late_gathered"))
            add_layers(EARLY_LAYERS, late["ffn_w_gate"], late["ffn_w_up"], late["ffn_w_down"], late)
        x, s2 = _ffn_fwd(f"{i}b", x, w["ffn_norm"][i, 1:2], ffn[i][0][1], ffn[i][1][1])
        saved.append((s0, s1, s2))
    loss, dx, dfinal = _loss_head(x, w["final_norm"].reshape(1, D), tgt3.reshape(T, D))

    g = {"final_norm": dfinal}
    dffn_norm = [[None, None] for _ in range(depth)]
    dwgu = [[None, None] for _ in range(depth)]
    dwd = [[None, None] for _ in range(depth)]
    dmix = [None] * depth
    travelled = []

    def layer_grads(i, extra):
        gate, up = (jnp.stack([dwgu[i][0][s], dwgu[i][1][s]])[None] for s in (0, 1))
        return [gate, up, jnp.stack(dwd[i])[None]] + [g.pop(n) for n in extra]

    for i in reversed(range(depth)):
        s0, s1, s2 = saved[i]
        dx, dffn_norm[i][1], dwgu[i][1], dwd[i][1] = _ffn_bwd(f"{i}b", dx, s2, w["ffn_norm"][i, 1:2], ffn[i][0][1], ffn[i][1][1])
        if plan is not None and i in (1, 2):
            extra = ("pool_w",) if i == 2 else ("hgrn_w_in", "hgrn_w_out")
            names = LATE[:3] + extra
            p["scatter"] = plan["dev_major"](names, layer_grads(i + 1, extra))
            travelled.append([names, p["scatter"]])
        dx, dmix[i], gm = _MIXERS[i % 4][1](dx, s1, B, p)
        if "scattered" in p:
            travelled[-1].append(p.pop("scattered"))
        g.update(gm)
        dx, dffn_norm[i][0], dwgu[i][0], dwd[i][0] = _ffn_bwd(f"{i}a", dx, s0, w["ffn_norm"][i, 0:1], ffn[i][0][0], ffn[i][1][0])
    kept = range(depth) if plan is None else range(EARLY_LAYERS)
    g["ffn_norm"] = jnp.stack([jnp.stack([a[0], b[0]]) for a, b in dffn_norm])
    g["ffn_w_gate"] = jnp.stack([jnp.stack([dwgu[i][0][0], dwgu[i][1][0]]) for i in kept])
    g["ffn_w_up"] = jnp.stack([jnp.stack([dwgu[i][0][1], dwgu[i][1][1]]) for i in kept])
    g["ffn_w_down"] = jnp.stack([jnp.stack(dwd[i]) for i in kept])
    g["mix_norm"] = jnp.concatenate(dmix, axis=0)
    g = {n: (t.reshape(w[n].shape) if n in w else t) for n, t in g.items()}
    return loss, dx[0].reshape(B, S, D), g, travelled


def _adamw(name, w, m, v, parts):
    shape = w.shape
    cols = shape[-1]
    rows = w.size // cols
    tr = _tile(rows, max(8, (1 << 19) // cols))
    n = len(parts)
    c1 = 1.0 - ADAM_B1 ** ADAM_STEP
    c2 = 1.0 - ADAM_B2 ** ADAM_STEP

    def body(*refs):
        w_ref, m_ref, v_ref = refs[:3]
        g_ref, d_ref, m2_ref, v2_ref = refs[3 + n:]
        g = refs[3][...].astype(F32)
        for k in range(1, n):
            g = g + refs[3 + k][...].astype(F32)
        m2 = ADAM_B1 * m_ref[...] + (1.0 - ADAM_B1) * g
        v2 = ADAM_B2 * v_ref[...] + (1.0 - ADAM_B2) * (g * g)
        g_ref[...] = g
        m2_ref[...] = m2
        v2_ref[...] = v2
        d_ref[...] = -ADAM_LR * ((m2 / c1) / (jnp.sqrt(v2 / c2) + ADAM_EPS) + ADAM_WD * w_ref[...])

    blk = pl.BlockSpec((tr, cols), lambda i: (i, 0))
    outs = _call(body, name, (rows // tr,), [blk] * (3 + n), [blk] * 4, [_sds((rows, cols), F32)] * 4, sem=("parallel",))(
        *[t.reshape(rows, cols) for t in (w, m, v, *parts)])
    return [o.reshape(shape) for o in outs]


ANY = pl.BlockSpec(memory_space=pl.ANY)
FLAT_COLS = 1024


def _place():
    return lax.axis_index("x"), lax.axis_index("y"), lax.axis_index("c")


def _all_gather(name, xs):
    K = len(xs)

    def body(*refs):
        x_refs, out_refs = refs[:K], refs[K:2 * K]
        send_sems, recv_sems, local_sems = refs[2 * K:]
        xi, yi, ci = _place()
        me, sibling = (xi, yi, ci), (xi, yi, 1 - ci)
        chips = [(1 - xi, yi), (xi, 1 - yi), (1 - xi, 1 - yi)]

        def slot(a, px, py, pc):
            return out_refs[a].at[4 * px + 2 * py + pc]

        def copy(a, k, block, to, own=False):
            return pltpu.make_async_remote_copy(src_ref=x_refs[a] if own else slot(a, *block), dst_ref=slot(a, *block),
                                                send_sem=send_sems.at[7 * a + k], recv_sem=recv_sems.at[7 * a + k],
                                                device_id=to, device_id_type=MESH)

        mine = [pltpu.make_async_copy(x_refs[a], slot(a, *me), local_sems.at[a]) for a in range(K)]
        first = [copy(a, 1 + j, me, (*chip, ci), own=True) for j, chip in enumerate(chips) for a in range(K)]
        first += [copy(a, 0, me, sibling, own=True) for a in range(K)]
        for cp in mine + first:
            cp.start()
        passed = []
        for j, chip in enumerate(chips):
            for a in range(K):
                copy(a, 1 + j, (*chip, ci), me).wait_recv()
                passed.append(copy(a, 4 + j, (*chip, ci), sibling))
                passed[-1].start()
        for a in range(K):
            copy(a, 0, sibling, me).wait_recv()
            for j, chip in enumerate(chips):
                copy(a, 4 + j, (*chip, 1 - ci), me).wait_recv()
        for cp in first + passed:
            cp.wait_send()
        for cp in mine:
            cp.wait()

    return pl.pallas_call(body, name=name, out_shape=[_sds((N_DEV,) + x.shape, x.dtype) for x in xs], in_specs=[ANY] * K,
                          out_specs=[ANY] * K,
                          scratch_shapes=[pltpu.SemaphoreType.DMA((7 * K,)), pltpu.SemaphoreType.DMA((7 * K,)),
                                          pltpu.SemaphoreType.DMA((K,))])(*xs)


def _swap_sibling(name, ts):
    K = len(ts)

    def body(*refs):
        t_refs, out_refs, send_sems, recv_sems = refs[:K], refs[K:2 * K], refs[2 * K], refs[2 * K + 1]
        xi, yi, ci = _place()
        cps = [pltpu.make_async_remote_copy(src_ref=t_refs[a], dst_ref=out_refs[a], send_sem=send_sems.at[a],
                                            recv_sem=recv_sems.at[a], device_id=(xi, yi, 1 - ci), device_id_type=MESH)
               for a in range(K)]
        for cp in cps:
            cp.start()
        for cp in cps:
            cp.wait()

    return pl.pallas_call(body, name=name, out_shape=[_sds(t.shape, t.dtype) for t in ts], in_specs=[ANY] * K,
                          out_specs=[ANY] * K,
                          scratch_shapes=[pltpu.SemaphoreType.DMA((K,)), pltpu.SemaphoreType.DMA((K,))])(*ts)


def _scatter_chips(name, ts):
    K = len(ts)

    def body(*refs):
        t_refs, out_refs, send_sems, recv_sems = refs[:K], refs[K:2 * K], refs[2 * K], refs[2 * K + 1]
        xi, yi, ci = _place()
        chips = [(1 - xi, yi), (xi, 1 - yi), (1 - xi, 1 - yi)]
        cps = [pltpu.make_async_remote_copy(src_ref=t_refs[a].at[2 * cx + cy], dst_ref=out_refs[a].at[j],
                                            send_sem=send_sems.at[3 * a + j], recv_sem=recv_sems.at[3 * a + j],
                                            device_id=(cx, cy, ci), device_id_type=MESH)
               for j, (cx, cy) in enumerate(chips) for a in range(K)]
        for cp in cps:
            cp.start()
        for cp in cps:
            cp.wait()

    return pl.pallas_call(body, name=name, out_shape=[_sds((3,) + t.shape[1:], t.dtype) for t in ts], in_specs=[ANY] * K,
                          out_specs=[ANY] * K,
                          scratch_shapes=[pltpu.SemaphoreType.DMA((3 * K,)), pltpu.SemaphoreType.DMA((3 * K,))])(*ts)


def _add_bf16(name, a, b):
    shape = a.shape
    N, C = shape[0], shape[-1]
    R = a.size // (N * C)
    tr = _tile(R, max(8, (1 << 19) // C))

    def body(a_ref, b_ref, o_ref):
        o_ref[...] = (a_ref[...].astype(F32) + b_ref[...].astype(F32)).astype(BF16)

    blk = pl.BlockSpec((None, tr, C), lambda n, i: (n, i, 0))
    return _call(body, name, (N, R // tr), [blk, blk], blk, _sds((N, R, C), BF16), sem=("parallel", "parallel"))(
        a.reshape(N, R, C), b.reshape(N, R, C)).reshape(shape)


def _sum_parts(name, parts):
    N, C = parts.shape

    def body(p_ref, o_ref):
        s = p_ref[pl.ds(0, 1), :]
        for d in range(1, N):
            s = s + p_ref[pl.ds(d, 1), :]
        o_ref[...] = s

    return _call(body, name, (1,), [pl.BlockSpec((N, C), lambda i: (0, 0))], pl.BlockSpec((1, C), lambda i: (0, 0)),
                 _sds((1, C), F32))(parts)


def _flat(parts, dtype, lead=()):
    flat = jnp.concatenate([t.reshape(lead + (-1,)).astype(dtype) for t in parts], axis=-1)
    n = flat.shape[-1]
    unit = 16 * FLAT_COLS
    padded = -(-n // unit) * unit
    flat = jnp.pad(flat, [(0, 0)] * len(lead) + [(0, padded - n)])
    return flat.reshape(lead + (padded // FLAT_COLS, FLAT_COLS))


def _unflat(flat, shapes, lead=()):
    flat = flat.reshape(lead + (-1,))
    out, off = [], 0
    for shp in shapes:
        n = math.prod(shp)
        out.append(flat[..., off:off + n].reshape(lead + tuple(shp)))
        off += n
    return out


def _dev_major(full, ax):
    shp = full.shape
    return jnp.moveaxis(full.reshape(shp[:ax] + (N_DEV, shp[ax] // N_DEV) + shp[ax + 1:]), ax, 0)


def _from_dev_major(blocks, ax):
    t = jnp.moveaxis(blocks, 0, ax)
    shp = t.shape
    return t.reshape(shp[:ax] + (shp[ax] * shp[ax + 1],) + shp[ax + 2:])


def kernel(x, *rest):
    nw = len(WEIGHTS)
    w = dict(zip(WEIGHTS, rest[:nw]))
    tgt = rest[nw]
    m = dict(zip(WEIGHTS, rest[nw + 1:2 * nw + 1]))
    v = dict(zip(WEIGHTS, rest[2 * nw + 1:3 * nw + 1]))
    xi, yi, ci = _place()
    dev = 4 * xi + 2 * yi + ci

    ffn3 = LATE[:3]
    early = [n for n in BIG if n not in LATE]
    shard = {n: (w[n][:EARLY_LAYERS] if n in ffn3 else w[n]).astype(BF16) for n in ffn3 + tuple(early)}
    late_shard = {n: (w[n][EARLY_LAYERS:] if n in ffn3 else w[n]).astype(BF16) for n in LATE}
    big = _all_gather("gather_matrices", [shard[n] for n in ffn3 + tuple(early)])
    small = _all_gather("gather_vectors", [_flat([w[n] for n in SMALL], F32)])[0]
    full = {n: w[n] for n in REPL}
    for n, blocks in zip(ffn3 + tuple(early), big):
        full[n] = _from_dev_major(blocks, SHARDED[n])
    for n, blocks in zip(SMALL, _unflat(small, [w[n].shape for n in SMALL], (N_DEV,))):
        full[n] = _from_dev_major(blocks, SHARDED[n])

    def by_owner(names, grads):
        return [_dev_major(t.reshape(t.shape[:1] + w[n].shape[1:SHARDED[n]] + (-1,) + w[n].shape[SHARDED[n] + 1:]),
                           SHARDED[n]).astype(BF16) for n, t in zip(names, grads)]

    plan = dict(late_shards=[late_shard[n] for n in LATE], dev_major=by_owner,
                assemble=lambda got: {n: _from_dev_major(t, SHARDED[n]) for n, t in zip(LATE, got)})
    loss, gx, g, travelled = _local_step(x, tgt, full, plan)

    vec_names = SMALL + REPL
    vec = _all_gather("gather_vector_grads", [_flat([g[n] for n in vec_names] + [loss], F32)])[0]
    *vec_list, losses = _unflat(vec, [g[n].shape for n in vec_names] + [(1, 1)], (N_DEV,))
    vec_parts = dict(zip(vec_names, vec_list))
    loss = _sum_parts("loss_sum", losses.reshape(N_DEV, 1))[0, 0]

    last = ffn3 + tuple(early)
    keep, send = [], []
    for gd in by_owner(last, [g[n] for n in last]):
        gd = gd.reshape((4, 2) + gd.shape[1:])
        keep.append(lax.dynamic_index_in_dim(gd, ci, 1, keepdims=False))
        send.append(lax.dynamic_index_in_dim(gd, 1 - ci, 1, keepdims=False))
    got_sib = _swap_sibling("grads_to_sibling", send)
    pair = [_add_bf16(f"grad_pair_sum_{n}", a, b) for n, a, b in zip(last, keep, got_sib)]
    got_chips = _scatter_chips("grads_to_chips", pair)
    chip = 2 * xi + yi
    parts = {}
    for k, n in enumerate(last):
        parts[n] = [[lax.dynamic_index_in_dim(keep[k], chip, 0, keepdims=False),
                     lax.dynamic_index_in_dim(got_sib[k], chip, 0, keepdims=False), got_chips[k][0], got_chips[k][1], got_chips[k][2]]]
    for names, sent, got in reversed(travelled):
        for n, s, t in zip(names, sent, got):
            parts.setdefault(n, []).append([lax.dynamic_index_in_dim(s, dev, 0, keepdims=False)] + [t[k] for k in range(len(FLIPS))])

    res = {}
    for n in BIG:
        runs = parts[n]
        count = max(len(r) for r in runs)
        whole = [jnp.concatenate([r[k] if k < len(r) else jnp.zeros_like(r[0]) for r in runs], axis=0) if len(runs) > 1 else runs[0][k]
                 for k in range(count)]
        res[n] = _adamw(f"adamw_{n}", w[n], m[n], v[n], whole)
    for n in vec_names:
        parts = vec_parts[n]
        if n in SHARDED:
            ax = SHARDED[n]
            parts = lax.dynamic_slice_in_dim(parts, dev * w[n].shape[ax], w[n].shape[ax], ax + 1)
        res[n] = _adamw(f"adamw_{n}", w[n], m[n], v[n], [parts[d] for d in range(N_DEV)])
    return (loss, gx, *[res[n][0] for n in WEIGHTS], *[res[n][1] for n in WEIGHTS], *[res[n][2] for n in WEIGHTS],
            *[res[n][3] for n in WEIGHTS])
```

```python
import functools
import math

import jax
import jax.numpy as jnp
from jax import lax
from jax.experimental import pallas as pl
from jax.experimental.pallas import tpu as pltpu

F32 = jnp.float32
BF16 = jnp.bfloat16
MESH = pl.DeviceIdType.MESH

N_DEV = 8
RMS_EPS = 1e-6
LN_EPS = 1e-5
FOX_HEADS = 16
HGRN_EXPAND = 128
HGRN_CHUNK = 32
POOL_WINDOWS = (2, 4, 8, 16)
ADAM_LR, ADAM_B1, ADAM_B2, ADAM_EPS, ADAM_WD, ADAM_STEP = 0.001, 0.9, 0.999, 1e-08, 0.01, 10
LANES = 128
VMEM_LIMIT_MB = 48

SHARDED = dict(
    ffn_norm=2, ffn_w_gate=3, ffn_w_up=3, ffn_w_down=2, conv_w_in=2, conv_dw=2, conv_w_out=1, fox_w_in=2, fox_w_out=1,
    hgrn_w_in=2, hgrn_norm=1, hgrn_w_out=1, pool_w=2, pool_scale=1)
BIG = ("ffn_w_gate", "ffn_w_up", "ffn_w_down", "conv_w_in", "conv_w_out", "fox_w_in", "fox_w_out", "hgrn_w_in",
       "hgrn_w_out", "pool_w")
SMALL = ("ffn_norm", "conv_dw", "hgrn_norm", "pool_scale")
REPL = ("mix_norm", "final_norm", "conv_b_in", "conv_dw_b", "conv_ln_g", "conv_ln_b", "fox_b_f", "hgrn_lb_logits")
WEIGHTS = ("ffn_norm", "ffn_w_gate", "ffn_w_up", "ffn_w_down", "mix_norm", "final_norm", "conv_w_in", "conv_b_in",
           "conv_dw", "conv_dw_b", "conv_ln_g", "conv_ln_b", "conv_w_out", "fox_w_in", "fox_b_f", "fox_w_out",
           "hgrn_w_in", "hgrn_lb_logits", "hgrn_norm", "hgrn_w_out", "pool_w", "pool_scale")


def _tile(n, pref, mult=8):
    if n <= pref:
        return n
    for t in range(pref - pref % mult, 0, -mult):
        if n % t == 0:
            return t
    return n


def _sig(x):
    return 1.0 / (1.0 + jnp.exp(-x))


def _call(body, name, grid, in_specs, out_specs, out_shape, scratch=(), sem=None):
    params = dict(vmem_limit_bytes=VMEM_LIMIT_MB << 20)
    if sem is not None:
        params["dimension_semantics"] = sem
    return pl.pallas_call(body, name=name, grid=grid, in_specs=in_specs, out_specs=out_specs, out_shape=out_shape,
                          scratch_shapes=list(scratch), compiler_params=pltpu.CompilerParams(**params))


FLIPS = [(fx, fy, fc) for fx in (0, 1) for fy in (0, 1) for fc in (0, 1)][1:]


def _carry_call(body, name, grid, in_specs, out_specs, out_shape, scratch, kind, arrays):
    K = len(arrays)
    n_in, n_out, n_scr = len(in_specs), len(out_shape), len(scratch)
    n_peer = len(FLIPS)
    if kind == "gather":
        landed = [_sds((N_DEV,) + a.shape, a.dtype) for a in arrays]
    else:
        landed = [_sds((n_peer,) + a.shape[1:], a.dtype) for a in arrays]

    def wrapped(*refs):
        ins, sent = refs[:n_in], refs[n_in:n_in + K]
        outs, got = refs[n_in + K:n_in + K + n_out], refs[n_in + K + n_out:n_in + 2 * K + n_out]
        scr = refs[n_in + 2 * K + n_out:n_in + 2 * K + n_out + n_scr]
        send_sems, recv_sems, local_sems = refs[-3:]
        ids = [pl.program_id(d) for d in range(len(grid))]
        first = functools.reduce(jnp.logical_and, [i == 0 for i in ids])
        last = functools.reduce(jnp.logical_and, [i == n - 1 for i, n in zip(ids, grid)])
        xi, yi, ci = _place()
        me = 4 * xi + 2 * yi + ci

        def copies():
            cps = []
            for a in range(K):
                for k, (fx, fy, fc) in enumerate(FLIPS):
                    px, py, pc = (xi + fx) % 2, (yi + fy) % 2, (ci + fc) % 2
                    if kind == "gather":
                        src, dst = sent[a], got[a].at[me]
                    else:
                        src, dst = sent[a].at[4 * px + 2 * py + pc], got[a].at[k]
                    cps.append(pltpu.make_async_remote_copy(src_ref=src, dst_ref=dst, send_sem=send_sems.at[n_peer * a + k],
                                                            recv_sem=recv_sems.at[n_peer * a + k], device_id=(px, py, pc),
                                                            device_id_type=MESH))
            return cps

        def own():
            return [pltpu.make_async_copy(sent[a], got[a].at[me], local_sems.at[a]) for a in range(K)] if kind == "gather" else []

        @pl.when(first)
        def _():
            for cp in copies() + own():
                cp.start()

        body(*ins, *outs, *scr)

        @pl.when(last)
        def _():
            for cp in copies() + own():
                cp.wait()

    params = dict(vmem_limit_bytes=VMEM_LIMIT_MB << 20, dimension_semantics=("arbitrary",) * len(grid))
    res = pl.pallas_call(wrapped, name=name, grid=grid, in_specs=list(in_specs) + [ANY] * K,
                         out_specs=list(out_specs) + [ANY] * K, out_shape=list(out_shape) + landed,
                         scratch_shapes=list(scratch) + [pltpu.SemaphoreType.DMA((n_peer * K,)), pltpu.SemaphoreType.DMA((n_peer * K,)),
                                                         pltpu.SemaphoreType.DMA((K,))],
                         compiler_params=pltpu.CompilerParams(**params))
    return res


def _sds(shape, dtype):
    return jax.ShapeDtypeStruct(tuple(shape), dtype)


_DN = {"nn": (((1,), (0,)), ((), ())), "nt": (((1,), (1,)), ((), ())), "tn": (((0,), (0,)), ((), ()))}


def _dot(a, b, mode="nn"):
    return lax.dot_general(a.astype(BF16), b.astype(BF16), _DN[mode], preferred_element_type=F32)


def _roll(x, shift, axis=0):
    n = x.shape[axis]
    shift = shift % n
    return x if shift == 0 else pltpu.roll(x, shift, axis)


def _scan_rows(x, reverse=False):
    n = x.shape[0]
    row = lax.broadcasted_iota(jnp.int32, x.shape, 0)
    sh = 1
    while sh < n:
        if reverse:
            x = x + jnp.where(row < n - sh, _roll(x, n - sh), 0.0)
        else:
            x = x + jnp.where(row >= sh, _roll(x, sh), 0.0)
        sh *= 2
    return x


def _mm(name, mode, pairs, pair_specs, grid, k_axis, out_shapes, out_specs, acc_shape=None, extras=(), extra_specs=(),
        epilogue=None, alpha=1.0, sem=None):
    npair, nex, nout = len(pairs), len(extras), len(out_shapes)
    nk = grid[k_axis] if k_axis is not None else 1

    def body(*refs):
        prs = refs[:2 * npair]
        ex = refs[2 * npair:2 * npair + nex]
        outs = refs[2 * npair + nex:2 * npair + nex + nout]

        def partial():
            p = None
            for i in range(npair):
                d = _dot(prs[2 * i][...], prs[2 * i + 1][...], mode)
                p = d if p is None else p + d
            return p

        def finish(res):
            if alpha != 1.0:
                res = res * alpha
            if epilogue is None:
                outs[0][...] = res.astype(outs[0].dtype)
            else:
                epilogue(res, ex, outs)

        if k_axis is None:
            finish(partial())
        else:
            acc = refs[-1]
            k = pl.program_id(k_axis)

            @pl.when(k == 0)
            def _():
                acc[...] = partial()

            @pl.when(k > 0)
            def _():
                acc[...] += partial()

            @pl.when(k == nk - 1)
            def _():
                finish(acc[...])

    if sem is None:
        sem = tuple("arbitrary" if i == k_axis else "parallel" for i in range(len(grid)))
    scratch = [pltpu.VMEM(acc_shape, F32)] if k_axis is not None else []
    flat = [t for p in pairs for t in p]
    flat_specs = [s for p in pair_specs for s in p]
    return _call(body, name, grid, flat_specs + list(extra_specs), out_specs, out_shapes, scratch, sem)(*flat, *extras)


def _mm_nn(name, a, b, out_dtype, tm=1024, tn=1024):
    M, K = a.shape
    N = b.shape[1]
    tm, tn = _tile(M, tm), _tile(N, tn, LANES)
    return _mm(name, "nn", [(a, b)], [(pl.BlockSpec((tm, K), lambda i, j: (i, 0)), pl.BlockSpec((K, tn), lambda i, j: (0, j)))],
               (M // tm, N // tn), None, [_sds((M, N), out_dtype)], [pl.BlockSpec((tm, tn), lambda i, j: (i, j))])[0]


def _mm_res(name, u, w, x, alpha, tm=512):
    T, K = u.shape
    D = w.shape[1]
    tm = _tile(T, tm)

    def epi(res, ex, outs):
        outs[0][...] = ex[0][...] + res

    return _mm(name, "nn", [(u, w)], [(pl.BlockSpec((tm, K), lambda i: (i, 0)), pl.BlockSpec((K, D), lambda i: (0, 0)))],
               (T // tm,), None, [_sds((T, D), F32)], [pl.BlockSpec((tm, D), lambda i: (i, 0))],
               extras=[x], extra_specs=[pl.BlockSpec((tm, D), lambda i: (i, 0))], epilogue=epi, alpha=alpha)[0]


def _mm_nt(name, pairs, out_dtype, tm=1024, tn=1024, alpha=1.0):
    M = pairs[0][0].shape[0]
    N = pairs[0][1].shape[0]
    tm, tn = _tile(M, tm), _tile(N, tn, LANES)
    specs = [(pl.BlockSpec((tm, a.shape[1]), lambda i, j: (i, 0)), pl.BlockSpec((tn, b.shape[1]), lambda i, j: (j, 0)))
             for a, b in pairs]
    return _mm(name, "nt", pairs, specs, (M // tm, N // tn), None, [_sds((M, N), out_dtype)],
               [pl.BlockSpec((tm, tn), lambda i, j: (i, j))], alpha=alpha)[0]


def _mm_nt3(name, a3, b3, tm=1024, tn=1024):
    S, M, K = a3.shape
    N = b3.shape[1]
    tm, tn = _tile(M, tm), _tile(N, tn, LANES)
    return _mm(name, "nt", [(a3, b3)],
               [(pl.BlockSpec((None, tm, K), lambda i, j, s: (s, i, 0)), pl.BlockSpec((None, tn, K), lambda i, j, s: (s, j, 0)))],
               (M // tm, N // tn, S), 2, [_sds((M, N), F32)], [pl.BlockSpec((tm, tn), lambda i, j, s: (i, j))],
               acc_shape=(tm, tn))[0]


def _mm_tn(name, a, b, alpha=1.0, tm=1408, tn=1408, tk=512):
    T, M = a.shape
    N = b.shape[1]
    tm, tn, tk = _tile(M, tm, LANES), _tile(N, tn, LANES), _tile(T, tk)
    return _mm(name, "tn", [(a, b)],
               [(pl.BlockSpec((tk, tm), lambda i, j, k: (k, i)), pl.BlockSpec((tk, tn), lambda i, j, k: (k, j)))],
               (M // tm, N // tn, T // tk), 2, [_sds((M, N), F32)], [pl.BlockSpec((tm, tn), lambda i, j, k: (i, j))],
               acc_shape=(tm, tn), alpha=alpha)[0]


def _mm_tn3(name, a, b3, tm=1408, tn=1408, tk=512):
    T, M = a.shape
    S, _, N = b3.shape
    tm, tn, tk = _tile(M, tm, LANES), _tile(N, tn, LANES), _tile(T, tk)
    return _mm(name, "tn", [(a, b3)],
               [(pl.BlockSpec((tk, tm), lambda s, i, j, k: (k, i)), pl.BlockSpec((None, tk, tn), lambda s, i, j, k: (s, k, j)))],
               (S, M // tm, N // tn, T // tk), 3, [_sds((S, M, N), F32)],
               [pl.BlockSpec((None, tm, tn), lambda s, i, j, k: (s, i, j))], acc_shape=(tm, tn))[0]


def _rms_fwd(name, x, g):
    T, D = x.shape
    tt = _tile(T, 512)

    def body(x_ref, g_ref, h_ref):
        xv = x_ref[...]
        r = lax.rsqrt(jnp.mean(xv * xv, axis=-1, keepdims=True) + RMS_EPS)
        h_ref[...] = (xv * r * g_ref[...]).astype(h_ref.dtype)

    row = pl.BlockSpec((tt, D), lambda i: (i, 0))
    return _call(body, name, (T // tt,), [row, pl.BlockSpec((1, D), lambda i: (0, 0))], row, _sds((T, D), BF16))(x, g)


def _rms_bwd(name, dh, x, g, dx_in):
    T, D = x.shape
    tt = _tile(T, 512)

    def body(dh_ref, x_ref, g_ref, dxi_ref, dx_ref, dxb_ref, dg_ref):
        xv = x_ref[...]
        r = lax.rsqrt(jnp.mean(xv * xv, axis=-1, keepdims=True) + RMS_EPS)
        xh = xv * r
        dhv = dh_ref[...]
        dxh = dhv * g_ref[...]
        dx = dxi_ref[...] + r * (dxh - xh * jnp.mean(dxh * xh, axis=-1, keepdims=True))
        dx_ref[...] = dx
        dxb_ref[...] = dx.astype(BF16)
        part = jnp.sum(dhv * xh, axis=0, keepdims=True)

        @pl.when(pl.program_id(0) == 0)
        def _():
            dg_ref[...] = part

        @pl.when(pl.program_id(0) > 0)
        def _():
            dg_ref[...] += part

    row = pl.BlockSpec((tt, D), lambda i: (i, 0))
    vec = pl.BlockSpec((1, D), lambda i: (0, 0))
    dx, dxb, dg = _call(body, name, (T // tt,), [row, row, vec, row], [row, row, vec],
                        [_sds((T, D), F32), _sds((T, D), BF16), _sds((1, D), F32)], sem=("arbitrary",))(dh, x, g, dx_in)
    return (dx, dxb), dg


def _loss_head(x, g, tgt):
    T, D = x.shape
    tt = _tile(T, 512)

    def body(x_ref, g_ref, t_ref, loss_ref, dx_ref, dxb_ref, dg_ref):
        xv = x_ref[...]
        r = lax.rsqrt(jnp.mean(xv * xv, axis=-1, keepdims=True) + RMS_EPS)
        xh = xv * r
        e = xh * g_ref[...] - t_ref[...]
        lp = 0.5 * jnp.sum(jnp.sum(e * e, axis=-1, keepdims=True), axis=0, keepdims=True) / D
        dy = e / D
        dxh = dy * g_ref[...]
        dx = r * (dxh - xh * jnp.mean(dxh * xh, axis=-1, keepdims=True))
        dx_ref[...] = dx
        dxb_ref[...] = dx.astype(BF16)
        part = jnp.sum(dy * xh, axis=0, keepdims=True)

        @pl.when(pl.program_id(0) == 0)
        def _():
            dg_ref[...] = part
            loss_ref[...] = lp

        @pl.when(pl.program_id(0) > 0)
        def _():
            dg_ref[...] += part
            loss_ref[...] += lp

    row = pl.BlockSpec((tt, D), lambda i: (i, 0))
    vec = pl.BlockSpec((1, D), lambda i: (0, 0))
    one = pl.BlockSpec((1, 1), lambda i: (0, 0))
    loss, dx, dxb, dg = _call(body, "loss_head", (T // tt,), [row, vec, row], [one, row, row, vec],
                              [_sds((1, 1), F32), _sds((T, D), F32), _sds((T, D), BF16), _sds((1, D), F32)],
                              sem=("arbitrary",))(x, g, tgt)
    return loss, (dx, dxb), dg


def _colsum3(name, a3):
    S, T, N = a3.shape
    tt = _tile(T, 512)

    def body(a_ref, o_ref):
        part = jnp.sum(a_ref[...].astype(F32), axis=0, keepdims=True)

        @pl.when(pl.program_id(1) == 0)
        def _():
            o_ref[...] = part

        @pl.when(pl.program_id(1) > 0)
        def _():
            o_ref[...] += part

    return _call(body, name, (S, T // tt), [pl.BlockSpec((None, tt, N), lambda s, i: (s, i, 0))],
                 pl.BlockSpec((None, 1, N), lambda s, i: (s, 0, 0)), _sds((S, 1, N), F32), sem=("parallel", "arbitrary"))(a3)


def _glu_mm(name, h, w2, bias2, mode, u_dtype, tm=1024, tn=256):
    T, K = h.shape
    N = w2.shape[2]
    tm, tn = _tile(T, tm), _tile(N, tn, LANES)
    has_bias = bias2 is not None

    def body(*refs):
        h_ref, w_ref = refs[0], refs[1]
        ab_ref, u_ref = refs[-2], refs[-1]
        hv = h_ref[...]
        a = _dot(hv, w_ref[0])
        b = _dot(hv, w_ref[1])
        if has_bias:
            a = a + refs[2][0]
            b = b + refs[2][1]
        u = a * _sig(a) * b if mode == "swiglu" else a * _sig(b)
        ab_ref[0] = a.astype(BF16)
        ab_ref[1] = b.astype(BF16)
        u_ref[...] = u.astype(u_ref.dtype)

    in_specs = [pl.BlockSpec((tm, K), lambda i, j: (i, 0)), pl.BlockSpec((2, K, tn), lambda i, j: (0, 0, j))]
    args = [h, w2]
    if has_bias:
        in_specs.append(pl.BlockSpec((2, 1, tn), lambda i, j: (0, 0, j)))
        args.append(bias2)
    return _call(body, name, (T // tm, N // tn), in_specs,
                 [pl.BlockSpec((2, tm, tn), lambda i, j: (0, i, j)), pl.BlockSpec((tm, tn), lambda i, j: (i, j))],
                 [_sds((2, T, N), BF16), _sds((T, N), u_dtype)], sem=("parallel", "parallel"))(*args)


def _swiglu_bwd_mm(name, dxb, wd, ab, alpha, tm=1024, tn=256):
    T, D = dxb.shape
    N = wd.shape[0]
    tm, tn = _tile(T, tm), _tile(N, tn, LANES)
    def body(dx_ref, wd_ref, ab_ref, dab_ref, u_ref):
        du = _dot(dx_ref[...], wd_ref[...], "nt") * alpha
        a = ab_ref[0].astype(F32)
        b = ab_ref[1].astype(F32)
        sg = _sig(a)
        sa = a * sg
        dab_ref[0] = (du * b * (sg * (1.0 + a * (1.0 - sg)))).astype(BF16)
        dab_ref[1] = (du * sa).astype(BF16)
        u_ref[...] = (sa * b).astype(BF16)

    ab_spec = pl.BlockSpec((2, tm, tn), lambda i, j: (0, i, j))
    return _call(body, name, (T // tm, N // tn),
                 [pl.BlockSpec((tm, D), lambda i, j: (i, 0)), pl.BlockSpec((tn, D), lambda i, j: (j, 0)), ab_spec],
                 [ab_spec, pl.BlockSpec((tm, tn), lambda i, j: (i, j))], [_sds((2, T, N), BF16), _sds((T, N), BF16)],
                 sem=("parallel", "parallel"))(dxb, wd, ab)


def _ffn_fwd(tag, x, g, wgu, wd):
    h = _rms_fwd(f"ffn_rms_{tag}", x, g)
    ab, u = _glu_mm(f"ffn_gu_{tag}", h, wgu, None, "swiglu", BF16)
    return _mm_res(f"ffn_down_{tag}", u, wd, x, 0.5), (x, h, ab)


def _ffn_bwd(tag, dxp, saved, g, wgu, wd):
    x, h, ab = saved
    dx, dxb = dxp
    dab, u = _swiglu_bwd_mm(f"ffn_dgu_{tag}", dxb, wd, ab, 0.5)
    dwd = _mm_tn(f"ffn_dwd_{tag}", u, dxb, alpha=0.5)
    dwgu = _mm_tn3(f"ffn_dwgu_{tag}", h, dab)
    dh = _mm_nt3(f"ffn_dh_{tag}", dab, wgu)
    dx2, dg = _rms_bwd(f"ffn_drms_{tag}", dh, x, g, dx)
    return dx2, dg, dwgu, dwd


HALO = 32


def _conv_fwd(u3, dw32, dwb, lng, lnb):
    B, S, D = u3.shape
    W = dw32.shape[0]
    taps = CONV_TAPS
    tt = _tile(S, 256, HALO)
    hb = tt // HALO

    def body(u_ref, halo_ref, dw_ref, dwb_ref, g_ref, b_ref, v_ref, s_ref):
        i = pl.program_id(1)
        halo = jnp.where(i > 0, halo_ref[...], 0.0)
        ext = jnp.concatenate([halo, u_ref[...]], axis=0)
        acc = jnp.zeros((tt, D), F32) + dwb_ref[...]
        for j in range(taps):
            acc = acc + dw_ref[pl.ds(j, 1), :] * _roll(ext, taps - 1 - j)[HALO:]
        v_ref[...] = acc
        mu = jnp.mean(acc, axis=-1, keepdims=True)
        xc = acc - mu
        ln = xc * lax.rsqrt(jnp.mean(xc * xc, axis=-1, keepdims=True) + LN_EPS) * g_ref[...] + b_ref[...]
        s_ref[...] = (ln * _sig(ln)).astype(BF16)

    main = pl.BlockSpec((None, tt, D), lambda b, i: (b, i, 0))
    halo = pl.BlockSpec((None, HALO, D), lambda b, i: (b, jnp.maximum(i * hb - 1, 0), 0))
    vec = pl.BlockSpec((1, D), lambda b, i: (0, 0))
    return _call(body, "conv_fwd", (B, S // tt), [main, halo, pl.BlockSpec((W, D), lambda b, i: (0, 0)), vec, vec, vec],
                 [main, main], [_sds((B, S, D), F32), _sds((B, S, D), BF16)], sem=("parallel", "parallel"))(
        u3, u3, dw32, dwb, lng, lnb)


def _conv_bwd_ln(v, ds, lng, lnb):
    T, D = v.shape
    tt = _tile(T, 256)

    def body(v_ref, ds_ref, g_ref, b_ref, dv_ref, red_ref):
        vv = v_ref[...]
        mu = jnp.mean(vv, axis=-1, keepdims=True)
        xc = vv - mu
        rstd = lax.rsqrt(jnp.mean(xc * xc, axis=-1, keepdims=True) + LN_EPS)
        xh = xc * rstd
        ln = xh * g_ref[...] + b_ref[...]
        sg = _sig(ln)
        dln = ds_ref[...] * (sg * (1.0 + ln * (1.0 - sg)))
        dxh = dln * g_ref[...]
        dv = rstd * (dxh - jnp.mean(dxh, axis=-1, keepdims=True) - xh * jnp.mean(dxh * xh, axis=-1, keepdims=True))
        dv_ref[...] = dv
        parts = (jnp.sum(dln * xh, axis=0, keepdims=True), jnp.sum(dln, axis=0, keepdims=True),
                 jnp.sum(dv, axis=0, keepdims=True))

        @pl.when(pl.program_id(0) == 0)
        def _():
            for k in range(3):
                red_ref[k] = parts[k]

        @pl.when(pl.program_id(0) > 0)
        def _():
            for k in range(3):
                red_ref[k] += parts[k]

    row = pl.BlockSpec((tt, D), lambda i: (i, 0))
    vec = pl.BlockSpec((1, D), lambda i: (0, 0))
    return _call(body, "conv_bwd_ln", (T // tt,), [row, row, vec, vec], [row, pl.BlockSpec((3, 1, D), lambda i: (0, 0, 0))],
                 [_sds((T, D), F32), _sds((3, 1, D), F32)], sem=("arbitrary",))(v, ds, lng, lnb)


def _conv_bwd_dw(dv3, u3, ab, dw32, carried=None):
    B, S, D = u3.shape
    W = dw32.shape[0]
    taps = CONV_TAPS
    tt = _tile(S, 256, HALO)
    hb = tt // HALO
    nt = S // tt
    L = tt + HALO

    def body(dv_ref, dvn_ref, u_ref, up_ref, ab_ref, dw_ref, dab_ref, ddw_ref):
        b, i = pl.program_id(0), pl.program_id(1)
        dv = dv_ref[...]
        ext_dv = jnp.concatenate([dv, jnp.where(i < nt - 1, dvn_ref[...], 0.0)], axis=0)
        ext_u = jnp.concatenate([jnp.where(i > 0, up_ref[...], 0.0), u_ref[...]], axis=0)
        du = jnp.zeros((tt, D), F32)
        first = jnp.logical_and(b == 0, i == 0)

        @pl.when(first)
        def _():
            ddw_ref[...] = jnp.zeros((W, D), F32)

        for j in range(taps):
            sh = taps - 1 - j
            du = du + dw_ref[pl.ds(j, 1), :] * _roll(ext_dv, L - sh)[:tt]
            ddw_ref[pl.ds(j, 1), :] += jnp.sum(dv * _roll(ext_u, sh)[HALO:], axis=0, keepdims=True)
        a = ab_ref[0].astype(F32)
        sb = _sig(ab_ref[1].astype(F32))
        dab_ref[0] = (du * sb).astype(BF16)
        dab_ref[1] = (du * a * sb * (1.0 - sb)).astype(BF16)

    main = pl.BlockSpec((None, tt, D), lambda b, i: (b, i, 0))
    prev = pl.BlockSpec((None, HALO, D), lambda b, i: (b, jnp.maximum(i * hb - 1, 0), 0))
    nxt = pl.BlockSpec((None, HALO, D), lambda b, i: (b, jnp.minimum((i + 1) * hb, S // HALO - 1), 0))
    abs_ = pl.BlockSpec((2, tt, D), lambda b, i: (0, b * nt + i, 0))
    wsp = pl.BlockSpec((W, D), lambda b, i: (0, 0))
    specs = ((B, nt), [main, nxt, main, prev, abs_, wsp], [abs_, wsp], [_sds((2, B * S, D), BF16), _sds((W, D), F32)], [])
    args = (dv3, dv3, u3, u3, ab, dw32)
    if carried is None:
        return _call(body, "conv_bwd_dw", *specs, sem=("arbitrary", "arbitrary"))(*args)
    dab, ddw, *got = _carry_call(body, "conv_bwd_dw", *specs, "scatter", carried)(*args, *carried)
    return dab, ddw, got


CONV_TAPS = 31


def _conv_mixer_fwd(x, B, p):
    T, D = x.shape
    h = _rms_fwd("conv_rms", x, p["mix_norm"][0:1])
    ab, u = _glu_mm("conv_in", h, p["conv_w_in2"], p["conv_b_in2"], "glu", F32)
    v, s = _conv_fwd(u.reshape(B, T // B, D), p["conv_dw32"], p["conv_dw_b"], p["conv_ln_g"], p["conv_ln_b"])
    v, s = v.reshape(T, D), s.reshape(T, D)
    return _mm_res("conv_out", s, p["conv_w_out"], x, 1.0), (x, h, ab, u, v, s)


def _conv_mixer_bwd(dxp, saved, B, p):
    x, h, ab, u, v, s = saved
    T, D = x.shape
    dx, dxb = dxp
    g = {}
    ds = _mm_nt("conv_ds", [(dxb, p["conv_w_out"])], F32)
    g["conv_w_out"] = _mm_tn("conv_dwout", s, dxb)
    dv, red = _conv_bwd_ln(v, ds, p["conv_ln_g"], p["conv_ln_b"])
    g["conv_ln_g"], g["conv_ln_b"], g["conv_dw_b"] = red[0], red[1], red[2]
    if "scatter" in p:
        dab, ddw, p["scattered"] = _conv_bwd_dw(dv.reshape(B, T // B, D), u.reshape(B, T // B, D), ab, p["conv_dw32"],
                                               p.pop("scatter"))
    else:
        dab, ddw = _conv_bwd_dw(dv.reshape(B, T // B, D), u.reshape(B, T // B, D), ab, p["conv_dw32"])
    g["conv_dw"] = ddw[:CONV_TAPS][None]
    g["conv_b_in"] = _colsum3("conv_dbin", dab).reshape(1, 2 * D)
    dwin = _mm_tn3("conv_dwin", h, dab)
    g["conv_w_in"] = jnp.moveaxis(dwin, 0, 1).reshape(1, D, 2 * D)
    dh = _mm_nt3("conv_dh", dab, p["conv_w_in2"])
    dx2, dg = _rms_bwd("conv_drms", dh, x, p["mix_norm"][0:1], dx)
    return dx2, dg, g


def _log_sigmoid(z):
    return jnp.minimum(z, 0.0) - jnp.log(1.0 + jnp.exp(-jnp.abs(z)))


def _fox_gate_fwd(fl3, bf):
    B, S, N = fl3.shape
    tt = _tile(S, 512)

    def body(fl_ref, bf_ref, c_ref, carry):
        @pl.when(pl.program_id(1) == 0)
        def _():
            carry[...] = jnp.zeros((1, N), F32)

        c = _scan_rows(_log_sigmoid(fl_ref[...] + bf_ref[...])) + carry[...]
        c_ref[...] = c
        carry[...] = c_ref[pl.ds(tt - 1, 1), :]

    row = pl.BlockSpec((None, tt, N), lambda b, i: (b, i, 0))
    return _call(body, "fox_gate_fwd", (B, S // tt), [row, pl.BlockSpec((1, N), lambda b, i: (0, 0))], row,
                 _sds((B, S, N), F32), [pltpu.VMEM((1, N), F32)], sem=("parallel", "arbitrary"))(fl3, bf)


def _fox_gate_bwd(dc3, fl3, bf):
    B, S, N = fl3.shape
    tt = _tile(S, 512)
    nt = S // tt

    def body(dc_ref, fl_ref, bf_ref, dfl_ref, dbf_ref, carry):
        b, i = pl.program_id(0), pl.program_id(1)

        @pl.when(i == 0)
        def _():
            carry[...] = jnp.zeros((1, N), F32)

        dc = dc_ref[0] - dc_ref[1]
        dlf = _scan_rows(dc, reverse=True) + carry[...]
        dfl = dlf * _sig(-(fl_ref[...] + bf_ref[...]))
        dfl_ref[...] = dfl
        carry[...] += jnp.sum(dc, axis=0, keepdims=True)
        part = jnp.sum(dfl, axis=0, keepdims=True)

        @pl.when(jnp.logical_and(b == 0, i == 0))
        def _():
            dbf_ref[...] = part

        @pl.when(jnp.logical_or(b > 0, i > 0))
        def _():
            dbf_ref[...] += part

    row = pl.BlockSpec((None, tt, N), lambda b, i: (b, nt - 1 - i, 0))
    vec = pl.BlockSpec((1, N), lambda b, i: (0, 0))
    row2 = pl.BlockSpec((2, None, tt, N), lambda b, i: (0, b, nt - 1 - i, 0))
    return _call(body, "fox_gate_bwd", (B, nt), [row2, row, vec], [row, vec], [_sds((B, S, N), F32), _sds((1, N), F32)],
                 [pltpu.VMEM((1, N), F32)], sem=("arbitrary", "arbitrary"))(dc3, fl3, bf)


NEG = -1e30


def _fox_attn_fwd(qa, ka, vat, crow, ckb, dh, scale, carried=None):
    B, H, S, P = qa.shape
    tq = _tile(S, 512, LANES)
    tk = tq
    nl = tq // LANES

    def body(q_ref, k_ref, vt_ref, cr_ref, ck_ref, o_ref, lse_ref, s_scr, p_scr, m_scr, al_scr, acc_scr):
        i = pl.program_id(2)
        qv = q_ref[...]
        m_scr[...] = jnp.full((1, tq), NEG, F32)
        acc_scr[...] = jnp.zeros((P, tq), F32)
        key = lax.broadcasted_iota(jnp.int32, (tk, LANES), 0)
        qry = lax.broadcasted_iota(jnp.int32, (tk, LANES), 1)

        def kv_step(j, diagonal):
            off = pl.multiple_of(j * tk, tk)
            s_scr[...] = _dot(k_ref[pl.ds(off, tk), :], qv, "nt")
            ck = ck_ref[pl.ds(off, tk), :]
            for lt in range(nl):
                ls = slice(lt * LANES, (lt + 1) * LANES)
                s = s_scr[:, ls] * scale + cr_ref[:, ls] - ck
                if diagonal:
                    s = jnp.where(key <= qry + lt * LANES, s, -jnp.inf)
                m1 = m_scr[:, ls]
                m2 = jnp.maximum(m1, jnp.max(s, axis=0, keepdims=True))
                p_scr[:, ls] = jnp.exp(s - m2).astype(BF16)
                al_scr[:, ls] = jnp.exp(m1 - m2)
                m_scr[:, ls] = m2
            acc_scr[...] = al_scr[...] * acc_scr[...] + _dot(vt_ref[:, pl.ds(off, tk)], p_scr[...])

        def before(j, carry):
            kv_step(j, False)
            return carry

        lax.fori_loop(0, i, before, 0)
        kv_step(i, True)
        l = acc_scr[pl.ds(dh, 1), :]
        o_ref[...] = (acc_scr[...] / l).T
        lse_ref[...] = m_scr[...] + jnp.log(l)

    qs = pl.BlockSpec((None, None, tq, P), lambda b, h, i: (b, h, i, 0))
    fullk = pl.BlockSpec((None, None, S, P), lambda b, h, i: (b, h, 0, 0))
    fullt = pl.BlockSpec((None, None, P, S), lambda b, h, i: (b, h, 0, 0))
    fullc = pl.BlockSpec((None, None, S, LANES), lambda b, h, i: (b, h, 0, 0))
    rowt = pl.BlockSpec((None, None, 1, tq), lambda b, h, i: (b, h, 0, i))
    scratch = [pltpu.VMEM((tk, tq), F32), pltpu.VMEM((tk, tq), BF16), pltpu.VMEM((1, tq), F32), pltpu.VMEM((1, tq), F32),
               pltpu.VMEM((P, tq), F32)]
    specs = ((B, H, S // tq), [qs, fullk, fullt, rowt, fullc], [qs, rowt], [_sds((B, H, S, P), F32), _sds((B, H, 1, S), F32)],
             scratch)
    if carried is None:
        return _call(body, "fox_attn_fwd", *specs, sem=("parallel", "parallel", "parallel"))(qa, ka, vat, crow, ckb)
    o, lse, *got = _carry_call(body, "fox_attn_fwd", *specs, "gather", carried)(qa, ka, vat, crow, ckb, *carried)
    return o, lse, got


def _fox_rowstats(do, o):
    B, H, S, P = o.shape
    tq = _tile(S, 4096)

    def body(do_ref, o_ref, dl_ref):
        dl_ref[...] = jnp.sum(do_ref[...] * o_ref[...], axis=-1, keepdims=True)

    qs = pl.BlockSpec((None, None, tq, P), lambda b, h, i: (b, h, i, 0))
    col = pl.BlockSpec((None, None, tq, 1), lambda b, h, i: (b, h, i, 0))
    return _call(body, "fox_rowstats", (B, H, S // tq), [qs, qs], col, _sds((B, H, S, 1), F32), sem=("parallel",) * 3)(do, o)


def _fox_attn_bwd(qa, ka, kat, va, do, crow, lse, delta, ckb, dh, scale, carried=None):
    B, H, S, P = qa.shape
    tk = _tile(S, 512, LANES)
    tq = tk
    nq = S // tq
    nl = tq // LANES

    def body(q_ref, k_ref, kt_ref, v_ref, do_ref, cr_ref, lse_ref, dl_ref, ck_ref, dqt_ref, dk_ref, dv_ref, rs_ref,
             s_scr, dp_scr, p_scr, ds_scr, dk_acc, dv_acc):
        j = pl.program_id(2)

        @pl.when(j == 0)
        def _():
            dqt_ref[...] = jnp.zeros((P, S), F32)

        kj, vj, ck = k_ref[...], v_ref[...], ck_ref[...]
        dk_acc[...] = jnp.zeros((tk, P), F32)
        dv_acc[...] = jnp.zeros((tk, P), F32)
        key = lax.broadcasted_iota(jnp.int32, (tk, LANES), 0)
        qry = lax.broadcasted_iota(jnp.int32, (tk, LANES), 1)

        def q_step(i, diagonal):
            off = pl.multiple_of(i * tq, tq)
            qi, doi = q_ref[pl.ds(off, tq), :], do_ref[pl.ds(off, tq), :].astype(BF16)
            s_scr[...] = _dot(kj, qi, "nt")
            dp_scr[...] = _dot(vj, doi, "nt")
            for lt in range(nl):
                ls = slice(lt * LANES, (lt + 1) * LANES)
                gl = pl.ds(pl.multiple_of(off + lt * LANES, LANES), LANES)
                p = jnp.exp(s_scr[:, ls] * scale + (cr_ref[:, gl] - lse_ref[:, gl]) - ck)
                if diagonal:
                    p = jnp.where(key <= qry + lt * LANES, p, 0.0)
                p_scr[:, ls] = p.astype(BF16)
                ds_scr[:, ls] = (p * (dp_scr[:, ls] - dl_ref[:, gl])).astype(BF16)
            ds = ds_scr[...]
            dqt_ref[:, pl.ds(off, tq)] += _dot(kt_ref[...], ds)
            dk_acc[...] += _dot(ds, qi)
            dv_acc[...] += _dot(p_scr[...], doi)

        def after(i, carry):
            q_step(i, False)
            return carry

        q_step(j, True)
        lax.fori_loop(j + 1, nq, after, 0)
        dk_ref[...] = dk_acc[...] * jnp.where(lax.broadcasted_iota(jnp.int32, (tk, P), 1) < dh, scale, 1.0)
        dv_ref[...] = dv_acc[...]

        @pl.when(j == S // tk - 1)
        def _():
            rs_ref[...] = dqt_ref[pl.ds(dh, 1), :]
            dqt_ref[...] = dqt_ref[...] * jnp.where(lax.broadcasted_iota(jnp.int32, (P, S), 0) < dh, scale, 1.0)

    ks = pl.BlockSpec((None, None, tk, P), lambda b, h, j: (b, h, j, 0))
    kts = pl.BlockSpec((None, None, P, tk), lambda b, h, j: (b, h, 0, j))
    cks = pl.BlockSpec((None, None, tk, LANES), lambda b, h, j: (b, h, j, 0))
    full = pl.BlockSpec((None, None, S, P), lambda b, h, j: (b, h, 0, 0))
    fullt = pl.BlockSpec((None, None, P, S), lambda b, h, j: (b, h, 0, 0))
    rowf = pl.BlockSpec((None, None, 1, S), lambda b, h, j: (b, h, 0, 0))
    scratch = [pltpu.VMEM((tk, tq), F32), pltpu.VMEM((tk, tq), F32), pltpu.VMEM((tk, tq), BF16), pltpu.VMEM((tk, tq), BF16),
               pltpu.VMEM((tk, P), F32), pltpu.VMEM((tk, P), F32)]
    specs = ((B, H, S // tk), [full, ks, kts, ks, full, rowf, rowf, rowf, cks], [fullt, ks, ks, rowf],
             [_sds((B, H, P, S), F32), _sds((B, H, S, P), F32), _sds((B, H, S, P), F32), _sds((B, H, 1, S), F32)], scratch)
    args = (qa, ka, kat, va, do, crow, lse, delta, ckb)
    if carried is None:
        return _call(body, "fox_attn_bwd", *specs, sem=("parallel", "parallel", "arbitrary"))(*args)
    dqt, dk, dv, rs, *got = _carry_call(body, "fox_attn_bwd", *specs, "scatter", carried)(*args, *carried)
    return dqt, dk, dv, rs, got


def _heads(t, B, H):
    T, D = t.shape
    return t.reshape(B, T // B, H, D // H).transpose(0, 2, 1, 3)


def _unheads(t):
    B, H, S, dh = t.shape
    return t.transpose(0, 2, 1, 3).reshape(B * S, H * dh)


def _fox_mixer_fwd(x, B, p):
    T, D = x.shape
    H = FOX_HEADS
    S = T // B
    scale = (D // H) ** -0.5
    h = _rms_fwd("fox_rms", x, p["mix_norm"][1:2])
    qkv = _mm_nn("fox_qkv", h, p["fox_w_qkv"], BF16)
    fl = _mm_nn("fox_fl", h, p["fox_w_f"], F32)
    c = _fox_gate_fwd(fl.reshape(B, S, LANES), p["fox_b_f128"])
    ch = c[:, :, :H].transpose(0, 2, 1)
    crow = ch[:, :, None, :]
    ckb = jnp.broadcast_to(ch[..., None], (B, H, S, LANES))
    q, k, v = _heads(qkv[:, :D], B, H), _heads(qkv[:, D:2 * D], B, H), _heads(qkv[:, 2 * D:], B, H)
    dh = D // H
    P = -(-(dh + 2) // LANES) * LANES
    one, zero = jnp.ones((B, H, S, 1), q.dtype), jnp.zeros((B, H, S, 1), q.dtype)
    rest = jnp.zeros((B, H, S, P - dh - 2), q.dtype)
    qa = jnp.concatenate([q, zero, one, rest], axis=-1)
    ka = jnp.concatenate([k, one, zero, rest], axis=-1)
    va = jnp.concatenate([v, one, zero, rest], axis=-1)
    if "late_gather" in p:
        o, lse, p["late_gathered"] = _fox_attn_fwd(qa, ka, va.transpose(0, 1, 3, 2), crow, ckb, dh, scale, p.pop("late_gather"))
    else:
        o, lse = _fox_attn_fwd(qa, ka, va.transpose(0, 1, 3, 2), crow, ckb, dh, scale)
    of = _unheads(o[..., :dh])
    return _mm_res("fox_out", of, p["fox_w_out"], x, 1.0), (x, h, fl, qa, ka, va, crow, ckb, o, lse, of)


def _fox_mixer_bwd(dxp, saved, B, p):
    x, h, fl, qa, ka, va, crow, ckb, o, lse, of = saved
    T, D = x.shape
    H = FOX_HEADS
    S = T // B
    dh_ = D // H
    P = qa.shape[-1]
    scale = dh_ ** -0.5
    dx, dxb = dxp
    g = {}
    do = _heads(_mm_nt("fox_do", [(dxb, p["fox_w_out"])], F32), B, H)
    do = jnp.pad(do, ((0, 0), (0, 0), (0, 0), (0, P - dh_)))
    g["fox_w_out"] = _mm_tn("fox_dwout", of, dxb)[None]
    delta = _fox_rowstats(do, o).reshape(B, H, 1, S)
    if "scatter" in p:
        dqt, dk, dv, rowsum, p["scattered"] = _fox_attn_bwd(qa, ka, ka.transpose(0, 1, 3, 2), va, do, crow, lse, delta, ckb,
                                                          dh_, scale, p.pop("scatter"))
    else:
        dqt, dk, dv, rowsum = _fox_attn_bwd(qa, ka, ka.transpose(0, 1, 3, 2), va, do, crow, lse, delta, ckb, dh_, scale)
    dq = dqt[:, :, :dh_, :].transpose(0, 3, 1, 2).reshape(T, D)
    dqkv = jnp.concatenate([dq, _unheads(dk[..., :dh_]), _unheads(dv[..., :dh_])], axis=1).astype(BF16)
    dc = jnp.stack([rowsum[:, :, 0, :], dk[..., dh_ + 1]])
    dc = jnp.pad(dc.transpose(0, 1, 3, 2), ((0, 0), (0, 0), (0, 0), (0, LANES - H)))
    dfl, dbf = _fox_gate_bwd(dc, fl.reshape(B, S, LANES), p["fox_b_f128"])
    dfl = dfl.reshape(T, LANES)
    g["fox_b_f"] = dbf[:, :H]
    dwqkv = _mm_tn("fox_dwqkv", h, dqkv)
    dwf = _mm_tn("fox_dwf", h, dfl)
    g["fox_w_in"] = jnp.concatenate([dwqkv, dwf[:, :H]], axis=1)[None]
    dh = _mm_nt("fox_dh", [(dqkv, p["fox_w_qkv"]), (dfl, p["fox_w_f"])], F32)
    dx2, dg = _rms_bwd("fox_drms", dh, x, p["mix_norm"][1:2], dx)
    return dx2, dg, g


def _lb_fwd(logits):
    L, D = logits.shape

    def body(l_ref, lb_ref):
        z = l_ref[...]
        e = jnp.exp(z - jnp.max(z, axis=0, keepdims=True))
        p = e / jnp.sum(e, axis=0, keepdims=True)
        lb_ref[...] = jnp.sum(jnp.where(_lb_rows(z.shape), p, 0.0), axis=0, keepdims=True)

    return _call(body, "hgrn_lb", (1,), [pl.BlockSpec((L, D), lambda i: (0, 0))], pl.BlockSpec((1, D), lambda i: (0, 0)),
                 _sds((1, D), F32))(logits)


def _lb_rows(shape):
    r = lax.broadcasted_iota(jnp.int32, shape, 0)
    return jnp.logical_and(r >= 1, r <= HGRN_LAYER)


HGRN_LAYER = 2


def _lb_bwd(logits, dlb):
    L, D = logits.shape

    def body(l_ref, d_ref, o_ref):
        z = l_ref[...]
        e = jnp.exp(z - jnp.max(z, axis=0, keepdims=True))
        p = e / jnp.sum(e, axis=0, keepdims=True)
        dp = jnp.where(_lb_rows(z.shape), d_ref[...], 0.0)
        o_ref[...] = p * (dp - jnp.sum(p * dp, axis=0, keepdims=True))

    full = pl.BlockSpec((L, D), lambda i: (0, 0))
    return _call(body, "hgrn_dlb", (1,), [full, pl.BlockSpec((1, D), lambda i: (0, 0))], full, _sds((L, D), F32))(logits, dlb)


def _hgrn_gates(qr, fr, lbv):
    e = jnp.exp(-jnp.abs(fr))
    big, small = 1.0 / (1.0 + e), e / (1.0 + e)
    sf = jnp.where(fr >= 0, big, small)
    snf = jnp.where(fr >= 0, small, big)
    f = lbv + (1.0 - lbv) * sf
    sq = _sig(qr)
    return qr * sq, (1.0 - lbv) * snf, jnp.log(f), sf, snf, f, sq


def _hgrn_intra(G, q, kk, g_scr, q_scr):
    C = HGRN_CHUNK
    g_scr[...] = G
    q_scr[...] = q
    srow = lax.broadcasted_iota(jnp.int32, (C, LANES), 0)
    lane = lax.broadcasted_iota(jnp.int32, (C, LANES), 1)
    at = jnp.zeros((C, LANES), F32)
    for t in range(C):
        e = jnp.where(srow <= t, jnp.exp(g_scr[pl.ds(t, 1), :] - G), 0.0)
        col = jnp.sum(e * kk * q_scr[pl.ds(t, 1), :], axis=-1, keepdims=True)
        at = jnp.where(lane == t, col, at)
    return at


def _hgrn_fwd(proj3, lb, ng):
    B, S, D4 = proj3.shape
    D = D4 // 4
    H = D // HGRN_EXPAND
    C = HGRN_CHUNK
    R = _tile(S, 256, 2 * C)
    ncb = R // C
    dk = HGRN_EXPAND

    def body(q_ref, f_ref, i_ref, go_ref, lb_ref, ng_ref, y_ref, o_ref, st_ref, st, g_scr, q_scr):
        @pl.when(pl.program_id(2) == 0)
        def _():
            st[...] = jnp.zeros((dk, dk), F32)

        lbv = lb_ref[...]

        def chunk(c, slot):
            r0 = pl.multiple_of(c * C, C)
            rows = pl.ds(r0, C)
            q, kk, lf, *_ = _hgrn_gates(q_ref[rows, :], f_ref[rows, :], lbv)
            vv = i_ref[rows, :]
            G = _scan_rows(lf)
            at = _hgrn_intra(G, q, kk, g_scr.at[slot], q_scr.at[slot])
            gl = g_scr[slot, pl.ds(C - 1, 1), :]
            stv = st[...]
            st_ref[c] = stv
            o = _dot(q * jnp.exp(G), stv, "nt") + _dot(at, vv, "tn")[:C]
            st[...] = stv * jnp.exp(gl) + _dot(vv, kk * jnp.exp(gl - G), "tn")
            o_ref[rows, :] = o
            gv = go_ref[rows, :]
            y = o * lax.rsqrt(jnp.mean(o * o, axis=-1, keepdims=True) + RMS_EPS) * ng_ref[...] * (gv * _sig(gv))
            y_ref[rows, :] = y.astype(BF16)

        def pair(c2, carry):
            chunk(2 * c2, 0)
            chunk(2 * c2 + 1, 1)
            return carry

        lax.fori_loop(0, ncb // 2, pair, 0)

    def col(k):
        return pl.BlockSpec((None, R, dk), lambda b, h, i: (b, i, h + k * H))

    vec = pl.BlockSpec((1, dk), lambda b, h, i: (0, h))
    out = pl.BlockSpec((None, R, dk), lambda b, h, i: (b, i, h))
    return _call(body, "hgrn_fwd", (B, H, S // R), [col(0), col(1), col(2), col(3), vec, vec],
                 [out, out, pl.BlockSpec((None, None, ncb, dk, dk), lambda b, h, i: (b, h, i, 0, 0))],
                 [_sds((B, S, D), BF16), _sds((B, S, D), F32), _sds((B, H, S // C, dk, dk), F32)],
                 [pltpu.VMEM((dk, dk), F32), pltpu.VMEM((2, C, dk), F32), pltpu.VMEM((2, C, dk), F32)],
                 sem=("parallel", "parallel", "arbitrary"))(proj3, proj3, proj3, proj3, lb, ng)


def _hgrn_bwd(proj3, o3, dy3, states, lb, ng, carried=None):
    B, S, D4 = proj3.shape
    D = D4 // 4
    H = D // HGRN_EXPAND
    C = HGRN_CHUNK
    R = _tile(S, 256, 2 * C)
    ncb = R // C
    nb = S // R
    dk = HGRN_EXPAND

    def body(q_ref, f_ref, i_ref, go_ref, o_ref, dy_ref, st_ref, lb_ref, ng_ref,
             dp_ref, red_ref, dst, g_scr, q_scr, dq_scr, acc):
        b, i = pl.program_id(1), pl.program_id(2)

        @pl.when(i == 0)
        def _():
            dst[...] = jnp.zeros((dk, dk), F32)

        @pl.when(jnp.logical_and(b == 0, i == 0))
        def _():
            acc[...] = jnp.zeros((2, dk), F32)

        lbv = lb_ref[...]
        ngv = ng_ref[...]
        srow = lax.broadcasted_iota(jnp.int32, (C, LANES), 0)
        lane = lax.broadcasted_iota(jnp.int32, (C, LANES), 1)

        def chunk(c, slot):
            r0 = pl.multiple_of(c * C, C)
            rows = pl.ds(r0, C)
            qr, fr, vv, gv = q_ref[rows, :], f_ref[rows, :], i_ref[rows, :], go_ref[rows, :]
            q, kk, lf, sf, snf, f, sq = _hgrn_gates(qr, fr, lbv)
            o = o_ref[rows, :]
            dy = dy_ref[rows, :]
            rinv = lax.rsqrt(jnp.mean(o * o, axis=-1, keepdims=True) + RMS_EPS)
            on = o * rinv
            sgv = _sig(gv)
            dz = dy * (gv * sgv)
            dp_ref[3, rows, :] = (dy * on * ngv * (sgv * (1.0 + gv * (1.0 - sgv)))).astype(BF16)
            acc[pl.ds(1, 1), :] += jnp.sum(dz * on, axis=0, keepdims=True)
            don = dz * ngv
            do = rinv * (don - on * jnp.mean(don * on, axis=-1, keepdims=True))
            G = _scan_rows(lf)
            g_scr[slot] = G
            q_scr[slot] = q
            gl = g_scr[slot, pl.ds(C - 1, 1), :]
            egl = jnp.exp(gl)
            eG = jnp.exp(G)
            eK = jnp.exp(gl - G)
            qg, kg = q * eG, kk * eK
            stv = st_ref[c]
            dsv = dst[...]
            dqg = _dot(do, stv)
            do_pad = jnp.concatenate([do, jnp.zeros((LANES - C, dk), F32)], axis=0)
            dat = _dot(vv, do_pad, "nt")
            dkg = _dot(vv, dsv)
            dgl = egl * jnp.sum(stv * dsv, axis=0, keepdims=True) + jnp.sum(dkg * kg, axis=0, keepdims=True)
            at = jnp.zeros((C, LANES), F32)
            dki = jnp.zeros((C, dk), F32)
            for t in range(C):
                e = jnp.where(srow <= t, jnp.exp(g_scr[slot, pl.ds(t, 1), :] - G), 0.0)
                qt = q_scr[slot, pl.ds(t, 1), :]
                at = jnp.where(lane == t, jnp.sum(e * kk * qt, axis=-1, keepdims=True), at)
                z = e * jnp.sum(jnp.where(lane == t, dat, 0.0), axis=-1, keepdims=True)
                dq_scr[slot, pl.ds(t, 1), :] = jnp.sum(z * kk, axis=0, keepdims=True)
                dki = dki + z * qt
            dqi = dq_scr[slot]
            dp_ref[2, rows, :] = (_dot(at, do_pad) + _dot(kg, dsv, "nt")).astype(BF16)
            dst[...] = dsv * egl + _dot(do, qg, "tn")
            dq = dqg * eG + dqi
            dkk = dkg * eK + dki
            dG = dqg * qg - dkg * kg + q * dqi - kk * dki
            dG = dG + jnp.where(srow == C - 1, dgl, 0.0)
            dlf = _scan_rows(dG, reverse=True)
            dsf = (1.0 - lbv) * sf * snf
            dp_ref[1, rows, :] = (dlf * dsf / f - dkk * dsf).astype(BF16)
            acc[pl.ds(0, 1), :] += jnp.sum(dlf * snf / f - dkk * snf, axis=0, keepdims=True)
            dp_ref[0, rows, :] = (dq * (sq * (1.0 + qr * (1.0 - sq)))).astype(BF16)

        def pair(c2, carry):
            chunk(ncb - 1 - 2 * c2, 0)
            chunk(ncb - 2 - 2 * c2, 1)
            return carry

        lax.fori_loop(0, ncb // 2, pair, 0)

        @pl.when(jnp.logical_and(b == B - 1, i == nb - 1))
        def _():
            red_ref[0] = acc[pl.ds(0, 1), :]
            red_ref[1] = acc[pl.ds(1, 1), :]

    def col(k):
        return pl.BlockSpec((None, R, dk), lambda h, b, i: (b, nb - 1 - i, h + k * H))

    vec = pl.BlockSpec((1, dk), lambda h, b, i: (0, h))
    row = pl.BlockSpec((None, R, dk), lambda h, b, i: (b, nb - 1 - i, h))
    stsp = pl.BlockSpec((None, None, ncb, dk, dk), lambda h, b, i: (b, h, nb - 1 - i, 0, 0))
    specs = ((H, B, nb), [col(0), col(1), col(2), col(3), row, row, stsp, vec, vec],
             [pl.BlockSpec((4, None, R, dk), lambda h, b, i: (0, b, nb - 1 - i, h)),
              pl.BlockSpec((2, 1, dk), lambda h, b, i: (0, 0, h))],
             [_sds((4, B, S, D), BF16), _sds((2, 1, D), F32)],
             [pltpu.VMEM((dk, dk), F32), pltpu.VMEM((2, C, dk), F32), pltpu.VMEM((2, C, dk), F32),
              pltpu.VMEM((2, C, dk), F32), pltpu.VMEM((2, dk), F32)])
    args = (proj3, proj3, proj3, proj3, o3, dy3, states, lb, ng)
    if carried is None:
        return _call(body, "hgrn_bwd", *specs, sem=("parallel", "arbitrary", "arbitrary"))(*args)
    dp, red, *got = _carry_call(body, "hgrn_bwd", *specs, "scatter", carried)(*args, *carried)
    return dp, red, got


def _hgrn_mixer_fwd(x, B, p):
    T, D = x.shape
    S = T // B
    h = _rms_fwd("hgrn_rms", x, p["mix_norm"][2:3])
    proj = _mm_nn("hgrn_in", h, p["hgrn_w_in"], F32)
    lb = _lb_fwd(p["hgrn_lb_logits"])
    y, o, states = _hgrn_fwd(proj.reshape(B, S, 4 * D), lb, p["hgrn_norm"])
    y = y.reshape(T, D)
    return _mm_res("hgrn_out", y, p["hgrn_w_out"], x, 1.0), (x, h, proj, lb, y, o, states)


def _hgrn_mixer_bwd(dxp, saved, B, p):
    x, h, proj, lb, y, o, states = saved
    T, D = x.shape
    S = T // B
    dx, dxb = dxp
    g = {}
    dy = _mm_nt("hgrn_dy", [(dxb, p["hgrn_w_out"])], F32)
    g["hgrn_w_out"] = _mm_tn("hgrn_dwout", y, dxb)[None]
    if "scatter" in p:
        dp, red, p["scattered"] = _hgrn_bwd(proj.reshape(B, S, 4 * D), o, dy.reshape(B, S, D), states, lb, p["hgrn_norm"],
                                           p.pop("scatter"))
    else:
        dp, red = _hgrn_bwd(proj.reshape(B, S, 4 * D), o, dy.reshape(B, S, D), states, lb, p["hgrn_norm"])
    dp = dp.reshape(4, T, D)
    g["hgrn_norm"] = red[1]
    g["hgrn_lb_logits"] = _lb_bwd(p["hgrn_lb_logits"], red[0])
    dwin = _mm_tn3("hgrn_dwin", h, dp)
    g["hgrn_w_in"] = jnp.moveaxis(dwin, 0, 1).reshape(1, D, 4 * D)
    dh = _mm_nt3("hgrn_dh", dp, p["hgrn_w_in4"])
    dx2, dg = _rms_bwd("hgrn_drms", dh, x, p["mix_norm"][2:3], dx)
    return dx2, dg, g


POOL_HALO = 16


def _pool_fwd(x3, g):
    B, S, D = x3.shape
    tt = _tile(S, 256, POOL_HALO)
    hb = tt // POOL_HALO
    G = D // len(POOL_WINDOWS)

    def body(x_ref, halo_ref, g_ref, m_ref):
        i = pl.program_id(1)

        def norm(xv):
            return xv * lax.rsqrt(jnp.mean(xv * xv, axis=-1, keepdims=True) + RMS_EPS) * g_ref[...]

        hm = norm(x_ref[...])
        ext = jnp.concatenate([jnp.where(i > 0, norm(halo_ref[...]), 0.0), hm], axis=0)
        pos = (i * tt + lax.broadcasted_iota(jnp.int32, (tt, 1), 0) + 1).astype(F32)
        for gi, win in enumerate(POOL_WINDOWS):
            s = ext[:, gi * G:(gi + 1) * G]
            w = 1
            while w < win:
                s = s + _roll(s, w)
                w *= 2
            m_ref[:, gi * G:(gi + 1) * G] = (s[POOL_HALO:] / jnp.minimum(pos, float(win)) - hm[:, gi * G:(gi + 1) * G]).astype(BF16)

    main = pl.BlockSpec((None, tt, D), lambda b, i: (b, i, 0))
    halo = pl.BlockSpec((None, POOL_HALO, D), lambda b, i: (b, jnp.maximum(i * hb - 1, 0), 0))
    return _call(body, "pool_fwd", (B, S // tt), [main, halo, pl.BlockSpec((1, D), lambda b, i: (0, 0))], main,
                 _sds((B, S, D), BF16), sem=("parallel", "parallel"))(x3, x3, g)


def _pool_bwd(dm3):
    B, S, D = dm3.shape
    tt = _tile(S, 256, POOL_HALO)
    hb = tt // POOL_HALO
    nt = S // tt
    G = D // len(POOL_WINDOWS)
    L = tt + POOL_HALO

    def body(dm_ref, nxt_ref, dh_ref):
        i = pl.program_id(1)
        posm = (i * tt + lax.broadcasted_iota(jnp.int32, (tt, 1), 0) + 1).astype(F32)
        posn = ((i + 1) * tt + lax.broadcasted_iota(jnp.int32, (POOL_HALO, 1), 0) + 1).astype(F32)
        for gi, win in enumerate(POOL_WINDOWS):
            sl = slice(gi * G, (gi + 1) * G)
            dm = dm_ref[:, sl]
            s = jnp.concatenate([dm / jnp.minimum(posm, float(win)),
                                 jnp.where(i < nt - 1, nxt_ref[:, sl] / jnp.minimum(posn, float(win)), 0.0)], axis=0)
            w = 1
            while w < win:
                s = s + _roll(s, L - w)
                w *= 2
            dh_ref[:, sl] = s[:tt] - dm

    main = pl.BlockSpec((None, tt, D), lambda b, i: (b, i, 0))
    nxt = pl.BlockSpec((None, POOL_HALO, D), lambda b, i: (b, jnp.minimum((i + 1) * hb, S // POOL_HALO - 1), 0))
    return _call(body, "pool_bwd", (B, nt), [main, nxt], main, _sds((B, S, D), F32), sem=("parallel", "parallel"))(dm3, dm3)


def _pool_mixer_fwd(x, B, p):
    T, D = x.shape
    NG = len(POOL_WINDOWS)
    G = D // NG
    tm = _tile(T, 512)
    m = _pool_fwd(x.reshape(B, T // B, D), p["mix_norm"][3:4]).reshape(T, D)

    def epi(res, ex, outs):
        outs[0][...] = ex[1][...] + res * ex[0][...]

    blk = pl.BlockSpec((tm, G), lambda i, g: (i, g))
    x2 = _mm("pool_out", "nn", [(m, p["pool_w4"])], [(blk, pl.BlockSpec((None, G, G), lambda i, g: (g, 0, 0)))],
             (T // tm, NG), None, [_sds((T, D), F32)], [blk], extras=[p["pool_scale"], x],
             extra_specs=[pl.BlockSpec((1, G), lambda i, g: (0, g)), blk], epilogue=epi)[0]
    return x2, (x, m)


def _pool_mixer_bwd(dxp, saved, B, p):
    x, m = saved
    dx, dxb = dxp
    T, D = x.shape
    NG = len(POOL_WINDOWS)
    G = D // NG
    tm = _tile(T, 512)
    g = {}

    def epi(zz, ex, outs):
        dy = ex[0][...]
        outs[0][...] = (dy * ex[1][...]).astype(BF16)
        part = jnp.sum(dy * zz, axis=0, keepdims=True)

        @pl.when(pl.program_id(1) == 0)
        def _():
            outs[1][...] = part

        @pl.when(pl.program_id(1) > 0)
        def _():
            outs[1][...] += part

    blk = pl.BlockSpec((tm, G), lambda g_, i: (i, g_))
    wsp = pl.BlockSpec((None, G, G), lambda g_, i: (g_, 0, 0))
    vec = pl.BlockSpec((1, G), lambda g_, i: (0, g_))
    dz, dsc = _mm("pool_dz", "nn", [(m, p["pool_w4"])], [(blk, wsp)], (NG, T // tm), None,
                  [_sds((T, D), BF16), _sds((1, D), F32)], [blk, vec], extras=[dx, p["pool_scale"]], extra_specs=[blk, vec],
                  epilogue=epi, sem=("parallel", "arbitrary"))
    g["pool_scale"] = dsc
    tk = _tile(T, 512)
    kb = pl.BlockSpec((tk, G), lambda g_, k: (k, g_))
    g["pool_w"] = _mm("pool_dw", "tn", [(m, dz)], [(kb, kb)], (NG, T // tk), 1, [_sds((NG, G, G), F32)],
                      [pl.BlockSpec((None, G, G), lambda g_, k: (g_, 0, 0))], acc_shape=(G, G))[0][None]
    blk2 = pl.BlockSpec((tm, G), lambda i, g_: (i, g_))
    dm = _mm("pool_dm", "nt", [(dz, p["pool_w4"])], [(blk2, pl.BlockSpec((None, G, G), lambda i, g_: (g_, 0, 0)))],
             (T // tm, NG), None, [_sds((T, D), F32)], [blk2])[0]
    dh = _pool_bwd(dm.reshape(B, T // B, D)).reshape(T, D)
    dx2, dg = _rms_bwd("pool_drms", dh, x, p["mix_norm"][3:4], dx)
    return dx2, dg, g


_MIXERS = ((_conv_mixer_fwd, _conv_mixer_bwd), (_fox_mixer_fwd, _fox_mixer_bwd), (_hgrn_mixer_fwd, _hgrn_mixer_bwd),
           (_pool_mixer_fwd, _pool_mixer_bwd))


EARLY_LAYERS = 2
LATE = ("ffn_w_gate", "ffn_w_up", "ffn_w_down", "hgrn_w_in", "hgrn_w_out", "pool_w")
MIXER_MATRICES = (("conv_w_in", "conv_w_out"), ("fox_w_in", "fox_w_out"), ("hgrn_w_in", "hgrn_w_out"), ("pool_w",))


def _local_step(x3, tgt3, w, plan=None):
    B, S, D = x3.shape
    T = B * S
    depth = w["ffn_norm"].shape[0]
    H = FOX_HEADS
    p = dict(w)
    p["conv_w_in2"] = w["conv_w_in"][0].reshape(D, 2, D).transpose(1, 0, 2)
    p["conv_b_in2"] = w["conv_b_in"].reshape(2, 1, D)
    p["conv_dw32"] = jnp.pad(w["conv_dw"][0], ((0, HALO - CONV_TAPS), (0, 0)))
    p["conv_w_out"] = w["conv_w_out"][0]
    p["fox_w_qkv"] = w["fox_w_in"][0][:, :3 * D]
    p["fox_w_f"] = jnp.pad(w["fox_w_in"][0][:, 3 * D:], ((0, 0), (0, LANES - H)))
    p["fox_b_f128"] = jnp.pad(w["fox_b_f"], ((0, 0), (0, LANES - H)))
    p["fox_w_out"] = w["fox_w_out"][0]
    ffn = {}

    def add_layers(first, gate, up, down, rest):
        for k in range(gate.shape[0]):
            ffn[first + k] = (jnp.stack((gate[k], up[k]), axis=1), down[k])
        if rest is not None:
            p["hgrn_w_in"] = rest["hgrn_w_in"][0]
            p["hgrn_w_in4"] = rest["hgrn_w_in"][0].reshape(D, 4, D).transpose(1, 0, 2)
            p["hgrn_w_out"] = rest["hgrn_w_out"][0]
            p["pool_w4"] = rest["pool_w"][0]

    add_layers(0, w["ffn_w_gate"], w["ffn_w_up"], w["ffn_w_down"], w if plan is None else None)
    if plan is not None:
        p["late_gather"] = plan["late_shards"]

    x = x3.reshape(T, D)
    saved = []
    for i in range(depth):
        x, s0 = _ffn_fwd(f"{i}a", x, w["ffn_norm"][i, 0:1], ffn[i][0][0], ffn[i][1][0])
        x, s1 = _MIXERS[i % 4][0](x, B, p)
        if "late_gathered" in p:
            late = plan["assemble"](p.pop("late_gathered"))
            add_layers(EARLY_LAYERS, late["ffn_w_gate"], late["ffn_w_up"], late["ffn_w_down"], late)
        x, s2 = _ffn_fwd(f"{i}b", x, w["ffn_norm"][i, 1:2], ffn[i][0][1], ffn[i][1][1])
        saved.append((s0, s1, s2))
    loss, dx, dfinal = _loss_head(x, w["final_norm"].reshape(1, D), tgt3.reshape(T, D))

    g = {"final_norm": dfinal}
    dffn_norm = [[None, None] for _ in range(depth)]
    dwgu = [[None, None] for _ in range(depth)]
    dwd = [[None, None] for _ in range(depth)]
    dmix = [None] * depth
    travelled = []

    def layer_grads(i, extra):
        gate, up = (jnp.stack([dwgu[i][0][s], dwgu[i][1][s]])[None] for s in (0, 1))
        return [gate, up, jnp.stack(dwd[i])[None]] + [g.pop(n) for n in extra]

    for i in reversed(range(depth)):
        s0, s1, s2 = saved[i]
        dx, dffn_norm[i][1], dwgu[i][1], dwd[i][1] = _ffn_bwd(f"{i}b", dx, s2, w["ffn_norm"][i, 1:2], ffn[i][0][1], ffn[i][1][1])
        if plan is not None and i < depth - 1:
            extra = MIXER_MATRICES[i + 1]
            names = LATE[:3] + extra
            p["scatter"] = plan["dev_major"](names, layer_grads(i + 1, extra))
            travelled.append([names, p["scatter"]])
        dx, dmix[i], gm = _MIXERS[i % 4][1](dx, s1, B, p)
        if "scattered" in p:
            travelled[-1].append(p.pop("scattered"))
        g.update(gm)
        dx, dffn_norm[i][0], dwgu[i][0], dwd[i][0] = _ffn_bwd(f"{i}a", dx, s0, w["ffn_norm"][i, 0:1], ffn[i][0][0], ffn[i][1][0])
    kept = range(depth) if plan is None else range(1)
    g["ffn_norm"] = jnp.stack([jnp.stack([a[0], b[0]]) for a, b in dffn_norm])
    g["ffn_w_gate"] = jnp.stack([jnp.stack([dwgu[i][0][0], dwgu[i][1][0]]) for i in kept])
    g["ffn_w_up"] = jnp.stack([jnp.stack([dwgu[i][0][1], dwgu[i][1][1]]) for i in kept])
    g["ffn_w_down"] = jnp.stack([jnp.stack(dwd[i]) for i in kept])
    g["mix_norm"] = jnp.concatenate(dmix, axis=0)
    g = {n: (t.reshape(w[n].shape) if n in w and n not in LATE else t) for n, t in g.items()}
    return loss, dx[0].reshape(B, S, D), g, travelled


def _adamw(name, w, m, v, parts):
    shape = w.shape
    cols = shape[-1]
    rows = w.size // cols
    tr = _tile(rows, max(8, (1 << 19) // cols))
    n = len(parts)
    c1 = 1.0 - ADAM_B1 ** ADAM_STEP
    c2 = 1.0 - ADAM_B2 ** ADAM_STEP

    def body(*refs):
        w_ref, m_ref, v_ref = refs[:3]
        g_ref, d_ref, m2_ref, v2_ref = refs[3 + n:]
        g = refs[3][...].astype(F32)
        for k in range(1, n):
            g = g + refs[3 + k][...].astype(F32)
        m2 = ADAM_B1 * m_ref[...] + (1.0 - ADAM_B1) * g
        v2 = ADAM_B2 * v_ref[...] + (1.0 - ADAM_B2) * (g * g)
        g_ref[...] = g
        m2_ref[...] = m2
        v2_ref[...] = v2
        d_ref[...] = -ADAM_LR * ((m2 / c1) / (jnp.sqrt(v2 / c2) + ADAM_EPS) + ADAM_WD * w_ref[...])

    blk = pl.BlockSpec((tr, cols), lambda i: (i, 0))
    outs = _call(body, name, (rows // tr,), [blk] * (3 + n), [blk] * 4, [_sds((rows, cols), F32)] * 4, sem=("parallel",))(
        *[t.reshape(rows, cols) for t in (w, m, v, *parts)])
    return [o.reshape(shape) for o in outs]


ANY = pl.BlockSpec(memory_space=pl.ANY)
FLAT_COLS = 1024


def _place():
    return lax.axis_index("x"), lax.axis_index("y"), lax.axis_index("c")


def _all_gather(name, xs):
    K = len(xs)

    def body(*refs):
        x_refs, out_refs = refs[:K], refs[K:2 * K]
        send_sems, recv_sems, local_sems = refs[2 * K:]
        xi, yi, ci = _place()
        me, sibling = (xi, yi, ci), (xi, yi, 1 - ci)
        chips = [(1 - xi, yi), (xi, 1 - yi), (1 - xi, 1 - yi)]

        def slot(a, px, py, pc):
            return out_refs[a].at[4 * px + 2 * py + pc]

        def copy(a, k, block, to, own=False):
            return pltpu.make_async_remote_copy(src_ref=x_refs[a] if own else slot(a, *block), dst_ref=slot(a, *block),
                                                send_sem=send_sems.at[7 * a + k], recv_sem=recv_sems.at[7 * a + k],
                                                device_id=to, device_id_type=MESH)

        mine = [pltpu.make_async_copy(x_refs[a], slot(a, *me), local_sems.at[a]) for a in range(K)]
        first = [copy(a, 1 + j, me, (*chip, ci), own=True) for j, chip in enumerate(chips) for a in range(K)]
        first += [copy(a, 0, me, sibling, own=True) for a in range(K)]
        for cp in mine + first:
            cp.start()
        passed = []
        for j, chip in enumerate(chips):
            for a in range(K):
                copy(a, 1 + j, (*chip, ci), me).wait_recv()
                passed.append(copy(a, 4 + j, (*chip, ci), sibling))
                passed[-1].start()
        for a in range(K):
            copy(a, 0, sibling, me).wait_recv()
            for j, chip in enumerate(chips):
                copy(a, 4 + j, (*chip, 1 - ci), me).wait_recv()
        for cp in first + passed:
            cp.wait_send()
        for cp in mine:
            cp.wait()

    return pl.pallas_call(body, name=name, out_shape=[_sds((N_DEV,) + x.shape, x.dtype) for x in xs], in_specs=[ANY] * K,
                          out_specs=[ANY] * K,
                          scratch_shapes=[pltpu.SemaphoreType.DMA((7 * K,)), pltpu.SemaphoreType.DMA((7 * K,)),
                                          pltpu.SemaphoreType.DMA((K,))])(*xs)


def _swap_sibling(name, ts):
    K = len(ts)

    def body(*refs):
        t_refs, out_refs, send_sems, recv_sems = refs[:K], refs[K:2 * K], refs[2 * K], refs[2 * K + 1]
        xi, yi, ci = _place()
        cps = [pltpu.make_async_remote_copy(src_ref=t_refs[a], dst_ref=out_refs[a], send_sem=send_sems.at[a],
                                            recv_sem=recv_sems.at[a], device_id=(xi, yi, 1 - ci), device_id_type=MESH)
               for a in range(K)]
        for cp in cps:
            cp.start()
        for cp in cps:
            cp.wait()

    return pl.pallas_call(body, name=name, out_shape=[_sds(t.shape, t.dtype) for t in ts], in_specs=[ANY] * K,
                          out_specs=[ANY] * K,
                          scratch_shapes=[pltpu.SemaphoreType.DMA((K,)), pltpu.SemaphoreType.DMA((K,))])(*ts)


def _scatter_chips(name, ts):
    K = len(ts)

    def body(*refs):
        t_refs, out_refs, send_sems, recv_sems = refs[:K], refs[K:2 * K], refs[2 * K], refs[2 * K + 1]
        xi, yi, ci = _place()
        chips = [(1 - xi, yi), (xi, 1 - yi), (1 - xi, 1 - yi)]
        cps = [pltpu.make_async_remote_copy(src_ref=t_refs[a].at[2 * cx + cy], dst_ref=out_refs[a].at[j],
                                            send_sem=send_sems.at[3 * a + j], recv_sem=recv_sems.at[3 * a + j],
                                            device_id=(cx, cy, ci), device_id_type=MESH)
               for j, (cx, cy) in enumerate(chips) for a in range(K)]
        for cp in cps:
            cp.start()
        for cp in cps:
            cp.wait()

    return pl.pallas_call(body, name=name, out_shape=[_sds((3,) + t.shape[1:], t.dtype) for t in ts], in_specs=[ANY] * K,
                          out_specs=[ANY] * K,
                          scratch_shapes=[pltpu.SemaphoreType.DMA((3 * K,)), pltpu.SemaphoreType.DMA((3 * K,))])(*ts)


def _add_bf16(name, a, b):
    shape = a.shape
    N, C = shape[0], shape[-1]
    R = a.size // (N * C)
    tr = _tile(R, max(8, (1 << 19) // C))

    def body(a_ref, b_ref, o_ref):
        o_ref[...] = (a_ref[...].astype(F32) + b_ref[...].astype(F32)).astype(BF16)

    blk = pl.BlockSpec((None, tr, C), lambda n, i: (n, i, 0))
    return _call(body, name, (N, R // tr), [blk, blk], blk, _sds((N, R, C), BF16), sem=("parallel", "parallel"))(
        a.reshape(N, R, C), b.reshape(N, R, C)).reshape(shape)


def _sum_parts(name, parts):
    N, C = parts.shape

    def body(p_ref, o_ref):
        s = p_ref[pl.ds(0, 1), :]
        for d in range(1, N):
            s = s + p_ref[pl.ds(d, 1), :]
        o_ref[...] = s

    return _call(body, name, (1,), [pl.BlockSpec((N, C), lambda i: (0, 0))], pl.BlockSpec((1, C), lambda i: (0, 0)),
                 _sds((1, C), F32))(parts)


def _flat(parts, dtype, lead=()):
    flat = jnp.concatenate([t.reshape(lead + (-1,)).astype(dtype) for t in parts], axis=-1)
    n = flat.shape[-1]
    unit = 16 * FLAT_COLS
    padded = -(-n // unit) * unit
    flat = jnp.pad(flat, [(0, 0)] * len(lead) + [(0, padded - n)])
    return flat.reshape(lead + (padded // FLAT_COLS, FLAT_COLS))


def _unflat(flat, shapes, lead=()):
    flat = flat.reshape(lead + (-1,))
    out, off = [], 0
    for shp in shapes:
        n = math.prod(shp)
        out.append(flat[..., off:off + n].reshape(lead + tuple(shp)))
        off += n
    return out


def _dev_major(full, ax):
    shp = full.shape
    return jnp.moveaxis(full.reshape(shp[:ax] + (N_DEV, shp[ax] // N_DEV) + shp[ax + 1:]), ax, 0)


def _from_dev_major(blocks, ax):
    t = jnp.moveaxis(blocks, 0, ax)
    shp = t.shape
    return t.reshape(shp[:ax] + (shp[ax] * shp[ax + 1],) + shp[ax + 2:])


def kernel(x, *rest):
    nw = len(WEIGHTS)
    w = dict(zip(WEIGHTS, rest[:nw]))
    tgt = rest[nw]
    m = dict(zip(WEIGHTS, rest[nw + 1:2 * nw + 1]))
    v = dict(zip(WEIGHTS, rest[2 * nw + 1:3 * nw + 1]))
    xi, yi, ci = _place()
    dev = 4 * xi + 2 * yi + ci

    ffn3 = LATE[:3]
    early = [n for n in BIG if n not in LATE]
    shard = {n: (w[n][:EARLY_LAYERS] if n in ffn3 else w[n]).astype(BF16) for n in ffn3 + tuple(early)}
    late_shard = {n: (w[n][EARLY_LAYERS:] if n in ffn3 else w[n]).astype(BF16) for n in LATE}
    big = _all_gather("gather_matrices", [shard[n] for n in ffn3 + tuple(early)])
    small = _all_gather("gather_vectors", [_flat([w[n] for n in SMALL], F32)])[0]
    full = {n: w[n] for n in REPL}
    for n, blocks in zip(ffn3 + tuple(early), big):
        full[n] = _from_dev_major(blocks, SHARDED[n])
    for n, blocks in zip(SMALL, _unflat(small, [w[n].shape for n in SMALL], (N_DEV,))):
        full[n] = _from_dev_major(blocks, SHARDED[n])

    def by_owner(names, grads):
        return [_dev_major(t.reshape(t.shape[:1] + w[n].shape[1:SHARDED[n]] + (-1,) + w[n].shape[SHARDED[n] + 1:]),
                           SHARDED[n]).astype(BF16) for n, t in zip(names, grads)]

    plan = dict(late_shards=[late_shard[n] for n in LATE], dev_major=by_owner,
                assemble=lambda got: {n: _from_dev_major(t, SHARDED[n]) for n, t in zip(LATE, got)})
    loss, gx, g, travelled = _local_step(x, tgt, full, plan)

    vec_names = SMALL + REPL
    vec = _all_gather("gather_vector_grads", [_flat([g[n] for n in vec_names] + [loss], F32)])[0]
    *vec_list, losses = _unflat(vec, [g[n].shape for n in vec_names] + [(1, 1)], (N_DEV,))
    vec_parts = dict(zip(vec_names, vec_list))
    loss = _sum_parts("loss_sum", losses.reshape(N_DEV, 1))[0, 0]

    last = ffn3 + MIXER_MATRICES[0]
    keep, send = [], []
    for gd in by_owner(last, [g[n] for n in last]):
        gd = gd.reshape((4, 2) + gd.shape[1:])
        keep.append(lax.dynamic_index_in_dim(gd, ci, 1, keepdims=False))
        send.append(lax.dynamic_index_in_dim(gd, 1 - ci, 1, keepdims=False))
    got_sib = _swap_sibling("grads_to_sibling", send)
    pair = [_add_bf16(f"grad_pair_sum_{n}", a, b) for n, a, b in zip(last, keep, got_sib)]
    got_chips = _scatter_chips("grads_to_chips", pair)
    chip = 2 * xi + yi
    parts = {}
    for k, n in enumerate(last):
        parts[n] = [[lax.dynamic_index_in_dim(keep[k], chip, 0, keepdims=False),
                     lax.dynamic_index_in_dim(got_sib[k], chip, 0, keepdims=False), got_chips[k][0], got_chips[k][1], got_chips[k][2]]]
    for names, sent, got in reversed(travelled):
        for n, s, t in zip(names, sent, got):
            parts.setdefault(n, []).append([lax.dynamic_index_in_dim(s, dev, 0, keepdims=False)] + [t[k] for k in range(len(FLIPS))])

    res = {}
    for n in BIG:
        runs = parts[n]
        count = max(len(r) for r in runs)
        whole = [jnp.concatenate([r[k] if k < len(r) else jnp.zeros_like(r[0]) for r in runs], axis=0) if len(runs) > 1 else runs[0][k]
                 for k in range(count)]
        res[n] = _adamw(f"adamw_{n}", w[n], m[n], v[n], whole)
    for n in vec_names:
        parts = vec_parts[n]
        if n in SHARDED:
            ax = SHARDED[n]
            parts = lax.dynamic_slice_in_dim(parts, dev * w[n].shape[ax], w[n].shape[ax], ax + 1)
        res[n] = _adamw(f"adamw_{n}", w[n], m[n], v[n], [parts[d] for d in range(N_DEV)])
    return (loss, gx, *[res[n][0] for n in WEIGHTS], *[res[n][1] for n in WEIGHTS], *[res[n][2] for n in WEIGHTS],
            *[res[n][3] for n in WEIGHTS])
```

```python
import functools
import math

import jax
import jax.numpy as jnp
from jax import lax
from jax.experimental import pallas as pl
from jax.experimental.pallas import tpu as pltpu

F32 = jnp.float32
BF16 = jnp.bfloat16
MESH = pl.DeviceIdType.MESH

N_DEV = 8
RMS_EPS = 1e-6
LN_EPS = 1e-5
FOX_HEADS = 16
HGRN_EXPAND = 128
HGRN_CHUNK = 32
POOL_WINDOWS = (2, 4, 8, 16)
ADAM_LR, ADAM_B1, ADAM_B2, ADAM_EPS, ADAM_WD, ADAM_STEP = 0.001, 0.9, 0.999, 1e-08, 0.01, 10
LANES = 128
VMEM_LIMIT_MB = 48

SHARDED = dict(
    ffn_norm=2, ffn_w_gate=3, ffn_w_up=3, ffn_w_down=2, conv_w_in=2, conv_dw=2, conv_w_out=1, fox_w_in=2, fox_w_out=1,
    hgrn_w_in=2, hgrn_norm=1, hgrn_w_out=1, pool_w=2, pool_scale=1)
BIG = ("ffn_w_gate", "ffn_w_up", "ffn_w_down", "conv_w_in", "conv_w_out", "fox_w_in", "fox_w_out", "hgrn_w_in",
       "hgrn_w_out", "pool_w")
SMALL = ("ffn_norm", "conv_dw", "hgrn_norm", "pool_scale")
REPL = ("mix_norm", "final_norm", "conv_b_in", "conv_dw_b", "conv_ln_g", "conv_ln_b", "fox_b_f", "hgrn_lb_logits")
WEIGHTS = ("ffn_norm", "ffn_w_gate", "ffn_w_up", "ffn_w_down", "mix_norm", "final_norm", "conv_w_in", "conv_b_in",
           "conv_dw", "conv_dw_b", "conv_ln_g", "conv_ln_b", "conv_w_out", "fox_w_in", "fox_b_f", "fox_w_out",
           "hgrn_w_in", "hgrn_lb_logits", "hgrn_norm", "hgrn_w_out", "pool_w", "pool_scale")


def _tile(n, pref, mult=8):
    if n <= pref:
        return n
    for t in range(pref - pref % mult, 0, -mult):
        if n % t == 0:
            return t
    return n


def _sig(x):
    return 1.0 / (1.0 + jnp.exp(-x))


def _call(body, name, grid, in_specs, out_specs, out_shape, scratch=(), sem=None):
    params = dict(vmem_limit_bytes=VMEM_LIMIT_MB << 20)
    if sem is not None:
        params["dimension_semantics"] = sem
    return pl.pallas_call(body, name=name, grid=grid, in_specs=in_specs, out_specs=out_specs, out_shape=out_shape,
                          scratch_shapes=list(scratch), compiler_params=pltpu.CompilerParams(**params))


FLIPS = [(fx, fy, fc) for fx in (0, 1) for fy in (0, 1) for fc in (0, 1)][1:]


def _carry_call(body, name, grid, in_specs, out_specs, out_shape, scratch, kind, arrays):
    K = len(arrays)
    n_in, n_out, n_scr = len(in_specs), len(out_shape), len(scratch)
    n_peer = len(FLIPS)
    if kind == "gather":
        landed = [_sds((N_DEV,) + a.shape, a.dtype) for a in arrays]
    else:
        landed = [_sds((n_peer,) + a.shape[1:], a.dtype) for a in arrays]

    def wrapped(*refs):
        ins, sent = refs[:n_in], refs[n_in:n_in + K]
        outs, got = refs[n_in + K:n_in + K + n_out], refs[n_in + K + n_out:n_in + 2 * K + n_out]
        scr = refs[n_in + 2 * K + n_out:n_in + 2 * K + n_out + n_scr]
        send_sems, recv_sems, local_sems = refs[-3:]
        ids = [pl.program_id(d) for d in range(len(grid))]
        first = functools.reduce(jnp.logical_and, [i == 0 for i in ids])
        last = functools.reduce(jnp.logical_and, [i == n - 1 for i, n in zip(ids, grid)])
        xi, yi, ci = _place()
        me = 4 * xi + 2 * yi + ci

        def copies():
            cps = []
            for a in range(K):
                for k, (fx, fy, fc) in enumerate(FLIPS):
                    px, py, pc = (xi + fx) % 2, (yi + fy) % 2, (ci + fc) % 2
                    if kind == "gather":
                        src, dst = sent[a], got[a].at[me]
                    else:
                        src, dst = sent[a].at[4 * px + 2 * py + pc], got[a].at[k]
                    cps.append(pltpu.make_async_remote_copy(src_ref=src, dst_ref=dst, send_sem=send_sems.at[n_peer * a + k],
                                                            recv_sem=recv_sems.at[n_peer * a + k], device_id=(px, py, pc),
                                                            device_id_type=MESH))
            return cps

        def own():
            return [pltpu.make_async_copy(sent[a], got[a].at[me], local_sems.at[a]) for a in range(K)] if kind == "gather" else []

        @pl.when(first)
        def _():
            for cp in copies() + own():
                cp.start()

        body(*ins, *outs, *scr)

        @pl.when(last)
        def _():
            for cp in copies() + own():
                cp.wait()

    params = dict(vmem_limit_bytes=VMEM_LIMIT_MB << 20, dimension_semantics=("arbitrary",) * len(grid))
    res = pl.pallas_call(wrapped, name=name, grid=grid, in_specs=list(in_specs) + [ANY] * K,
                         out_specs=list(out_specs) + [ANY] * K, out_shape=list(out_shape) + landed,
                         scratch_shapes=list(scratch) + [pltpu.SemaphoreType.DMA((n_peer * K,)), pltpu.SemaphoreType.DMA((n_peer * K,)),
                                                         pltpu.SemaphoreType.DMA((K,))],
                         compiler_params=pltpu.CompilerParams(**params))
    return res


def _sds(shape, dtype):
    return jax.ShapeDtypeStruct(tuple(shape), dtype)


_DN = {"nn": (((1,), (0,)), ((), ())), "nt": (((1,), (1,)), ((), ())), "tn": (((0,), (0,)), ((), ()))}


def _dot(a, b, mode="nn"):
    return lax.dot_general(a.astype(BF16), b.astype(BF16), _DN[mode], preferred_element_type=F32)


def _roll(x, shift, axis=0):
    n = x.shape[axis]
    shift = shift % n
    return x if shift == 0 else pltpu.roll(x, shift, axis)


def _scan_rows(x, reverse=False):
    n = x.shape[0]
    row = lax.broadcasted_iota(jnp.int32, x.shape, 0)
    sh = 1
    while sh < n:
        if reverse:
            x = x + jnp.where(row < n - sh, _roll(x, n - sh), 0.0)
        else:
            x = x + jnp.where(row >= sh, _roll(x, sh), 0.0)
        sh *= 2
    return x


def _mm(name, mode, pairs, pair_specs, grid, k_axis, out_shapes, out_specs, acc_shape=None, extras=(), extra_specs=(),
        epilogue=None, alpha=1.0, sem=None):
    npair, nex, nout = len(pairs), len(extras), len(out_shapes)
    nk = grid[k_axis] if k_axis is not None else 1

    def body(*refs):
        prs = refs[:2 * npair]
        ex = refs[2 * npair:2 * npair + nex]
        outs = refs[2 * npair + nex:2 * npair + nex + nout]

        def partial():
            p = None
            for i in range(npair):
                d = _dot(prs[2 * i][...], prs[2 * i + 1][...], mode)
                p = d if p is None else p + d
            return p

        def finish(res):
            if alpha != 1.0:
                res = res * alpha
            if epilogue is None:
                outs[0][...] = res.astype(outs[0].dtype)
            else:
                epilogue(res, ex, outs)

        if k_axis is None:
            finish(partial())
        else:
            acc = refs[-1]
            k = pl.program_id(k_axis)

            @pl.when(k == 0)
            def _():
                acc[...] = partial()

            @pl.when(k > 0)
            def _():
                acc[...] += partial()

            @pl.when(k == nk - 1)
            def _():
                finish(acc[...])

    if sem is None:
        sem = tuple("arbitrary" if i == k_axis else "parallel" for i in range(len(grid)))
    scratch = [pltpu.VMEM(acc_shape, F32)] if k_axis is not None else []
    flat = [t for p in pairs for t in p]
    flat_specs = [s for p in pair_specs for s in p]
    return _call(body, name, grid, flat_specs + list(extra_specs), out_specs, out_shapes, scratch, sem)(*flat, *extras)


def _mm_nn(name, a, b, out_dtype, tm=1024, tn=1024):
    M, K = a.shape
    N = b.shape[1]
    tm, tn = _tile(M, tm), _tile(N, tn, LANES)
    return _mm(name, "nn", [(a, b)], [(pl.BlockSpec((tm, K), lambda i, j: (i, 0)), pl.BlockSpec((K, tn), lambda i, j: (0, j)))],
               (M // tm, N // tn), None, [_sds((M, N), out_dtype)], [pl.BlockSpec((tm, tn), lambda i, j: (i, j))])[0]


def _mm_res(name, u, w, x, alpha, tm=512):
    T, K = u.shape
    D = w.shape[1]
    tm = _tile(T, tm)

    def epi(res, ex, outs):
        outs[0][...] = ex[0][...] + res

    return _mm(name, "nn", [(u, w)], [(pl.BlockSpec((tm, K), lambda i: (i, 0)), pl.BlockSpec((K, D), lambda i: (0, 0)))],
               (T // tm,), None, [_sds((T, D), F32)], [pl.BlockSpec((tm, D), lambda i: (i, 0))],
               extras=[x], extra_specs=[pl.BlockSpec((tm, D), lambda i: (i, 0))], epilogue=epi, alpha=alpha)[0]


def _mm_nt(name, pairs, out_dtype, tm=1024, tn=1024, alpha=1.0):
    M = pairs[0][0].shape[0]
    N = pairs[0][1].shape[0]
    tm, tn = _tile(M, tm), _tile(N, tn, LANES)
    specs = [(pl.BlockSpec((tm, a.shape[1]), lambda i, j: (i, 0)), pl.BlockSpec((tn, b.shape[1]), lambda i, j: (j, 0)))
             for a, b in pairs]
    return _mm(name, "nt", pairs, specs, (M // tm, N // tn), None, [_sds((M, N), out_dtype)],
               [pl.BlockSpec((tm, tn), lambda i, j: (i, j))], alpha=alpha)[0]


def _mm_nt3(name, a3, b3, tm=1024, tn=1024):
    S, M, K = a3.shape
    N = b3.shape[1]
    tm, tn = _tile(M, tm), _tile(N, tn, LANES)
    return _mm(name, "nt", [(a3, b3)],
               [(pl.BlockSpec((None, tm, K), lambda i, j, s: (s, i, 0)), pl.BlockSpec((None, tn, K), lambda i, j, s: (s, j, 0)))],
               (M // tm, N // tn, S), 2, [_sds((M, N), F32)], [pl.BlockSpec((tm, tn), lambda i, j, s: (i, j))],
               acc_shape=(tm, tn))[0]


def _mm_tn(name, a, b, alpha=1.0, tm=1408, tn=1408, tk=512):
    T, M = a.shape
    N = b.shape[1]
    tm, tn, tk = _tile(M, tm, LANES), _tile(N, tn, LANES), _tile(T, tk)
    return _mm(name, "tn", [(a, b)],
               [(pl.BlockSpec((tk, tm), lambda i, j, k: (k, i)), pl.BlockSpec((tk, tn), lambda i, j, k: (k, j)))],
               (M // tm, N // tn, T // tk), 2, [_sds((M, N), F32)], [pl.BlockSpec((tm, tn), lambda i, j, k: (i, j))],
               acc_shape=(tm, tn), alpha=alpha)[0]


def _mm_tn3(name, a, b3, tm=1408, tn=1408, tk=512):
    T, M = a.shape
    S, _, N = b3.shape
    tm, tn, tk = _tile(M, tm, LANES), _tile(N, tn, LANES), _tile(T, tk)
    return _mm(name, "tn", [(a, b3)],
               [(pl.BlockSpec((tk, tm), lambda s, i, j, k: (k, i)), pl.BlockSpec((None, tk, tn), lambda s, i, j, k: (s, k, j)))],
               (S, M // tm, N // tn, T // tk), 3, [_sds((S, M, N), F32)],
               [pl.BlockSpec((None, tm, tn), lambda s, i, j, k: (s, i, j))], acc_shape=(tm, tn))[0]


def _rms_fwd(name, x, g):
    T, D = x.shape
    tt = _tile(T, 512)

    def body(x_ref, g_ref, h_ref):
        xv = x_ref[...]
        r = lax.rsqrt(jnp.mean(xv * xv, axis=-1, keepdims=True) + RMS_EPS)
        h_ref[...] = (xv * r * g_ref[...]).astype(h_ref.dtype)

    row = pl.BlockSpec((tt, D), lambda i: (i, 0))
    return _call(body, name, (T // tt,), [row, pl.BlockSpec((1, D), lambda i: (0, 0))], row, _sds((T, D), BF16))(x, g)


def _rms_bwd(name, dh, x, g, dx_in):
    T, D = x.shape
    tt = _tile(T, 512)

    def body(dh_ref, x_ref, g_ref, dxi_ref, dx_ref, dxb_ref, dg_ref):
        xv = x_ref[...]
        r = lax.rsqrt(jnp.mean(xv * xv, axis=-1, keepdims=True) + RMS_EPS)
        xh = xv * r
        dhv = dh_ref[...]
        dxh = dhv * g_ref[...]
        dx = dxi_ref[...] + r * (dxh - xh * jnp.mean(dxh * xh, axis=-1, keepdims=True))
        dx_ref[...] = dx
        dxb_ref[...] = dx.astype(BF16)
        part = jnp.sum(dhv * xh, axis=0, keepdims=True)

        @pl.when(pl.program_id(0) == 0)
        def _():
            dg_ref[...] = part

        @pl.when(pl.program_id(0) > 0)
        def _():
            dg_ref[...] += part

    row = pl.BlockSpec((tt, D), lambda i: (i, 0))
    vec = pl.BlockSpec((1, D), lambda i: (0, 0))
    dx, dxb, dg = _call(body, name, (T // tt,), [row, row, vec, row], [row, row, vec],
                        [_sds((T, D), F32), _sds((T, D), BF16), _sds((1, D), F32)], sem=("arbitrary",))(dh, x, g, dx_in)
    return (dx, dxb), dg


def _loss_head(x, g, tgt):
    T, D = x.shape
    tt = _tile(T, 512)

    def body(x_ref, g_ref, t_ref, loss_ref, dx_ref, dxb_ref, dg_ref):
        xv = x_ref[...]
        r = lax.rsqrt(jnp.mean(xv * xv, axis=-1, keepdims=True) + RMS_EPS)
        xh = xv * r
        e = xh * g_ref[...] - t_ref[...]
        lp = 0.5 * jnp.sum(jnp.sum(e * e, axis=-1, keepdims=True), axis=0, keepdims=True) / D
        dy = e / D
        dxh = dy * g_ref[...]
        dx = r * (dxh - xh * jnp.mean(dxh * xh, axis=-1, keepdims=True))
        dx_ref[...] = dx
        dxb_ref[...] = dx.astype(BF16)
        part = jnp.sum(dy * xh, axis=0, keepdims=True)

        @pl.when(pl.program_id(0) == 0)
        def _():
            dg_ref[...] = part
            loss_ref[...] = lp

        @pl.when(pl.program_id(0) > 0)
        def _():
            dg_ref[...] += part
            loss_ref[...] += lp

    row = pl.BlockSpec((tt, D), lambda i: (i, 0))
    vec = pl.BlockSpec((1, D), lambda i: (0, 0))
    one = pl.BlockSpec((1, 1), lambda i: (0, 0))
    loss, dx, dxb, dg = _call(body, "loss_head", (T // tt,), [row, vec, row], [one, row, row, vec],
                              [_sds((1, 1), F32), _sds((T, D), F32), _sds((T, D), BF16), _sds((1, D), F32)],
                              sem=("arbitrary",))(x, g, tgt)
    return loss, (dx, dxb), dg


def _colsum3(name, a3):
    S, T, N = a3.shape
    tt = _tile(T, 512)

    def body(a_ref, o_ref):
        part = jnp.sum(a_ref[...].astype(F32), axis=0, keepdims=True)

        @pl.when(pl.program_id(1) == 0)
        def _():
            o_ref[...] = part

        @pl.when(pl.program_id(1) > 0)
        def _():
            o_ref[...] += part

    return _call(body, name, (S, T // tt), [pl.BlockSpec((None, tt, N), lambda s, i: (s, i, 0))],
                 pl.BlockSpec((None, 1, N), lambda s, i: (s, 0, 0)), _sds((S, 1, N), F32), sem=("parallel", "arbitrary"))(a3)


def _glu_mm(name, h, w2, bias2, mode, u_dtype, tm=1024, tn=256):
    T, K = h.shape
    N = w2.shape[2]
    tm, tn = _tile(T, tm), _tile(N, tn, LANES)
    has_bias = bias2 is not None

    halves = 2 if tm % 32 == 0 else 1

    def body(*refs):
        h_ref, w_ref = refs[0], refs[1]
        ab_ref, u_ref = refs[-2], refs[-1]
        for r in range(halves):
            rows = pl.ds(r * (tm // halves), tm // halves)
            hv = h_ref[rows, :]
            a = _dot(hv, w_ref[0])
            b = _dot(hv, w_ref[1])
            if has_bias:
                a = a + refs[2][0]
                b = b + refs[2][1]
            u = a * _sig(a) * b if mode == "swiglu" else a * _sig(b)
            ab_ref[0, rows, :] = a.astype(BF16)
            ab_ref[1, rows, :] = b.astype(BF16)
            u_ref[rows, :] = u.astype(u_ref.dtype)

    in_specs = [pl.BlockSpec((tm, K), lambda i, j: (i, 0)), pl.BlockSpec((2, K, tn), lambda i, j: (0, 0, j))]
    args = [h, w2]
    if has_bias:
        in_specs.append(pl.BlockSpec((2, 1, tn), lambda i, j: (0, 0, j)))
        args.append(bias2)
    return _call(body, name, (T // tm, N // tn), in_specs,
                 [pl.BlockSpec((2, tm, tn), lambda i, j: (0, i, j)), pl.BlockSpec((tm, tn), lambda i, j: (i, j))],
                 [_sds((2, T, N), BF16), _sds((T, N), u_dtype)], sem=("parallel", "parallel"))(*args)


def _swiglu_bwd_mm(name, dxb, wd, ab, alpha, tm=1024, tn=256):
    T, D = dxb.shape
    N = wd.shape[0]
    tm, tn = _tile(T, tm), _tile(N, tn, LANES)
    halves = 4 if tm % 64 == 0 else 1

    def body(dx_ref, wd_ref, ab_ref, dab_ref, u_ref):
        wdv = wd_ref[...]
        for r in range(halves):
            rows = pl.ds(r * (tm // halves), tm // halves)
            du = _dot(dx_ref[rows, :], wdv, "nt") * alpha
            a = ab_ref[0, rows, :].astype(F32)
            b = ab_ref[1, rows, :].astype(F32)
            sg = _sig(a)
            sa = a * sg
            dab_ref[0, rows, :] = (du * b * (sg * (1.0 + a * (1.0 - sg)))).astype(BF16)
            dab_ref[1, rows, :] = (du * sa).astype(BF16)
            u_ref[rows, :] = (sa * b).astype(BF16)

    ab_spec = pl.BlockSpec((2, tm, tn), lambda i, j: (0, i, j))
    return _call(body, name, (T // tm, N // tn),
                 [pl.BlockSpec((tm, D), lambda i, j: (i, 0)), pl.BlockSpec((tn, D), lambda i, j: (j, 0)), ab_spec],
                 [ab_spec, pl.BlockSpec((tm, tn), lambda i, j: (i, j))], [_sds((2, T, N), BF16), _sds((T, N), BF16)],
                 sem=("parallel", "parallel"))(dxb, wd, ab)


def _ffn_fwd(tag, x, g, wgu, wd):
    h = _rms_fwd(f"ffn_rms_{tag}", x, g)
    ab, u = _glu_mm(f"ffn_gu_{tag}", h, wgu, None, "swiglu", BF16)
    return _mm_res(f"ffn_down_{tag}", u, wd, x, 0.5), (x, h, ab)


def _ffn_bwd(tag, dxp, saved, g, wgu, wd):
    x, h, ab = saved
    dx, dxb = dxp
    dab, u = _swiglu_bwd_mm(f"ffn_dgu_{tag}", dxb, wd, ab, 0.5)
    dwd = _mm_tn(f"ffn_dwd_{tag}", u, dxb, alpha=0.5)
    dwgu = _mm_tn3(f"ffn_dwgu_{tag}", h, dab)
    dh = _mm_nt3(f"ffn_dh_{tag}", dab, wgu)
    dx2, dg = _rms_bwd(f"ffn_drms_{tag}", dh, x, g, dx)
    return dx2, dg, dwgu, dwd


HALO = 32


def _conv_fwd(u3, dw32, dwb, lng, lnb):
    B, S, D = u3.shape
    W = dw32.shape[0]
    taps = CONV_TAPS
    tt = _tile(S, 256, HALO)
    hb = tt // HALO

    def body(u_ref, halo_ref, dw_ref, dwb_ref, g_ref, b_ref, v_ref, s_ref):
        i = pl.program_id(1)
        halo = jnp.where(i > 0, halo_ref[...], 0.0)
        ext = jnp.concatenate([halo, u_ref[...]], axis=0)
        acc = jnp.zeros((tt, D), F32) + dwb_ref[...]
        for j in range(taps):
            acc = acc + dw_ref[pl.ds(j, 1), :] * _roll(ext, taps - 1 - j)[HALO:]
        v_ref[...] = acc
        mu = jnp.mean(acc, axis=-1, keepdims=True)
        xc = acc - mu
        ln = xc * lax.rsqrt(jnp.mean(xc * xc, axis=-1, keepdims=True) + LN_EPS) * g_ref[...] + b_ref[...]
        s_ref[...] = (ln * _sig(ln)).astype(BF16)

    main = pl.BlockSpec((None, tt, D), lambda b, i: (b, i, 0))
    halo = pl.BlockSpec((None, HALO, D), lambda b, i: (b, jnp.maximum(i * hb - 1, 0), 0))
    vec = pl.BlockSpec((1, D), lambda b, i: (0, 0))
    return _call(body, "conv_fwd", (B, S // tt), [main, halo, pl.BlockSpec((W, D), lambda b, i: (0, 0)), vec, vec, vec],
                 [main, main], [_sds((B, S, D), F32), _sds((B, S, D), BF16)], sem=("parallel", "parallel"))(
        u3, u3, dw32, dwb, lng, lnb)


def _conv_bwd_ln(v, ds, lng, lnb):
    T, D = v.shape
    tt = _tile(T, 256)

    def body(v_ref, ds_ref, g_ref, b_ref, dv_ref, red_ref):
        vv = v_ref[...]
        mu = jnp.mean(vv, axis=-1, keepdims=True)
        xc = vv - mu
        rstd = lax.rsqrt(jnp.mean(xc * xc, axis=-1, keepdims=True) + LN_EPS)
        xh = xc * rstd
        ln = xh * g_ref[...] + b_ref[...]
        sg = _sig(ln)
        dln = ds_ref[...] * (sg * (1.0 + ln * (1.0 - sg)))
        dxh = dln * g_ref[...]
        dv = rstd * (dxh - jnp.mean(dxh, axis=-1, keepdims=True) - xh * jnp.mean(dxh * xh, axis=-1, keepdims=True))
        dv_ref[...] = dv
        parts = (jnp.sum(dln * xh, axis=0, keepdims=True), jnp.sum(dln, axis=0, keepdims=True),
                 jnp.sum(dv, axis=0, keepdims=True))

        @pl.when(pl.program_id(0) == 0)
        def _():
            for k in range(3):
                red_ref[k] = parts[k]

        @pl.when(pl.program_id(0) > 0)
        def _():
            for k in range(3):
                red_ref[k] += parts[k]

    row = pl.BlockSpec((tt, D), lambda i: (i, 0))
    vec = pl.BlockSpec((1, D), lambda i: (0, 0))
    return _call(body, "conv_bwd_ln", (T // tt,), [row, row, vec, vec], [row, pl.BlockSpec((3, 1, D), lambda i: (0, 0, 0))],
                 [_sds((T, D), F32), _sds((3, 1, D), F32)], sem=("arbitrary",))(v, ds, lng, lnb)


def _conv_bwd_dw(dv3, u3, ab, dw32, carried=None):
    B, S, D = u3.shape
    W = dw32.shape[0]
    taps = CONV_TAPS
    tt = _tile(S, 256, HALO)
    hb = tt // HALO
    nt = S // tt
    L = tt + HALO

    def body(dv_ref, dvn_ref, u_ref, up_ref, ab_ref, dw_ref, dab_ref, ddw_ref):
        b, i = pl.program_id(0), pl.program_id(1)
        dv = dv_ref[...]
        ext_dv = jnp.concatenate([dv, jnp.where(i < nt - 1, dvn_ref[...], 0.0)], axis=0)
        ext_u = jnp.concatenate([jnp.where(i > 0, up_ref[...], 0.0), u_ref[...]], axis=0)
        du = jnp.zeros((tt, D), F32)
        first = jnp.logical_and(b == 0, i == 0)

        @pl.when(first)
        def _():
            ddw_ref[...] = jnp.zeros((W, D), F32)

        for j in range(taps):
            sh = taps - 1 - j
            du = du + dw_ref[pl.ds(j, 1), :] * _roll(ext_dv, L - sh)[:tt]
            ddw_ref[pl.ds(j, 1), :] += jnp.sum(dv * _roll(ext_u, sh)[HALO:], axis=0, keepdims=True)
        a = ab_ref[0].astype(F32)
        sb = _sig(ab_ref[1].astype(F32))
        dab_ref[0] = (du * sb).astype(BF16)
        dab_ref[1] = (du * a * sb * (1.0 - sb)).astype(BF16)

    main = pl.BlockSpec((None, tt, D), lambda b, i: (b, i, 0))
    prev = pl.BlockSpec((None, HALO, D), lambda b, i: (b, jnp.maximum(i * hb - 1, 0), 0))
    nxt = pl.BlockSpec((None, HALO, D), lambda b, i: (b, jnp.minimum((i + 1) * hb, S // HALO - 1), 0))
    abs_ = pl.BlockSpec((2, tt, D), lambda b, i: (0, b * nt + i, 0))
    wsp = pl.BlockSpec((W, D), lambda b, i: (0, 0))
    specs = ((B, nt), [main, nxt, main, prev, abs_, wsp], [abs_, wsp], [_sds((2, B * S, D), BF16), _sds((W, D), F32)], [])
    args = (dv3, dv3, u3, u3, ab, dw32)
    if carried is None:
        return _call(body, "conv_bwd_dw", *specs, sem=("arbitrary", "arbitrary"))(*args)
    dab, ddw, *got = _carry_call(body, "conv_bwd_dw", *specs, "scatter", carried)(*args, *carried)
    return dab, ddw, got


CONV_TAPS = 31


def _conv_mixer_fwd(x, B, p):
    T, D = x.shape
    h = _rms_fwd("conv_rms", x, p["mix_norm"][0:1])
    ab, u = _glu_mm("conv_in", h, p["conv_w_in2"], p["conv_b_in2"], "glu", F32)
    v, s = _conv_fwd(u.reshape(B, T // B, D), p["conv_dw32"], p["conv_dw_b"], p["conv_ln_g"], p["conv_ln_b"])
    v, s = v.reshape(T, D), s.reshape(T, D)
    return _mm_res("conv_out", s, p["conv_w_out"], x, 1.0), (x, h, ab, u, v, s)


def _conv_mixer_bwd(dxp, saved, B, p):
    x, h, ab, u, v, s = saved
    T, D = x.shape
    dx, dxb = dxp
    g = {}
    ds = _mm_nt("conv_ds", [(dxb, p["conv_w_out"])], F32)
    g["conv_w_out"] = _mm_tn("conv_dwout", s, dxb)
    dv, red = _conv_bwd_ln(v, ds, p["conv_ln_g"], p["conv_ln_b"])
    g["conv_ln_g"], g["conv_ln_b"], g["conv_dw_b"] = red[0], red[1], red[2]
    if "scatter" in p:
        dab, ddw, p["scattered"] = _conv_bwd_dw(dv.reshape(B, T // B, D), u.reshape(B, T // B, D), ab, p["conv_dw32"],
                                               p.pop("scatter"))
    else:
        dab, ddw = _conv_bwd_dw(dv.reshape(B, T // B, D), u.reshape(B, T // B, D), ab, p["conv_dw32"])
    g["conv_dw"] = ddw[:CONV_TAPS][None]
    g["conv_b_in"] = _colsum3("conv_dbin", dab).reshape(1, 2 * D)
    dwin = _mm_tn3("conv_dwin", h, dab)
    g["conv_w_in"] = jnp.moveaxis(dwin, 0, 1).reshape(1, D, 2 * D)
    dh = _mm_nt3("conv_dh", dab, p["conv_w_in2"])
    dx2, dg = _rms_bwd("conv_drms", dh, x, p["mix_norm"][0:1], dx)
    return dx2, dg, g


def _log_sigmoid(z):
    return jnp.minimum(z, 0.0) - jnp.log(1.0 + jnp.exp(-jnp.abs(z)))


def _fox_gate_fwd(fl3, bf):
    B, S, N = fl3.shape
    tt = _tile(S, 512)

    def body(fl_ref, bf_ref, c_ref, carry):
        @pl.when(pl.program_id(1) == 0)
        def _():
            carry[...] = jnp.zeros((1, N), F32)

        c = _scan_rows(_log_sigmoid(fl_ref[...] + bf_ref[...])) + carry[...]
        c_ref[...] = c
        carry[...] = c_ref[pl.ds(tt - 1, 1), :]

    row = pl.BlockSpec((None, tt, N), lambda b, i: (b, i, 0))
    return _call(body, "fox_gate_fwd", (B, S // tt), [row, pl.BlockSpec((1, N), lambda b, i: (0, 0))], row,
                 _sds((B, S, N), F32), [pltpu.VMEM((1, N), F32)], sem=("parallel", "arbitrary"))(fl3, bf)


def _fox_gate_bwd(dc3, fl3, bf):
    B, S, N = fl3.shape
    tt = _tile(S, 512)
    nt = S // tt

    def body(dc_ref, fl_ref, bf_ref, dfl_ref, dbf_ref, carry):
        b, i = pl.program_id(0), pl.program_id(1)

        @pl.when(i == 0)
        def _():
            carry[...] = jnp.zeros((1, N), F32)

        dc = dc_ref[0] - dc_ref[1]
        dlf = _scan_rows(dc, reverse=True) + carry[...]
        dfl = dlf * _sig(-(fl_ref[...] + bf_ref[...]))
        dfl_ref[...] = dfl
        carry[...] += jnp.sum(dc, axis=0, keepdims=True)
        part = jnp.sum(dfl, axis=0, keepdims=True)

        @pl.when(jnp.logical_and(b == 0, i == 0))
        def _():
            dbf_ref[...] = part

        @pl.when(jnp.logical_or(b > 0, i > 0))
        def _():
            dbf_ref[...] += part

    row = pl.BlockSpec((None, tt, N), lambda b, i: (b, nt - 1 - i, 0))
    vec = pl.BlockSpec((1, N), lambda b, i: (0, 0))
    row2 = pl.BlockSpec((2, None, tt, N), lambda b, i: (0, b, nt - 1 - i, 0))
    return _call(body, "fox_gate_bwd", (B, nt), [row2, row, vec], [row, vec], [_sds((B, S, N), F32), _sds((1, N), F32)],
                 [pltpu.VMEM((1, N), F32)], sem=("arbitrary", "arbitrary"))(dc3, fl3, bf)


NEG = -1e30


def _fox_attn_fwd(qa, ka, vat, crow, ckb, dh, scale, carried=None):
    B, H, S, P = qa.shape
    tq = _tile(S, 512, LANES)
    tk = tq
    nl = tq // LANES

    def body(q_ref, k_ref, vt_ref, cr_ref, ck_ref, o_ref, lse_ref, s_scr, p_scr, m_scr, al_scr, acc_scr):
        i = pl.program_id(2)
        qv = q_ref[...]
        m_scr[...] = jnp.full((1, tq), NEG, F32)
        acc_scr[...] = jnp.zeros((P, tq), F32)
        key = lax.broadcasted_iota(jnp.int32, (tk, LANES), 0)
        qry = lax.broadcasted_iota(jnp.int32, (tk, LANES), 1)

        def kv_step(j, diagonal):
            off = pl.multiple_of(j * tk, tk)
            s_scr[...] = _dot(k_ref[pl.ds(off, tk), :], qv, "nt")
            ck = ck_ref[pl.ds(off, tk), :]
            for lt in range(nl):
                ls = slice(lt * LANES, (lt + 1) * LANES)
                s = s_scr[:, ls] * scale + cr_ref[:, ls] - ck
                if diagonal:
                    s = jnp.where(key <= qry + lt * LANES, s, -jnp.inf)
                m1 = m_scr[:, ls]
                m2 = jnp.maximum(m1, jnp.max(s, axis=0, keepdims=True))
                p_scr[:, ls] = jnp.exp(s - m2).astype(BF16)
                al_scr[:, ls] = jnp.exp(m1 - m2)
                m_scr[:, ls] = m2
            acc_scr[...] = al_scr[...] * acc_scr[...] + _dot(vt_ref[:, pl.ds(off, tk)], p_scr[...])

        def before(j, carry):
            kv_step(j, False)
            return carry

        lax.fori_loop(0, i, before, 0)
        kv_step(i, True)
        l = acc_scr[pl.ds(dh, 1), :]
        o_ref[...] = (acc_scr[...] / l).T
        lse_ref[...] = m_scr[...] + jnp.log(l)

    qs = pl.BlockSpec((None, None, tq, P), lambda b, h, i: (b, h, i, 0))
    fullk = pl.BlockSpec((None, None, S, P), lambda b, h, i: (b, h, 0, 0))
    fullt = pl.BlockSpec((None, None, P, S), lambda b, h, i: (b, h, 0, 0))
    fullc = pl.BlockSpec((None, None, S, LANES), lambda b, h, i: (b, h, 0, 0))
    rowt = pl.BlockSpec((None, None, 1, tq), lambda b, h, i: (b, h, 0, i))
    scratch = [pltpu.VMEM((tk, tq), F32), pltpu.VMEM((tk, tq), BF16), pltpu.VMEM((1, tq), F32), pltpu.VMEM((1, tq), F32),
               pltpu.VMEM((P, tq), F32)]
    specs = ((B, H, S // tq), [qs, fullk, fullt, rowt, fullc], [qs, rowt], [_sds((B, H, S, P), F32), _sds((B, H, 1, S), F32)],
             scratch)
    if carried is None:
        return _call(body, "fox_attn_fwd", *specs, sem=("parallel", "parallel", "parallel"))(qa, ka, vat, crow, ckb)
    o, lse, *got = _carry_call(body, "fox_attn_fwd", *specs, "gather", carried)(qa, ka, vat, crow, ckb, *carried)
    return o, lse, got


def _fox_rowstats(do, o):
    B, H, S, P = o.shape
    tq = _tile(S, 4096)

    def body(do_ref, o_ref, dl_ref):
        dl_ref[...] = jnp.sum(do_ref[...] * o_ref[...], axis=-1, keepdims=True)

    qs = pl.BlockSpec((None, None, tq, P), lambda b, h, i: (b, h, i, 0))
    col = pl.BlockSpec((None, None, tq, 1), lambda b, h, i: (b, h, i, 0))
    return _call(body, "fox_rowstats", (B, H, S // tq), [qs, qs], col, _sds((B, H, S, 1), F32), sem=("parallel",) * 3)(do, o)


def _fox_attn_bwd(qa, ka, kat, va, do, crow, lse, delta, ckb, dh, scale, carried=None):
    B, H, S, P = qa.shape
    tk = _tile(S, 512, LANES)
    tq = tk
    nq = S // tq
    nl = tq // LANES

    def body(q_ref, k_ref, kt_ref, v_ref, do_ref, cr_ref, lse_ref, dl_ref, ck_ref, dqt_ref, dk_ref, dv_ref, rs_ref,
             s_scr, dp_scr, p_scr, ds_scr, dk_acc, dv_acc):
        j = pl.program_id(2)

        @pl.when(j == 0)
        def _():
            dqt_ref[...] = jnp.zeros((P, S), F32)

        kj, vj, ck = k_ref[...], v_ref[...], ck_ref[...]
        dk_acc[...] = jnp.zeros((tk, P), F32)
        dv_acc[...] = jnp.zeros((tk, P), F32)
        key = lax.broadcasted_iota(jnp.int32, (tk, LANES), 0)
        qry = lax.broadcasted_iota(jnp.int32, (tk, LANES), 1)

        def q_step(i, diagonal):
            off = pl.multiple_of(i * tq, tq)
            qi, doi = q_ref[pl.ds(off, tq), :], do_ref[pl.ds(off, tq), :].astype(BF16)
            s_scr[...] = _dot(kj, qi, "nt")
            dp_scr[...] = _dot(vj, doi, "nt")
            for lt in range(nl):
                ls = slice(lt * LANES, (lt + 1) * LANES)
                gl = pl.ds(pl.multiple_of(off + lt * LANES, LANES), LANES)
                p = jnp.exp(s_scr[:, ls] * scale + (cr_ref[:, gl] - lse_ref[:, gl]) - ck)
                if diagonal:
                    p = jnp.where(key <= qry + lt * LANES, p, 0.0)
                p_scr[:, ls] = p.astype(BF16)
                ds_scr[:, ls] = (p * (dp_scr[:, ls] - dl_ref[:, gl])).astype(BF16)
            ds = ds_scr[...]
            dqt_ref[:, pl.ds(off, tq)] += _dot(kt_ref[...], ds)
            dk_acc[...] += _dot(ds, qi)
            dv_acc[...] += _dot(p_scr[...], doi)

        def after(i, carry):
            q_step(i, False)
            return carry

        q_step(j, True)
        lax.fori_loop(j + 1, nq, after, 0)
        dk_ref[...] = dk_acc[...] * jnp.where(lax.broadcasted_iota(jnp.int32, (tk, P), 1) < dh, scale, 1.0)
        dv_ref[...] = dv_acc[...]

        @pl.when(j == S // tk - 1)
        def _():
            rs_ref[...] = dqt_ref[pl.ds(dh, 1), :]
            dqt_ref[...] = dqt_ref[...] * jnp.where(lax.broadcasted_iota(jnp.int32, (P, S), 0) < dh, scale, 1.0)

    ks = pl.BlockSpec((None, None, tk, P), lambda b, h, j: (b, h, j, 0))
    kts = pl.BlockSpec((None, None, P, tk), lambda b, h, j: (b, h, 0, j))
    cks = pl.BlockSpec((None, None, tk, LANES), lambda b, h, j: (b, h, j, 0))
    full = pl.BlockSpec((None, None, S, P), lambda b, h, j: (b, h, 0, 0))
    fullt = pl.BlockSpec((None, None, P, S), lambda b, h, j: (b, h, 0, 0))
    rowf = pl.BlockSpec((None, None, 1, S), lambda b, h, j: (b, h, 0, 0))
    scratch = [pltpu.VMEM((tk, tq), F32), pltpu.VMEM((tk, tq), F32), pltpu.VMEM((tk, tq), BF16), pltpu.VMEM((tk, tq), BF16),
               pltpu.VMEM((tk, P), F32), pltpu.VMEM((tk, P), F32)]
    specs = ((B, H, S // tk), [full, ks, kts, ks, full, rowf, rowf, rowf, cks], [fullt, ks, ks, rowf],
             [_sds((B, H, P, S), F32), _sds((B, H, S, P), F32), _sds((B, H, S, P), F32), _sds((B, H, 1, S), F32)], scratch)
    args = (qa, ka, kat, va, do, crow, lse, delta, ckb)
    if carried is None:
        return _call(body, "fox_attn_bwd", *specs, sem=("parallel", "parallel", "arbitrary"))(*args)
    dqt, dk, dv, rs, *got = _carry_call(body, "fox_attn_bwd", *specs, "scatter", carried)(*args, *carried)
    return dqt, dk, dv, rs, got


def _heads(t, B, H):
    T, D = t.shape
    return t.reshape(B, T // B, H, D // H).transpose(0, 2, 1, 3)


def _unheads(t):
    B, H, S, dh = t.shape
    return t.transpose(0, 2, 1, 3).reshape(B * S, H * dh)


def _fox_mixer_fwd(x, B, p):
    T, D = x.shape
    H = FOX_HEADS
    S = T // B
    scale = (D // H) ** -0.5
    h = _rms_fwd("fox_rms", x, p["mix_norm"][1:2])
    qkv = _mm_nn("fox_qkv", h, p["fox_w_qkv"], BF16)
    fl = _mm_nn("fox_fl", h, p["fox_w_f"], F32)
    c = _fox_gate_fwd(fl.reshape(B, S, LANES), p["fox_b_f128"])
    ch = c[:, :, :H].transpose(0, 2, 1)
    crow = ch[:, :, None, :]
    ckb = jnp.broadcast_to(ch[..., None], (B, H, S, LANES))
    q, k, v = _heads(qkv[:, :D], B, H), _heads(qkv[:, D:2 * D], B, H), _heads(qkv[:, 2 * D:], B, H)
    dh = D // H
    P = -(-(dh + 2) // LANES) * LANES
    one, zero = jnp.ones((B, H, S, 1), q.dtype), jnp.zeros((B, H, S, 1), q.dtype)
    rest = jnp.zeros((B, H, S, P - dh - 2), q.dtype)
    qa = jnp.concatenate([q, zero, one, rest], axis=-1)
    ka = jnp.concatenate([k, one, zero, rest], axis=-1)
    va = jnp.concatenate([v, one, zero, rest], axis=-1)
    if "late_gather" in p:
        o, lse, p["late_gathered"] = _fox_attn_fwd(qa, ka, va.transpose(0, 1, 3, 2), crow, ckb, dh, scale, p.pop("late_gather"))
    else:
        o, lse = _fox_attn_fwd(qa, ka, va.transpose(0, 1, 3, 2), crow, ckb, dh, scale)
    of = _unheads(o[..., :dh])
    return _mm_res("fox_out", of, p["fox_w_out"], x, 1.0), (x, h, fl, qa, ka, va, crow, ckb, o, lse, of)


def _fox_mixer_bwd(dxp, saved, B, p):
    x, h, fl, qa, ka, va, crow, ckb, o, lse, of = saved
    T, D = x.shape
    H = FOX_HEADS
    S = T // B
    dh_ = D // H
    P = qa.shape[-1]
    scale = dh_ ** -0.5
    dx, dxb = dxp
    g = {}
    do = _heads(_mm_nt("fox_do", [(dxb, p["fox_w_out"])], F32), B, H)
    do = jnp.pad(do, ((0, 0), (0, 0), (0, 0), (0, P - dh_)))
    g["fox_w_out"] = _mm_tn("fox_dwout", of, dxb)[None]
    delta = _fox_rowstats(do, o).reshape(B, H, 1, S)
    if "scatter" in p:
        dqt, dk, dv, rowsum, p["scattered"] = _fox_attn_bwd(qa, ka, ka.transpose(0, 1, 3, 2), va, do, crow, lse, delta, ckb,
                                                          dh_, scale, p.pop("scatter"))
    else:
        dqt, dk, dv, rowsum = _fox_attn_bwd(qa, ka, ka.transpose(0, 1, 3, 2), va, do, crow, lse, delta, ckb, dh_, scale)
    dq = dqt[:, :, :dh_, :].transpose(0, 3, 1, 2).reshape(T, D)
    dqkv = jnp.concatenate([dq, _unheads(dk[..., :dh_]), _unheads(dv[..., :dh_])], axis=1).astype(BF16)
    dc = jnp.stack([rowsum[:, :, 0, :], dk[..., dh_ + 1]])
    dc = jnp.pad(dc.transpose(0, 1, 3, 2), ((0, 0), (0, 0), (0, 0), (0, LANES - H)))
    dfl, dbf = _fox_gate_bwd(dc, fl.reshape(B, S, LANES), p["fox_b_f128"])
    dfl = dfl.reshape(T, LANES)
    g["fox_b_f"] = dbf[:, :H]
    dwqkv = _mm_tn("fox_dwqkv", h, dqkv)
    dwf = _mm_tn("fox_dwf", h, dfl)
    g["fox_w_in"] = jnp.concatenate([dwqkv, dwf[:, :H]], axis=1)[None]
    dh = _mm_nt("fox_dh", [(dqkv, p["fox_w_qkv"]), (dfl, p["fox_w_f"])], F32)
    dx2, dg = _rms_bwd("fox_drms", dh, x, p["mix_norm"][1:2], dx)
    return dx2, dg, g


def _lb_fwd(logits):
    L, D = logits.shape

    def body(l_ref, lb_ref):
        z = l_ref[...]
        e = jnp.exp(z - jnp.max(z, axis=0, keepdims=True))
        p = e / jnp.sum(e, axis=0, keepdims=True)
        lb_ref[...] = jnp.sum(jnp.where(_lb_rows(z.shape), p, 0.0), axis=0, keepdims=True)

    return _call(body, "hgrn_lb", (1,), [pl.BlockSpec((L, D), lambda i: (0, 0))], pl.BlockSpec((1, D), lambda i: (0, 0)),
                 _sds((1, D), F32))(logits)


def _lb_rows(shape):
    r = lax.broadcasted_iota(jnp.int32, shape, 0)
    return jnp.logical_and(r >= 1, r <= HGRN_LAYER)


HGRN_LAYER = 2


def _lb_bwd(logits, dlb):
    L, D = logits.shape

    def body(l_ref, d_ref, o_ref):
        z = l_ref[...]
        e = jnp.exp(z - jnp.max(z, axis=0, keepdims=True))
        p = e / jnp.sum(e, axis=0, keepdims=True)
        dp = jnp.where(_lb_rows(z.shape), d_ref[...], 0.0)
        o_ref[...] = p * (dp - jnp.sum(p * dp, axis=0, keepdims=True))

    full = pl.BlockSpec((L, D), lambda i: (0, 0))
    return _call(body, "hgrn_dlb", (1,), [full, pl.BlockSpec((1, D), lambda i: (0, 0))], full, _sds((L, D), F32))(logits, dlb)


def _hgrn_gates(qr, fr, lbv):
    e = jnp.exp(-jnp.abs(fr))
    big, small = 1.0 / (1.0 + e), e / (1.0 + e)
    sf = jnp.where(fr >= 0, big, small)
    snf = jnp.where(fr >= 0, small, big)
    f = lbv + (1.0 - lbv) * sf
    sq = _sig(qr)
    return qr * sq, (1.0 - lbv) * snf, jnp.log(f), sf, snf, f, sq


def _hgrn_intra(G, q, kk, g_scr, q_scr):
    C = HGRN_CHUNK
    g_scr[...] = G
    q_scr[...] = q
    srow = lax.broadcasted_iota(jnp.int32, (C, LANES), 0)
    lane = lax.broadcasted_iota(jnp.int32, (C, LANES), 1)
    at = jnp.zeros((C, LANES), F32)
    for t in range(C):
        e = jnp.where(srow <= t, jnp.exp(g_scr[pl.ds(t, 1), :] - G), 0.0)
        col = jnp.sum(e * kk * q_scr[pl.ds(t, 1), :], axis=-1, keepdims=True)
        at = jnp.where(lane == t, col, at)
    return at


def _hgrn_fwd(proj3, lb, ng):
    B, S, D4 = proj3.shape
    D = D4 // 4
    H = D // HGRN_EXPAND
    C = HGRN_CHUNK
    R = _tile(S, 256, 2 * C)
    ncb = R // C
    dk = HGRN_EXPAND

    def body(q_ref, f_ref, i_ref, go_ref, lb_ref, ng_ref, y_ref, o_ref, st_ref, st, g_scr, q_scr):
        @pl.when(pl.program_id(2) == 0)
        def _():
            st[...] = jnp.zeros((dk, dk), F32)

        lbv = lb_ref[...]

        def chunk(c, slot):
            r0 = pl.multiple_of(c * C, C)
            rows = pl.ds(r0, C)
            q, kk, lf, *_ = _hgrn_gates(q_ref[rows, :], f_ref[rows, :], lbv)
            vv = i_ref[rows, :]
            G = _scan_rows(lf)
            at = _hgrn_intra(G, q, kk, g_scr.at[slot], q_scr.at[slot])
            gl = g_scr[slot, pl.ds(C - 1, 1), :]
            stv = st[...]
            st_ref[c] = stv
            o = _dot(q * jnp.exp(G), stv, "nt") + _dot(at, vv, "tn")[:C]
            st[...] = stv * jnp.exp(gl) + _dot(vv, kk * jnp.exp(gl - G), "tn")
            o_ref[rows, :] = o
            gv = go_ref[rows, :]
            y = o * lax.rsqrt(jnp.mean(o * o, axis=-1, keepdims=True) + RMS_EPS) * ng_ref[...] * (gv * _sig(gv))
            y_ref[rows, :] = y.astype(BF16)

        def pair(c2, carry):
            chunk(2 * c2, 0)
            chunk(2 * c2 + 1, 1)
            return carry

        lax.fori_loop(0, ncb // 2, pair, 0)

    def col(k):
        return pl.BlockSpec((None, R, dk), lambda b, h, i: (b, i, h + k * H))

    vec = pl.BlockSpec((1, dk), lambda b, h, i: (0, h))
    out = pl.BlockSpec((None, R, dk), lambda b, h, i: (b, i, h))
    return _call(body, "hgrn_fwd", (B, H, S // R), [col(0), col(1), col(2), col(3), vec, vec],
                 [out, out, pl.BlockSpec((None, None, ncb, dk, dk), lambda b, h, i: (b, h, i, 0, 0))],
                 [_sds((B, S, D), BF16), _sds((B, S, D), F32), _sds((B, H, S // C, dk, dk), F32)],
                 [pltpu.VMEM((dk, dk), F32), pltpu.VMEM((2, C, dk), F32), pltpu.VMEM((2, C, dk), F32)],
                 sem=("parallel", "parallel", "arbitrary"))(proj3, proj3, proj3, proj3, lb, ng)


def _hgrn_bwd(proj3, o3, dy3, states, lb, ng, carried=None):
    B, S, D4 = proj3.shape
    D = D4 // 4
    H = D // HGRN_EXPAND
    C = HGRN_CHUNK
    R = _tile(S, 256, 2 * C)
    ncb = R // C
    nb = S // R
    dk = HGRN_EXPAND

    def body(q_ref, f_ref, i_ref, go_ref, o_ref, dy_ref, st_ref, lb_ref, ng_ref,
             dp_ref, red_ref, dst, g_scr, q_scr, dq_scr, acc):
        b, i = pl.program_id(1), pl.program_id(2)

        @pl.when(i == 0)
        def _():
            dst[...] = jnp.zeros((dk, dk), F32)

        @pl.when(jnp.logical_and(b == 0, i == 0))
        def _():
            acc[...] = jnp.zeros((2, dk), F32)

        lbv = lb_ref[...]
        ngv = ng_ref[...]
        srow = lax.broadcasted_iota(jnp.int32, (C, LANES), 0)
        lane = lax.broadcasted_iota(jnp.int32, (C, LANES), 1)

        def chunk(c, slot):
            r0 = pl.multiple_of(c * C, C)
            rows = pl.ds(r0, C)
            qr, fr, vv, gv = q_ref[rows, :], f_ref[rows, :], i_ref[rows, :], go_ref[rows, :]
            q, kk, lf, sf, snf, f, sq = _hgrn_gates(qr, fr, lbv)
            o = o_ref[rows, :]
            dy = dy_ref[rows, :]
            rinv = lax.rsqrt(jnp.mean(o * o, axis=-1, keepdims=True) + RMS_EPS)
            on = o * rinv
            sgv = _sig(gv)
            dz = dy * (gv * sgv)
            dp_ref[3, rows, :] = (dy * on * ngv * (sgv * (1.0 + gv * (1.0 - sgv)))).astype(BF16)
            acc[pl.ds(1, 1), :] += jnp.sum(dz * on, axis=0, keepdims=True)
            don = dz * ngv
            do = rinv * (don - on * jnp.mean(don * on, axis=-1, keepdims=True))
            G = _scan_rows(lf)
            g_scr[slot] = G
            q_scr[slot] = q
            gl = g_scr[slot, pl.ds(C - 1, 1), :]
            egl = jnp.exp(gl)
            eG = jnp.exp(G)
            eK = jnp.exp(gl - G)
            qg, kg = q * eG, kk * eK
            stv = st_ref[c]
            dsv = dst[...]
            dqg = _dot(do, stv)
            do_pad = jnp.concatenate([do, jnp.zeros((LANES - C, dk), F32)], axis=0)
            dat = _dot(vv, do_pad, "nt")
            dkg = _dot(vv, dsv)
            dgl = egl * jnp.sum(stv * dsv, axis=0, keepdims=True) + jnp.sum(dkg * kg, axis=0, keepdims=True)
            at = jnp.zeros((C, LANES), F32)
            dki = jnp.zeros((C, dk), F32)
            for t in range(C):
                e = jnp.where(srow <= t, jnp.exp(g_scr[slot, pl.ds(t, 1), :] - G), 0.0)
                qt = q_scr[slot, pl.ds(t, 1), :]
                at = jnp.where(lane == t, jnp.sum(e * kk * qt, axis=-1, keepdims=True), at)
                z = e * jnp.sum(jnp.where(lane == t, dat, 0.0), axis=-1, keepdims=True)
                dq_scr[slot, pl.ds(t, 1), :] = jnp.sum(z * kk, axis=0, keepdims=True)
                dki = dki + z * qt
            dqi = dq_scr[slot]
            dp_ref[2, rows, :] = (_dot(at, do_pad) + _dot(kg, dsv, "nt")).astype(BF16)
            dst[...] = dsv * egl + _dot(do, qg, "tn")
            dq = dqg * eG + dqi
            dkk = dkg * eK + dki
            dG = dqg * qg - dkg * kg + q * dqi - kk * dki
            dG = dG + jnp.where(srow == C - 1, dgl, 0.0)
            dlf = _scan_rows(dG, reverse=True)
            dsf = (1.0 - lbv) * sf * snf
            dp_ref[1, rows, :] = (dlf * dsf / f - dkk * dsf).astype(BF16)
            acc[pl.ds(0, 1), :] += jnp.sum(dlf * snf / f - dkk * snf, axis=0, keepdims=True)
            dp_ref[0, rows, :] = (dq * (sq * (1.0 + qr * (1.0 - sq)))).astype(BF16)

        def pair(c2, carry):
            chunk(ncb - 1 - 2 * c2, 0)
            chunk(ncb - 2 - 2 * c2, 1)
            return carry

        lax.fori_loop(0, ncb // 2, pair, 0)

        @pl.when(jnp.logical_and(b == B - 1, i == nb - 1))
        def _():
            red_ref[0] = acc[pl.ds(0, 1), :]
            red_ref[1] = acc[pl.ds(1, 1), :]

    def col(k):
        return pl.BlockSpec((None, R, dk), lambda h, b, i: (b, nb - 1 - i, h + k * H))

    vec = pl.BlockSpec((1, dk), lambda h, b, i: (0, h))
    row = pl.BlockSpec((None, R, dk), lambda h, b, i: (b, nb - 1 - i, h))
    stsp = pl.BlockSpec((None, None, ncb, dk, dk), lambda h, b, i: (b, h, nb - 1 - i, 0, 0))
    specs = ((H, B, nb), [col(0), col(1), col(2), col(3), row, row, stsp, vec, vec],
             [pl.BlockSpec((4, None, R, dk), lambda h, b, i: (0, b, nb - 1 - i, h)),
              pl.BlockSpec((2, 1, dk), lambda h, b, i: (0, 0, h))],
             [_sds((4, B, S, D), BF16), _sds((2, 1, D), F32)],
             [pltpu.VMEM((dk, dk), F32), pltpu.VMEM((2, C, dk), F32), pltpu.VMEM((2, C, dk), F32),
              pltpu.VMEM((2, C, dk), F32), pltpu.VMEM((2, dk), F32)])
    args = (proj3, proj3, proj3, proj3, o3, dy3, states, lb, ng)
    if carried is None:
        return _call(body, "hgrn_bwd", *specs, sem=("parallel", "arbitrary", "arbitrary"))(*args)
    dp, red, *got = _carry_call(body, "hgrn_bwd", *specs, "scatter", carried)(*args, *carried)
    return dp, red, got


def _hgrn_mixer_fwd(x, B, p):
    T, D = x.shape
    S = T // B
    h = _rms_fwd("hgrn_rms", x, p["mix_norm"][2:3])
    proj = _mm_nn("hgrn_in", h, p["hgrn_w_in"], F32)
    lb = _lb_fwd(p["hgrn_lb_logits"])
    y, o, states = _hgrn_fwd(proj.reshape(B, S, 4 * D), lb, p["hgrn_norm"])
    y = y.reshape(T, D)
    return _mm_res("hgrn_out", y, p["hgrn_w_out"], x, 1.0), (x, h, proj, lb, y, o, states)


def _hgrn_mixer_bwd(dxp, saved, B, p):
    x, h, proj, lb, y, o, states = saved
    T, D = x.shape
    S = T // B
    dx, dxb = dxp
    g = {}
    dy = _mm_nt("hgrn_dy", [(dxb, p["hgrn_w_out"])], F32)
    g["hgrn_w_out"] = _mm_tn("hgrn_dwout", y, dxb)[None]
    if "scatter" in p:
        dp, red, p["scattered"] = _hgrn_bwd(proj.reshape(B, S, 4 * D), o, dy.reshape(B, S, D), states, lb, p["hgrn_norm"],
                                           p.pop("scatter"))
    else:
        dp, red = _hgrn_bwd(proj.reshape(B, S, 4 * D), o, dy.reshape(B, S, D), states, lb, p["hgrn_norm"])
    dp = dp.reshape(4, T, D)
    g["hgrn_norm"] = red[1]
    g["hgrn_lb_logits"] = _lb_bwd(p["hgrn_lb_logits"], red[0])
    dwin = _mm_tn3("hgrn_dwin", h, dp)
    g["hgrn_w_in"] = jnp.moveaxis(dwin, 0, 1).reshape(1, D, 4 * D)
    dh = _mm_nt3("hgrn_dh", dp, p["hgrn_w_in4"])
    dx2, dg = _rms_bwd("hgrn_drms", dh, x, p["mix_norm"][2:3], dx)
    return dx2, dg, g


POOL_HALO = 16


def _pool_fwd(x3, g):
    B, S, D = x3.shape
    tt = _tile(S, 256, POOL_HALO)
    hb = tt // POOL_HALO
    G = D // len(POOL_WINDOWS)

    def body(x_ref, halo_ref, g_ref, m_ref):
        i = pl.program_id(1)

        def norm(xv):
            return xv * lax.rsqrt(jnp.mean(xv * xv, axis=-1, keepdims=True) + RMS_EPS) * g_ref[...]

        hm = norm(x_ref[...])
        ext = jnp.concatenate([jnp.where(i > 0, norm(halo_ref[...]), 0.0), hm], axis=0)
        pos = (i * tt + lax.broadcasted_iota(jnp.int32, (tt, 1), 0) + 1).astype(F32)
        for gi, win in enumerate(POOL_WINDOWS):
            s = ext[:, gi * G:(gi + 1) * G]
            w = 1
            while w < win:
                s = s + _roll(s, w)
                w *= 2
            m_ref[:, gi * G:(gi + 1) * G] = (s[POOL_HALO:] / jnp.minimum(pos, float(win)) - hm[:, gi * G:(gi + 1) * G]).astype(BF16)

    main = pl.BlockSpec((None, tt, D), lambda b, i: (b, i, 0))
    halo = pl.BlockSpec((None, POOL_HALO, D), lambda b, i: (b, jnp.maximum(i * hb - 1, 0), 0))
    return _call(body, "pool_fwd", (B, S // tt), [main, halo, pl.BlockSpec((1, D), lambda b, i: (0, 0))], main,
                 _sds((B, S, D), BF16), sem=("parallel", "parallel"))(x3, x3, g)


def _pool_bwd(dm3):
    B, S, D = dm3.shape
    tt = _tile(S, 256, POOL_HALO)
    hb = tt // POOL_HALO
    nt = S // tt
    G = D // len(POOL_WINDOWS)
    L = tt + POOL_HALO

    def body(dm_ref, nxt_ref, dh_ref):
        i = pl.program_id(1)
        posm = (i * tt + lax.broadcasted_iota(jnp.int32, (tt, 1), 0) + 1).astype(F32)
        posn = ((i + 1) * tt + lax.broadcasted_iota(jnp.int32, (POOL_HALO, 1), 0) + 1).astype(F32)
        for gi, win in enumerate(POOL_WINDOWS):
            sl = slice(gi * G, (gi + 1) * G)
            dm = dm_ref[:, sl]
            s = jnp.concatenate([dm / jnp.minimum(posm, float(win)),
                                 jnp.where(i < nt - 1, nxt_ref[:, sl] / jnp.minimum(posn, float(win)), 0.0)], axis=0)
            w = 1
            while w < win:
                s = s + _roll(s, L - w)
                w *= 2
            dh_ref[:, sl] = s[:tt] - dm

    main = pl.BlockSpec((None, tt, D), lambda b, i: (b, i, 0))
    nxt = pl.BlockSpec((None, POOL_HALO, D), lambda b, i: (b, jnp.minimum((i + 1) * hb, S // POOL_HALO - 1), 0))
    return _call(body, "pool_bwd", (B, nt), [main, nxt], main, _sds((B, S, D), F32), sem=("parallel", "parallel"))(dm3, dm3)


def _pool_mixer_fwd(x, B, p):
    T, D = x.shape
    NG = len(POOL_WINDOWS)
    G = D // NG
    tm = _tile(T, 512)
    m = _pool_fwd(x.reshape(B, T // B, D), p["mix_norm"][3:4]).reshape(T, D)

    def epi(res, ex, outs):
        outs[0][...] = ex[1][...] + res * ex[0][...]

    blk = pl.BlockSpec((tm, G), lambda i, g: (i, g))
    x2 = _mm("pool_out", "nn", [(m, p["pool_w4"])], [(blk, pl.BlockSpec((None, G, G), lambda i, g: (g, 0, 0)))],
             (T // tm, NG), None, [_sds((T, D), F32)], [blk], extras=[p["pool_scale"], x],
             extra_specs=[pl.BlockSpec((1, G), lambda i, g: (0, g)), blk], epilogue=epi)[0]
    return x2, (x, m)


def _pool_mixer_bwd(dxp, saved, B, p):
    x, m = saved
    dx, dxb = dxp
    T, D = x.shape
    NG = len(POOL_WINDOWS)
    G = D // NG
    tm = _tile(T, 512)
    g = {}

    def epi(zz, ex, outs):
        dy = ex[0][...]
        outs[0][...] = (dy * ex[1][...]).astype(BF16)
        part = jnp.sum(dy * zz, axis=0, keepdims=True)

        @pl.when(pl.program_id(1) == 0)
        def _():
            outs[1][...] = part

        @pl.when(pl.program_id(1) > 0)
        def _():
            outs[1][...] += part

    blk = pl.BlockSpec((tm, G), lambda g_, i: (i, g_))
    wsp = pl.BlockSpec((None, G, G), lambda g_, i: (g_, 0, 0))
    vec = pl.BlockSpec((1, G), lambda g_, i: (0, g_))
    dz, dsc = _mm("pool_dz", "nn", [(m, p["pool_w4"])], [(blk, wsp)], (NG, T // tm), None,
                  [_sds((T, D), BF16), _sds((1, D), F32)], [blk, vec], extras=[dx, p["pool_scale"]], extra_specs=[blk, vec],
                  epilogue=epi, sem=("parallel", "arbitrary"))
    g["pool_scale"] = dsc
    tk = _tile(T, 512)
    kb = pl.BlockSpec((tk, G), lambda g_, k: (k, g_))
    g["pool_w"] = _mm("pool_dw", "tn", [(m, dz)], [(kb, kb)], (NG, T // tk), 1, [_sds((NG, G, G), F32)],
                      [pl.BlockSpec((None, G, G), lambda g_, k: (g_, 0, 0))], acc_shape=(G, G))[0][None]
    blk2 = pl.BlockSpec((tm, G), lambda i, g_: (i, g_))
    dm = _mm("pool_dm", "nt", [(dz, p["pool_w4"])], [(blk2, pl.BlockSpec((None, G, G), lambda i, g_: (g_, 0, 0)))],
             (T // tm, NG), None, [_sds((T, D), F32)], [blk2])[0]
    dh = _pool_bwd(dm.reshape(B, T // B, D)).reshape(T, D)
    dx2, dg = _rms_bwd("pool_drms", dh, x, p["mix_norm"][3:4], dx)
    return dx2, dg, g


_MIXERS = ((_conv_mixer_fwd, _conv_mixer_bwd), (_fox_mixer_fwd, _fox_mixer_bwd), (_hgrn_mixer_fwd, _hgrn_mixer_bwd),
           (_pool_mixer_fwd, _pool_mixer_bwd))


EARLY_LAYERS = 2
LATE = ("ffn_w_gate", "ffn_w_up", "ffn_w_down", "hgrn_w_in", "hgrn_w_out", "pool_w")
MIXER_MATRICES = (("conv_w_in", "conv_w_out"), ("fox_w_in", "fox_w_out"), ("hgrn_w_in", "hgrn_w_out"), ("pool_w",))


def _local_step(x3, tgt3, w, plan=None):
    B, S, D = x3.shape
    T = B * S
    depth = w["ffn_norm"].shape[0]
    H = FOX_HEADS
    p = dict(w)
    p["conv_w_in2"] = w["conv_w_in"][0].reshape(D, 2, D).transpose(1, 0, 2)
    p["conv_b_in2"] = w["conv_b_in"].reshape(2, 1, D)
    p["conv_dw32"] = jnp.pad(w["conv_dw"][0], ((0, HALO - CONV_TAPS), (0, 0)))
    p["conv_w_out"] = w["conv_w_out"][0]
    p["fox_w_qkv"] = w["fox_w_in"][0][:, :3 * D]
    p["fox_w_f"] = jnp.pad(w["fox_w_in"][0][:, 3 * D:], ((0, 0), (0, LANES - H)))
    p["fox_b_f128"] = jnp.pad(w["fox_b_f"], ((0, 0), (0, LANES - H)))
    p["fox_w_out"] = w["fox_w_out"][0]
    ffn = {}

    def add_layers(first, gate, up, down, rest):
        for k in range(gate.shape[0]):
            ffn[first + k] = (jnp.stack((gate[k], up[k]), axis=1), down[k])
        if rest is not None:
            p["hgrn_w_in"] = rest["hgrn_w_in"][0]
            p["hgrn_w_in4"] = rest["hgrn_w_in"][0].reshape(D, 4, D).transpose(1, 0, 2)
            p["hgrn_w_out"] = rest["hgrn_w_out"][0]
            p["pool_w4"] = rest["pool_w"][0]

    add_layers(0, w["ffn_w_gate"], w["ffn_w_up"], w["ffn_w_down"], w if plan is None else None)
    if plan is not None:
        p["late_gather"] = plan["late_shards"]

    x = x3.reshape(T, D)
    saved = []
    for i in range(depth):
        x, s0 = _ffn_fwd(f"{i}a", x, w["ffn_norm"][i, 0:1], ffn[i][0][0], ffn[i][1][0])
        x, s1 = _MIXERS[i % 4][0](x, B, p)
        if "late_gathered" in p:
            late = plan["assemble"](p.pop("late_gathered"))
            add_layers(EARLY_LAYERS, late["ffn_w_gate"], late["ffn_w_up"], late["ffn_w_down"], late)
        x, s2 = _ffn_fwd(f"{i}b", x, w["ffn_norm"][i, 1:2], ffn[i][0][1], ffn[i][1][1])
        saved.append((s0, s1, s2))
    loss, dx, dfinal = _loss_head(x, w["final_norm"].reshape(1, D), tgt3.reshape(T, D))

    g = {"final_norm": dfinal}
    dffn_norm = [[None, None] for _ in range(depth)]
    dwgu = [[None, None] for _ in range(depth)]
    dwd = [[None, None] for _ in range(depth)]
    dmix = [None] * depth
    travelled = []

    def layer_grads(i, extra):
        gate, up = (jnp.stack([dwgu[i][0][s], dwgu[i][1][s]])[None] for s in (0, 1))
        return [gate, up, jnp.stack(dwd[i])[None]] + [g.pop(n) for n in extra]

    for i in reversed(range(depth)):
        s0, s1, s2 = saved[i]
        dx, dffn_norm[i][1], dwgu[i][1], dwd[i][1] = _ffn_bwd(f"{i}b", dx, s2, w["ffn_norm"][i, 1:2], ffn[i][0][1], ffn[i][1][1])
        if plan is not None and i < depth - 1:
            extra = MIXER_MATRICES[i + 1]
            names = LATE[:3] + extra
            p["scatter"] = plan["dev_major"](names, layer_grads(i + 1, extra))
            travelled.append([names, p["scatter"]])
        dx, dmix[i], gm = _MIXERS[i % 4][1](dx, s1, B, p)
        if "scattered" in p:
            travelled[-1].append(p.pop("scattered"))
        g.update(gm)
        dx, dffn_norm[i][0], dwgu[i][0], dwd[i][0] = _ffn_bwd(f"{i}a", dx, s0, w["ffn_norm"][i, 0:1], ffn[i][0][0], ffn[i][1][0])
    kept = range(depth) if plan is None else range(1)
    g["ffn_norm"] = jnp.stack([jnp.stack([a[0], b[0]]) for a, b in dffn_norm])
    g["ffn_w_gate"] = jnp.stack([jnp.stack([dwgu[i][0][0], dwgu[i][1][0]]) for i in kept])
    g["ffn_w_up"] = jnp.stack([jnp.stack([dwgu[i][0][1], dwgu[i][1][1]]) for i in kept])
    g["ffn_w_down"] = jnp.stack([jnp.stack(dwd[i]) for i in kept])
    g["mix_norm"] = jnp.concatenate(dmix, axis=0)
    g = {n: (t.reshape(w[n].shape) if n in w and n not in LATE else t) for n, t in g.items()}
    return loss, dx[0].reshape(B, S, D), g, travelled


def _adamw(name, w, m, v, parts):
    shape = w.shape
    cols = shape[-1]
    rows = w.size // cols
    tr = _tile(rows, max(8, (1 << 19) // cols))
    n = len(parts)
    c1 = 1.0 - ADAM_B1 ** ADAM_STEP
    c2 = 1.0 - ADAM_B2 ** ADAM_STEP

    def body(*refs):
        w_ref, m_ref, v_ref = refs[:3]
        g_ref, d_ref, m2_ref, v2_ref = refs[3 + n:]
        g = refs[3][...].astype(F32)
        for k in range(1, n):
            g = g + refs[3 + k][...].astype(F32)
        m2 = ADAM_B1 * m_ref[...] + (1.0 - ADAM_B1) * g
        v2 = ADAM_B2 * v_ref[...] + (1.0 - ADAM_B2) * (g * g)
        g_ref[...] = g
        m2_ref[...] = m2
        v2_ref[...] = v2
        d_ref[...] = -ADAM_LR * ((m2 / c1) / (jnp.sqrt(v2 / c2) + ADAM_EPS) + ADAM_WD * w_ref[...])

    blk = pl.BlockSpec((tr, cols), lambda i: (i, 0))
    outs = _call(body, name, (rows // tr,), [blk] * (3 + n), [blk] * 4, [_sds((rows, cols), F32)] * 4, sem=("parallel",))(
        *[t.reshape(rows, cols) for t in (w, m, v, *parts)])
    return [o.reshape(shape) for o in outs]


ANY = pl.BlockSpec(memory_space=pl.ANY)
FLAT_COLS = 1024


def _place():
    return lax.axis_index("x"), lax.axis_index("y"), lax.axis_index("c")


def _all_gather(name, xs):
    K = len(xs)

    def body(*refs):
        x_refs, out_refs = refs[:K], refs[K:2 * K]
        send_sems, recv_sems, local_sems = refs[2 * K:]
        xi, yi, ci = _place()
        me, sibling = (xi, yi, ci), (xi, yi, 1 - ci)
        chips = [(1 - xi, yi), (xi, 1 - yi), (1 - xi, 1 - yi)]

        def slot(a, px, py, pc):
            return out_refs[a].at[4 * px + 2 * py + pc]

        def copy(a, k, block, to, own=False):
            return pltpu.make_async_remote_copy(src_ref=x_refs[a] if own else slot(a, *block), dst_ref=slot(a, *block),
                                                send_sem=send_sems.at[7 * a + k], recv_sem=recv_sems.at[7 * a + k],
                                                device_id=to, device_id_type=MESH)

        mine = [pltpu.make_async_copy(x_refs[a], slot(a, *me), local_sems.at[a]) for a in range(K)]
        first = [copy(a, 1 + j, me, (*chip, ci), own=True) for j, chip in enumerate(chips) for a in range(K)]
        first += [copy(a, 0, me, sibling, own=True) for a in range(K)]
        for cp in mine + first:
            cp.start()
        passed = []
        for j, chip in enumerate(chips):
            for a in range(K):
                copy(a, 1 + j, (*chip, ci), me).wait_recv()
                passed.append(copy(a, 4 + j, (*chip, ci), sibling))
                passed[-1].start()
        for a in range(K):
            copy(a, 0, sibling, me).wait_recv()
            for j, chip in enumerate(chips):
                copy(a, 4 + j, (*chip, 1 - ci), me).wait_recv()
        for cp in first + passed:
            cp.wait_send()
        for cp in mine:
            cp.wait()

    return pl.pallas_call(body, name=name, out_shape=[_sds((N_DEV,) + x.shape, x.dtype) for x in xs], in_specs=[ANY] * K,
                          out_specs=[ANY] * K,
                          scratch_shapes=[pltpu.SemaphoreType.DMA((7 * K,)), pltpu.SemaphoreType.DMA((7 * K,)),
                                          pltpu.SemaphoreType.DMA((K,))])(*xs)


def _swap_sibling(name, ts):
    K = len(ts)

    def body(*refs):
        t_refs, out_refs, send_sems, recv_sems = refs[:K], refs[K:2 * K], refs[2 * K], refs[2 * K + 1]
        xi, yi, ci = _place()
        cps = [pltpu.make_async_remote_copy(src_ref=t_refs[a], dst_ref=out_refs[a], send_sem=send_sems.at[a],
                                            recv_sem=recv_sems.at[a], device_id=(xi, yi, 1 - ci), device_id_type=MESH)
               for a in range(K)]
        for cp in cps:
            cp.start()
        for cp in cps:
            cp.wait()

    return pl.pallas_call(body, name=name, out_shape=[_sds(t.shape, t.dtype) for t in ts], in_specs=[ANY] * K,
                          out_specs=[ANY] * K,
                          scratch_shapes=[pltpu.SemaphoreType.DMA((K,)), pltpu.SemaphoreType.DMA((K,))])(*ts)


def _scatter_chips(name, ts):
    K = len(ts)

    def body(*refs):
        t_refs, out_refs, send_sems, recv_sems = refs[:K], refs[K:2 * K], refs[2 * K], refs[2 * K + 1]
        xi, yi, ci = _place()
        chips = [(1 - xi, yi), (xi, 1 - yi), (1 - xi, 1 - yi)]
        cps = [pltpu.make_async_remote_copy(src_ref=t_refs[a].at[2 * cx + cy], dst_ref=out_refs[a].at[j],
                                            send_sem=send_sems.at[3 * a + j], recv_sem=recv_sems.at[3 * a + j],
                                            device_id=(cx, cy, ci), device_id_type=MESH)
               for j, (cx, cy) in enumerate(chips) for a in range(K)]
        for cp in cps:
            cp.start()
        for cp in cps:
            cp.wait()

    return pl.pallas_call(body, name=name, out_shape=[_sds((3,) + t.shape[1:], t.dtype) for t in ts], in_specs=[ANY] * K,
                          out_specs=[ANY] * K,
                          scratch_shapes=[pltpu.SemaphoreType.DMA((3 * K,)), pltpu.SemaphoreType.DMA((3 * K,))])(*ts)


def _add_bf16(name, a, b):
    shape = a.shape
    N, C = shape[0], shape[-1]
    R = a.size // (N * C)
    tr = _tile(R, max(8, (1 << 19) // C))

    def body(a_ref, b_ref, o_ref):
        o_ref[...] = (a_ref[...].astype(F32) + b_ref[...].astype(F32)).astype(BF16)

    blk = pl.BlockSpec((None, tr, C), lambda n, i: (n, i, 0))
    return _call(body, name, (N, R // tr), [blk, blk], blk, _sds((N, R, C), BF16), sem=("parallel", "parallel"))(
        a.reshape(N, R, C), b.reshape(N, R, C)).reshape(shape)


def _sum_parts(name, parts):
    N, C = parts.shape

    def body(p_ref, o_ref):
        s = p_ref[pl.ds(0, 1), :]
        for d in range(1, N):
            s = s + p_ref[pl.ds(d, 1), :]
        o_ref[...] = s

    return _call(body, name, (1,), [pl.BlockSpec((N, C), lambda i: (0, 0))], pl.BlockSpec((1, C), lambda i: (0, 0)),
                 _sds((1, C), F32))(parts)


def _flat(parts, dtype, lead=()):
    flat = jnp.concatenate([t.reshape(lead + (-1,)).astype(dtype) for t in parts], axis=-1)
    n = flat.shape[-1]
    unit = 16 * FLAT_COLS
    padded = -(-n // unit) * unit
    flat = jnp.pad(flat, [(0, 0)] * len(lead) + [(0, padded - n)])
    return flat.reshape(lead + (padded // FLAT_COLS, FLAT_COLS))


def _unflat(flat, shapes, lead=()):
    flat = flat.reshape(lead + (-1,))
    out, off = [], 0
    for shp in shapes:
        n = math.prod(shp)
        out.append(flat[..., off:off + n].reshape(lead + tuple(shp)))
        off += n
    return out


def _dev_major(full, ax):
    shp = full.shape
    return jnp.moveaxis(full.reshape(shp[:ax] + (N_DEV, shp[ax] // N_DEV) + shp[ax + 1:]), ax, 0)


def _from_dev_major(blocks, ax):
    t = jnp.moveaxis(blocks, 0, ax)
    shp = t.shape
    return t.reshape(shp[:ax] + (shp[ax] * shp[ax + 1],) + shp[ax + 2:])


def kernel(x, *rest):
    nw = len(WEIGHTS)
    w = dict(zip(WEIGHTS, rest[:nw]))
    tgt = rest[nw]
    m = dict(zip(WEIGHTS, rest[nw + 1:2 * nw + 1]))
    v = dict(zip(WEIGHTS, rest[2 * nw + 1:3 * nw + 1]))
    xi, yi, ci = _place()
    dev = 4 * xi + 2 * yi + ci

    ffn3 = LATE[:3]
    early = [n for n in BIG if n not in LATE]
    shard = {n: (w[n][:EARLY_LAYERS] if n in ffn3 else w[n]).astype(BF16) for n in ffn3 + tuple(early)}
    late_shard = {n: (w[n][EARLY_LAYERS:] if n in ffn3 else w[n]).astype(BF16) for n in LATE}
    big = _all_gather("gather_matrices", [shard[n] for n in ffn3 + tuple(early)])
    small = _all_gather("gather_vectors", [_flat([w[n] for n in SMALL], F32)])[0]
    full = {n: w[n] for n in REPL}
    for n, blocks in zip(ffn3 + tuple(early), big):
        full[n] = _from_dev_major(blocks, SHARDED[n])
    for n, blocks in zip(SMALL, _unflat(small, [w[n].shape for n in SMALL], (N_DEV,))):
        full[n] = _from_dev_major(blocks, SHARDED[n])

    def by_owner(names, grads):
        return [_dev_major(t.reshape(t.shape[:1] + w[n].shape[1:SHARDED[n]] + (-1,) + w[n].shape[SHARDED[n] + 1:]),
                           SHARDED[n]).astype(BF16) for n, t in zip(names, grads)]

    plan = dict(late_shards=[late_shard[n] for n in LATE], dev_major=by_owner,
                assemble=lambda got: {n: _from_dev_major(t, SHARDED[n]) for n, t in zip(LATE, got)})
    loss, gx, g, travelled = _local_step(x, tgt, full, plan)

    vec_names = SMALL + REPL
    vec = _all_gather("gather_vector_grads", [_flat([g[n] for n in vec_names] + [loss], F32)])[0]
    *vec_list, losses = _unflat(vec, [g[n].shape for n in vec_names] + [(1, 1)], (N_DEV,))
    vec_parts = dict(zip(vec_names, vec_list))
    loss = _sum_parts("loss_sum", losses.reshape(N_DEV, 1))[0, 0]

    last = ffn3 + MIXER_MATRICES[0]
    keep, send = [], []
    for gd in by_owner(last, [g[n] for n in last]):
        gd = gd.reshape((4, 2) + gd.shape[1:])
        keep.append(lax.dynamic_index_in_dim(gd, ci, 1, keepdims=False))
        send.append(lax.dynamic_index_in_dim(gd, 1 - ci, 1, keepdims=False))
    got_sib = _swap_sibling("grads_to_sibling", send)
    pair = [_add_bf16(f"grad_pair_sum_{n}", a, b) for n, a, b in zip(last, keep, got_sib)]
    got_chips = _scatter_chips("grads_to_chips", pair)
    chip = 2 * xi + yi
    parts = {}
    for k, n in enumerate(last):
        parts[n] = [[lax.dynamic_index_in_dim(keep[k], chip, 0, keepdims=False),
                     lax.dynamic_index_in_dim(got_sib[k], chip, 0, keepdims=False), got_chips[k][0], got_chips[k][1], got_chips[k][2]]]
    for names, sent, got in reversed(travelled):
        for n, s, t in zip(names, sent, got):
            parts.setdefault(n, []).append([lax.dynamic_index_in_dim(s, dev, 0, keepdims=False)] + [t[k] for k in range(len(FLIPS))])

    res = {}
    for n in BIG:
        runs = parts[n]
        count = max(len(r) for r in runs)
        whole = [jnp.concatenate([r[k] if k < len(r) else jnp.zeros_like(r[0]) for r in runs], axis=0) if len(runs) > 1 else runs[0][k]
                 for k in range(count)]
        res[n] = _adamw(f"adamw_{n}", w[n], m[n], v[n], whole)
    for n in vec_names:
        parts = vec_parts[n]
        if n in SHARDED:
            ax = SHARDED[n]
            parts = lax.dynamic_slice_in_dim(parts, dev * w[n].shape[ax], w[n].shape[ax], ax + 1)
        res[n] = _adamw(f"adamw_{n}", w[n], m[n], v[n], [parts[d] for d in range(N_DEV)])
    return (loss, gx, *[res[n][0] for n in WEIGHTS], *[res[n][1] for n in WEIGHTS], *[res[n][2] for n in WEIGHTS],
            *[res[n][3] for n in WEIGHTS])
```

```python
import functools
import math

import jax
import jax.numpy as jnp
from jax import lax
from jax.experimental import pallas as pl
from jax.experimental.pallas import tpu as pltpu

F32 = jnp.float32
BF16 = jnp.bfloat16
MESH = pl.DeviceIdType.MESH

N_DEV = 8
RMS_EPS = 1e-6
LN_EPS = 1e-5
FOX_HEADS = 16
HGRN_EXPAND = 128
HGRN_CHUNK = 32
POOL_WINDOWS = (2, 4, 8, 16)
ADAM_LR, ADAM_B1, ADAM_B2, ADAM_EPS, ADAM_WD, ADAM_STEP = 0.001, 0.9, 0.999, 1e-08, 0.01, 10
LANES = 128
VMEM_LIMIT_MB = 48

SHARDED = dict(
    ffn_norm=2, ffn_w_gate=3, ffn_w_up=3, ffn_w_down=2, conv_w_in=2, conv_dw=2, conv_w_out=1, fox_w_in=2, fox_w_out=1,
    hgrn_w_in=2, hgrn_norm=1, hgrn_w_out=1, pool_w=2, pool_scale=1)
BIG = ("ffn_w_gate", "ffn_w_up", "ffn_w_down", "conv_w_in", "conv_w_out", "fox_w_in", "fox_w_out", "hgrn_w_in",
       "hgrn_w_out", "pool_w")
SMALL = ("ffn_norm", "conv_dw", "hgrn_norm", "pool_scale")
REPL = ("mix_norm", "final_norm", "conv_b_in", "conv_dw_b", "conv_ln_g", "conv_ln_b", "fox_b_f", "hgrn_lb_logits")
WEIGHTS = ("ffn_norm", "ffn_w_gate", "ffn_w_up", "ffn_w_down", "mix_norm", "final_norm", "conv_w_in", "conv_b_in",
           "conv_dw", "conv_dw_b", "conv_ln_g", "conv_ln_b", "conv_w_out", "fox_w_in", "fox_b_f", "fox_w_out",
           "hgrn_w_in", "hgrn_lb_logits", "hgrn_norm", "hgrn_w_out", "pool_w", "pool_scale")


def _tile(n, pref, mult=8):
    if n <= pref:
        return n
    for t in range(pref - pref % mult, 0, -mult):
        if n % t == 0:
            return t
    return n


def _sig(x):
    return 1.0 / (1.0 + jnp.exp(-x))


def _call(body, name, grid, in_specs, out_specs, out_shape, scratch=(), sem=None):
    params = dict(vmem_limit_bytes=VMEM_LIMIT_MB << 20)
    if sem is not None:
        params["dimension_semantics"] = sem
    return pl.pallas_call(body, name=name, grid=grid, in_specs=in_specs, out_specs=out_specs, out_shape=out_shape,
                          scratch_shapes=list(scratch), compiler_params=pltpu.CompilerParams(**params))


FLIPS = [(fx, fy, fc) for fx in (0, 1) for fy in (0, 1) for fc in (0, 1)][1:]


def _carry_call(body, name, grid, in_specs, out_specs, out_shape, scratch, kind, arrays):
    K = len(arrays)
    n_in, n_out, n_scr = len(in_specs), len(out_shape), len(scratch)
    n_peer = len(FLIPS)
    if kind == "gather":
        landed = [_sds((N_DEV,) + a.shape, a.dtype) for a in arrays]
    else:
        landed = [_sds((n_peer,) + a.shape[1:], a.dtype) for a in arrays]

    def wrapped(*refs):
        ins, sent = refs[:n_in], refs[n_in:n_in + K]
        outs, got = refs[n_in + K:n_in + K + n_out], refs[n_in + K + n_out:n_in + 2 * K + n_out]
        scr = refs[n_in + 2 * K + n_out:n_in + 2 * K + n_out + n_scr]
        send_sems, recv_sems, local_sems = refs[-3:]
        ids = [pl.program_id(d) for d in range(len(grid))]
        first = functools.reduce(jnp.logical_and, [i == 0 for i in ids])
        last = functools.reduce(jnp.logical_and, [i == n - 1 for i, n in zip(ids, grid)])
        xi, yi, ci = _place()
        me = 4 * xi + 2 * yi + ci

        def copies():
            cps = []
            for a in range(K):
                for k, (fx, fy, fc) in enumerate(FLIPS):
                    px, py, pc = (xi + fx) % 2, (yi + fy) % 2, (ci + fc) % 2
                    if kind == "gather":
                        src, dst = sent[a], got[a].at[me]
                    else:
                        src, dst = sent[a].at[4 * px + 2 * py + pc], got[a].at[k]
                    cps.append(pltpu.make_async_remote_copy(src_ref=src, dst_ref=dst, send_sem=send_sems.at[n_peer * a + k],
                                                            recv_sem=recv_sems.at[n_peer * a + k], device_id=(px, py, pc),
                                                            device_id_type=MESH))
            return cps

        def own():
            return [pltpu.make_async_copy(sent[a], got[a].at[me], local_sems.at[a]) for a in range(K)] if kind == "gather" else []

        @pl.when(first)
        def _():
            for cp in copies() + own():
                cp.start()

        body(*ins, *outs, *scr)

        @pl.when(last)
        def _():
            for cp in copies() + own():
                cp.wait()

    params = dict(vmem_limit_bytes=VMEM_LIMIT_MB << 20, dimension_semantics=("arbitrary",) * len(grid))
    res = pl.pallas_call(wrapped, name=name, grid=grid, in_specs=list(in_specs) + [ANY] * K,
                         out_specs=list(out_specs) + [ANY] * K, out_shape=list(out_shape) + landed,
                         scratch_shapes=list(scratch) + [pltpu.SemaphoreType.DMA((n_peer * K,)), pltpu.SemaphoreType.DMA((n_peer * K,)),
                                                         pltpu.SemaphoreType.DMA((K,))],
                         compiler_params=pltpu.CompilerParams(**params))
    return res


def _sds(shape, dtype):
    return jax.ShapeDtypeStruct(tuple(shape), dtype)


_DN = {"nn": (((1,), (0,)), ((), ())), "nt": (((1,), (1,)), ((), ())), "tn": (((0,), (0,)), ((), ()))}


def _dot(a, b, mode="nn"):
    return lax.dot_general(a.astype(BF16), b.astype(BF16), _DN[mode], preferred_element_type=F32)


def _roll(x, shift, axis=0):
    n = x.shape[axis]
    shift = shift % n
    return x if shift == 0 else pltpu.roll(x, shift, axis)


def _scan_rows(x, reverse=False):
    n = x.shape[0]
    row = lax.broadcasted_iota(jnp.int32, x.shape, 0)
    sh = 1
    while sh < n:
        if reverse:
            x = x + jnp.where(row < n - sh, _roll(x, n - sh), 0.0)
        else:
            x = x + jnp.where(row >= sh, _roll(x, sh), 0.0)
        sh *= 2
    return x


def _mm(name, mode, pairs, pair_specs, grid, k_axis, out_shapes, out_specs, acc_shape=None, extras=(), extra_specs=(),
        epilogue=None, alpha=1.0, sem=None):
    npair, nex, nout = len(pairs), len(extras), len(out_shapes)
    nk = grid[k_axis] if k_axis is not None else 1

    def body(*refs):
        prs = refs[:2 * npair]
        ex = refs[2 * npair:2 * npair + nex]
        outs = refs[2 * npair + nex:2 * npair + nex + nout]

        def partial():
            p = None
            for i in range(npair):
                d = _dot(prs[2 * i][...], prs[2 * i + 1][...], mode)
                p = d if p is None else p + d
            return p

        def finish(res):
            if alpha != 1.0:
                res = res * alpha
            if epilogue is None:
                outs[0][...] = res.astype(outs[0].dtype)
            else:
                epilogue(res, ex, outs)

        if k_axis is None:
            finish(partial())
        else:
            acc = refs[-1]
            k = pl.program_id(k_axis)

            @pl.when(k == 0)
            def _():
                acc[...] = partial()

            @pl.when(k > 0)
            def _():
                acc[...] += partial()

            @pl.when(k == nk - 1)
            def _():
                finish(acc[...])

    if sem is None:
        sem = tuple("arbitrary" if i == k_axis else "parallel" for i in range(len(grid)))
    scratch = [pltpu.VMEM(acc_shape, F32)] if k_axis is not None else []
    flat = [t for p in pairs for t in p]
    flat_specs = [s for p in pair_specs for s in p]
    return _call(body, name, grid, flat_specs + list(extra_specs), out_specs, out_shapes, scratch, sem)(*flat, *extras)


def _mm_nn(name, a, b, out_dtype, tm=1024, tn=1024):
    M, K = a.shape
    N = b.shape[1]
    tm, tn = _tile(M, tm), _tile(N, tn, LANES)
    return _mm(name, "nn", [(a, b)], [(pl.BlockSpec((tm, K), lambda i, j: (i, 0)), pl.BlockSpec((K, tn), lambda i, j: (0, j)))],
               (M // tm, N // tn), None, [_sds((M, N), out_dtype)], [pl.BlockSpec((tm, tn), lambda i, j: (i, j))])[0]


def _mm_res(name, u, w, x, alpha, tm=512):
    T, K = u.shape
    D = w.shape[1]
    tm = _tile(T, tm)

    def epi(res, ex, outs):
        outs[0][...] = ex[0][...] + res

    return _mm(name, "nn", [(u, w)], [(pl.BlockSpec((tm, K), lambda i: (i, 0)), pl.BlockSpec((K, D), lambda i: (0, 0)))],
               (T // tm,), None, [_sds((T, D), F32)], [pl.BlockSpec((tm, D), lambda i: (i, 0))],
               extras=[x], extra_specs=[pl.BlockSpec((tm, D), lambda i: (i, 0))], epilogue=epi, alpha=alpha)[0]


def _mm_nt(name, pairs, out_dtype, tm=1024, tn=1024, alpha=1.0):
    M = pairs[0][0].shape[0]
    N = pairs[0][1].shape[0]
    tm, tn = _tile(M, tm), _tile(N, tn, LANES)
    specs = [(pl.BlockSpec((tm, a.shape[1]), lambda i, j: (i, 0)), pl.BlockSpec((tn, b.shape[1]), lambda i, j: (j, 0)))
             for a, b in pairs]
    return _mm(name, "nt", pairs, specs, (M // tm, N // tn), None, [_sds((M, N), out_dtype)],
               [pl.BlockSpec((tm, tn), lambda i, j: (i, j))], alpha=alpha)[0]


def _mm_nt3(name, a3, b3, tm=1024, tn=1024):
    S, M, K = a3.shape
    N = b3.shape[1]
    tm, tn = _tile(M, tm), _tile(N, tn, LANES)
    return _mm(name, "nt", [(a3, b3)],
               [(pl.BlockSpec((None, tm, K), lambda i, j, s: (s, i, 0)), pl.BlockSpec((None, tn, K), lambda i, j, s: (s, j, 0)))],
               (M // tm, N // tn, S), 2, [_sds((M, N), F32)], [pl.BlockSpec((tm, tn), lambda i, j, s: (i, j))],
               acc_shape=(tm, tn))[0]


def _mm_tn(name, a, b, alpha=1.0, tm=1408, tn=1408, tk=512):
    T, M = a.shape
    N = b.shape[1]
    tm, tn, tk = _tile(M, tm, LANES), _tile(N, tn, LANES), _tile(T, tk)
    return _mm(name, "tn", [(a, b)],
               [(pl.BlockSpec((tk, tm), lambda i, j, k: (k, i)), pl.BlockSpec((tk, tn), lambda i, j, k: (k, j)))],
               (M // tm, N // tn, T // tk), 2, [_sds((M, N), F32)], [pl.BlockSpec((tm, tn), lambda i, j, k: (i, j))],
               acc_shape=(tm, tn), alpha=alpha)[0]


def _mm_tn3(name, a, b3, tm=1408, tn=1408, tk=512):
    T, M = a.shape
    S, _, N = b3.shape
    tm, tn, tk = _tile(M, tm, LANES), _tile(N, tn, LANES), _tile(T, tk)
    return _mm(name, "tn", [(a, b3)],
               [(pl.BlockSpec((tk, tm), lambda s, i, j, k: (k, i)), pl.BlockSpec((None, tk, tn), lambda s, i, j, k: (s, k, j)))],
               (S, M // tm, N // tn, T // tk), 3, [_sds((S, M, N), F32)],
               [pl.BlockSpec((None, tm, tn), lambda s, i, j, k: (s, i, j))], acc_shape=(tm, tn))[0]


def _rms_fwd(name, x, g):
    T, D = x.shape
    tt = _tile(T, 512)

    def body(x_ref, g_ref, h_ref):
        xv = x_ref[...]
        r = lax.rsqrt(jnp.mean(xv * xv, axis=-1, keepdims=True) + RMS_EPS)
        h_ref[...] = (xv * r * g_ref[...]).astype(h_ref.dtype)

    row = pl.BlockSpec((tt, D), lambda i: (i, 0))
    return _call(body, name, (T // tt,), [row, pl.BlockSpec((1, D), lambda i: (0, 0))], row, _sds((T, D), BF16))(x, g)


def _rms_bwd(name, dh, x, g, dx_in):
    T, D = x.shape
    tt = _tile(T, 512)

    def body(dh_ref, x_ref, g_ref, dxi_ref, dx_ref, dxb_ref, dg_ref):
        xv = x_ref[...]
        r = lax.rsqrt(jnp.mean(xv * xv, axis=-1, keepdims=True) + RMS_EPS)
        xh = xv * r
        dhv = dh_ref[...]
        dxh = dhv * g_ref[...]
        dx = dxi_ref[...] + r * (dxh - xh * jnp.mean(dxh * xh, axis=-1, keepdims=True))
        dx_ref[...] = dx
        dxb_ref[...] = dx.astype(BF16)
        part = jnp.sum(dhv * xh, axis=0, keepdims=True)

        @pl.when(pl.program_id(0) == 0)
        def _():
            dg_ref[...] = part

        @pl.when(pl.program_id(0) > 0)
        def _():
            dg_ref[...] += part

    row = pl.BlockSpec((tt, D), lambda i: (i, 0))
    vec = pl.BlockSpec((1, D), lambda i: (0, 0))
    dx, dxb, dg = _call(body, name, (T // tt,), [row, row, vec, row], [row, row, vec],
                        [_sds((T, D), F32), _sds((T, D), BF16), _sds((1, D), F32)], sem=("arbitrary",))(dh, x, g, dx_in)
    return (dx, dxb), dg


def _loss_head(x, g, tgt):
    T, D = x.shape
    tt = _tile(T, 512)

    def body(x_ref, g_ref, t_ref, loss_ref, dx_ref, dxb_ref, dg_ref):
        xv = x_ref[...]
        r = lax.rsqrt(jnp.mean(xv * xv, axis=-1, keepdims=True) + RMS_EPS)
        xh = xv * r
        e = xh * g_ref[...] - t_ref[...]
        lp = 0.5 * jnp.sum(jnp.sum(e * e, axis=-1, keepdims=True), axis=0, keepdims=True) / D
        dy = e / D
        dxh = dy * g_ref[...]
        dx = r * (dxh - xh * jnp.mean(dxh * xh, axis=-1, keepdims=True))
        dx_ref[...] = dx
        dxb_ref[...] = dx.astype(BF16)
        part = jnp.sum(dy * xh, axis=0, keepdims=True)

        @pl.when(pl.program_id(0) == 0)
        def _():
            dg_ref[...] = part
            loss_ref[...] = lp

        @pl.when(pl.program_id(0) > 0)
        def _():
            dg_ref[...] += part
            loss_ref[...] += lp

    row = pl.BlockSpec((tt, D), lambda i: (i, 0))
    vec = pl.BlockSpec((1, D), lambda i: (0, 0))
    one = pl.BlockSpec((1, 1), lambda i: (0, 0))
    loss, dx, dxb, dg = _call(body, "loss_head", (T // tt,), [row, vec, row], [one, row, row, vec],
                              [_sds((1, 1), F32), _sds((T, D), F32), _sds((T, D), BF16), _sds((1, D), F32)],
                              sem=("arbitrary",))(x, g, tgt)
    return loss, (dx, dxb), dg


def _colsum3(name, a3):
    S, T, N = a3.shape
    tt = _tile(T, 512)

    def body(a_ref, o_ref):
        part = jnp.sum(a_ref[...].astype(F32), axis=0, keepdims=True)

        @pl.when(pl.program_id(1) == 0)
        def _():
            o_ref[...] = part

        @pl.when(pl.program_id(1) > 0)
        def _():
            o_ref[...] += part

    return _call(body, name, (S, T // tt), [pl.BlockSpec((None, tt, N), lambda s, i: (s, i, 0))],
                 pl.BlockSpec((None, 1, N), lambda s, i: (s, 0, 0)), _sds((S, 1, N), F32), sem=("parallel", "arbitrary"))(a3)


def _glu_mm(name, h, w2, bias2, mode, u_dtype, tm=1024, tn=256, carried=None):
    T, K = h.shape
    N = w2.shape[2]
    tm, tn = _tile(T, tm), _tile(N, tn, LANES)
    has_bias = bias2 is not None

    halves = 2 if tm % 32 == 0 else 1

    def body(*refs):
        h_ref, w_ref = refs[0], refs[1]
        ab_ref, u_ref = refs[-2], refs[-1]
        for r in range(halves):
            rows = pl.ds(r * (tm // halves), tm // halves)
            hv = h_ref[rows, :]
            a = _dot(hv, w_ref[0])
            b = _dot(hv, w_ref[1])
            if has_bias:
                a = a + refs[2][0]
                b = b + refs[2][1]
            u = a * _sig(a) * b if mode == "swiglu" else a * _sig(b)
            ab_ref[0, rows, :] = a.astype(BF16)
            ab_ref[1, rows, :] = b.astype(BF16)
            u_ref[rows, :] = u.astype(u_ref.dtype)

    in_specs = [pl.BlockSpec((tm, K), lambda i, j: (i, 0)), pl.BlockSpec((2, K, tn), lambda i, j: (0, 0, j))]
    args = [h, w2]
    if has_bias:
        in_specs.append(pl.BlockSpec((2, 1, tn), lambda i, j: (0, 0, j)))
        args.append(bias2)
    specs = ((T // tm, N // tn), in_specs,
             [pl.BlockSpec((2, tm, tn), lambda i, j: (0, i, j)), pl.BlockSpec((tm, tn), lambda i, j: (i, j))],
             [_sds((2, T, N), BF16), _sds((T, N), u_dtype)], [])
    if carried is None:
        return _call(body, name, *specs, sem=("parallel", "parallel"))(*args)
    ab, u, *got = _carry_call(body, name, *specs, "gather", carried)(*args, *carried)
    return ab, u, got


def _swiglu_bwd_mm(name, dxb, wd, ab, alpha, tm=1024, tn=256):
    T, D = dxb.shape
    N = wd.shape[0]
    tm, tn = _tile(T, tm), _tile(N, tn, LANES)
    halves = 4 if tm % 64 == 0 else 1

    def body(dx_ref, wd_ref, ab_ref, dab_ref, u_ref):
        wdv = wd_ref[...]
        for r in range(halves):
            rows = pl.ds(r * (tm // halves), tm // halves)
            du = _dot(dx_ref[rows, :], wdv, "nt") * alpha
            a = ab_ref[0, rows, :].astype(F32)
            b = ab_ref[1, rows, :].astype(F32)
            sg = _sig(a)
            sa = a * sg
            dab_ref[0, rows, :] = (du * b * (sg * (1.0 + a * (1.0 - sg)))).astype(BF16)
            dab_ref[1, rows, :] = (du * sa).astype(BF16)
            u_ref[rows, :] = (sa * b).astype(BF16)

    ab_spec = pl.BlockSpec((2, tm, tn), lambda i, j: (0, i, j))
    return _call(body, name, (T // tm, N // tn),
                 [pl.BlockSpec((tm, D), lambda i, j: (i, 0)), pl.BlockSpec((tn, D), lambda i, j: (j, 0)), ab_spec],
                 [ab_spec, pl.BlockSpec((tm, tn), lambda i, j: (i, j))], [_sds((2, T, N), BF16), _sds((T, N), BF16)],
                 sem=("parallel", "parallel"))(dxb, wd, ab)


def _ffn_fwd(tag, x, g, wgu, wd, carried=None):
    h = _rms_fwd(f"ffn_rms_{tag}", x, g)
    if carried is None:
        ab, u = _glu_mm(f"ffn_gu_{tag}", h, wgu, None, "swiglu", BF16)
        return _mm_res(f"ffn_down_{tag}", u, wd, x, 0.5), (x, h, ab)
    ab, u, got = _glu_mm(f"ffn_gu_{tag}", h, wgu, None, "swiglu", BF16, carried=carried)
    return _mm_res(f"ffn_down_{tag}", u, wd, x, 0.5), (x, h, ab), got


def _ffn_bwd(tag, dxp, saved, g, wgu, wd):
    x, h, ab = saved
    dx, dxb = dxp
    dab, u = _swiglu_bwd_mm(f"ffn_dgu_{tag}", dxb, wd, ab, 0.5)
    dwd = _mm_tn(f"ffn_dwd_{tag}", u, dxb, alpha=0.5)
    dwgu = _mm_tn3(f"ffn_dwgu_{tag}", h, dab)
    dh = _mm_nt3(f"ffn_dh_{tag}", dab, wgu)
    dx2, dg = _rms_bwd(f"ffn_drms_{tag}", dh, x, g, dx)
    return dx2, dg, dwgu, dwd


HALO = 32


def _conv_fwd(u3, dw32, dwb, lng, lnb):
    B, S, D = u3.shape
    W = dw32.shape[0]
    taps = CONV_TAPS
    tt = _tile(S, 256, HALO)
    hb = tt // HALO

    def body(u_ref, halo_ref, dw_ref, dwb_ref, g_ref, b_ref, v_ref, s_ref):
        i = pl.program_id(1)
        halo = jnp.where(i > 0, halo_ref[...], 0.0)
        ext = jnp.concatenate([halo, u_ref[...]], axis=0)
        acc = jnp.zeros((tt, D), F32) + dwb_ref[...]
        for j in range(taps):
            acc = acc + dw_ref[pl.ds(j, 1), :] * _roll(ext, taps - 1 - j)[HALO:]
        v_ref[...] = acc
        mu = jnp.mean(acc, axis=-1, keepdims=True)
        xc = acc - mu
        ln = xc * lax.rsqrt(jnp.mean(xc * xc, axis=-1, keepdims=True) + LN_EPS) * g_ref[...] + b_ref[...]
        s_ref[...] = (ln * _sig(ln)).astype(BF16)

    main = pl.BlockSpec((None, tt, D), lambda b, i: (b, i, 0))
    halo = pl.BlockSpec((None, HALO, D), lambda b, i: (b, jnp.maximum(i * hb - 1, 0), 0))
    vec = pl.BlockSpec((1, D), lambda b, i: (0, 0))
    return _call(body, "conv_fwd", (B, S // tt), [main, halo, pl.BlockSpec((W, D), lambda b, i: (0, 0)), vec, vec, vec],
                 [main, main], [_sds((B, S, D), F32), _sds((B, S, D), BF16)], sem=("parallel", "parallel"))(
        u3, u3, dw32, dwb, lng, lnb)


def _conv_bwd_ln(v, ds, lng, lnb):
    T, D = v.shape
    tt = _tile(T, 256)

    def body(v_ref, ds_ref, g_ref, b_ref, dv_ref, red_ref):
        vv = v_ref[...]
        mu = jnp.mean(vv, axis=-1, keepdims=True)
        xc = vv - mu
        rstd = lax.rsqrt(jnp.mean(xc * xc, axis=-1, keepdims=True) + LN_EPS)
        xh = xc * rstd
        ln = xh * g_ref[...] + b_ref[...]
        sg = _sig(ln)
        dln = ds_ref[...] * (sg * (1.0 + ln * (1.0 - sg)))
        dxh = dln * g_ref[...]
        dv = rstd * (dxh - jnp.mean(dxh, axis=-1, keepdims=True) - xh * jnp.mean(dxh * xh, axis=-1, keepdims=True))
        dv_ref[...] = dv
        parts = (jnp.sum(dln * xh, axis=0, keepdims=True), jnp.sum(dln, axis=0, keepdims=True),
                 jnp.sum(dv, axis=0, keepdims=True))

        @pl.when(pl.program_id(0) == 0)
        def _():
            for k in range(3):
                red_ref[k] = parts[k]

        @pl.when(pl.program_id(0) > 0)
        def _():
            for k in range(3):
                red_ref[k] += parts[k]

    row = pl.BlockSpec((tt, D), lambda i: (i, 0))
    vec = pl.BlockSpec((1, D), lambda i: (0, 0))
    return _call(body, "conv_bwd_ln", (T // tt,), [row, row, vec, vec], [row, pl.BlockSpec((3, 1, D), lambda i: (0, 0, 0))],
                 [_sds((T, D), F32), _sds((3, 1, D), F32)], sem=("arbitrary",))(v, ds, lng, lnb)


def _conv_bwd_dw(dv3, u3, ab, dw32, carried=None):
    B, S, D = u3.shape
    W = dw32.shape[0]
    taps = CONV_TAPS
    tt = _tile(S, 256, HALO)
    hb = tt // HALO
    nt = S // tt
    L = tt + HALO

    def body(dv_ref, dvn_ref, u_ref, up_ref, ab_ref, dw_ref, dab_ref, ddw_ref):
        b, i = pl.program_id(0), pl.program_id(1)
        dv = dv_ref[...]
        ext_dv = jnp.concatenate([dv, jnp.where(i < nt - 1, dvn_ref[...], 0.0)], axis=0)
        ext_u = jnp.concatenate([jnp.where(i > 0, up_ref[...], 0.0), u_ref[...]], axis=0)
        du = jnp.zeros((tt, D), F32)
        first = jnp.logical_and(b == 0, i == 0)

        @pl.when(first)
        def _():
            ddw_ref[...] = jnp.zeros((W, D), F32)

        for j in range(taps):
            sh = taps - 1 - j
            du = du + dw_ref[pl.ds(j, 1), :] * _roll(ext_dv, L - sh)[:tt]
            ddw_ref[pl.ds(j, 1), :] += jnp.sum(dv * _roll(ext_u, sh)[HALO:], axis=0, keepdims=True)
        a = ab_ref[0].astype(F32)
        sb = _sig(ab_ref[1].astype(F32))
        dab_ref[0] = (du * sb).astype(BF16)
        dab_ref[1] = (du * a * sb * (1.0 - sb)).astype(BF16)

    main = pl.BlockSpec((None, tt, D), lambda b, i: (b, i, 0))
    prev = pl.BlockSpec((None, HALO, D), lambda b, i: (b, jnp.maximum(i * hb - 1, 0), 0))
    nxt = pl.BlockSpec((None, HALO, D), lambda b, i: (b, jnp.minimum((i + 1) * hb, S // HALO - 1), 0))
    abs_ = pl.BlockSpec((2, tt, D), lambda b, i: (0, b * nt + i, 0))
    wsp = pl.BlockSpec((W, D), lambda b, i: (0, 0))
    specs = ((B, nt), [main, nxt, main, prev, abs_, wsp], [abs_, wsp], [_sds((2, B * S, D), BF16), _sds((W, D), F32)], [])
    args = (dv3, dv3, u3, u3, ab, dw32)
    if carried is None:
        return _call(body, "conv_bwd_dw", *specs, sem=("arbitrary", "arbitrary"))(*args)
    dab, ddw, *got = _carry_call(body, "conv_bwd_dw", *specs, "scatter", carried)(*args, *carried)
    return dab, ddw, got


CONV_TAPS = 31


def _conv_mixer_fwd(x, B, p):
    T, D = x.shape
    h = _rms_fwd("conv_rms", x, p["mix_norm"][0:1])
    ab, u = _glu_mm("conv_in", h, p["conv_w_in2"], p["conv_b_in2"], "glu", F32)
    v, s = _conv_fwd(u.reshape(B, T // B, D), p["conv_dw32"], p["conv_dw_b"], p["conv_ln_g"], p["conv_ln_b"])
    v, s = v.reshape(T, D), s.reshape(T, D)
    return _mm_res("conv_out", s, p["conv_w_out"], x, 1.0), (x, h, ab, u, v, s)


def _conv_mixer_bwd(dxp, saved, B, p):
    x, h, ab, u, v, s = saved
    T, D = x.shape
    dx, dxb = dxp
    g = {}
    ds = _mm_nt("conv_ds", [(dxb, p["conv_w_out"])], F32)
    g["conv_w_out"] = _mm_tn("conv_dwout", s, dxb)
    dv, red = _conv_bwd_ln(v, ds, p["conv_ln_g"], p["conv_ln_b"])
    g["conv_ln_g"], g["conv_ln_b"], g["conv_dw_b"] = red[0], red[1], red[2]
    if "scatter" in p:
        dab, ddw, p["scattered"] = _conv_bwd_dw(dv.reshape(B, T // B, D), u.reshape(B, T // B, D), ab, p["conv_dw32"],
                                               p.pop("scatter"))
    else:
        dab, ddw = _conv_bwd_dw(dv.reshape(B, T // B, D), u.reshape(B, T // B, D), ab, p["conv_dw32"])
    g["conv_dw"] = ddw[:CONV_TAPS][None]
    g["conv_b_in"] = _colsum3("conv_dbin", dab).reshape(1, 2 * D)
    dwin = _mm_tn3("conv_dwin", h, dab)
    g["conv_w_in"] = jnp.moveaxis(dwin, 0, 1).reshape(1, D, 2 * D)
    dh = _mm_nt3("conv_dh", dab, p["conv_w_in2"])
    dx2, dg = _rms_bwd("conv_drms", dh, x, p["mix_norm"][0:1], dx)
    return dx2, dg, g


def _log_sigmoid(z):
    return jnp.minimum(z, 0.0) - jnp.log(1.0 + jnp.exp(-jnp.abs(z)))


def _fox_gate_fwd(fl3, bf):
    B, S, N = fl3.shape
    tt = _tile(S, 512)

    def body(fl_ref, bf_ref, c_ref, carry):
        @pl.when(pl.program_id(1) == 0)
        def _():
            carry[...] = jnp.zeros((1, N), F32)

        c = _scan_rows(_log_sigmoid(fl_ref[...] + bf_ref[...])) + carry[...]
        c_ref[...] = c
        carry[...] = c_ref[pl.ds(tt - 1, 1), :]

    row = pl.BlockSpec((None, tt, N), lambda b, i: (b, i, 0))
    return _call(body, "fox_gate_fwd", (B, S // tt), [row, pl.BlockSpec((1, N), lambda b, i: (0, 0))], row,
                 _sds((B, S, N), F32), [pltpu.VMEM((1, N), F32)], sem=("parallel", "arbitrary"))(fl3, bf)


def _fox_gate_bwd(dc3, fl3, bf):
    B, S, N = fl3.shape
    tt = _tile(S, 512)
    nt = S // tt

    def body(dc_ref, fl_ref, bf_ref, dfl_ref, dbf_ref, carry):
        b, i = pl.program_id(0), pl.program_id(1)

        @pl.when(i == 0)
        def _():
            carry[...] = jnp.zeros((1, N), F32)

        dc = dc_ref[0] - dc_ref[1]
        dlf = _scan_rows(dc, reverse=True) + carry[...]
        dfl = dlf * _sig(-(fl_ref[...] + bf_ref[...]))
        dfl_ref[...] = dfl
        carry[...] += jnp.sum(dc, axis=0, keepdims=True)
        part = jnp.sum(dfl, axis=0, keepdims=True)

        @pl.when(jnp.logical_and(b == 0, i == 0))
        def _():
            dbf_ref[...] = part

        @pl.when(jnp.logical_or(b > 0, i > 0))
        def _():
            dbf_ref[...] += part

    row = pl.BlockSpec((None, tt, N), lambda b, i: (b, nt - 1 - i, 0))
    vec = pl.BlockSpec((1, N), lambda b, i: (0, 0))
    row2 = pl.BlockSpec((2, None, tt, N), lambda b, i: (0, b, nt - 1 - i, 0))
    return _call(body, "fox_gate_bwd", (B, nt), [row2, row, vec], [row, vec], [_sds((B, S, N), F32), _sds((1, N), F32)],
                 [pltpu.VMEM((1, N), F32)], sem=("arbitrary", "arbitrary"))(dc3, fl3, bf)


NEG = -1e30


def _fox_attn_fwd(qa, ka, vat, crow, ckb, dh, scale, carried=None):
    B, H, S, P = qa.shape
    tq = _tile(S, 512, LANES)
    tk = tq
    nl = tq // LANES

    def body(q_ref, k_ref, vt_ref, cr_ref, ck_ref, o_ref, lse_ref, s_scr, p_scr, m_scr, al_scr, acc_scr):
        i = pl.program_id(2)
        qv = q_ref[...]
        m_scr[...] = jnp.full((1, tq), NEG, F32)
        acc_scr[...] = jnp.zeros((P, tq), F32)
        key = lax.broadcasted_iota(jnp.int32, (tk, LANES), 0)
        qry = lax.broadcasted_iota(jnp.int32, (tk, LANES), 1)

        def kv_step(j, diagonal):
            off = pl.multiple_of(j * tk, tk)
            s_scr[...] = _dot(k_ref[pl.ds(off, tk), :], qv, "nt")
            ck = ck_ref[pl.ds(off, tk), :]
            for lt in range(nl):
                ls = slice(lt * LANES, (lt + 1) * LANES)
                s = s_scr[:, ls] * scale + cr_ref[:, ls] - ck
                if diagonal:
                    s = jnp.where(key <= qry + lt * LANES, s, -jnp.inf)
                m1 = m_scr[:, ls]
                m2 = jnp.maximum(m1, jnp.max(s, axis=0, keepdims=True))
                p_scr[:, ls] = jnp.exp(s - m2).astype(BF16)
                al_scr[:, ls] = jnp.exp(m1 - m2)
                m_scr[:, ls] = m2
            acc_scr[...] = al_scr[...] * acc_scr[...] + _dot(vt_ref[:, pl.ds(off, tk)], p_scr[...])

        def before(j, carry):
            kv_step(j, False)
            return carry

        lax.fori_loop(0, i, before, 0)
        kv_step(i, True)
        l = acc_scr[pl.ds(dh, 1), :]
        o_ref[...] = (acc_scr[...] / l).T
        lse_ref[...] = m_scr[...] + jnp.log(l)

    qs = pl.BlockSpec((None, None, tq, P), lambda b, h, i: (b, h, i, 0))
    fullk = pl.BlockSpec((None, None, S, P), lambda b, h, i: (b, h, 0, 0))
    fullt = pl.BlockSpec((None, None, P, S), lambda b, h, i: (b, h, 0, 0))
    fullc = pl.BlockSpec((None, None, S, LANES), lambda b, h, i: (b, h, 0, 0))
    rowt = pl.BlockSpec((None, None, 1, tq), lambda b, h, i: (b, h, 0, i))
    scratch = [pltpu.VMEM((tk, tq), F32), pltpu.VMEM((tk, tq), BF16), pltpu.VMEM((1, tq), F32), pltpu.VMEM((1, tq), F32),
               pltpu.VMEM((P, tq), F32)]
    specs = ((B, H, S // tq), [qs, fullk, fullt, rowt, fullc], [qs, rowt], [_sds((B, H, S, P), F32), _sds((B, H, 1, S), F32)],
             scratch)
    if carried is None:
        return _call(body, "fox_attn_fwd", *specs, sem=("parallel", "parallel", "parallel"))(qa, ka, vat, crow, ckb)
    o, lse, *got = _carry_call(body, "fox_attn_fwd", *specs, "gather", carried)(qa, ka, vat, crow, ckb, *carried)
    return o, lse, got


def _fox_rowstats(do, o):
    B, H, S, P = o.shape
    tq = _tile(S, 4096)

    def body(do_ref, o_ref, dl_ref):
        dl_ref[...] = jnp.sum(do_ref[...] * o_ref[...], axis=-1, keepdims=True)

    qs = pl.BlockSpec((None, None, tq, P), lambda b, h, i: (b, h, i, 0))
    col = pl.BlockSpec((None, None, tq, 1), lambda b, h, i: (b, h, i, 0))
    return _call(body, "fox_rowstats", (B, H, S // tq), [qs, qs], col, _sds((B, H, S, 1), F32), sem=("parallel",) * 3)(do, o)


def _fox_attn_bwd(qa, ka, kat, va, do, crow, lse, delta, ckb, dh, scale, carried=None):
    B, H, S, P = qa.shape
    tk = _tile(S, 512, LANES)
    tq = tk
    nq = S // tq
    nl = tq // LANES

    def body(q_ref, k_ref, kt_ref, v_ref, do_ref, cr_ref, lse_ref, dl_ref, ck_ref, dqt_ref, dk_ref, dv_ref, rs_ref,
             s_scr, dp_scr, p_scr, ds_scr, dk_acc, dv_acc):
        j = pl.program_id(2)

        @pl.when(j == 0)
        def _():
            dqt_ref[...] = jnp.zeros((P, S), F32)

        kj, vj, ck = k_ref[...], v_ref[...], ck_ref[...]
        dk_acc[...] = jnp.zeros((tk, P), F32)
        dv_acc[...] = jnp.zeros((tk, P), F32)
        key = lax.broadcasted_iota(jnp.int32, (tk, LANES), 0)
        qry = lax.broadcasted_iota(jnp.int32, (tk, LANES), 1)

        def q_step(i, diagonal):
            off = pl.multiple_of(i * tq, tq)
            qi, doi = q_ref[pl.ds(off, tq), :], do_ref[pl.ds(off, tq), :].astype(BF16)
            s_scr[...] = _dot(kj, qi, "nt")
            dp_scr[...] = _dot(vj, doi, "nt")
            for lt in range(nl):
                ls = slice(lt * LANES, (lt + 1) * LANES)
                gl = pl.ds(pl.multiple_of(off + lt * LANES, LANES), LANES)
                p = jnp.exp(s_scr[:, ls] * scale + (cr_ref[:, gl] - lse_ref[:, gl]) - ck)
                if diagonal:
                    p = jnp.where(key <= qry + lt * LANES, p, 0.0)
                p_scr[:, ls] = p.astype(BF16)
                ds_scr[:, ls] = (p * (dp_scr[:, ls] - dl_ref[:, gl])).astype(BF16)
            ds = ds_scr[...]
            dqt_ref[:, pl.ds(off, tq)] += _dot(kt_ref[...], ds)
            dk_acc[...] += _dot(ds, qi)
            dv_acc[...] += _dot(p_scr[...], doi)

        def after(i, carry):
            q_step(i, False)
            return carry

        q_step(j, True)
        lax.fori_loop(j + 1, nq, after, 0)
        dk_ref[...] = dk_acc[...] * jnp.where(lax.broadcasted_iota(jnp.int32, (tk, P), 1) < dh, scale, 1.0)
        dv_ref[...] = dv_acc[...]

        @pl.when(j == S // tk - 1)
        def _():
            rs_ref[...] = dqt_ref[pl.ds(dh, 1), :]
            dqt_ref[...] = dqt_ref[...] * jnp.where(lax.broadcasted_iota(jnp.int32, (P, S), 0) < dh, scale, 1.0)

    ks = pl.BlockSpec((None, None, tk, P), lambda b, h, j: (b, h, j, 0))
    kts = pl.BlockSpec((None, None, P, tk), lambda b, h, j: (b, h, 0, j))
    cks = pl.BlockSpec((None, None, tk, LANES), lambda b, h, j: (b, h, j, 0))
    full = pl.BlockSpec((None, None, S, P), lambda b, h, j: (b, h, 0, 0))
    fullt = pl.BlockSpec((None, None, P, S), lambda b, h, j: (b, h, 0, 0))
    rowf = pl.BlockSpec((None, None, 1, S), lambda b, h, j: (b, h, 0, 0))
    scratch = [pltpu.VMEM((tk, tq), F32), pltpu.VMEM((tk, tq), F32), pltpu.VMEM((tk, tq), BF16), pltpu.VMEM((tk, tq), BF16),
               pltpu.VMEM((tk, P), F32), pltpu.VMEM((tk, P), F32)]
    specs = ((B, H, S // tk), [full, ks, kts, ks, full, rowf, rowf, rowf, cks], [fullt, ks, ks, rowf],
             [_sds((B, H, P, S), F32), _sds((B, H, S, P), F32), _sds((B, H, S, P), F32), _sds((B, H, 1, S), F32)], scratch)
    args = (qa, ka, kat, va, do, crow, lse, delta, ckb)
    if carried is None:
        return _call(body, "fox_attn_bwd", *specs, sem=("parallel", "parallel", "arbitrary"))(*args)
    dqt, dk, dv, rs, *got = _carry_call(body, "fox_attn_bwd", *specs, "scatter", carried)(*args, *carried)
    return dqt, dk, dv, rs, got


def _heads(t, B, H):
    T, D = t.shape
    return t.reshape(B, T // B, H, D // H).transpose(0, 2, 1, 3)


def _unheads(t):
    B, H, S, dh = t.shape
    return t.transpose(0, 2, 1, 3).reshape(B * S, H * dh)


def _fox_mixer_fwd(x, B, p):
    T, D = x.shape
    H = FOX_HEADS
    S = T // B
    scale = (D // H) ** -0.5
    h = _rms_fwd("fox_rms", x, p["mix_norm"][1:2])
    qkv = _mm_nn("fox_qkv", h, p["fox_w_qkv"], BF16)
    fl = _mm_nn("fox_fl", h, p["fox_w_f"], F32)
    c = _fox_gate_fwd(fl.reshape(B, S, LANES), p["fox_b_f128"])
    ch = c[:, :, :H].transpose(0, 2, 1)
    crow = ch[:, :, None, :]
    ckb = jnp.broadcast_to(ch[..., None], (B, H, S, LANES))
    q, k, v = _heads(qkv[:, :D], B, H), _heads(qkv[:, D:2 * D], B, H), _heads(qkv[:, 2 * D:], B, H)
    dh = D // H
    P = -(-(dh + 2) // LANES) * LANES
    one, zero = jnp.ones((B, H, S, 1), q.dtype), jnp.zeros((B, H, S, 1), q.dtype)
    rest = jnp.zeros((B, H, S, P - dh - 2), q.dtype)
    qa = jnp.concatenate([q, zero, one, rest], axis=-1)
    ka = jnp.concatenate([k, one, zero, rest], axis=-1)
    va = jnp.concatenate([v, one, zero, rest], axis=-1)
    if "late_gather" in p:
        o, lse, p["late_gathered"] = _fox_attn_fwd(qa, ka, va.transpose(0, 1, 3, 2), crow, ckb, dh, scale, p.pop("late_gather"))
    else:
        o, lse = _fox_attn_fwd(qa, ka, va.transpose(0, 1, 3, 2), crow, ckb, dh, scale)
    of = _unheads(o[..., :dh])
    return _mm_res("fox_out", of, p["fox_w_out"], x, 1.0), (x, h, fl, qa, ka, va, crow, ckb, o, lse, of)


def _fox_mixer_bwd(dxp, saved, B, p):
    x, h, fl, qa, ka, va, crow, ckb, o, lse, of = saved
    T, D = x.shape
    H = FOX_HEADS
    S = T // B
    dh_ = D // H
    P = qa.shape[-1]
    scale = dh_ ** -0.5
    dx, dxb = dxp
    g = {}
    do = _heads(_mm_nt("fox_do", [(dxb, p["fox_w_out"])], F32), B, H)
    do = jnp.pad(do, ((0, 0), (0, 0), (0, 0), (0, P - dh_)))
    g["fox_w_out"] = _mm_tn("fox_dwout", of, dxb)[None]
    delta = _fox_rowstats(do, o).reshape(B, H, 1, S)
    if "scatter" in p:
        dqt, dk, dv, rowsum, p["scattered"] = _fox_attn_bwd(qa, ka, ka.transpose(0, 1, 3, 2), va, do, crow, lse, delta, ckb,
                                                          dh_, scale, p.pop("scatter"))
    else:
        dqt, dk, dv, rowsum = _fox_attn_bwd(qa, ka, ka.transpose(0, 1, 3, 2), va, do, crow, lse, delta, ckb, dh_, scale)
    dq = dqt[:, :, :dh_, :].transpose(0, 3, 1, 2).reshape(T, D)
    dqkv = jnp.concatenate([dq, _unheads(dk[..., :dh_]), _unheads(dv[..., :dh_])], axis=1).astype(BF16)
    dc = jnp.stack([rowsum[:, :, 0, :], dk[..., dh_ + 1]])
    dc = jnp.pad(dc.transpose(0, 1, 3, 2), ((0, 0), (0, 0), (0, 0), (0, LANES - H)))
    dfl, dbf = _fox_gate_bwd(dc, fl.reshape(B, S, LANES), p["fox_b_f128"])
    dfl = dfl.reshape(T, LANES)
    g["fox_b_f"] = dbf[:, :H]
    dwqkv = _mm_tn("fox_dwqkv", h, dqkv)
    dwf = _mm_tn("fox_dwf", h, dfl)
    g["fox_w_in"] = jnp.concatenate([dwqkv, dwf[:, :H]], axis=1)[None]
    dh = _mm_nt("fox_dh", [(dqkv, p["fox_w_qkv"]), (dfl, p["fox_w_f"])], F32)
    dx2, dg = _rms_bwd("fox_drms", dh, x, p["mix_norm"][1:2], dx)
    return dx2, dg, g


def _lb_fwd(logits):
    L, D = logits.shape

    def body(l_ref, lb_ref):
        z = l_ref[...]
        e = jnp.exp(z - jnp.max(z, axis=0, keepdims=True))
        p = e / jnp.sum(e, axis=0, keepdims=True)
        lb_ref[...] = jnp.sum(jnp.where(_lb_rows(z.shape), p, 0.0), axis=0, keepdims=True)

    return _call(body, "hgrn_lb", (1,), [pl.BlockSpec((L, D), lambda i: (0, 0))], pl.BlockSpec((1, D), lambda i: (0, 0)),
                 _sds((1, D), F32))(logits)


def _lb_rows(shape):
    r = lax.broadcasted_iota(jnp.int32, shape, 0)
    return jnp.logical_and(r >= 1, r <= HGRN_LAYER)


HGRN_LAYER = 2


def _lb_bwd(logits, dlb):
    L, D = logits.shape

    def body(l_ref, d_ref, o_ref):
        z = l_ref[...]
        e = jnp.exp(z - jnp.max(z, axis=0, keepdims=True))
        p = e / jnp.sum(e, axis=0, keepdims=True)
        dp = jnp.where(_lb_rows(z.shape), d_ref[...], 0.0)
        o_ref[...] = p * (dp - jnp.sum(p * dp, axis=0, keepdims=True))

    full = pl.BlockSpec((L, D), lambda i: (0, 0))
    return _call(body, "hgrn_dlb", (1,), [full, pl.BlockSpec((1, D), lambda i: (0, 0))], full, _sds((L, D), F32))(logits, dlb)


def _hgrn_gates(qr, fr, lbv):
    e = jnp.exp(-jnp.abs(fr))
    big, small = 1.0 / (1.0 + e), e / (1.0 + e)
    sf = jnp.where(fr >= 0, big, small)
    snf = jnp.where(fr >= 0, small, big)
    f = lbv + (1.0 - lbv) * sf
    sq = _sig(qr)
    return qr * sq, (1.0 - lbv) * snf, jnp.log(f), sf, snf, f, sq


def _hgrn_intra(G, q, kk, g_scr, q_scr):
    C = HGRN_CHUNK
    g_scr[...] = G
    q_scr[...] = q
    srow = lax.broadcasted_iota(jnp.int32, (C, LANES), 0)
    lane = lax.broadcasted_iota(jnp.int32, (C, LANES), 1)
    at = jnp.zeros((C, LANES), F32)
    for t in range(C):
        e = jnp.where(srow <= t, jnp.exp(g_scr[pl.ds(t, 1), :] - G), 0.0)
        col = jnp.sum(e * kk * q_scr[pl.ds(t, 1), :], axis=-1, keepdims=True)
        at = jnp.where(lane == t, col, at)
    return at


def _hgrn_fwd(proj3, lb, ng):
    B, S, D4 = proj3.shape
    D = D4 // 4
    H = D // HGRN_EXPAND
    C = HGRN_CHUNK
    R = _tile(S, 256, 2 * C)
    ncb = R // C
    dk = HGRN_EXPAND

    def body(q_ref, f_ref, i_ref, go_ref, lb_ref, ng_ref, y_ref, o_ref, st_ref, st, g_scr, q_scr):
        @pl.when(pl.program_id(2) == 0)
        def _():
            st[...] = jnp.zeros((dk, dk), F32)

        lbv = lb_ref[...]

        def chunk(c, slot):
            r0 = pl.multiple_of(c * C, C)
            rows = pl.ds(r0, C)
            q, kk, lf, *_ = _hgrn_gates(q_ref[rows, :], f_ref[rows, :], lbv)
            vv = i_ref[rows, :]
            G = _scan_rows(lf)
            at = _hgrn_intra(G, q, kk, g_scr.at[slot], q_scr.at[slot])
            gl = g_scr[slot, pl.ds(C - 1, 1), :]
            stv = st[...]
            st_ref[c] = stv
            o = _dot(q * jnp.exp(G), stv, "nt") + _dot(at, vv, "tn")[:C]
            st[...] = stv * jnp.exp(gl) + _dot(vv, kk * jnp.exp(gl - G), "tn")
            o_ref[rows, :] = o
            gv = go_ref[rows, :]
            y = o * lax.rsqrt(jnp.mean(o * o, axis=-1, keepdims=True) + RMS_EPS) * ng_ref[...] * (gv * _sig(gv))
            y_ref[rows, :] = y.astype(BF16)

        def pair(c2, carry):
            chunk(2 * c2, 0)
            chunk(2 * c2 + 1, 1)
            return carry

        lax.fori_loop(0, ncb // 2, pair, 0)

    def col(k):
        return pl.BlockSpec((None, R, dk), lambda b, h, i: (b, i, h + k * H))

    vec = pl.BlockSpec((1, dk), lambda b, h, i: (0, h))
    out = pl.BlockSpec((None, R, dk), lambda b, h, i: (b, i, h))
    return _call(body, "hgrn_fwd", (B, H, S // R), [col(0), col(1), col(2), col(3), vec, vec],
                 [out, out, pl.BlockSpec((None, None, ncb, dk, dk), lambda b, h, i: (b, h, i, 0, 0))],
                 [_sds((B, S, D), BF16), _sds((B, S, D), F32), _sds((B, H, S // C, dk, dk), F32)],
                 [pltpu.VMEM((dk, dk), F32), pltpu.VMEM((2, C, dk), F32), pltpu.VMEM((2, C, dk), F32)],
                 sem=("parallel", "parallel", "arbitrary"))(proj3, proj3, proj3, proj3, lb, ng)


def _hgrn_bwd(proj3, o3, dy3, states, lb, ng, carried=None):
    B, S, D4 = proj3.shape
    D = D4 // 4
    H = D // HGRN_EXPAND
    C = HGRN_CHUNK
    R = _tile(S, 256, 2 * C)
    ncb = R // C
    nb = S // R
    dk = HGRN_EXPAND

    def body(q_ref, f_ref, i_ref, go_ref, o_ref, dy_ref, st_ref, lb_ref, ng_ref,
             dp_ref, red_ref, dst, g_scr, q_scr, dq_scr, acc):
        b, i = pl.program_id(1), pl.program_id(2)

        @pl.when(i == 0)
        def _():
            dst[...] = jnp.zeros((dk, dk), F32)

        @pl.when(jnp.logical_and(b == 0, i == 0))
        def _():
            acc[...] = jnp.zeros((2, dk), F32)

        lbv = lb_ref[...]
        ngv = ng_ref[...]
        srow = lax.broadcasted_iota(jnp.int32, (C, LANES), 0)
        lane = lax.broadcasted_iota(jnp.int32, (C, LANES), 1)

        def chunk(c, slot):
            r0 = pl.multiple_of(c * C, C)
            rows = pl.ds(r0, C)
            qr, fr, vv, gv = q_ref[rows, :], f_ref[rows, :], i_ref[rows, :], go_ref[rows, :]
            q, kk, lf, sf, snf, f, sq = _hgrn_gates(qr, fr, lbv)
            o = o_ref[rows, :]
            dy = dy_ref[rows, :]
            rinv = lax.rsqrt(jnp.mean(o * o, axis=-1, keepdims=True) + RMS_EPS)
            on = o * rinv
            sgv = _sig(gv)
            dz = dy * (gv * sgv)
            dp_ref[3, rows, :] = (dy * on * ngv * (sgv * (1.0 + gv * (1.0 - sgv)))).astype(BF16)
            acc[pl.ds(1, 1), :] += jnp.sum(dz * on, axis=0, keepdims=True)
            don = dz * ngv
            do = rinv * (don - on * jnp.mean(don * on, axis=-1, keepdims=True))
            G = _scan_rows(lf)
            g_scr[slot] = G
            q_scr[slot] = q
            gl = g_scr[slot, pl.ds(C - 1, 1), :]
            egl = jnp.exp(gl)
            eG = jnp.exp(G)
            eK = jnp.exp(gl - G)
            qg, kg = q * eG, kk * eK
            stv = st_ref[c]
            dsv = dst[...]
            dqg = _dot(do, stv)
            do_pad = jnp.concatenate([do, jnp.zeros((LANES - C, dk), F32)], axis=0)
            dat = _dot(vv, do_pad, "nt")
            dkg = _dot(vv, dsv)
            dgl = egl * jnp.sum(stv * dsv, axis=0, keepdims=True) + jnp.sum(dkg * kg, axis=0, keepdims=True)
            at = jnp.zeros((C, LANES), F32)
            dki = jnp.zeros((C, dk), F32)
            for t in range(C):
                e = jnp.where(srow <= t, jnp.exp(g_scr[slot, pl.ds(t, 1), :] - G), 0.0)
                qt = q_scr[slot, pl.ds(t, 1), :]
                at = jnp.where(lane == t, jnp.sum(e * kk * qt, axis=-1, keepdims=True), at)
                z = e * jnp.sum(jnp.where(lane == t, dat, 0.0), axis=-1, keepdims=True)
                dq_scr[slot, pl.ds(t, 1), :] = jnp.sum(z * kk, axis=0, keepdims=True)
                dki = dki + z * qt
            dqi = dq_scr[slot]
            dp_ref[2, rows, :] = (_dot(at, do_pad) + _dot(kg, dsv, "nt")).astype(BF16)
            dst[...] = dsv * egl + _dot(do, qg, "tn")
            dq = dqg * eG + dqi
            dkk = dkg * eK + dki
            dG = dqg * qg - dkg * kg + q * dqi - kk * dki
            dG = dG + jnp.where(srow == C - 1, dgl, 0.0)
            dlf = _scan_rows(dG, reverse=True)
            dsf = (1.0 - lbv) * sf * snf
            dp_ref[1, rows, :] = (dlf * dsf / f - dkk * dsf).astype(BF16)
            acc[pl.ds(0, 1), :] += jnp.sum(dlf * snf / f - dkk * snf, axis=0, keepdims=True)
            dp_ref[0, rows, :] = (dq * (sq * (1.0 + qr * (1.0 - sq)))).astype(BF16)

        def pair(c2, carry):
            chunk(ncb - 1 - 2 * c2, 0)
            chunk(ncb - 2 - 2 * c2, 1)
            return carry

        lax.fori_loop(0, ncb // 2, pair, 0)

        @pl.when(jnp.logical_and(b == B - 1, i == nb - 1))
        def _():
            red_ref[0] = acc[pl.ds(0, 1), :]
            red_ref[1] = acc[pl.ds(1, 1), :]

    def col(k):
        return pl.BlockSpec((None, R, dk), lambda h, b, i: (b, nb - 1 - i, h + k * H))

    vec = pl.BlockSpec((1, dk), lambda h, b, i: (0, h))
    row = pl.BlockSpec((None, R, dk), lambda h, b, i: (b, nb - 1 - i, h))
    stsp = pl.BlockSpec((None, None, ncb, dk, dk), lambda h, b, i: (b, h, nb - 1 - i, 0, 0))
    specs = ((H, B, nb), [col(0), col(1), col(2), col(3), row, row, stsp, vec, vec],
             [pl.BlockSpec((4, None, R, dk), lambda h, b, i: (0, b, nb - 1 - i, h)),
              pl.BlockSpec((2, 1, dk), lambda h, b, i: (0, 0, h))],
             [_sds((4, B, S, D), BF16), _sds((2, 1, D), F32)],
             [pltpu.VMEM((dk, dk), F32), pltpu.VMEM((2, C, dk), F32), pltpu.VMEM((2, C, dk), F32),
              pltpu.VMEM((2, C, dk), F32), pltpu.VMEM((2, dk), F32)])
    args = (proj3, proj3, proj3, proj3, o3, dy3, states, lb, ng)
    if carried is None:
        return _call(body, "hgrn_bwd", *specs, sem=("parallel", "arbitrary", "arbitrary"))(*args)
    dp, red, *got = _carry_call(body, "hgrn_bwd", *specs, "scatter", carried)(*args, *carried)
    return dp, red, got


def _hgrn_mixer_fwd(x, B, p):
    T, D = x.shape
    S = T // B
    h = _rms_fwd("hgrn_rms", x, p["mix_norm"][2:3])
    proj = _mm_nn("hgrn_in", h, p["hgrn_w_in"], F32)
    lb = _lb_fwd(p["hgrn_lb_logits"])
    y, o, states = _hgrn_fwd(proj.reshape(B, S, 4 * D), lb, p["hgrn_norm"])
    y = y.reshape(T, D)
    return _mm_res("hgrn_out", y, p["hgrn_w_out"], x, 1.0), (x, h, proj, lb, y, o, states)


def _hgrn_mixer_bwd(dxp, saved, B, p):
    x, h, proj, lb, y, o, states = saved
    T, D = x.shape
    S = T // B
    dx, dxb = dxp
    g = {}
    dy = _mm_nt("hgrn_dy", [(dxb, p["hgrn_w_out"])], F32)
    g["hgrn_w_out"] = _mm_tn("hgrn_dwout", y, dxb)[None]
    if "scatter" in p:
        dp, red, p["scattered"] = _hgrn_bwd(proj.reshape(B, S, 4 * D), o, dy.reshape(B, S, D), states, lb, p["hgrn_norm"],
                                           p.pop("scatter"))
    else:
        dp, red = _hgrn_bwd(proj.reshape(B, S, 4 * D), o, dy.reshape(B, S, D), states, lb, p["hgrn_norm"])
    dp = dp.reshape(4, T, D)
    g["hgrn_norm"] = red[1]
    g["hgrn_lb_logits"] = _lb_bwd(p["hgrn_lb_logits"], red[0])
    dwin = _mm_tn3("hgrn_dwin", h, dp)
    g["hgrn_w_in"] = jnp.moveaxis(dwin, 0, 1).reshape(1, D, 4 * D)
    dh = _mm_nt3("hgrn_dh", dp, p["hgrn_w_in4"])
    dx2, dg = _rms_bwd("hgrn_drms", dh, x, p["mix_norm"][2:3], dx)
    return dx2, dg, g


POOL_HALO = 16


def _pool_fwd(x3, g):
    B, S, D = x3.shape
    tt = _tile(S, 256, POOL_HALO)
    hb = tt // POOL_HALO
    G = D // len(POOL_WINDOWS)

    def body(x_ref, halo_ref, g_ref, m_ref):
        i = pl.program_id(1)

        def norm(xv):
            return xv * lax.rsqrt(jnp.mean(xv * xv, axis=-1, keepdims=True) + RMS_EPS) * g_ref[...]

        hm = norm(x_ref[...])
        ext = jnp.concatenate([jnp.where(i > 0, norm(halo_ref[...]), 0.0), hm], axis=0)
        pos = (i * tt + lax.broadcasted_iota(jnp.int32, (tt, 1), 0) + 1).astype(F32)
        for gi, win in enumerate(POOL_WINDOWS):
            s = ext[:, gi * G:(gi + 1) * G]
            w = 1
            while w < win:
                s = s + _roll(s, w)
                w *= 2
            m_ref[:, gi * G:(gi + 1) * G] = (s[POOL_HALO:] / jnp.minimum(pos, float(win)) - hm[:, gi * G:(gi + 1) * G]).astype(BF16)

    main = pl.BlockSpec((None, tt, D), lambda b, i: (b, i, 0))
    halo = pl.BlockSpec((None, POOL_HALO, D), lambda b, i: (b, jnp.maximum(i * hb - 1, 0), 0))
    return _call(body, "pool_fwd", (B, S // tt), [main, halo, pl.BlockSpec((1, D), lambda b, i: (0, 0))], main,
                 _sds((B, S, D), BF16), sem=("parallel", "parallel"))(x3, x3, g)


def _pool_bwd(dm3):
    B, S, D = dm3.shape
    tt = _tile(S, 256, POOL_HALO)
    hb = tt // POOL_HALO
    nt = S // tt
    G = D // len(POOL_WINDOWS)
    L = tt + POOL_HALO

    def body(dm_ref, nxt_ref, dh_ref):
        i = pl.program_id(1)
        posm = (i * tt + lax.broadcasted_iota(jnp.int32, (tt, 1), 0) + 1).astype(F32)
        posn = ((i + 1) * tt + lax.broadcasted_iota(jnp.int32, (POOL_HALO, 1), 0) + 1).astype(F32)
        for gi, win in enumerate(POOL_WINDOWS):
            sl = slice(gi * G, (gi + 1) * G)
            dm = dm_ref[:, sl]
            s = jnp.concatenate([dm / jnp.minimum(posm, float(win)),
                                 jnp.where(i < nt - 1, nxt_ref[:, sl] / jnp.minimum(posn, float(win)), 0.0)], axis=0)
            w = 1
            while w < win:
                s = s + _roll(s, L - w)
                w *= 2
            dh_ref[:, sl] = s[:tt] - dm

    main = pl.BlockSpec((None, tt, D), lambda b, i: (b, i, 0))
    nxt = pl.BlockSpec((None, POOL_HALO, D), lambda b, i: (b, jnp.minimum((i + 1) * hb, S // POOL_HALO - 1), 0))
    return _call(body, "pool_bwd", (B, nt), [main, nxt], main, _sds((B, S, D), F32), sem=("parallel", "parallel"))(dm3, dm3)


def _pool_mixer_fwd(x, B, p):
    T, D = x.shape
    NG = len(POOL_WINDOWS)
    G = D // NG
    tm = _tile(T, 512)
    m = _pool_fwd(x.reshape(B, T // B, D), p["mix_norm"][3:4]).reshape(T, D)

    def epi(res, ex, outs):
        outs[0][...] = ex[1][...] + res * ex[0][...]

    blk = pl.BlockSpec((tm, G), lambda i, g: (i, g))
    x2 = _mm("pool_out", "nn", [(m, p["pool_w4"])], [(blk, pl.BlockSpec((None, G, G), lambda i, g: (g, 0, 0)))],
             (T // tm, NG), None, [_sds((T, D), F32)], [blk], extras=[p["pool_scale"], x],
             extra_specs=[pl.BlockSpec((1, G), lambda i, g: (0, g)), blk], epilogue=epi)[0]
    return x2, (x, m)


def _pool_mixer_bwd(dxp, saved, B, p):
    x, m = saved
    dx, dxb = dxp
    T, D = x.shape
    NG = len(POOL_WINDOWS)
    G = D // NG
    tm = _tile(T, 512)
    g = {}

    def epi(zz, ex, outs):
        dy = ex[0][...]
        outs[0][...] = (dy * ex[1][...]).astype(BF16)
        part = jnp.sum(dy * zz, axis=0, keepdims=True)

        @pl.when(pl.program_id(1) == 0)
        def _():
            outs[1][...] = part

        @pl.when(pl.program_id(1) > 0)
        def _():
            outs[1][...] += part

    blk = pl.BlockSpec((tm, G), lambda g_, i: (i, g_))
    wsp = pl.BlockSpec((None, G, G), lambda g_, i: (g_, 0, 0))
    vec = pl.BlockSpec((1, G), lambda g_, i: (0, g_))
    dz, dsc = _mm("pool_dz", "nn", [(m, p["pool_w4"])], [(blk, wsp)], (NG, T // tm), None,
                  [_sds((T, D), BF16), _sds((1, D), F32)], [blk, vec], extras=[dx, p["pool_scale"]], extra_specs=[blk, vec],
                  epilogue=epi, sem=("parallel", "arbitrary"))
    g["pool_scale"] = dsc
    tk = _tile(T, 512)
    kb = pl.BlockSpec((tk, G), lambda g_, k: (k, g_))
    g["pool_w"] = _mm("pool_dw", "tn", [(m, dz)], [(kb, kb)], (NG, T // tk), 1, [_sds((NG, G, G), F32)],
                      [pl.BlockSpec((None, G, G), lambda g_, k: (g_, 0, 0))], acc_shape=(G, G))[0][None]
    blk2 = pl.BlockSpec((tm, G), lambda i, g_: (i, g_))
    dm = _mm("pool_dm", "nt", [(dz, p["pool_w4"])], [(blk2, pl.BlockSpec((None, G, G), lambda i, g_: (g_, 0, 0)))],
             (T // tm, NG), None, [_sds((T, D), F32)], [blk2])[0]
    dh = _pool_bwd(dm.reshape(B, T // B, D)).reshape(T, D)
    dx2, dg = _rms_bwd("pool_drms", dh, x, p["mix_norm"][3:4], dx)
    return dx2, dg, g


_MIXERS = ((_conv_mixer_fwd, _conv_mixer_bwd), (_fox_mixer_fwd, _fox_mixer_bwd), (_hgrn_mixer_fwd, _hgrn_mixer_bwd),
           (_pool_mixer_fwd, _pool_mixer_bwd))


EARLY_LAYERS = 2
LATE = ("ffn_w_gate", "ffn_w_up", "ffn_w_down", "hgrn_w_in", "hgrn_w_out", "pool_w")
MIXER_MATRICES = (("conv_w_in", "conv_w_out"), ("fox_w_in", "fox_w_out"), ("hgrn_w_in", "hgrn_w_out"), ("pool_w",))


def _local_step(x3, tgt3, w, plan=None):
    B, S, D = x3.shape
    T = B * S
    depth = w["ffn_norm"].shape[0]
    H = FOX_HEADS
    p = dict(w)
    p["conv_w_in2"] = w["conv_w_in"][0].reshape(D, 2, D).transpose(1, 0, 2)
    p["conv_b_in2"] = w["conv_b_in"].reshape(2, 1, D)
    p["conv_dw32"] = jnp.pad(w["conv_dw"][0], ((0, HALO - CONV_TAPS), (0, 0)))
    p["conv_w_out"] = w["conv_w_out"][0]
    p["fox_b_f128"] = jnp.pad(w["fox_b_f"], ((0, 0), (0, LANES - H)))
    ffn = {}

    def install(first, d):
        if "ffn_w_gate" in d:
            for k in range(d["ffn_w_gate"].shape[0]):
                ffn[first + k] = (jnp.stack((d["ffn_w_gate"][k], d["ffn_w_up"][k]), axis=1), d["ffn_w_down"][k])
        if "fox_w_in" in d:
            p["fox_w_qkv"] = d["fox_w_in"][0][:, :3 * D]
            p["fox_w_f"] = jnp.pad(d["fox_w_in"][0][:, 3 * D:], ((0, 0), (0, LANES - H)))
            p["fox_w_out"] = d["fox_w_out"][0]
        if "hgrn_w_in" in d:
            p["hgrn_w_in"] = d["hgrn_w_in"][0]
            p["hgrn_w_in4"] = d["hgrn_w_in"][0].reshape(D, 4, D).transpose(1, 0, 2)
            p["hgrn_w_out"] = d["hgrn_w_out"][0]
            p["pool_w4"] = d["pool_w"][0]

    install(0, w)
    beside = {} if plan is None else plan["beside_ffn"]
    if plan is not None:
        p["late_gather"] = plan["late_shards"]

    def ffn_fwd(i, half, x):
        tag = f"{i}{'ab'[half]}"
        args = (tag, x, w["ffn_norm"][i, half:half + 1], ffn[i][0][half], ffn[i][1][half])
        if tag not in beside:
            return _ffn_fwd(*args)
        shards, first, assemble = beside[tag]
        x, s, got = _ffn_fwd(*args, carried=shards)
        install(first, assemble(got))
        return x, s

    x = x3.reshape(T, D)
    saved = []
    for i in range(depth):
        x, s0 = ffn_fwd(i, 0, x)
        x, s1 = _MIXERS[i % 4][0](x, B, p)
        if "late_gathered" in p:
            install(EARLY_LAYERS, plan["assemble"](p.pop("late_gathered")))
        x, s2 = ffn_fwd(i, 1, x)
        saved.append((s0, s1, s2))
    loss, dx, dfinal = _loss_head(x, w["final_norm"].reshape(1, D), tgt3.reshape(T, D))

    g = {"final_norm": dfinal}
    dffn_norm = [[None, None] for _ in range(depth)]
    dwgu = [[None, None] for _ in range(depth)]
    dwd = [[None, None] for _ in range(depth)]
    dmix = [None] * depth
    travelled = []

    def layer_grads(i, extra):
        gate, up = (jnp.stack([dwgu[i][0][s], dwgu[i][1][s]])[None] for s in (0, 1))
        return [gate, up, jnp.stack(dwd[i])[None]] + [g.pop(n) for n in extra]

    for i in reversed(range(depth)):
        s0, s1, s2 = saved[i]
        dx, dffn_norm[i][1], dwgu[i][1], dwd[i][1] = _ffn_bwd(f"{i}b", dx, s2, w["ffn_norm"][i, 1:2], ffn[i][0][1], ffn[i][1][1])
        if plan is not None and i < depth - 1:
            extra = MIXER_MATRICES[i + 1]
            names = LATE[:3] + extra
            p["scatter"] = plan["dev_major"](names, layer_grads(i + 1, extra))
            travelled.append([names, p["scatter"]])
        dx, dmix[i], gm = _MIXERS[i % 4][1](dx, s1, B, p)
        if "scattered" in p:
            travelled[-1].append(p.pop("scattered"))
        g.update(gm)
        dx, dffn_norm[i][0], dwgu[i][0], dwd[i][0] = _ffn_bwd(f"{i}a", dx, s0, w["ffn_norm"][i, 0:1], ffn[i][0][0], ffn[i][1][0])
    kept = range(depth) if plan is None else range(1)
    g["ffn_norm"] = jnp.stack([jnp.stack([a[0], b[0]]) for a, b in dffn_norm])
    g["ffn_w_gate"] = jnp.stack([jnp.stack([dwgu[i][0][0], dwgu[i][1][0]]) for i in kept])
    g["ffn_w_up"] = jnp.stack([jnp.stack([dwgu[i][0][1], dwgu[i][1][1]]) for i in kept])
    g["ffn_w_down"] = jnp.stack([jnp.stack(dwd[i]) for i in kept])
    g["mix_norm"] = jnp.concatenate(dmix, axis=0)
    g = {n: (t.reshape(w[n].shape) if n in w and n not in LATE else t) for n, t in g.items()}
    return loss, dx[0].reshape(B, S, D), g, travelled


def _adamw(name, w, m, v, parts):
    shape = w.shape
    cols = shape[-1]
    rows = w.size // cols
    tr = _tile(rows, max(8, (1 << 19) // cols))
    n = len(parts)
    c1 = 1.0 - ADAM_B1 ** ADAM_STEP
    c2 = 1.0 - ADAM_B2 ** ADAM_STEP

    def body(*refs):
        w_ref, m_ref, v_ref = refs[:3]
        g_ref, d_ref, m2_ref, v2_ref = refs[3 + n:]
        g = refs[3][...].astype(F32)
        for k in range(1, n):
            g = g + refs[3 + k][...].astype(F32)
        m2 = ADAM_B1 * m_ref[...] + (1.0 - ADAM_B1) * g
        v2 = ADAM_B2 * v_ref[...] + (1.0 - ADAM_B2) * (g * g)
        g_ref[...] = g
        m2_ref[...] = m2
        v2_ref[...] = v2
        d_ref[...] = -ADAM_LR * ((m2 / c1) / (jnp.sqrt(v2 / c2) + ADAM_EPS) + ADAM_WD * w_ref[...])

    blk = pl.BlockSpec((tr, cols), lambda i: (i, 0))
    outs = _call(body, name, (rows // tr,), [blk] * (3 + n), [blk] * 4, [_sds((rows, cols), F32)] * 4, sem=("parallel",))(
        *[t.reshape(rows, cols) for t in (w, m, v, *parts)])
    return [o.reshape(shape) for o in outs]


ANY = pl.BlockSpec(memory_space=pl.ANY)
FLAT_COLS = 1024


def _place():
    return lax.axis_index("x"), lax.axis_index("y"), lax.axis_index("c")


def _all_gather(name, xs):
    K = len(xs)

    def body(*refs):
        x_refs, out_refs = refs[:K], refs[K:2 * K]
        send_sems, recv_sems, local_sems = refs[2 * K:]
        xi, yi, ci = _place()
        me, sibling = (xi, yi, ci), (xi, yi, 1 - ci)
        chips = [(1 - xi, yi), (xi, 1 - yi), (1 - xi, 1 - yi)]

        def slot(a, px, py, pc):
            return out_refs[a].at[4 * px + 2 * py + pc]

        def copy(a, k, block, to, own=False):
            return pltpu.make_async_remote_copy(src_ref=x_refs[a] if own else slot(a, *block), dst_ref=slot(a, *block),
                                                send_sem=send_sems.at[7 * a + k], recv_sem=recv_sems.at[7 * a + k],
                                                device_id=to, device_id_type=MESH)

        mine = [pltpu.make_async_copy(x_refs[a], slot(a, *me), local_sems.at[a]) for a in range(K)]
        first = [copy(a, 1 + j, me, (*chip, ci), own=True) for j, chip in enumerate(chips) for a in range(K)]
        first += [copy(a, 0, me, sibling, own=True) for a in range(K)]
        for cp in mine + first:
            cp.start()
        passed = []
        for j, chip in enumerate(chips):
            for a in range(K):
                copy(a, 1 + j, (*chip, ci), me).wait_recv()
                passed.append(copy(a, 4 + j, (*chip, ci), sibling))
                passed[-1].start()
        for a in range(K):
            copy(a, 0, sibling, me).wait_recv()
            for j, chip in enumerate(chips):
                copy(a, 4 + j, (*chip, 1 - ci), me).wait_recv()
        for cp in first + passed:
            cp.wait_send()
        for cp in mine:
            cp.wait()

    return pl.pallas_call(body, name=name, out_shape=[_sds((N_DEV,) + x.shape, x.dtype) for x in xs], in_specs=[ANY] * K,
                          out_specs=[ANY] * K,
                          scratch_shapes=[pltpu.SemaphoreType.DMA((7 * K,)), pltpu.SemaphoreType.DMA((7 * K,)),
                                          pltpu.SemaphoreType.DMA((K,))])(*xs)


def _swap_sibling(name, ts):
    K = len(ts)

    def body(*refs):
        t_refs, out_refs, send_sems, recv_sems = refs[:K], refs[K:2 * K], refs[2 * K], refs[2 * K + 1]
        xi, yi, ci = _place()
        cps = [pltpu.make_async_remote_copy(src_ref=t_refs[a], dst_ref=out_refs[a], send_sem=send_sems.at[a],
                                            recv_sem=recv_sems.at[a], device_id=(xi, yi, 1 - ci), device_id_type=MESH)
               for a in range(K)]
        for cp in cps:
            cp.start()
        for cp in cps:
            cp.wait()

    return pl.pallas_call(body, name=name, out_shape=[_sds(t.shape, t.dtype) for t in ts], in_specs=[ANY] * K,
                          out_specs=[ANY] * K,
                          scratch_shapes=[pltpu.SemaphoreType.DMA((K,)), pltpu.SemaphoreType.DMA((K,))])(*ts)


def _scatter_chips(name, ts):
    K = len(ts)

    def body(*refs):
        t_refs, out_refs, send_sems, recv_sems = refs[:K], refs[K:2 * K], refs[2 * K], refs[2 * K + 1]
        xi, yi, ci = _place()
        chips = [(1 - xi, yi), (xi, 1 - yi), (1 - xi, 1 - yi)]
        cps = [pltpu.make_async_remote_copy(src_ref=t_refs[a].at[2 * cx + cy], dst_ref=out_refs[a].at[j],
                                            send_sem=send_sems.at[3 * a + j], recv_sem=recv_sems.at[3 * a + j],
                                            device_id=(cx, cy, ci), device_id_type=MESH)
               for j, (cx, cy) in enumerate(chips) for a in range(K)]
        for cp in cps:
            cp.start()
        for cp in cps:
            cp.wait()

    return pl.pallas_call(body, name=name, out_shape=[_sds((3,) + t.shape[1:], t.dtype) for t in ts], in_specs=[ANY] * K,
                          out_specs=[ANY] * K,
                          scratch_shapes=[pltpu.SemaphoreType.DMA((3 * K,)), pltpu.SemaphoreType.DMA((3 * K,))])(*ts)


def _add_bf16(name, a, b):
    shape = a.shape
    N, C = shape[0], shape[-1]
    R = a.size // (N * C)
    tr = _tile(R, max(8, (1 << 19) // C))

    def body(a_ref, b_ref, o_ref):
        o_ref[...] = (a_ref[...].astype(F32) + b_ref[...].astype(F32)).astype(BF16)

    blk = pl.BlockSpec((None, tr, C), lambda n, i: (n, i, 0))
    return _call(body, name, (N, R // tr), [blk, blk], blk, _sds((N, R, C), BF16), sem=("parallel", "parallel"))(
        a.reshape(N, R, C), b.reshape(N, R, C)).reshape(shape)


def _sum_parts(name, parts):
    N, C = parts.shape

    def body(p_ref, o_ref):
        s = p_ref[pl.ds(0, 1), :]
        for d in range(1, N):
            s = s + p_ref[pl.ds(d, 1), :]
        o_ref[...] = s

    return _call(body, name, (1,), [pl.BlockSpec((N, C), lambda i: (0, 0))], pl.BlockSpec((1, C), lambda i: (0, 0)),
                 _sds((1, C), F32))(parts)


def _flat(parts, dtype, lead=()):
    flat = jnp.concatenate([t.reshape(lead + (-1,)).astype(dtype) for t in parts], axis=-1)
    n = flat.shape[-1]
    unit = 16 * FLAT_COLS
    padded = -(-n // unit) * unit
    flat = jnp.pad(flat, [(0, 0)] * len(lead) + [(0, padded - n)])
    return flat.reshape(lead + (padded // FLAT_COLS, FLAT_COLS))


def _unflat(flat, shapes, lead=()):
    flat = flat.reshape(lead + (-1,))
    out, off = [], 0
    for shp in shapes:
        n = math.prod(shp)
        out.append(flat[..., off:off + n].reshape(lead + tuple(shp)))
        off += n
    return out


def _dev_major(full, ax):
    shp = full.shape
    return jnp.moveaxis(full.reshape(shp[:ax] + (N_DEV, shp[ax] // N_DEV) + shp[ax + 1:]), ax, 0)


def _from_dev_major(blocks, ax):
    t = jnp.moveaxis(blocks, 0, ax)
    shp = t.shape
    return t.reshape(shp[:ax] + (shp[ax] * shp[ax + 1],) + shp[ax + 2:])


def kernel(x, *rest):
    nw = len(WEIGHTS)
    w = dict(zip(WEIGHTS, rest[:nw]))
    tgt = rest[nw]
    m = dict(zip(WEIGHTS, rest[nw + 1:2 * nw + 1]))
    v = dict(zip(WEIGHTS, rest[2 * nw + 1:3 * nw + 1]))
    xi, yi, ci = _place()
    dev = 4 * xi + 2 * yi + ci

    ffn3 = LATE[:3]
    first = ffn3 + MIXER_MATRICES[0]
    shard = {n: (w[n][:1] if n in ffn3 else w[n]).astype(BF16) for n in first}
    late_shard = {n: (w[n][EARLY_LAYERS:] if n in ffn3 else w[n]).astype(BF16) for n in LATE}

    def joined(names):
        return lambda got: {n: _from_dev_major(t, SHARDED[n]) for n, t in zip(names, got)}

    big = _all_gather("gather_matrices", [shard[n] for n in first])
    small = _all_gather("gather_vectors", [_flat([w[n] for n in SMALL], F32)])[0]
    full = {n: w[n] for n in REPL}
    for n, blocks in zip(first, big):
        full[n] = _from_dev_major(blocks, SHARDED[n])
    for n, blocks in zip(SMALL, _unflat(small, [w[n].shape for n in SMALL], (N_DEV,))):
        full[n] = _from_dev_major(blocks, SHARDED[n])

    def by_owner(names, grads):
        return [_dev_major(t.reshape(t.shape[:1] + w[n].shape[1:SHARDED[n]] + (-1,) + w[n].shape[SHARDED[n] + 1:]),
                           SHARDED[n]).astype(BF16) for n, t in zip(names, grads)]

    beside = {"0a": ([w[n][1:EARLY_LAYERS].astype(BF16) for n in ffn3], 1, joined(ffn3)),
              "0b": ([w[n].astype(BF16) for n in MIXER_MATRICES[1]], None, joined(MIXER_MATRICES[1]))}
    plan = dict(late_shards=[late_shard[n] for n in LATE], dev_major=by_owner, assemble=joined(LATE), beside_ffn=beside)
    loss, gx, g, travelled = _local_step(x, tgt, full, plan)

    vec_names = SMALL + REPL
    vec = _all_gather("gather_vector_grads", [_flat([g[n] for n in vec_names] + [loss], F32)])[0]
    *vec_list, losses = _unflat(vec, [g[n].shape for n in vec_names] + [(1, 1)], (N_DEV,))
    vec_parts = dict(zip(vec_names, vec_list))
    loss = _sum_parts("loss_sum", losses.reshape(N_DEV, 1))[0, 0]

    last = ffn3 + MIXER_MATRICES[0]
    keep, send = [], []
    for gd in by_owner(last, [g[n] for n in last]):
        gd = gd.reshape((4, 2) + gd.shape[1:])
        keep.append(lax.dynamic_index_in_dim(gd, ci, 1, keepdims=False))
        send.append(lax.dynamic_index_in_dim(gd, 1 - ci, 1, keepdims=False))
    got_sib = _swap_sibling("grads_to_sibling", send)
    pair = [_add_bf16(f"grad_pair_sum_{n}", a, b) for n, a, b in zip(last, keep, got_sib)]
    got_chips = _scatter_chips("grads_to_chips", pair)
    chip = 2 * xi + yi
    parts = {}
    for k, n in enumerate(last):
        parts[n] = [[lax.dynamic_index_in_dim(keep[k], chip, 0, keepdims=False),
                     lax.dynamic_index_in_dim(got_sib[k], chip, 0, keepdims=False), got_chips[k][0], got_chips[k][1], got_chips[k][2]]]
    for names, sent, got in reversed(travelled):
        for n, s, t in zip(names, sent, got):
            parts.setdefault(n, []).append([lax.dynamic_index_in_dim(s, dev, 0, keepdims=False)] + [t[k] for k in range(len(FLIPS))])

    res = {}
    for n in BIG:
        runs = parts[n]
        count = max(len(r) for r in runs)
        whole = [jnp.concatenate([r[k] if k < len(r) else jnp.zeros_like(r[0]) for r in runs], axis=0) if len(runs) > 1 else runs[0][k]
                 for k in range(count)]
        res[n] = _adamw(f"adamw_{n}", w[n], m[n], v[n], whole)
    for n in vec_names:
        parts = vec_parts[n]
        if n in SHARDED:
            ax = SHARDED[n]
            parts = lax.dynamic_slice_in_dim(parts, dev * w[n].shape[ax], w[n].shape[ax], ax + 1)
        res[n] = _adamw(f"adamw_{n}", w[n], m[n], v[n], [parts[d] for d in range(N_DEV)])
    return (loss, gx, *[res[n][0] for n in WEIGHTS], *[res[n][1] for n in WEIGHTS], *[res[n][2] for n in WEIGHTS],
            *[res[n][3] for n in WEIGHTS])
```

```python
import functools
import math

import jax
import jax.numpy as jnp
from jax import lax
from jax.experimental import pallas as pl
from jax.experimental.pallas import tpu as pltpu

F32 = jnp.float32
BF16 = jnp.bfloat16
MESH = pl.DeviceIdType.MESH

N_DEV = 8
RMS_EPS = 1e-6
LN_EPS = 1e-5
FOX_HEADS = 16
HGRN_EXPAND = 128
HGRN_CHUNK = 32
POOL_WINDOWS = (2, 4, 8, 16)
ADAM_LR, ADAM_B1, ADAM_B2, ADAM_EPS, ADAM_WD, ADAM_STEP = 0.001, 0.9, 0.999, 1e-08, 0.01, 10
LANES = 128
VMEM_LIMIT_MB = 48

SHARDED = dict(
    ffn_norm=2, ffn_w_gate=3, ffn_w_up=3, ffn_w_down=2, conv_w_in=2, conv_dw=2, conv_w_out=1, fox_w_in=2, fox_w_out=1,
    hgrn_w_in=2, hgrn_norm=1, hgrn_w_out=1, pool_w=2, pool_scale=1)
BIG = ("ffn_w_gate", "ffn_w_up", "ffn_w_down", "conv_w_in", "conv_w_out", "fox_w_in", "fox_w_out", "hgrn_w_in",
       "hgrn_w_out", "pool_w")
SMALL = ("ffn_norm", "conv_dw", "hgrn_norm", "pool_scale")
REPL = ("mix_norm", "final_norm", "conv_b_in", "conv_dw_b", "conv_ln_g", "conv_ln_b", "fox_b_f", "hgrn_lb_logits")
WEIGHTS = ("ffn_norm", "ffn_w_gate", "ffn_w_up", "ffn_w_down", "mix_norm", "final_norm", "conv_w_in", "conv_b_in",
           "conv_dw", "conv_dw_b", "conv_ln_g", "conv_ln_b", "conv_w_out", "fox_w_in", "fox_b_f", "fox_w_out",
           "hgrn_w_in", "hgrn_lb_logits", "hgrn_norm", "hgrn_w_out", "pool_w", "pool_scale")


def _tile(n, pref, mult=8):
    if n <= pref:
        return n
    for t in range(pref - pref % mult, 0, -mult):
        if n % t == 0:
            return t
    return n


def _sig(x):
    return 1.0 / (1.0 + jnp.exp(-x))


def _call(body, name, grid, in_specs, out_specs, out_shape, scratch=(), sem=None):
    params = dict(vmem_limit_bytes=VMEM_LIMIT_MB << 20)
    if sem is not None:
        params["dimension_semantics"] = sem
    return pl.pallas_call(body, name=name, grid=grid, in_specs=in_specs, out_specs=out_specs, out_shape=out_shape,
                          scratch_shapes=list(scratch), compiler_params=pltpu.CompilerParams(**params))


FLIPS = [(fx, fy, fc) for fx in (0, 1) for fy in (0, 1) for fc in (0, 1)][1:]


def _carry_call(body, name, grid, in_specs, out_specs, out_shape, scratch, kind, arrays):
    K = len(arrays)
    n_in, n_out, n_scr = len(in_specs), len(out_shape), len(scratch)
    n_peer = len(FLIPS)
    if kind == "gather":
        landed = [_sds((N_DEV,) + a.shape, a.dtype) for a in arrays]
    else:
        landed = [_sds((n_peer,) + a.shape[1:], a.dtype) for a in arrays]

    def wrapped(*refs):
        ins, sent = refs[:n_in], refs[n_in:n_in + K]
        outs, got = refs[n_in + K:n_in + K + n_out], refs[n_in + K + n_out:n_in + 2 * K + n_out]
        scr = refs[n_in + 2 * K + n_out:n_in + 2 * K + n_out + n_scr]
        send_sems, recv_sems, local_sems = refs[-3:]
        ids = [pl.program_id(d) for d in range(len(grid))]
        first = functools.reduce(jnp.logical_and, [i == 0 for i in ids])
        last = functools.reduce(jnp.logical_and, [i == n - 1 for i, n in zip(ids, grid)])
        xi, yi, ci = _place()
        me = 4 * xi + 2 * yi + ci

        def copies():
            cps = []
            for a in range(K):
                for k, (fx, fy, fc) in enumerate(FLIPS):
                    px, py, pc = (xi + fx) % 2, (yi + fy) % 2, (ci + fc) % 2
                    if kind == "gather":
                        src, dst = sent[a], got[a].at[me]
                    else:
                        src, dst = sent[a].at[4 * px + 2 * py + pc], got[a].at[k]
                    cps.append(pltpu.make_async_remote_copy(src_ref=src, dst_ref=dst, send_sem=send_sems.at[n_peer * a + k],
                                                            recv_sem=recv_sems.at[n_peer * a + k], device_id=(px, py, pc),
                                                            device_id_type=MESH))
            return cps

        def own():
            return [pltpu.make_async_copy(sent[a], got[a].at[me], local_sems.at[a]) for a in range(K)] if kind == "gather" else []

        @pl.when(first)
        def _():
            for cp in copies() + own():
                cp.start()

        body(*ins, *outs, *scr)

        @pl.when(last)
        def _():
            for cp in copies() + own():
                cp.wait()

    params = dict(vmem_limit_bytes=VMEM_LIMIT_MB << 20, dimension_semantics=("arbitrary",) * len(grid))
    res = pl.pallas_call(wrapped, name=name, grid=grid, in_specs=list(in_specs) + [ANY] * K,
                         out_specs=list(out_specs) + [ANY] * K, out_shape=list(out_shape) + landed,
                         scratch_shapes=list(scratch) + [pltpu.SemaphoreType.DMA((n_peer * K,)), pltpu.SemaphoreType.DMA((n_peer * K,)),
                                                         pltpu.SemaphoreType.DMA((K,))],
                         compiler_params=pltpu.CompilerParams(**params))
    return res


def _sds(shape, dtype):
    return jax.ShapeDtypeStruct(tuple(shape), dtype)


_DN = {"nn": (((1,), (0,)), ((), ())), "nt": (((1,), (1,)), ((), ())), "tn": (((0,), (0,)), ((), ()))}


def _dot(a, b, mode="nn"):
    return lax.dot_general(a.astype(BF16), b.astype(BF16), _DN[mode], preferred_element_type=F32)


def _roll(x, shift, axis=0):
    n = x.shape[axis]
    shift = shift % n
    return x if shift == 0 else pltpu.roll(x, shift, axis)


def _scan_rows(x, reverse=False):
    n = x.shape[0]
    row = lax.broadcasted_iota(jnp.int32, x.shape, 0)
    sh = 1
    while sh < n:
        if reverse:
            x = x + jnp.where(row < n - sh, _roll(x, n - sh), 0.0)
        else:
            x = x + jnp.where(row >= sh, _roll(x, sh), 0.0)
        sh *= 2
    return x


def _mm(name, mode, pairs, pair_specs, grid, k_axis, out_shapes, out_specs, acc_shape=None, extras=(), extra_specs=(),
        epilogue=None, alpha=1.0, sem=None):
    npair, nex, nout = len(pairs), len(extras), len(out_shapes)
    nk = grid[k_axis] if k_axis is not None else 1

    def body(*refs):
        prs = refs[:2 * npair]
        ex = refs[2 * npair:2 * npair + nex]
        outs = refs[2 * npair + nex:2 * npair + nex + nout]

        def partial():
            p = None
            for i in range(npair):
                d = _dot(prs[2 * i][...], prs[2 * i + 1][...], mode)
                p = d if p is None else p + d
            return p

        def finish(res):
            if alpha != 1.0:
                res = res * alpha
            if epilogue is None:
                outs[0][...] = res.astype(outs[0].dtype)
            else:
                epilogue(res, ex, outs)

        if k_axis is None:
            finish(partial())
        else:
            acc = refs[-1]
            k = pl.program_id(k_axis)

            @pl.when(k == 0)
            def _():
                acc[...] = partial()

            @pl.when(k > 0)
            def _():
                acc[...] += partial()

            @pl.when(k == nk - 1)
            def _():
                finish(acc[...])

    if sem is None:
        sem = tuple("arbitrary" if i == k_axis else "parallel" for i in range(len(grid)))
    scratch = [pltpu.VMEM(acc_shape, F32)] if k_axis is not None else []
    flat = [t for p in pairs for t in p]
    flat_specs = [s for p in pair_specs for s in p]
    return _call(body, name, grid, flat_specs + list(extra_specs), out_specs, out_shapes, scratch, sem)(*flat, *extras)


def _mm_nn(name, a, b, out_dtype, tm=1024, tn=1024):
    M, K = a.shape
    N = b.shape[1]
    tm, tn = _tile(M, tm), _tile(N, tn, LANES)
    return _mm(name, "nn", [(a, b)], [(pl.BlockSpec((tm, K), lambda i, j: (i, 0)), pl.BlockSpec((K, tn), lambda i, j: (0, j)))],
               (M // tm, N // tn), None, [_sds((M, N), out_dtype)], [pl.BlockSpec((tm, tn), lambda i, j: (i, j))])[0]


def _mm_res(name, u, w, x, alpha, tm=512):
    T, K = u.shape
    D = w.shape[1]
    tm = _tile(T, tm)

    def epi(res, ex, outs):
        outs[0][...] = ex[0][...] + res

    return _mm(name, "nn", [(u, w)], [(pl.BlockSpec((tm, K), lambda i: (i, 0)), pl.BlockSpec((K, D), lambda i: (0, 0)))],
               (T // tm,), None, [_sds((T, D), F32)], [pl.BlockSpec((tm, D), lambda i: (i, 0))],
               extras=[x], extra_specs=[pl.BlockSpec((tm, D), lambda i: (i, 0))], epilogue=epi, alpha=alpha)[0]


def _mm_nt(name, pairs, out_dtype, tm=1024, tn=1024, alpha=1.0):
    M = pairs[0][0].shape[0]
    N = pairs[0][1].shape[0]
    tm, tn = _tile(M, tm), _tile(N, tn, LANES)
    specs = [(pl.BlockSpec((tm, a.shape[1]), lambda i, j: (i, 0)), pl.BlockSpec((tn, b.shape[1]), lambda i, j: (j, 0)))
             for a, b in pairs]
    return _mm(name, "nt", pairs, specs, (M // tm, N // tn), None, [_sds((M, N), out_dtype)],
               [pl.BlockSpec((tm, tn), lambda i, j: (i, j))], alpha=alpha)[0]


def _mm_nt3(name, a3, b3, tm=1024, tn=1024):
    S, M, K = a3.shape
    N = b3.shape[1]
    tm, tn = _tile(M, tm), _tile(N, tn, LANES)
    return _mm(name, "nt", [(a3, b3)],
               [(pl.BlockSpec((None, tm, K), lambda i, j, s: (s, i, 0)), pl.BlockSpec((None, tn, K), lambda i, j, s: (s, j, 0)))],
               (M // tm, N // tn, S), 2, [_sds((M, N), F32)], [pl.BlockSpec((tm, tn), lambda i, j, s: (i, j))],
               acc_shape=(tm, tn))[0]


def _mm_tn(name, a, b, alpha=1.0, tm=1408, tn=1408, tk=512):
    T, M = a.shape
    N = b.shape[1]
    tm, tn, tk = _tile(M, tm, LANES), _tile(N, tn, LANES), _tile(T, tk)
    return _mm(name, "tn", [(a, b)],
               [(pl.BlockSpec((tk, tm), lambda i, j, k: (k, i)), pl.BlockSpec((tk, tn), lambda i, j, k: (k, j)))],
               (M // tm, N // tn, T // tk), 2, [_sds((M, N), F32)], [pl.BlockSpec((tm, tn), lambda i, j, k: (i, j))],
               acc_shape=(tm, tn), alpha=alpha)[0]


def _mm_tn3(name, a, b3, tm=1408, tn=1408, tk=512):
    T, M = a.shape
    S, _, N = b3.shape
    tm, tn, tk = _tile(M, tm, LANES), _tile(N, tn, LANES), _tile(T, tk)
    return _mm(name, "tn", [(a, b3)],
               [(pl.BlockSpec((tk, tm), lambda s, i, j, k: (k, i)), pl.BlockSpec((None, tk, tn), lambda s, i, j, k: (s, k, j)))],
               (S, M // tm, N // tn, T // tk), 3, [_sds((S, M, N), F32)],
               [pl.BlockSpec((None, tm, tn), lambda s, i, j, k: (s, i, j))], acc_shape=(tm, tn))[0]


def _rms_fwd(name, x, g):
    T, D = x.shape
    tt = _tile(T, 512)

    def body(x_ref, g_ref, h_ref):
        xv = x_ref[...]
        r = lax.rsqrt(jnp.mean(xv * xv, axis=-1, keepdims=True) + RMS_EPS)
        h_ref[...] = (xv * r * g_ref[...]).astype(h_ref.dtype)

    row = pl.BlockSpec((tt, D), lambda i: (i, 0))
    return _call(body, name, (T // tt,), [row, pl.BlockSpec((1, D), lambda i: (0, 0))], row, _sds((T, D), BF16))(x, g)


def _rms_bwd(name, dh, x, g, dx_in):
    T, D = x.shape
    tt = _tile(T, 512)

    def body(dh_ref, x_ref, g_ref, dxi_ref, dx_ref, dxb_ref, dg_ref):
        xv = x_ref[...]
        r = lax.rsqrt(jnp.mean(xv * xv, axis=-1, keepdims=True) + RMS_EPS)
        xh = xv * r
        dhv = dh_ref[...]
        dxh = dhv * g_ref[...]
        dx = dxi_ref[...] + r * (dxh - xh * jnp.mean(dxh * xh, axis=-1, keepdims=True))
        dx_ref[...] = dx
        dxb_ref[...] = dx.astype(BF16)
        part = jnp.sum(dhv * xh, axis=0, keepdims=True)

        @pl.when(pl.program_id(0) == 0)
        def _():
            dg_ref[...] = part

        @pl.when(pl.program_id(0) > 0)
        def _():
            dg_ref[...] += part

    row = pl.BlockSpec((tt, D), lambda i: (i, 0))
    vec = pl.BlockSpec((1, D), lambda i: (0, 0))
    dx, dxb, dg = _call(body, name, (T // tt,), [row, row, vec, row], [row, row, vec],
                        [_sds((T, D), F32), _sds((T, D), BF16), _sds((1, D), F32)], sem=("arbitrary",))(dh, x, g, dx_in)
    return (dx, dxb), dg


def _loss_head(x, g, tgt):
    T, D = x.shape
    tt = _tile(T, 512)

    def body(x_ref, g_ref, t_ref, loss_ref, dx_ref, dxb_ref, dg_ref):
        xv = x_ref[...]
        r = lax.rsqrt(jnp.mean(xv * xv, axis=-1, keepdims=True) + RMS_EPS)
        xh = xv * r
        e = xh * g_ref[...] - t_ref[...]
        lp = 0.5 * jnp.sum(jnp.sum(e * e, axis=-1, keepdims=True), axis=0, keepdims=True) / D
        dy = e / D
        dxh = dy * g_ref[...]
        dx = r * (dxh - xh * jnp.mean(dxh * xh, axis=-1, keepdims=True))
        dx_ref[...] = dx
        dxb_ref[...] = dx.astype(BF16)
        part = jnp.sum(dy * xh, axis=0, keepdims=True)

        @pl.when(pl.program_id(0) == 0)
        def _():
            dg_ref[...] = part
            loss_ref[...] = lp

        @pl.when(pl.program_id(0) > 0)
        def _():
            dg_ref[...] += part
            loss_ref[...] += lp

    row = pl.BlockSpec((tt, D), lambda i: (i, 0))
    vec = pl.BlockSpec((1, D), lambda i: (0, 0))
    one = pl.BlockSpec((1, 1), lambda i: (0, 0))
    loss, dx, dxb, dg = _call(body, "loss_head", (T // tt,), [row, vec, row], [one, row, row, vec],
                              [_sds((1, 1), F32), _sds((T, D), F32), _sds((T, D), BF16), _sds((1, D), F32)],
                              sem=("arbitrary",))(x, g, tgt)
    return loss, (dx, dxb), dg


def _colsum3(name, a3):
    S, T, N = a3.shape
    tt = _tile(T, 512)

    def body(a_ref, o_ref):
        part = jnp.sum(a_ref[...].astype(F32), axis=0, keepdims=True)

        @pl.when(pl.program_id(1) == 0)
        def _():
            o_ref[...] = part

        @pl.when(pl.program_id(1) > 0)
        def _():
            o_ref[...] += part

    return _call(body, name, (S, T // tt), [pl.BlockSpec((None, tt, N), lambda s, i: (s, i, 0))],
                 pl.BlockSpec((None, 1, N), lambda s, i: (s, 0, 0)), _sds((S, 1, N), F32), sem=("parallel", "arbitrary"))(a3)


def _glu_mm(name, h, w2, bias2, mode, u_dtype, tm=1024, tn=256, carried=None):
    T, K = h.shape
    N = w2.shape[2]
    tm, tn = _tile(T, tm), _tile(N, tn, LANES)
    has_bias = bias2 is not None

    halves = 2 if tm % 32 == 0 else 1

    def body(*refs):
        h_ref, w_ref = refs[0], refs[1]
        ab_ref, u_ref = refs[-2], refs[-1]
        for r in range(halves):
            rows = pl.ds(r * (tm // halves), tm // halves)
            hv = h_ref[rows, :]
            a = _dot(hv, w_ref[0])
            b = _dot(hv, w_ref[1])
            if has_bias:
                a = a + refs[2][0]
                b = b + refs[2][1]
            u = a * _sig(a) * b if mode == "swiglu" else a * _sig(b)
            ab_ref[0, rows, :] = a.astype(BF16)
            ab_ref[1, rows, :] = b.astype(BF16)
            u_ref[rows, :] = u.astype(u_ref.dtype)

    in_specs = [pl.BlockSpec((tm, K), lambda i, j: (i, 0)), pl.BlockSpec((2, K, tn), lambda i, j: (0, 0, j))]
    args = [h, w2]
    if has_bias:
        in_specs.append(pl.BlockSpec((2, 1, tn), lambda i, j: (0, 0, j)))
        args.append(bias2)
    specs = ((T // tm, N // tn), in_specs,
             [pl.BlockSpec((2, tm, tn), lambda i, j: (0, i, j)), pl.BlockSpec((tm, tn), lambda i, j: (i, j))],
             [_sds((2, T, N), BF16), _sds((T, N), u_dtype)], [])
    if carried is None:
        return _call(body, name, *specs, sem=("parallel", "parallel"))(*args)
    ab, u, *got = _carry_call(body, name, *specs, "gather", carried)(*args, *carried)
    return ab, u, got


def _swiglu_bwd_mm(name, dxb, wd, ab, alpha, tm=1024, tn=256):
    T, D = dxb.shape
    N = wd.shape[0]
    tm, tn = _tile(T, tm), _tile(N, tn, LANES)
    halves = 4 if tm % 64 == 0 else 1

    def body(dx_ref, wd_ref, ab_ref, dab_ref, u_ref):
        wdv = wd_ref[...]
        for r in range(halves):
            rows = pl.ds(r * (tm // halves), tm // halves)
            du = _dot(dx_ref[rows, :], wdv, "nt") * alpha
            a = ab_ref[0, rows, :].astype(F32)
            b = ab_ref[1, rows, :].astype(F32)
            sg = _sig(a)
            sa = a * sg
            dab_ref[0, rows, :] = (du * b * (sg * (1.0 + a * (1.0 - sg)))).astype(BF16)
            dab_ref[1, rows, :] = (du * sa).astype(BF16)
            u_ref[rows, :] = (sa * b).astype(BF16)

    ab_spec = pl.BlockSpec((2, tm, tn), lambda i, j: (0, i, j))
    return _call(body, name, (T // tm, N // tn),
                 [pl.BlockSpec((tm, D), lambda i, j: (i, 0)), pl.BlockSpec((tn, D), lambda i, j: (j, 0)), ab_spec],
                 [ab_spec, pl.BlockSpec((tm, tn), lambda i, j: (i, j))], [_sds((2, T, N), BF16), _sds((T, N), BF16)],
                 sem=("parallel", "parallel"))(dxb, wd, ab)


def _ffn_fwd(tag, x, g, wgu, wd, carried=None):
    h = _rms_fwd(f"ffn_rms_{tag}", x, g)
    if carried is None:
        ab, u = _glu_mm(f"ffn_gu_{tag}", h, wgu, None, "swiglu", BF16)
        return _mm_res(f"ffn_down_{tag}", u, wd, x, 0.5), (x, h, ab)
    ab, u, got = _glu_mm(f"ffn_gu_{tag}", h, wgu, None, "swiglu", BF16, carried=carried)
    return _mm_res(f"ffn_down_{tag}", u, wd, x, 0.5), (x, h, ab), got


def _ffn_bwd(tag, dxp, saved, g, wgu, wd):
    x, h, ab = saved
    dx, dxb = dxp
    dab, u = _swiglu_bwd_mm(f"ffn_dgu_{tag}", dxb, wd, ab, 0.5)
    dwd = _mm_tn(f"ffn_dwd_{tag}", u, dxb, alpha=0.5)
    dwgu = _mm_tn3(f"ffn_dwgu_{tag}", h, dab)
    dh = _mm_nt3(f"ffn_dh_{tag}", dab, wgu)
    dx2, dg = _rms_bwd(f"ffn_drms_{tag}", dh, x, g, dx)
    return dx2, dg, dwgu, dwd


HALO = 32


def _conv_fwd(u3, dw32, dwb, lng, lnb):
    B, S, D = u3.shape
    W = dw32.shape[0]
    taps = CONV_TAPS
    tt = _tile(S, 256, HALO)
    hb = tt // HALO

    def body(u_ref, halo_ref, dw_ref, dwb_ref, g_ref, b_ref, v_ref, s_ref):
        i = pl.program_id(1)
        halo = jnp.where(i > 0, halo_ref[...], 0.0)
        ext = jnp.concatenate([halo, u_ref[...]], axis=0)
        acc = jnp.zeros((tt, D), F32) + dwb_ref[...]
        for j in range(taps):
            acc = acc + dw_ref[pl.ds(j, 1), :] * _roll(ext, taps - 1 - j)[HALO:]
        v_ref[...] = acc
        mu = jnp.mean(acc, axis=-1, keepdims=True)
        xc = acc - mu
        ln = xc * lax.rsqrt(jnp.mean(xc * xc, axis=-1, keepdims=True) + LN_EPS) * g_ref[...] + b_ref[...]
        s_ref[...] = (ln * _sig(ln)).astype(BF16)

    main = pl.BlockSpec((None, tt, D), lambda b, i: (b, i, 0))
    halo = pl.BlockSpec((None, HALO, D), lambda b, i: (b, jnp.maximum(i * hb - 1, 0), 0))
    vec = pl.BlockSpec((1, D), lambda b, i: (0, 0))
    return _call(body, "conv_fwd", (B, S // tt), [main, halo, pl.BlockSpec((W, D), lambda b, i: (0, 0)), vec, vec, vec],
                 [main, main], [_sds((B, S, D), F32), _sds((B, S, D), BF16)], sem=("parallel", "parallel"))(
        u3, u3, dw32, dwb, lng, lnb)


def _conv_bwd_ln(v, ds, lng, lnb):
    T, D = v.shape
    tt = _tile(T, 256)

    def body(v_ref, ds_ref, g_ref, b_ref, dv_ref, red_ref):
        vv = v_ref[...]
        mu = jnp.mean(vv, axis=-1, keepdims=True)
        xc = vv - mu
        rstd = lax.rsqrt(jnp.mean(xc * xc, axis=-1, keepdims=True) + LN_EPS)
        xh = xc * rstd
        ln = xh * g_ref[...] + b_ref[...]
        sg = _sig(ln)
        dln = ds_ref[...] * (sg * (1.0 + ln * (1.0 - sg)))
        dxh = dln * g_ref[...]
        dv = rstd * (dxh - jnp.mean(dxh, axis=-1, keepdims=True) - xh * jnp.mean(dxh * xh, axis=-1, keepdims=True))
        dv_ref[...] = dv
        parts = (jnp.sum(dln * xh, axis=0, keepdims=True), jnp.sum(dln, axis=0, keepdims=True),
                 jnp.sum(dv, axis=0, keepdims=True))

        @pl.when(pl.program_id(0) == 0)
        def _():
            for k in range(3):
                red_ref[k] = parts[k]

        @pl.when(pl.program_id(0) > 0)
        def _():
            for k in range(3):
                red_ref[k] += parts[k]

    row = pl.BlockSpec((tt, D), lambda i: (i, 0))
    vec = pl.BlockSpec((1, D), lambda i: (0, 0))
    return _call(body, "conv_bwd_ln", (T // tt,), [row, row, vec, vec], [row, pl.BlockSpec((3, 1, D), lambda i: (0, 0, 0))],
                 [_sds((T, D), F32), _sds((3, 1, D), F32)], sem=("arbitrary",))(v, ds, lng, lnb)


def _conv_bwd_dw(dv3, u3, ab, dw32, carried=None):
    B, S, D = u3.shape
    W = dw32.shape[0]
    taps = CONV_TAPS
    tt = _tile(S, 256, HALO)
    hb = tt // HALO
    nt = S // tt
    L = tt + HALO

    def body(dv_ref, dvn_ref, u_ref, up_ref, ab_ref, dw_ref, dab_ref, ddw_ref):
        b, i = pl.program_id(0), pl.program_id(1)
        dv = dv_ref[...]
        ext_dv = jnp.concatenate([dv, jnp.where(i < nt - 1, dvn_ref[...], 0.0)], axis=0)
        ext_u = jnp.concatenate([jnp.where(i > 0, up_ref[...], 0.0), u_ref[...]], axis=0)
        du = jnp.zeros((tt, D), F32)
        first = jnp.logical_and(b == 0, i == 0)

        @pl.when(first)
        def _():
            ddw_ref[...] = jnp.zeros((W, D), F32)

        for j in range(taps):
            sh = taps - 1 - j
            du = du + dw_ref[pl.ds(j, 1), :] * _roll(ext_dv, L - sh)[:tt]
            ddw_ref[pl.ds(j, 1), :] += jnp.sum(dv * _roll(ext_u, sh)[HALO:], axis=0, keepdims=True)
        a = ab_ref[0].astype(F32)
        sb = _sig(ab_ref[1].astype(F32))
        dab_ref[0] = (du * sb).astype(BF16)
        dab_ref[1] = (du * a * sb * (1.0 - sb)).astype(BF16)

    main = pl.BlockSpec((None, tt, D), lambda b, i: (b, i, 0))
    prev = pl.BlockSpec((None, HALO, D), lambda b, i: (b, jnp.maximum(i * hb - 1, 0), 0))
    nxt = pl.BlockSpec((None, HALO, D), lambda b, i: (b, jnp.minimum((i + 1) * hb, S // HALO - 1), 0))
    abs_ = pl.BlockSpec((2, tt, D), lambda b, i: (0, b * nt + i, 0))
    wsp = pl.BlockSpec((W, D), lambda b, i: (0, 0))
    specs = ((B, nt), [main, nxt, main, prev, abs_, wsp], [abs_, wsp], [_sds((2, B * S, D), BF16), _sds((W, D), F32)], [])
    args = (dv3, dv3, u3, u3, ab, dw32)
    if carried is None:
        return _call(body, "conv_bwd_dw", *specs, sem=("arbitrary", "arbitrary"))(*args)
    dab, ddw, *got = _carry_call(body, "conv_bwd_dw", *specs, "scatter", carried)(*args, *carried)
    return dab, ddw, got


CONV_TAPS = 31


def _conv_mixer_fwd(x, B, p):
    T, D = x.shape
    h = _rms_fwd("conv_rms", x, p["mix_norm"][0:1])
    ab, u = _glu_mm("conv_in", h, p["conv_w_in2"], p["conv_b_in2"], "glu", F32)
    v, s = _conv_fwd(u.reshape(B, T // B, D), p["conv_dw32"], p["conv_dw_b"], p["conv_ln_g"], p["conv_ln_b"])
    v, s = v.reshape(T, D), s.reshape(T, D)
    return _mm_res("conv_out", s, p["conv_w_out"], x, 1.0), (x, h, ab, u, v, s)


def _conv_mixer_bwd(dxp, saved, B, p):
    x, h, ab, u, v, s = saved
    T, D = x.shape
    dx, dxb = dxp
    g = {}
    ds = _mm_nt("conv_ds", [(dxb, p["conv_w_out"])], F32)
    g["conv_w_out"] = _mm_tn("conv_dwout", s, dxb)
    dv, red = _conv_bwd_ln(v, ds, p["conv_ln_g"], p["conv_ln_b"])
    g["conv_ln_g"], g["conv_ln_b"], g["conv_dw_b"] = red[0], red[1], red[2]
    if "scatter" in p:
        dab, ddw, p["scattered"] = _conv_bwd_dw(dv.reshape(B, T // B, D), u.reshape(B, T // B, D), ab, p["conv_dw32"],
                                               p.pop("scatter"))
    else:
        dab, ddw = _conv_bwd_dw(dv.reshape(B, T // B, D), u.reshape(B, T // B, D), ab, p["conv_dw32"])
    g["conv_dw"] = ddw[:CONV_TAPS][None]
    g["conv_b_in"] = _colsum3("conv_dbin", dab).reshape(1, 2 * D)
    dwin = _mm_tn3("conv_dwin", h, dab)
    g["conv_w_in"] = jnp.moveaxis(dwin, 0, 1).reshape(1, D, 2 * D)
    dh = _mm_nt3("conv_dh", dab, p["conv_w_in2"])
    dx2, dg = _rms_bwd("conv_drms", dh, x, p["mix_norm"][0:1], dx)
    return dx2, dg, g


def _log_sigmoid(z):
    return jnp.minimum(z, 0.0) - jnp.log(1.0 + jnp.exp(-jnp.abs(z)))


def _fox_gate_fwd(fl3, bf):
    B, S, N = fl3.shape
    tt = _tile(S, 512)

    def body(fl_ref, bf_ref, c_ref, carry):
        @pl.when(pl.program_id(1) == 0)
        def _():
            carry[...] = jnp.zeros((1, N), F32)

        c = _scan_rows(_log_sigmoid(fl_ref[...] + bf_ref[...])) + carry[...]
        c_ref[...] = c
        carry[...] = c_ref[pl.ds(tt - 1, 1), :]

    row = pl.BlockSpec((None, tt, N), lambda b, i: (b, i, 0))
    return _call(body, "fox_gate_fwd", (B, S // tt), [row, pl.BlockSpec((1, N), lambda b, i: (0, 0))], row,
                 _sds((B, S, N), F32), [pltpu.VMEM((1, N), F32)], sem=("parallel", "arbitrary"))(fl3, bf)


def _fox_gate_bwd(dc3, fl3, bf):
    B, S, N = fl3.shape
    tt = _tile(S, 512)
    nt = S // tt

    def body(dc_ref, fl_ref, bf_ref, dfl_ref, dbf_ref, carry):
        b, i = pl.program_id(0), pl.program_id(1)

        @pl.when(i == 0)
        def _():
            carry[...] = jnp.zeros((1, N), F32)

        dc = dc_ref[0] - dc_ref[1]
        dlf = _scan_rows(dc, reverse=True) + carry[...]
        dfl = dlf * _sig(-(fl_ref[...] + bf_ref[...]))
        dfl_ref[...] = dfl
        carry[...] += jnp.sum(dc, axis=0, keepdims=True)
        part = jnp.sum(dfl, axis=0, keepdims=True)

        @pl.when(jnp.logical_and(b == 0, i == 0))
        def _():
            dbf_ref[...] = part

        @pl.when(jnp.logical_or(b > 0, i > 0))
        def _():
            dbf_ref[...] += part

    row = pl.BlockSpec((None, tt, N), lambda b, i: (b, nt - 1 - i, 0))
    vec = pl.BlockSpec((1, N), lambda b, i: (0, 0))
    row2 = pl.BlockSpec((2, None, tt, N), lambda b, i: (0, b, nt - 1 - i, 0))
    return _call(body, "fox_gate_bwd", (B, nt), [row2, row, vec], [row, vec], [_sds((B, S, N), F32), _sds((1, N), F32)],
                 [pltpu.VMEM((1, N), F32)], sem=("arbitrary", "arbitrary"))(dc3, fl3, bf)


NEG = -1e30


def _fox_attn_fwd(qa, ka, vat, crow, ckb, dh, scale, carried=None):
    B, H, S, P = qa.shape
    tq = _tile(S, 512, LANES)
    tk = tq
    nl = tq // LANES

    def body(q_ref, k_ref, vt_ref, cr_ref, ck_ref, o_ref, lse_ref, s_scr, p_scr, m_scr, al_scr, acc_scr):
        i = pl.program_id(2)
        qv = q_ref[...]
        m_scr[...] = jnp.full((1, tq), NEG, F32)
        acc_scr[...] = jnp.zeros((P, tq), F32)
        key = lax.broadcasted_iota(jnp.int32, (tk, LANES), 0)
        qry = lax.broadcasted_iota(jnp.int32, (tk, LANES), 1)

        def kv_step(j, diagonal):
            off = pl.multiple_of(j * tk, tk)
            s_scr[...] = _dot(k_ref[pl.ds(off, tk), :], qv, "nt")
            ck = ck_ref[pl.ds(off, tk), :]
            for lt in range(nl):
                ls = slice(lt * LANES, (lt + 1) * LANES)
                s = s_scr[:, ls] * scale + cr_ref[:, ls] - ck
                if diagonal:
                    s = jnp.where(key <= qry + lt * LANES, s, -jnp.inf)
                m1 = m_scr[:, ls]
                m2 = jnp.maximum(m1, jnp.max(s, axis=0, keepdims=True))
                p_scr[:, ls] = jnp.exp(s - m2).astype(BF16)
                al_scr[:, ls] = jnp.exp(m1 - m2)
                m_scr[:, ls] = m2
            acc_scr[...] = al_scr[...] * acc_scr[...] + _dot(vt_ref[:, pl.ds(off, tk)], p_scr[...])

        def before(j, carry):
            kv_step(j, False)
            return carry

        lax.fori_loop(0, i, before, 0)
        kv_step(i, True)
        l = acc_scr[pl.ds(dh, 1), :]
        o_ref[...] = (acc_scr[...] / l).T
        lse_ref[...] = m_scr[...] + jnp.log(l)

    qs = pl.BlockSpec((None, None, tq, P), lambda b, h, i: (b, h, i, 0))
    fullk = pl.BlockSpec((None, None, S, P), lambda b, h, i: (b, h, 0, 0))
    fullt = pl.BlockSpec((None, None, P, S), lambda b, h, i: (b, h, 0, 0))
    fullc = pl.BlockSpec((None, None, S, LANES), lambda b, h, i: (b, h, 0, 0))
    rowt = pl.BlockSpec((None, None, 1, tq), lambda b, h, i: (b, h, 0, i))
    scratch = [pltpu.VMEM((tk, tq), F32), pltpu.VMEM((tk, tq), BF16), pltpu.VMEM((1, tq), F32), pltpu.VMEM((1, tq), F32),
               pltpu.VMEM((P, tq), F32)]
    specs = ((B, H, S // tq), [qs, fullk, fullt, rowt, fullc], [qs, rowt], [_sds((B, H, S, P), F32), _sds((B, H, 1, S), F32)],
             scratch)
    if carried is None:
        return _call(body, "fox_attn_fwd", *specs, sem=("parallel", "parallel", "parallel"))(qa, ka, vat, crow, ckb)
    o, lse, *got = _carry_call(body, "fox_attn_fwd", *specs, "gather", carried)(qa, ka, vat, crow, ckb, *carried)
    return o, lse, got


def _fox_rowstats(do, o):
    B, H, S, P = o.shape
    tq = _tile(S, 4096)

    def body(do_ref, o_ref, dl_ref):
        dl_ref[...] = jnp.sum(do_ref[...] * o_ref[...], axis=-1, keepdims=True)

    qs = pl.BlockSpec((None, None, tq, P), lambda b, h, i: (b, h, i, 0))
    col = pl.BlockSpec((None, None, tq, 1), lambda b, h, i: (b, h, i, 0))
    return _call(body, "fox_rowstats", (B, H, S // tq), [qs, qs], col, _sds((B, H, S, 1), F32), sem=("parallel",) * 3)(do, o)


def _fox_attn_bwd(qa, ka, kat, va, do, crow, lse, delta, ckb, dh, scale, carried=None):
    B, H, S, P = qa.shape
    tk = _tile(S, 512, LANES)
    tq = tk
    nq = S // tq
    nl = tq // LANES

    def body(q_ref, k_ref, kt_ref, v_ref, do_ref, cr_ref, lse_ref, dl_ref, ck_ref, dqt_ref, dk_ref, dv_ref, rs_ref,
             s_scr, dp_scr, p_scr, ds_scr, dk_acc, dv_acc):
        j = pl.program_id(2)

        @pl.when(j == 0)
        def _():
            dqt_ref[...] = jnp.zeros((P, S), F32)

        kj, vj, ck = k_ref[...], v_ref[...], ck_ref[...]
        dk_acc[...] = jnp.zeros((tk, P), F32)
        dv_acc[...] = jnp.zeros((tk, P), F32)
        key = lax.broadcasted_iota(jnp.int32, (tk, LANES), 0)
        qry = lax.broadcasted_iota(jnp.int32, (tk, LANES), 1)

        def q_step(i, diagonal):
            off = pl.multiple_of(i * tq, tq)
            qi, doi = q_ref[pl.ds(off, tq), :], do_ref[pl.ds(off, tq), :].astype(BF16)
            s_scr[...] = _dot(kj, qi, "nt")
            dp_scr[...] = _dot(vj, doi, "nt")
            for lt in range(nl):
                ls = slice(lt * LANES, (lt + 1) * LANES)
                gl = pl.ds(pl.multiple_of(off + lt * LANES, LANES), LANES)
                p = jnp.exp(s_scr[:, ls] * scale + (cr_ref[:, gl] - lse_ref[:, gl]) - ck)
                if diagonal:
                    p = jnp.where(key <= qry + lt * LANES, p, 0.0)
                p_scr[:, ls] = p.astype(BF16)
                ds_scr[:, ls] = (p * (dp_scr[:, ls] - dl_ref[:, gl])).astype(BF16)
            ds = ds_scr[...]
            dqt_ref[:, pl.ds(off, tq)] += _dot(kt_ref[...], ds)
            dk_acc[...] += _dot(ds, qi)
            dv_acc[...] += _dot(p_scr[...], doi)

        def after(i, carry):
            q_step(i, False)
            return carry

        q_step(j, True)
        lax.fori_loop(j + 1, nq, after, 0)
        dk_ref[...] = dk_acc[...] * jnp.where(lax.broadcasted_iota(jnp.int32, (tk, P), 1) < dh, scale, 1.0)
        dv_ref[...] = dv_acc[...]

        @pl.when(j == S // tk - 1)
        def _():
            rs_ref[...] = dqt_ref[pl.ds(dh, 1), :]
            dqt_ref[...] = dqt_ref[...] * jnp.where(lax.broadcasted_iota(jnp.int32, (P, S), 0) < dh, scale, 1.0)

    ks = pl.BlockSpec((None, None, tk, P), lambda b, h, j: (b, h, j, 0))
    kts = pl.BlockSpec((None, None, P, tk), lambda b, h, j: (b, h, 0, j))
    cks = pl.BlockSpec((None, None, tk, LANES), lambda b, h, j: (b, h, j, 0))
    full = pl.BlockSpec((None, None, S, P), lambda b, h, j: (b, h, 0, 0))
    fullt = pl.BlockSpec((None, None, P, S), lambda b, h, j: (b, h, 0, 0))
    rowf = pl.BlockSpec((None, None, 1, S), lambda b, h, j: (b, h, 0, 0))
    scratch = [pltpu.VMEM((tk, tq), F32), pltpu.VMEM((tk, tq), F32), pltpu.VMEM((tk, tq), BF16), pltpu.VMEM((tk, tq), BF16),
               pltpu.VMEM((tk, P), F32), pltpu.VMEM((tk, P), F32)]
    specs = ((B, H, S // tk), [full, ks, kts, ks, full, rowf, rowf, rowf, cks], [fullt, ks, ks, rowf],
             [_sds((B, H, P, S), F32), _sds((B, H, S, P), F32), _sds((B, H, S, P), F32), _sds((B, H, 1, S), F32)], scratch)
    args = (qa, ka, kat, va, do, crow, lse, delta, ckb)
    if carried is None:
        return _call(body, "fox_attn_bwd", *specs, sem=("parallel", "parallel", "arbitrary"))(*args)
    dqt, dk, dv, rs, *got = _carry_call(body, "fox_attn_bwd", *specs, "scatter", carried)(*args, *carried)
    return dqt, dk, dv, rs, got


def _heads(t, B, H):
    T, D = t.shape
    return t.reshape(B, T // B, H, D // H).transpose(0, 2, 1, 3)


def _unheads(t):
    B, H, S, dh = t.shape
    return t.transpose(0, 2, 1, 3).reshape(B * S, H * dh)


def _fox_mixer_fwd(x, B, p):
    T, D = x.shape
    H = FOX_HEADS
    S = T // B
    scale = (D // H) ** -0.5
    h = _rms_fwd("fox_rms", x, p["mix_norm"][1:2])
    qkv = _mm_nn("fox_qkv", h, p["fox_w_qkv"], BF16)
    fl = _mm_nn("fox_fl", h, p["fox_w_f"], F32)
    c = _fox_gate_fwd(fl.reshape(B, S, LANES), p["fox_b_f128"])
    ch = c[:, :, :H].transpose(0, 2, 1)
    crow = ch[:, :, None, :]
    ckb = jnp.broadcast_to(ch[..., None], (B, H, S, LANES))
    q, k, v = _heads(qkv[:, :D], B, H), _heads(qkv[:, D:2 * D], B, H), _heads(qkv[:, 2 * D:], B, H)
    dh = D // H
    P = -(-(dh + 2) // LANES) * LANES
    one, zero = jnp.ones((B, H, S, 1), q.dtype), jnp.zeros((B, H, S, 1), q.dtype)
    rest = jnp.zeros((B, H, S, P - dh - 2), q.dtype)
    qa = jnp.concatenate([q, zero, one, rest], axis=-1)
    ka = jnp.concatenate([k, one, zero, rest], axis=-1)
    va = jnp.concatenate([v, one, zero, rest], axis=-1)
    if "late_gather" in p:
        o, lse, p["late_gathered"] = _fox_attn_fwd(qa, ka, va.transpose(0, 1, 3, 2), crow, ckb, dh, scale, p.pop("late_gather"))
    else:
        o, lse = _fox_attn_fwd(qa, ka, va.transpose(0, 1, 3, 2), crow, ckb, dh, scale)
    of = _unheads(o[..., :dh])
    return _mm_res("fox_out", of, p["fox_w_out"], x, 1.0), (x, h, fl, qa, ka, va, crow, ckb, o, lse, of)


def _fox_mixer_bwd(dxp, saved, B, p):
    x, h, fl, qa, ka, va, crow, ckb, o, lse, of = saved
    T, D = x.shape
    H = FOX_HEADS
    S = T // B
    dh_ = D // H
    P = qa.shape[-1]
    scale = dh_ ** -0.5
    dx, dxb = dxp
    g = {}
    do = _heads(_mm_nt("fox_do", [(dxb, p["fox_w_out"])], F32), B, H)
    do = jnp.pad(do, ((0, 0), (0, 0), (0, 0), (0, P - dh_)))
    g["fox_w_out"] = _mm_tn("fox_dwout", of, dxb)[None]
    delta = _fox_rowstats(do, o).reshape(B, H, 1, S)
    if "scatter" in p:
        dqt, dk, dv, rowsum, p["scattered"] = _fox_attn_bwd(qa, ka, ka.transpose(0, 1, 3, 2), va, do, crow, lse, delta, ckb,
                                                          dh_, scale, p.pop("scatter"))
    else:
        dqt, dk, dv, rowsum = _fox_attn_bwd(qa, ka, ka.transpose(0, 1, 3, 2), va, do, crow, lse, delta, ckb, dh_, scale)
    dq = dqt[:, :, :dh_, :].transpose(0, 3, 1, 2).reshape(T, D)
    dqkv = jnp.concatenate([dq, _unheads(dk[..., :dh_]), _unheads(dv[..., :dh_])], axis=1).astype(BF16)
    dc = jnp.stack([rowsum[:, :, 0, :], dk[..., dh_ + 1]])
    dc = jnp.pad(dc.transpose(0, 1, 3, 2), ((0, 0), (0, 0), (0, 0), (0, LANES - H)))
    dfl, dbf = _fox_gate_bwd(dc, fl.reshape(B, S, LANES), p["fox_b_f128"])
    dfl = dfl.reshape(T, LANES)
    g["fox_b_f"] = dbf[:, :H]
    dwqkv = _mm_tn("fox_dwqkv", h, dqkv)
    dwf = _mm_tn("fox_dwf", h, dfl)
    g["fox_w_in"] = jnp.concatenate([dwqkv, dwf[:, :H]], axis=1)[None]
    dh = _mm_nt("fox_dh", [(dqkv, p["fox_w_qkv"]), (dfl, p["fox_w_f"])], F32)
    dx2, dg = _rms_bwd("fox_drms", dh, x, p["mix_norm"][1:2], dx)
    return dx2, dg, g


def _lb_fwd(logits):
    L, D = logits.shape

    def body(l_ref, lb_ref):
        z = l_ref[...]
        e = jnp.exp(z - jnp.max(z, axis=0, keepdims=True))
        p = e / jnp.sum(e, axis=0, keepdims=True)
        lb_ref[...] = jnp.sum(jnp.where(_lb_rows(z.shape), p, 0.0), axis=0, keepdims=True)

    return _call(body, "hgrn_lb", (1,), [pl.BlockSpec((L, D), lambda i: (0, 0))], pl.BlockSpec((1, D), lambda i: (0, 0)),
                 _sds((1, D), F32))(logits)


def _lb_rows(shape):
    r = lax.broadcasted_iota(jnp.int32, shape, 0)
    return jnp.logical_and(r >= 1, r <= HGRN_LAYER)


HGRN_LAYER = 2


def _lb_bwd(logits, dlb):
    L, D = logits.shape

    def body(l_ref, d_ref, o_ref):
        z = l_ref[...]
        e = jnp.exp(z - jnp.max(z, axis=0, keepdims=True))
        p = e / jnp.sum(e, axis=0, keepdims=True)
        dp = jnp.where(_lb_rows(z.shape), d_ref[...], 0.0)
        o_ref[...] = p * (dp - jnp.sum(p * dp, axis=0, keepdims=True))

    full = pl.BlockSpec((L, D), lambda i: (0, 0))
    return _call(body, "hgrn_dlb", (1,), [full, pl.BlockSpec((1, D), lambda i: (0, 0))], full, _sds((L, D), F32))(logits, dlb)


def _hgrn_gates(qr, fr, lbv):
    e = jnp.exp(-jnp.abs(fr))
    big, small = 1.0 / (1.0 + e), e / (1.0 + e)
    sf = jnp.where(fr >= 0, big, small)
    snf = jnp.where(fr >= 0, small, big)
    f = lbv + (1.0 - lbv) * sf
    sq = _sig(qr)
    return qr * sq, (1.0 - lbv) * snf, jnp.log(f), sf, snf, f, sq


def _hgrn_intra(G, q, kk, g_scr, q_scr):
    C = HGRN_CHUNK
    g_scr[...] = G
    q_scr[...] = q
    srow = lax.broadcasted_iota(jnp.int32, (C, LANES), 0)
    lane = lax.broadcasted_iota(jnp.int32, (C, LANES), 1)
    at = jnp.zeros((C, LANES), F32)
    for t in range(C):
        e = jnp.where(srow <= t, jnp.exp(g_scr[pl.ds(t, 1), :] - G), 0.0)
        col = jnp.sum(e * kk * q_scr[pl.ds(t, 1), :], axis=-1, keepdims=True)
        at = jnp.where(lane == t, col, at)
    return at


def _hgrn_fwd(proj3, lb, ng):
    B, S, D4 = proj3.shape
    D = D4 // 4
    H = D // HGRN_EXPAND
    C = HGRN_CHUNK
    R = _tile(S, 256, 2 * C)
    ncb = R // C
    dk = HGRN_EXPAND

    def body(q_ref, f_ref, i_ref, go_ref, lb_ref, ng_ref, y_ref, o_ref, st_ref, st, g_scr, q_scr):
        @pl.when(pl.program_id(2) == 0)
        def _():
            st[...] = jnp.zeros((dk, dk), F32)

        lbv = lb_ref[...]

        def chunk(c, slot):
            r0 = pl.multiple_of(c * C, C)
            rows = pl.ds(r0, C)
            q, kk, lf, *_ = _hgrn_gates(q_ref[rows, :], f_ref[rows, :], lbv)
            vv = i_ref[rows, :]
            G = _scan_rows(lf)
            at = _hgrn_intra(G, q, kk, g_scr.at[slot], q_scr.at[slot])
            gl = g_scr[slot, pl.ds(C - 1, 1), :]
            stv = st[...]
            st_ref[c] = stv
            o = _dot(q * jnp.exp(G), stv, "nt") + _dot(at, vv, "tn")[:C]
            st[...] = stv * jnp.exp(gl) + _dot(vv, kk * jnp.exp(gl - G), "tn")
            o_ref[rows, :] = o
            gv = go_ref[rows, :]
            y = o * lax.rsqrt(jnp.mean(o * o, axis=-1, keepdims=True) + RMS_EPS) * ng_ref[...] * (gv * _sig(gv))
            y_ref[rows, :] = y.astype(BF16)

        def pair(c2, carry):
            chunk(2 * c2, 0)
            chunk(2 * c2 + 1, 1)
            return carry

        lax.fori_loop(0, ncb // 2, pair, 0)

    def col(k):
        return pl.BlockSpec((None, R, dk), lambda b, h, i: (b, i, h + k * H))

    vec = pl.BlockSpec((1, dk), lambda b, h, i: (0, h))
    out = pl.BlockSpec((None, R, dk), lambda b, h, i: (b, i, h))
    return _call(body, "hgrn_fwd", (B, H, S // R), [col(0), col(1), col(2), col(3), vec, vec],
                 [out, out, pl.BlockSpec((None, None, ncb, dk, dk), lambda b, h, i: (b, h, i, 0, 0))],
                 [_sds((B, S, D), BF16), _sds((B, S, D), F32), _sds((B, H, S // C, dk, dk), F32)],
                 [pltpu.VMEM((dk, dk), F32), pltpu.VMEM((2, C, dk), F32), pltpu.VMEM((2, C, dk), F32)],
                 sem=("parallel", "parallel", "arbitrary"))(proj3, proj3, proj3, proj3, lb, ng)


def _hgrn_bwd(proj3, o3, dy3, states, lb, ng, carried=None):
    B, S, D4 = proj3.shape
    D = D4 // 4
    H = D // HGRN_EXPAND
    C = HGRN_CHUNK
    R = _tile(S, 256, 2 * C)
    ncb = R // C
    nb = S // R
    dk = HGRN_EXPAND

    def body(q_ref, f_ref, i_ref, go_ref, o_ref, dy_ref, st_ref, lb_ref, ng_ref,
             dp_ref, red_ref, dst, g_scr, q_scr, dq_scr, acc):
        b, i = pl.program_id(1), pl.program_id(2)

        @pl.when(i == 0)
        def _():
            dst[...] = jnp.zeros((dk, dk), F32)

        @pl.when(jnp.logical_and(b == 0, i == 0))
        def _():
            acc[...] = jnp.zeros((2, dk), F32)

        lbv = lb_ref[...]
        ngv = ng_ref[...]
        srow = lax.broadcasted_iota(jnp.int32, (C, LANES), 0)
        lane = lax.broadcasted_iota(jnp.int32, (C, LANES), 1)

        def chunk(c, slot):
            r0 = pl.multiple_of(c * C, C)
            rows = pl.ds(r0, C)
            qr, fr, vv, gv = q_ref[rows, :], f_ref[rows, :], i_ref[rows, :], go_ref[rows, :]
            q, kk, lf, sf, snf, f, sq = _hgrn_gates(qr, fr, lbv)
            o = o_ref[rows, :]
            dy = dy_ref[rows, :]
            rinv = lax.rsqrt(jnp.mean(o * o, axis=-1, keepdims=True) + RMS_EPS)
            on = o * rinv
            sgv = _sig(gv)
            dz = dy * (gv * sgv)
            dp_ref[3, rows, :] = (dy * on * ngv * (sgv * (1.0 + gv * (1.0 - sgv)))).astype(BF16)
            acc[pl.ds(1, 1), :] += jnp.sum(dz * on, axis=0, keepdims=True)
            don = dz * ngv
            do = rinv * (don - on * jnp.mean(don * on, axis=-1, keepdims=True))
            G = _scan_rows(lf)
            g_scr[slot] = G
            q_scr[slot] = q
            gl = g_scr[slot, pl.ds(C - 1, 1), :]
            egl = jnp.exp(gl)
            eG = jnp.exp(G)
            eK = jnp.exp(gl - G)
            qg, kg = q * eG, kk * eK
            stv = st_ref[c]
            dsv = dst[...]
            dqg = _dot(do, stv)
            do_pad = jnp.concatenate([do, jnp.zeros((LANES - C, dk), F32)], axis=0)
            dat = _dot(vv, do_pad, "nt")
            dkg = _dot(vv, dsv)
            dgl = egl * jnp.sum(stv * dsv, axis=0, keepdims=True) + jnp.sum(dkg * kg, axis=0, keepdims=True)
            at = jnp.zeros((C, LANES), F32)
            dki = jnp.zeros((C, dk), F32)
            for t in range(C):
                e = jnp.where(srow <= t, jnp.exp(g_scr[slot, pl.ds(t, 1), :] - G), 0.0)
                qt = q_scr[slot, pl.ds(t, 1), :]
                at = jnp.where(lane == t, jnp.sum(e * kk * qt, axis=-1, keepdims=True), at)
                z = e * jnp.sum(jnp.where(lane == t, dat, 0.0), axis=-1, keepdims=True)
                dq_scr[slot, pl.ds(t, 1), :] = jnp.sum(z * kk, axis=0, keepdims=True)
                dki = dki + z * qt
            dqi = dq_scr[slot]
            dp_ref[2, rows, :] = (_dot(at, do_pad) + _dot(kg, dsv, "nt")).astype(BF16)
            dst[...] = dsv * egl + _dot(do, qg, "tn")
            dq = dqg * eG + dqi
            dkk = dkg * eK + dki
            dG = dqg * qg - dkg * kg + q * dqi - kk * dki
            dG = dG + jnp.where(srow == C - 1, dgl, 0.0)
            dlf = _scan_rows(dG, reverse=True)
            dsf = (1.0 - lbv) * sf * snf
            dp_ref[1, rows, :] = (dlf * dsf / f - dkk * dsf).astype(BF16)
            acc[pl.ds(0, 1), :] += jnp.sum(dlf * snf / f - dkk * snf, axis=0, keepdims=True)
            dp_ref[0, rows, :] = (dq * (sq * (1.0 + qr * (1.0 - sq)))).astype(BF16)

        def pair(c2, carry):
            chunk(ncb - 1 - 2 * c2, 0)
            chunk(ncb - 2 - 2 * c2, 1)
            return carry

        lax.fori_loop(0, ncb // 2, pair, 0)

        @pl.when(jnp.logical_and(b == B - 1, i == nb - 1))
        def _():
            red_ref[0] = acc[pl.ds(0, 1), :]
            red_ref[1] = acc[pl.ds(1, 1), :]

    def col(k):
        return pl.BlockSpec((None, R, dk), lambda h, b, i: (b, nb - 1 - i, h + k * H))

    vec = pl.BlockSpec((1, dk), lambda h, b, i: (0, h))
    row = pl.BlockSpec((None, R, dk), lambda h, b, i: (b, nb - 1 - i, h))
    stsp = pl.BlockSpec((None, None, ncb, dk, dk), lambda h, b, i: (b, h, nb - 1 - i, 0, 0))
    specs = ((H, B, nb), [col(0), col(1), col(2), col(3), row, row, stsp, vec, vec],
             [pl.BlockSpec((4, None, R, dk), lambda h, b, i: (0, b, nb - 1 - i, h)),
              pl.BlockSpec((2, 1, dk), lambda h, b, i: (0, 0, h))],
             [_sds((4, B, S, D), BF16), _sds((2, 1, D), F32)],
             [pltpu.VMEM((dk, dk), F32), pltpu.VMEM((2, C, dk), F32), pltpu.VMEM((2, C, dk), F32),
              pltpu.VMEM((2, C, dk), F32), pltpu.VMEM((2, dk), F32)])
    args = (proj3, proj3, proj3, proj3, o3, dy3, states, lb, ng)
    if carried is None:
        return _call(body, "hgrn_bwd", *specs, sem=("parallel", "arbitrary", "arbitrary"))(*args)
    dp, red, *got = _carry_call(body, "hgrn_bwd", *specs, "scatter", carried)(*args, *carried)
    return dp, red, got


def _hgrn_mixer_fwd(x, B, p):
    T, D = x.shape
    S = T // B
    h = _rms_fwd("hgrn_rms", x, p["mix_norm"][2:3])
    proj = _mm_nn("hgrn_in", h, p["hgrn_w_in"], F32)
    lb = _lb_fwd(p["hgrn_lb_logits"])
    y, o, states = _hgrn_fwd(proj.reshape(B, S, 4 * D), lb, p["hgrn_norm"])
    y = y.reshape(T, D)
    return _mm_res("hgrn_out", y, p["hgrn_w_out"], x, 1.0), (x, h, proj, lb, y, o, states)


def _hgrn_mixer_bwd(dxp, saved, B, p):
    x, h, proj, lb, y, o, states = saved
    T, D = x.shape
    S = T // B
    dx, dxb = dxp
    g = {}
    dy = _mm_nt("hgrn_dy", [(dxb, p["hgrn_w_out"])], F32)
    g["hgrn_w_out"] = _mm_tn("hgrn_dwout", y, dxb)[None]
    if "scatter" in p:
        dp, red, p["scattered"] = _hgrn_bwd(proj.reshape(B, S, 4 * D), o, dy.reshape(B, S, D), states, lb, p["hgrn_norm"],
                                           p.pop("scatter"))
    else:
        dp, red = _hgrn_bwd(proj.reshape(B, S, 4 * D), o, dy.reshape(B, S, D), states, lb, p["hgrn_norm"])
    dp = dp.reshape(4, T, D)
    g["hgrn_norm"] = red[1]
    g["hgrn_lb_logits"] = _lb_bwd(p["hgrn_lb_logits"], red[0])
    dwin = _mm_tn3("hgrn_dwin", h, dp)
    g["hgrn_w_in"] = jnp.moveaxis(dwin, 0, 1).reshape(1, D, 4 * D)
    dh = _mm_nt3("hgrn_dh", dp, p["hgrn_w_in4"])
    dx2, dg = _rms_bwd("hgrn_drms", dh, x, p["mix_norm"][2:3], dx)
    return dx2, dg, g


POOL_HALO = 16


def _pool_fwd(x3, g):
    B, S, D = x3.shape
    tt = _tile(S, 256, POOL_HALO)
    hb = tt // POOL_HALO
    G = D // len(POOL_WINDOWS)

    def body(x_ref, halo_ref, g_ref, m_ref):
        i = pl.program_id(1)

        def norm(xv):
            return xv * lax.rsqrt(jnp.mean(xv * xv, axis=-1, keepdims=True) + RMS_EPS) * g_ref[...]

        hm = norm(x_ref[...])
        ext = jnp.concatenate([jnp.where(i > 0, norm(halo_ref[...]), 0.0), hm], axis=0)
        pos = (i * tt + lax.broadcasted_iota(jnp.int32, (tt, 1), 0) + 1).astype(F32)
        for gi, win in enumerate(POOL_WINDOWS):
            s = ext[:, gi * G:(gi + 1) * G]
            w = 1
            while w < win:
                s = s + _roll(s, w)
                w *= 2
            m_ref[:, gi * G:(gi + 1) * G] = (s[POOL_HALO:] / jnp.minimum(pos, float(win)) - hm[:, gi * G:(gi + 1) * G]).astype(BF16)

    main = pl.BlockSpec((None, tt, D), lambda b, i: (b, i, 0))
    halo = pl.BlockSpec((None, POOL_HALO, D), lambda b, i: (b, jnp.maximum(i * hb - 1, 0), 0))
    return _call(body, "pool_fwd", (B, S // tt), [main, halo, pl.BlockSpec((1, D), lambda b, i: (0, 0))], main,
                 _sds((B, S, D), BF16), sem=("parallel", "parallel"))(x3, x3, g)


def _pool_bwd(dm3):
    B, S, D = dm3.shape
    tt = _tile(S, 256, POOL_HALO)
    hb = tt // POOL_HALO
    nt = S // tt
    G = D // len(POOL_WINDOWS)
    L = tt + POOL_HALO

    def body(dm_ref, nxt_ref, dh_ref):
        i = pl.program_id(1)
        posm = (i * tt + lax.broadcasted_iota(jnp.int32, (tt, 1), 0) + 1).astype(F32)
        posn = ((i + 1) * tt + lax.broadcasted_iota(jnp.int32, (POOL_HALO, 1), 0) + 1).astype(F32)
        for gi, win in enumerate(POOL_WINDOWS):
            sl = slice(gi * G, (gi + 1) * G)
            dm = dm_ref[:, sl]
            s = jnp.concatenate([dm / jnp.minimum(posm, float(win)),
                                 jnp.where(i < nt - 1, nxt_ref[:, sl] / jnp.minimum(posn, float(win)), 0.0)], axis=0)
            w = 1
            while w < win:
                s = s + _roll(s, L - w)
                w *= 2
            dh_ref[:, sl] = s[:tt] - dm

    main = pl.BlockSpec((None, tt, D), lambda b, i: (b, i, 0))
    nxt = pl.BlockSpec((None, POOL_HALO, D), lambda b, i: (b, jnp.minimum((i + 1) * hb, S // POOL_HALO - 1), 0))
    return _call(body, "pool_bwd", (B, nt), [main, nxt], main, _sds((B, S, D), F32), sem=("parallel", "parallel"))(dm3, dm3)


def _pool_mixer_fwd(x, B, p):
    T, D = x.shape
    NG = len(POOL_WINDOWS)
    G = D // NG
    tm = _tile(T, 512)
    m = _pool_fwd(x.reshape(B, T // B, D), p["mix_norm"][3:4]).reshape(T, D)

    def epi(res, ex, outs):
        outs[0][...] = ex[1][...] + res * ex[0][...]

    blk = pl.BlockSpec((tm, G), lambda i, g: (i, g))
    x2 = _mm("pool_out", "nn", [(m, p["pool_w4"])], [(blk, pl.BlockSpec((None, G, G), lambda i, g: (g, 0, 0)))],
             (T // tm, NG), None, [_sds((T, D), F32)], [blk], extras=[p["pool_scale"], x],
             extra_specs=[pl.BlockSpec((1, G), lambda i, g: (0, g)), blk], epilogue=epi)[0]
    return x2, (x, m)


def _pool_mixer_bwd(dxp, saved, B, p):
    x, m = saved
    dx, dxb = dxp
    T, D = x.shape
    NG = len(POOL_WINDOWS)
    G = D // NG
    tm = _tile(T, 512)
    g = {}

    def epi(zz, ex, outs):
        dy = ex[0][...]
        outs[0][...] = (dy * ex[1][...]).astype(BF16)
        part = jnp.sum(dy * zz, axis=0, keepdims=True)

        @pl.when(pl.program_id(1) == 0)
        def _():
            outs[1][...] = part

        @pl.when(pl.program_id(1) > 0)
        def _():
            outs[1][...] += part

    blk = pl.BlockSpec((tm, G), lambda g_, i: (i, g_))
    wsp = pl.BlockSpec((None, G, G), lambda g_, i: (g_, 0, 0))
    vec = pl.BlockSpec((1, G), lambda g_, i: (0, g_))
    dz, dsc = _mm("pool_dz", "nn", [(m, p["pool_w4"])], [(blk, wsp)], (NG, T // tm), None,
                  [_sds((T, D), BF16), _sds((1, D), F32)], [blk, vec], extras=[dx, p["pool_scale"]], extra_specs=[blk, vec],
                  epilogue=epi, sem=("parallel", "arbitrary"))
    g["pool_scale"] = dsc
    tk = _tile(T, 512)
    kb = pl.BlockSpec((tk, G), lambda g_, k: (k, g_))
    g["pool_w"] = _mm("pool_dw", "tn", [(m, dz)], [(kb, kb)], (NG, T // tk), 1, [_sds((NG, G, G), F32)],
                      [pl.BlockSpec((None, G, G), lambda g_, k: (g_, 0, 0))], acc_shape=(G, G))[0][None]
    blk2 = pl.BlockSpec((tm, G), lambda i, g_: (i, g_))
    dm = _mm("pool_dm", "nt", [(dz, p["pool_w4"])], [(blk2, pl.BlockSpec((None, G, G), lambda i, g_: (g_, 0, 0)))],
             (T // tm, NG), None, [_sds((T, D), F32)], [blk2])[0]
    dh = _pool_bwd(dm.reshape(B, T // B, D)).reshape(T, D)
    dx2, dg = _rms_bwd("pool_drms", dh, x, p["mix_norm"][3:4], dx)
    return dx2, dg, g


_MIXERS = ((_conv_mixer_fwd, _conv_mixer_bwd), (_fox_mixer_fwd, _fox_mixer_bwd), (_hgrn_mixer_fwd, _hgrn_mixer_bwd),
           (_pool_mixer_fwd, _pool_mixer_bwd))


EARLY_LAYERS = 2
LATE = ("ffn_w_gate", "ffn_w_up", "ffn_w_down", "hgrn_w_in", "hgrn_w_out", "pool_w")
MIXER_MATRICES = (("conv_w_in", "conv_w_out"), ("fox_w_in", "fox_w_out"), ("hgrn_w_in", "hgrn_w_out"), ("pool_w",))


def _local_step(x3, tgt3, w, plan=None):
    B, S, D = x3.shape
    T = B * S
    depth = w["ffn_norm"].shape[0]
    H = FOX_HEADS
    p = dict(w)
    p["conv_w_in2"] = w["conv_w_in"][0].reshape(D, 2, D).transpose(1, 0, 2)
    p["conv_b_in2"] = w["conv_b_in"].reshape(2, 1, D)
    p["conv_dw32"] = jnp.pad(w["conv_dw"][0], ((0, HALO - CONV_TAPS), (0, 0)))
    p["conv_w_out"] = w["conv_w_out"][0]
    p["fox_b_f128"] = jnp.pad(w["fox_b_f"], ((0, 0), (0, LANES - H)))
    ffn = {}

    def install(first, d):
        if "ffn_w_gate" in d:
            down = d["ffn_w_down"] if "ffn_w_down" in d else w["ffn_w_down"][first:]
            for k in range(d["ffn_w_gate"].shape[0]):
                ffn[first + k] = (jnp.stack((d["ffn_w_gate"][k], d["ffn_w_up"][k]), axis=1), down[k])
        if "fox_w_in" in d:
            p["fox_w_qkv"] = d["fox_w_in"][0][:, :3 * D]
            p["fox_w_f"] = jnp.pad(d["fox_w_in"][0][:, 3 * D:], ((0, 0), (0, LANES - H)))
            p["fox_w_out"] = d["fox_w_out"][0]
        if "hgrn_w_in" in d:
            p["hgrn_w_in"] = d["hgrn_w_in"][0]
            p["hgrn_w_in4"] = d["hgrn_w_in"][0].reshape(D, 4, D).transpose(1, 0, 2)
            p["hgrn_w_out"] = d["hgrn_w_out"][0]
            p["pool_w4"] = d["pool_w"][0]

    install(0, w)
    beside = {} if plan is None else plan["beside_ffn"]
    if plan is not None:
        p["late_gather"] = plan["late_shards"]

    def ffn_fwd(i, half, x):
        tag = f"{i}{'ab'[half]}"
        args = (tag, x, w["ffn_norm"][i, half:half + 1], ffn[i][0][half], ffn[i][1][half])
        if tag not in beside:
            return _ffn_fwd(*args)
        shards, first, assemble = beside[tag]
        x, s, got = _ffn_fwd(*args, carried=shards)
        install(first, assemble(got))
        return x, s

    x = x3.reshape(T, D)
    saved = []
    for i in range(depth):
        x, s0 = ffn_fwd(i, 0, x)
        x, s1 = _MIXERS[i % 4][0](x, B, p)
        if "late_gathered" in p:
            install(EARLY_LAYERS, plan["assemble"](p.pop("late_gathered")))
        x, s2 = ffn_fwd(i, 1, x)
        saved.append((s0, s1, s2))
    loss, dx, dfinal = _loss_head(x, w["final_norm"].reshape(1, D), tgt3.reshape(T, D))

    g = {"final_norm": dfinal}
    dffn_norm = [[None, None] for _ in range(depth)]
    dwgu = [[None, None] for _ in range(depth)]
    dwd = [[None, None] for _ in range(depth)]
    dmix = [None] * depth
    travelled = []

    def layer_grads(i, extra):
        gate, up = (jnp.stack([dwgu[i][0][s], dwgu[i][1][s]])[None] for s in (0, 1))
        return [gate, up, jnp.stack(dwd[i])[None]] + [g.pop(n) for n in extra]

    for i in reversed(range(depth)):
        s0, s1, s2 = saved[i]
        dx, dffn_norm[i][1], dwgu[i][1], dwd[i][1] = _ffn_bwd(f"{i}b", dx, s2, w["ffn_norm"][i, 1:2], ffn[i][0][1], ffn[i][1][1])
        if plan is not None and i < depth - 1:
            extra = MIXER_MATRICES[i + 1] if i > 0 else ()
            names = LATE[:3] + extra
            p["scatter"] = plan["dev_major"](names, layer_grads(i + 1, extra))
            travelled.append([names, p["scatter"]])
        dx, dmix[i], gm = _MIXERS[i % 4][1](dx, s1, B, p)
        if "scattered" in p:
            travelled[-1].append(p.pop("scattered"))
        g.update(gm)
        dx, dffn_norm[i][0], dwgu[i][0], dwd[i][0] = _ffn_bwd(f"{i}a", dx, s0, w["ffn_norm"][i, 0:1], ffn[i][0][0], ffn[i][1][0])
    kept = range(depth) if plan is None else range(1)
    g["ffn_norm"] = jnp.stack([jnp.stack([a[0], b[0]]) for a, b in dffn_norm])
    g["ffn_w_gate"] = jnp.stack([jnp.stack([dwgu[i][0][0], dwgu[i][1][0]]) for i in kept])
    g["ffn_w_up"] = jnp.stack([jnp.stack([dwgu[i][0][1], dwgu[i][1][1]]) for i in kept])
    g["ffn_w_down"] = jnp.stack([jnp.stack(dwd[i]) for i in kept])
    g["mix_norm"] = jnp.concatenate(dmix, axis=0)
    g = {n: (t.reshape(w[n].shape) if n in w and n not in LATE else t) for n, t in g.items()}
    return loss, dx[0].reshape(B, S, D), g, travelled


def _adamw(name, w, m, v, parts):
    shape = w.shape
    cols = shape[-1]
    rows = w.size // cols
    tr = _tile(rows, max(8, (1 << 19) // cols))
    n = len(parts)
    c1 = 1.0 - ADAM_B1 ** ADAM_STEP
    c2 = 1.0 - ADAM_B2 ** ADAM_STEP

    def body(*refs):
        w_ref, m_ref, v_ref = refs[:3]
        g_ref, d_ref, m2_ref, v2_ref = refs[3 + n:]
        g = refs[3][...].astype(F32)
        for k in range(1, n):
            g = g + refs[3 + k][...].astype(F32)
        m2 = ADAM_B1 * m_ref[...] + (1.0 - ADAM_B1) * g
        v2 = ADAM_B2 * v_ref[...] + (1.0 - ADAM_B2) * (g * g)
        g_ref[...] = g
        m2_ref[...] = m2
        v2_ref[...] = v2
        d_ref[...] = -ADAM_LR * ((m2 / c1) / (jnp.sqrt(v2 / c2) + ADAM_EPS) + ADAM_WD * w_ref[...])

    blk = pl.BlockSpec((tr, cols), lambda i: (i, 0))
    outs = _call(body, name, (rows // tr,), [blk] * (3 + n), [blk] * 4, [_sds((rows, cols), F32)] * 4, sem=("parallel",))(
        *[t.reshape(rows, cols) for t in (w, m, v, *parts)])
    return [o.reshape(shape) for o in outs]


ANY = pl.BlockSpec(memory_space=pl.ANY)
FLAT_COLS = 1024


def _place():
    return lax.axis_index("x"), lax.axis_index("y"), lax.axis_index("c")


def _all_gather(name, xs):
    K = len(xs)

    def body(*refs):
        x_refs, out_refs = refs[:K], refs[K:2 * K]
        send_sems, recv_sems, local_sems = refs[2 * K:]
        xi, yi, ci = _place()
        me, sibling = (xi, yi, ci), (xi, yi, 1 - ci)
        chips = [(1 - xi, yi), (xi, 1 - yi), (1 - xi, 1 - yi)]

        def slot(a, px, py, pc):
            return out_refs[a].at[4 * px + 2 * py + pc]

        def copy(a, k, block, to, own=False):
            return pltpu.make_async_remote_copy(src_ref=x_refs[a] if own else slot(a, *block), dst_ref=slot(a, *block),
                                                send_sem=send_sems.at[7 * a + k], recv_sem=recv_sems.at[7 * a + k],
                                                device_id=to, device_id_type=MESH)

        mine = [pltpu.make_async_copy(x_refs[a], slot(a, *me), local_sems.at[a]) for a in range(K)]
        first = [copy(a, 1 + j, me, (*chip, ci), own=True) for j, chip in enumerate(chips) for a in range(K)]
        first += [copy(a, 0, me, sibling, own=True) for a in range(K)]
        for cp in mine + first:
            cp.start()
        passed = []
        for j, chip in enumerate(chips):
            for a in range(K):
                copy(a, 1 + j, (*chip, ci), me).wait_recv()
                passed.append(copy(a, 4 + j, (*chip, ci), sibling))
                passed[-1].start()
        for a in range(K):
            copy(a, 0, sibling, me).wait_recv()
            for j, chip in enumerate(chips):
                copy(a, 4 + j, (*chip, 1 - ci), me).wait_recv()
        for cp in first + passed:
            cp.wait_send()
        for cp in mine:
            cp.wait()

    return pl.pallas_call(body, name=name, out_shape=[_sds((N_DEV,) + x.shape, x.dtype) for x in xs], in_specs=[ANY] * K,
                          out_specs=[ANY] * K,
                          scratch_shapes=[pltpu.SemaphoreType.DMA((7 * K,)), pltpu.SemaphoreType.DMA((7 * K,)),
                                          pltpu.SemaphoreType.DMA((K,))])(*xs)


def _swap_sibling(name, ts):
    K = len(ts)

    def body(*refs):
        t_refs, out_refs, send_sems, recv_sems = refs[:K], refs[K:2 * K], refs[2 * K], refs[2 * K + 1]
        xi, yi, ci = _place()
        cps = [pltpu.make_async_remote_copy(src_ref=t_refs[a], dst_ref=out_refs[a], send_sem=send_sems.at[a],
                                            recv_sem=recv_sems.at[a], device_id=(xi, yi, 1 - ci), device_id_type=MESH)
               for a in range(K)]
        for cp in cps:
            cp.start()
        for cp in cps:
            cp.wait()

    return pl.pallas_call(body, name=name, out_shape=[_sds(t.shape, t.dtype) for t in ts], in_specs=[ANY] * K,
                          out_specs=[ANY] * K,
                          scratch_shapes=[pltpu.SemaphoreType.DMA((K,)), pltpu.SemaphoreType.DMA((K,))])(*ts)


def _scatter_chips(name, ts):
    K = len(ts)

    def body(*refs):
        t_refs, out_refs, send_sems, recv_sems = refs[:K], refs[K:2 * K], refs[2 * K], refs[2 * K + 1]
        xi, yi, ci = _place()
        chips = [(1 - xi, yi), (xi, 1 - yi), (1 - xi, 1 - yi)]
        cps = [pltpu.make_async_remote_copy(src_ref=t_refs[a].at[2 * cx + cy], dst_ref=out_refs[a].at[j],
                                            send_sem=send_sems.at[3 * a + j], recv_sem=recv_sems.at[3 * a + j],
                                            device_id=(cx, cy, ci), device_id_type=MESH)
               for j, (cx, cy) in enumerate(chips) for a in range(K)]
        for cp in cps:
            cp.start()
        for cp in cps:
            cp.wait()

    return pl.pallas_call(body, name=name, out_shape=[_sds((3,) + t.shape[1:], t.dtype) for t in ts], in_specs=[ANY] * K,
                          out_specs=[ANY] * K,
                          scratch_shapes=[pltpu.SemaphoreType.DMA((3 * K,)), pltpu.SemaphoreType.DMA((3 * K,))])(*ts)


def _add_bf16(name, a, b):
    shape = a.shape
    N, C = shape[0], shape[-1]
    R = a.size // (N * C)
    tr = _tile(R, max(8, (1 << 19) // C))

    def body(a_ref, b_ref, o_ref):
        o_ref[...] = (a_ref[...].astype(F32) + b_ref[...].astype(F32)).astype(BF16)

    blk = pl.BlockSpec((None, tr, C), lambda n, i: (n, i, 0))
    return _call(body, name, (N, R // tr), [blk, blk], blk, _sds((N, R, C), BF16), sem=("parallel", "parallel"))(
        a.reshape(N, R, C), b.reshape(N, R, C)).reshape(shape)


def _sum_parts(name, parts):
    N, C = parts.shape

    def body(p_ref, o_ref):
        s = p_ref[pl.ds(0, 1), :]
        for d in range(1, N):
            s = s + p_ref[pl.ds(d, 1), :]
        o_ref[...] = s

    return _call(body, name, (1,), [pl.BlockSpec((N, C), lambda i: (0, 0))], pl.BlockSpec((1, C), lambda i: (0, 0)),
                 _sds((1, C), F32))(parts)


def _flat(parts, dtype, lead=()):
    flat = jnp.concatenate([t.reshape(lead + (-1,)).astype(dtype) for t in parts], axis=-1)
    n = flat.shape[-1]
    unit = 16 * FLAT_COLS
    padded = -(-n // unit) * unit
    flat = jnp.pad(flat, [(0, 0)] * len(lead) + [(0, padded - n)])
    return flat.reshape(lead + (padded // FLAT_COLS, FLAT_COLS))


def _unflat(flat, shapes, lead=()):
    flat = flat.reshape(lead + (-1,))
    out, off = [], 0
    for shp in shapes:
        n = math.prod(shp)
        out.append(flat[..., off:off + n].reshape(lead + tuple(shp)))
        off += n
    return out


def _dev_major(full, ax):
    shp = full.shape
    return jnp.moveaxis(full.reshape(shp[:ax] + (N_DEV, shp[ax] // N_DEV) + shp[ax + 1:]), ax, 0)


def _from_dev_major(blocks, ax):
    t = jnp.moveaxis(blocks, 0, ax)
    shp = t.shape
    return t.reshape(shp[:ax] + (shp[ax] * shp[ax + 1],) + shp[ax + 2:])


def kernel(x, *rest):
    nw = len(WEIGHTS)
    w = dict(zip(WEIGHTS, rest[:nw]))
    tgt = rest[nw]
    m = dict(zip(WEIGHTS, rest[nw + 1:2 * nw + 1]))
    v = dict(zip(WEIGHTS, rest[2 * nw + 1:3 * nw + 1]))
    xi, yi, ci = _place()
    dev = 4 * xi + 2 * yi + ci

    ffn3 = LATE[:3]
    first = ffn3 + MIXER_MATRICES[0]
    now = {"ffn_w_gate": 1, "ffn_w_up": 1, "ffn_w_down": EARLY_LAYERS}
    shard = {n: (w[n][:now[n]] if n in ffn3 else w[n]).astype(BF16) for n in first}
    late_shard = {n: (w[n][EARLY_LAYERS:] if n in ffn3 else w[n]).astype(BF16) for n in LATE}

    def joined(names):
        return lambda got: {n: _from_dev_major(t, SHARDED[n]) for n, t in zip(names, got)}

    big = _all_gather("gather_matrices", [shard[n] for n in first])
    small = _all_gather("gather_vectors", [_flat([w[n] for n in SMALL], F32)])[0]
    full = {n: w[n] for n in REPL}
    for n, blocks in zip(first, big):
        full[n] = _from_dev_major(blocks, SHARDED[n])
    for n, blocks in zip(SMALL, _unflat(small, [w[n].shape for n in SMALL], (N_DEV,))):
        full[n] = _from_dev_major(blocks, SHARDED[n])

    def by_owner(names, grads):
        return [_dev_major(t.reshape(t.shape[:1] + w[n].shape[1:SHARDED[n]] + (-1,) + w[n].shape[SHARDED[n] + 1:]),
                           SHARDED[n]).astype(BF16) for n, t in zip(names, grads)]

    beside = {"0a": ([w[n][1:EARLY_LAYERS].astype(BF16) for n in ffn3[:2]], 1, joined(ffn3[:2])),
              "0b": ([w[n].astype(BF16) for n in MIXER_MATRICES[1]], None, joined(MIXER_MATRICES[1]))}
    plan = dict(late_shards=[late_shard[n] for n in LATE], dev_major=by_owner, assemble=joined(LATE), beside_ffn=beside)
    loss, gx, g, travelled = _local_step(x, tgt, full, plan)

    vec_names = SMALL + REPL
    vec = _all_gather("gather_vector_grads", [_flat([g[n] for n in vec_names] + [loss], F32)])[0]
    *vec_list, losses = _unflat(vec, [g[n].shape for n in vec_names] + [(1, 1)], (N_DEV,))
    vec_parts = dict(zip(vec_names, vec_list))
    loss = _sum_parts("loss_sum", losses.reshape(N_DEV, 1))[0, 0]

    last = ffn3 + MIXER_MATRICES[0] + MIXER_MATRICES[1]
    keep, send = [], []
    for gd in by_owner(last, [g[n] for n in last]):
        gd = gd.reshape((4, 2) + gd.shape[1:])
        keep.append(lax.dynamic_index_in_dim(gd, ci, 1, keepdims=False))
        send.append(lax.dynamic_index_in_dim(gd, 1 - ci, 1, keepdims=False))
    got_sib = _swap_sibling("grads_to_sibling", send)
    pair = [_add_bf16(f"grad_pair_sum_{n}", a, b) for n, a, b in zip(last, keep, got_sib)]
    got_chips = _scatter_chips("grads_to_chips", pair)
    chip = 2 * xi + yi
    parts = {}
    for k, n in enumerate(last):
        parts[n] = [[lax.dynamic_index_in_dim(keep[k], chip, 0, keepdims=False),
                     lax.dynamic_index_in_dim(got_sib[k], chip, 0, keepdims=False), got_chips[k][0], got_chips[k][1], got_chips[k][2]]]
    for names, sent, got in reversed(travelled):
        for n, s, t in zip(names, sent, got):
            parts.setdefault(n, []).append([lax.dynamic_index_in_dim(s, dev, 0, keepdims=False)] + [t[k] for k in range(len(FLIPS))])

    res = {}
    for n in BIG:
        runs = parts[n]
        count = max(len(r) for r in runs)
        whole = [jnp.concatenate([r[k] if k < len(r) else jnp.zeros_like(r[0]) for r in runs], axis=0) if len(runs) > 1 else runs[0][k]
                 for k in range(count)]
        res[n] = _adamw(f"adamw_{n}", w[n], m[n], v[n], whole)
    for n in vec_names:
        parts = vec_parts[n]
        if n in SHARDED:
            ax = SHARDED[n]
            parts = lax.dynamic_slice_in_dim(parts, dev * w[n].shape[ax], w[n].shape[ax], ax + 1)
        res[n] = _adamw(f"adamw_{n}", w[n], m[n], v[n], [parts[d] for d in range(N_DEV)])
    return (loss, gx, *[res[n][0] for n in WEIGHTS], *[res[n][1] for n in WEIGHTS], *[res[n][2] for n in WEIGHTS],
            *[res[n][3] for n in WEIGHTS])
```

```python
import functools
import math

import jax
import jax.numpy as jnp
from jax import lax
from jax.experimental import pallas as pl
from jax.experimental.pallas import tpu as pltpu

F32 = jnp.float32
BF16 = jnp.bfloat16
MESH = pl.DeviceIdType.MESH

N_DEV = 8
RMS_EPS = 1e-6
LN_EPS = 1e-5
FOX_HEADS = 16
HGRN_EXPAND = 128
HGRN_CHUNK = 32
POOL_WINDOWS = (2, 4, 8, 16)
ADAM_LR, ADAM_B1, ADAM_B2, ADAM_EPS, ADAM_WD, ADAM_STEP = 0.001, 0.9, 0.999, 1e-08, 0.01, 10
LANES = 128
VMEM_LIMIT_MB = 48

SHARDED = dict(
    ffn_norm=2, ffn_w_gate=3, ffn_w_up=3, ffn_w_down=2, conv_w_in=2, conv_dw=2, conv_w_out=1, fox_w_in=2, fox_w_out=1,
    hgrn_w_in=2, hgrn_norm=1, hgrn_w_out=1, pool_w=2, pool_scale=1)
BIG = ("ffn_w_gate", "ffn_w_up", "ffn_w_down", "conv_w_in", "conv_w_out", "fox_w_in", "fox_w_out", "hgrn_w_in",
       "hgrn_w_out", "pool_w")
SMALL = ("ffn_norm", "conv_dw", "hgrn_norm", "pool_scale")
REPL = ("mix_norm", "final_norm", "conv_b_in", "conv_dw_b", "conv_ln_g", "conv_ln_b", "fox_b_f", "hgrn_lb_logits")
WEIGHTS = ("ffn_norm", "ffn_w_gate", "ffn_w_up", "ffn_w_down", "mix_norm", "final_norm", "conv_w_in", "conv_b_in",
           "conv_dw", "conv_dw_b", "conv_ln_g", "conv_ln_b", "conv_w_out", "fox_w_in", "fox_b_f", "fox_w_out",
           "hgrn_w_in", "hgrn_lb_logits", "hgrn_norm", "hgrn_w_out", "pool_w", "pool_scale")


def _tile(n, pref, mult=8):
    if n <= pref:
        return n
    for t in range(pref - pref % mult, 0, -mult):
        if n % t == 0:
            return t
    return n


def _sig(x):
    return 1.0 / (1.0 + jnp.exp(-x))


def _call(body, name, grid, in_specs, out_specs, out_shape, scratch=(), sem=None):
    params = dict(vmem_limit_bytes=VMEM_LIMIT_MB << 20)
    if sem is not None:
        params["dimension_semantics"] = sem
    return pl.pallas_call(body, name=name, grid=grid, in_specs=in_specs, out_specs=out_specs, out_shape=out_shape,
                          scratch_shapes=list(scratch), compiler_params=pltpu.CompilerParams(**params))


FLIPS = [(fx, fy, fc) for fx in (0, 1) for fy in (0, 1) for fc in (0, 1)][1:]


def _carry_call(body, name, grid, in_specs, out_specs, out_shape, scratch, kind, arrays):
    K = len(arrays)
    n_in, n_out, n_scr = len(in_specs), len(out_shape), len(scratch)
    n_peer = len(FLIPS)
    if kind == "gather":
        landed = [_sds((N_DEV,) + a.shape, a.dtype) for a in arrays]
    else:
        landed = [_sds((n_peer,) + a.shape[1:], a.dtype) for a in arrays]

    def wrapped(*refs):
        ins, sent = refs[:n_in], refs[n_in:n_in + K]
        outs, got = refs[n_in + K:n_in + K + n_out], refs[n_in + K + n_out:n_in + 2 * K + n_out]
        scr = refs[n_in + 2 * K + n_out:n_in + 2 * K + n_out + n_scr]
        send_sems, recv_sems, local_sems = refs[-3:]
        ids = [pl.program_id(d) for d in range(len(grid))]
        first = functools.reduce(jnp.logical_and, [i == 0 for i in ids])
        last = functools.reduce(jnp.logical_and, [i == n - 1 for i, n in zip(ids, grid)])
        xi, yi, ci = _place()
        me = 4 * xi + 2 * yi + ci

        def copies():
            cps = []
            for a in range(K):
                for k, (fx, fy, fc) in enumerate(FLIPS):
                    px, py, pc = (xi + fx) % 2, (yi + fy) % 2, (ci + fc) % 2
                    if kind == "gather":
                        src, dst = sent[a], got[a].at[me]
                    else:
                        src, dst = sent[a].at[4 * px + 2 * py + pc], got[a].at[k]
                    cps.append(pltpu.make_async_remote_copy(src_ref=src, dst_ref=dst, send_sem=send_sems.at[n_peer * a + k],
                                                            recv_sem=recv_sems.at[n_peer * a + k], device_id=(px, py, pc),
                                                            device_id_type=MESH))
            return cps

        def own():
            return [pltpu.make_async_copy(sent[a], got[a].at[me], local_sems.at[a]) for a in range(K)] if kind == "gather" else []

        @pl.when(first)
        def _():
            for cp in copies() + own():
                cp.start()

        body(*ins, *outs, *scr)

        @pl.when(last)
        def _():
            for cp in copies() + own():
                cp.wait()

    params = dict(vmem_limit_bytes=VMEM_LIMIT_MB << 20, dimension_semantics=("arbitrary",) * len(grid))
    res = pl.pallas_call(wrapped, name=name, grid=grid, in_specs=list(in_specs) + [ANY] * K,
                         out_specs=list(out_specs) + [ANY] * K, out_shape=list(out_shape) + landed,
                         scratch_shapes=list(scratch) + [pltpu.SemaphoreType.DMA((n_peer * K,)), pltpu.SemaphoreType.DMA((n_peer * K,)),
                                                         pltpu.SemaphoreType.DMA((K,))],
                         compiler_params=pltpu.CompilerParams(**params))
    return res


def _sds(shape, dtype):
    return jax.ShapeDtypeStruct(tuple(shape), dtype)


_DN = {"nn": (((1,), (0,)), ((), ())), "nt": (((1,), (1,)), ((), ())), "tn": (((0,), (0,)), ((), ()))}


def _dot(a, b, mode="nn"):
    return lax.dot_general(a.astype(BF16), b.astype(BF16), _DN[mode], preferred_element_type=F32)


def _roll(x, shift, axis=0):
    n = x.shape[axis]
    shift = shift % n
    return x if shift == 0 else pltpu.roll(x, shift, axis)


def _scan_rows(x, reverse=False):
    n = x.shape[0]
    row = lax.broadcasted_iota(jnp.int32, x.shape, 0)
    sh = 1
    while sh < n:
        if reverse:
            x = x + jnp.where(row < n - sh, _roll(x, n - sh), 0.0)
        else:
            x = x + jnp.where(row >= sh, _roll(x, sh), 0.0)
        sh *= 2
    return x


def _mm(name, mode, pairs, pair_specs, grid, k_axis, out_shapes, out_specs, acc_shape=None, extras=(), extra_specs=(),
        epilogue=None, alpha=1.0, sem=None):
    npair, nex, nout = len(pairs), len(extras), len(out_shapes)
    nk = grid[k_axis] if k_axis is not None else 1

    def body(*refs):
        prs = refs[:2 * npair]
        ex = refs[2 * npair:2 * npair + nex]
        outs = refs[2 * npair + nex:2 * npair + nex + nout]

        def partial():
            p = None
            for i in range(npair):
                d = _dot(prs[2 * i][...], prs[2 * i + 1][...], mode)
                p = d if p is None else p + d
            return p

        def finish(res):
            if alpha != 1.0:
                res = res * alpha
            if epilogue is None:
                outs[0][...] = res.astype(outs[0].dtype)
            else:
                epilogue(res, ex, outs)

        if k_axis is None:
            finish(partial())
        else:
            acc = refs[-1]
            k = pl.program_id(k_axis)

            @pl.when(k == 0)
            def _():
                acc[...] = partial()

            @pl.when(k > 0)
            def _():
                acc[...] += partial()

            @pl.when(k == nk - 1)
            def _():
                finish(acc[...])

    if sem is None:
        sem = tuple("arbitrary" if i == k_axis else "parallel" for i in range(len(grid)))
    scratch = [pltpu.VMEM(acc_shape, F32)] if k_axis is not None else []
    flat = [t for p in pairs for t in p]
    flat_specs = [s for p in pair_specs for s in p]
    return _call(body, name, grid, flat_specs + list(extra_specs), out_specs, out_shapes, scratch, sem)(*flat, *extras)


def _mm_nn(name, a, b, out_dtype, tm=1024, tn=1024):
    M, K = a.shape
    N = b.shape[1]
    tm, tn = _tile(M, tm), _tile(N, tn, LANES)
    return _mm(name, "nn", [(a, b)], [(pl.BlockSpec((tm, K), lambda i, j: (i, 0)), pl.BlockSpec((K, tn), lambda i, j: (0, j)))],
               (M // tm, N // tn), None, [_sds((M, N), out_dtype)], [pl.BlockSpec((tm, tn), lambda i, j: (i, j))])[0]


def _mm_res(name, u, w, x, alpha, tm=512):
    T, K = u.shape
    D = w.shape[1]
    tm = _tile(T, tm)

    def epi(res, ex, outs):
        outs[0][...] = ex[0][...] + res

    return _mm(name, "nn", [(u, w)], [(pl.BlockSpec((tm, K), lambda i: (i, 0)), pl.BlockSpec((K, D), lambda i: (0, 0)))],
               (T // tm,), None, [_sds((T, D), F32)], [pl.BlockSpec((tm, D), lambda i: (i, 0))],
               extras=[x], extra_specs=[pl.BlockSpec((tm, D), lambda i: (i, 0))], epilogue=epi, alpha=alpha)[0]


def _mm_nt(name, pairs, out_dtype, tm=1024, tn=1024, alpha=1.0):
    M = pairs[0][0].shape[0]
    N = pairs[0][1].shape[0]
    tm, tn = _tile(M, tm), _tile(N, tn, LANES)
    specs = [(pl.BlockSpec((tm, a.shape[1]), lambda i, j: (i, 0)), pl.BlockSpec((tn, b.shape[1]), lambda i, j: (j, 0)))
             for a, b in pairs]
    return _mm(name, "nt", pairs, specs, (M // tm, N // tn), None, [_sds((M, N), out_dtype)],
               [pl.BlockSpec((tm, tn), lambda i, j: (i, j))], alpha=alpha)[0]


def _mm_nt3(name, a3, b3, tm=1024, tn=1024):
    S, M, K = a3.shape
    N = b3.shape[1]
    tm, tn = _tile(M, tm), _tile(N, tn, LANES)
    return _mm(name, "nt", [(a3, b3)],
               [(pl.BlockSpec((None, tm, K), lambda i, j, s: (s, i, 0)), pl.BlockSpec((None, tn, K), lambda i, j, s: (s, j, 0)))],
               (M // tm, N // tn, S), 2, [_sds((M, N), F32)], [pl.BlockSpec((tm, tn), lambda i, j, s: (i, j))],
               acc_shape=(tm, tn))[0]


def _mm_tn(name, a, b, alpha=1.0, tm=1408, tn=1408, tk=512):
    T, M = a.shape
    N = b.shape[1]
    tm, tn, tk = _tile(M, tm, LANES), _tile(N, tn, LANES), _tile(T, tk)
    return _mm(name, "tn", [(a, b)],
               [(pl.BlockSpec((tk, tm), lambda i, j, k: (k, i)), pl.BlockSpec((tk, tn), lambda i, j, k: (k, j)))],
               (M // tm, N // tn, T // tk), 2, [_sds((M, N), F32)], [pl.BlockSpec((tm, tn), lambda i, j, k: (i, j))],
               acc_shape=(tm, tn), alpha=alpha)[0]


def _mm_tn3(name, a, b3, tm=1408, tn=1408, tk=512):
    T, M = a.shape
    S, _, N = b3.shape
    tm, tn, tk = _tile(M, tm, LANES), _tile(N, tn, LANES), _tile(T, tk)
    return _mm(name, "tn", [(a, b3)],
               [(pl.BlockSpec((tk, tm), lambda s, i, j, k: (k, i)), pl.BlockSpec((None, tk, tn), lambda s, i, j, k: (s, k, j)))],
               (S, M // tm, N // tn, T // tk), 3, [_sds((S, M, N), F32)],
               [pl.BlockSpec((None, tm, tn), lambda s, i, j, k: (s, i, j))], acc_shape=(tm, tn))[0]


def _rms_fwd(name, x, g):
    T, D = x.shape
    tt = _tile(T, 512)

    def body(x_ref, g_ref, h_ref):
        xv = x_ref[...]
        r = lax.rsqrt(jnp.mean(xv * xv, axis=-1, keepdims=True) + RMS_EPS)
        h_ref[...] = (xv * r * g_ref[...]).astype(h_ref.dtype)

    row = pl.BlockSpec((tt, D), lambda i: (i, 0))
    return _call(body, name, (T // tt,), [row, pl.BlockSpec((1, D), lambda i: (0, 0))], row, _sds((T, D), BF16))(x, g)


def _rms_bwd(name, dh, x, g, dx_in):
    T, D = x.shape
    tt = _tile(T, 512)

    def body(dh_ref, x_ref, g_ref, dxi_ref, dx_ref, dxb_ref, dg_ref):
        xv = x_ref[...]
        r = lax.rsqrt(jnp.mean(xv * xv, axis=-1, keepdims=True) + RMS_EPS)
        xh = xv * r
        dhv = dh_ref[...]
        dxh = dhv * g_ref[...]
        dx = dxi_ref[...] + r * (dxh - xh * jnp.mean(dxh * xh, axis=-1, keepdims=True))
        dx_ref[...] = dx
        dxb_ref[...] = dx.astype(BF16)
        part = jnp.sum(dhv * xh, axis=0, keepdims=True)

        @pl.when(pl.program_id(0) == 0)
        def _():
            dg_ref[...] = part

        @pl.when(pl.program_id(0) > 0)
        def _():
            dg_ref[...] += part

    row = pl.BlockSpec((tt, D), lambda i: (i, 0))
    vec = pl.BlockSpec((1, D), lambda i: (0, 0))
    dx, dxb, dg = _call(body, name, (T // tt,), [row, row, vec, row], [row, row, vec],
                        [_sds((T, D), F32), _sds((T, D), BF16), _sds((1, D), F32)], sem=("arbitrary",))(dh, x, g, dx_in)
    return (dx, dxb), dg


def _loss_head(x, g, tgt):
    T, D = x.shape
    tt = _tile(T, 512)

    def body(x_ref, g_ref, t_ref, loss_ref, dx_ref, dxb_ref, dg_ref):
        xv = x_ref[...]
        r = lax.rsqrt(jnp.mean(xv * xv, axis=-1, keepdims=True) + RMS_EPS)
        xh = xv * r
        e = xh * g_ref[...] - t_ref[...]
        lp = 0.5 * jnp.sum(jnp.sum(e * e, axis=-1, keepdims=True), axis=0, keepdims=True) / D
        dy = e / D
        dxh = dy * g_ref[...]
        dx = r * (dxh - xh * jnp.mean(dxh * xh, axis=-1, keepdims=True))
        dx_ref[...] = dx
        dxb_ref[...] = dx.astype(BF16)
        part = jnp.sum(dy * xh, axis=0, keepdims=True)

        @pl.when(pl.program_id(0) == 0)
        def _():
            dg_ref[...] = part
            loss_ref[...] = lp

        @pl.when(pl.program_id(0) > 0)
        def _():
            dg_ref[...] += part
            loss_ref[...] += lp

    row = pl.BlockSpec((tt, D), lambda i: (i, 0))
    vec = pl.BlockSpec((1, D), lambda i: (0, 0))
    one = pl.BlockSpec((1, 1), lambda i: (0, 0))
    loss, dx, dxb, dg = _call(body, "loss_head", (T // tt,), [row, vec, row], [one, row, row, vec],
                              [_sds((1, 1), F32), _sds((T, D), F32), _sds((T, D), BF16), _sds((1, D), F32)],
                              sem=("arbitrary",))(x, g, tgt)
    return loss, (dx, dxb), dg


def _colsum3(name, a3):
    S, T, N = a3.shape
    tt = _tile(T, 512)

    def body(a_ref, o_ref):
        part = jnp.sum(a_ref[...].astype(F32), axis=0, keepdims=True)

        @pl.when(pl.program_id(1) == 0)
        def _():
            o_ref[...] = part

        @pl.when(pl.program_id(1) > 0)
        def _():
            o_ref[...] += part

    return _call(body, name, (S, T // tt), [pl.BlockSpec((None, tt, N), lambda s, i: (s, i, 0))],
                 pl.BlockSpec((None, 1, N), lambda s, i: (s, 0, 0)), _sds((S, 1, N), F32), sem=("parallel", "arbitrary"))(a3)


def _glu_mm(name, h, w2, bias2, mode, u_dtype, tm=1024, tn=256, carried=None):
    T, K = h.shape
    N = w2.shape[2]
    tm, tn = _tile(T, tm), _tile(N, tn, LANES)
    has_bias = bias2 is not None

    halves = 2 if tm % 32 == 0 else 1

    def body(*refs):
        h_ref, w_ref = refs[0], refs[1]
        ab_ref, u_ref = refs[-2], refs[-1]
        for r in range(halves):
            rows = pl.ds(r * (tm // halves), tm // halves)
            hv = h_ref[rows, :]
            a = _dot(hv, w_ref[0])
            b = _dot(hv, w_ref[1])
            if has_bias:
                a = a + refs[2][0]
                b = b + refs[2][1]
            u = a * _sig(a) * b if mode == "swiglu" else a * _sig(b)
            ab_ref[0, rows, :] = a.astype(BF16)
            ab_ref[1, rows, :] = b.astype(BF16)
            u_ref[rows, :] = u.astype(u_ref.dtype)

    in_specs = [pl.BlockSpec((tm, K), lambda i, j: (i, 0)), pl.BlockSpec((2, K, tn), lambda i, j: (0, 0, j))]
    args = [h, w2]
    if has_bias:
        in_specs.append(pl.BlockSpec((2, 1, tn), lambda i, j: (0, 0, j)))
        args.append(bias2)
    specs = ((T // tm, N // tn), in_specs,
             [pl.BlockSpec((2, tm, tn), lambda i, j: (0, i, j)), pl.BlockSpec((tm, tn), lambda i, j: (i, j))],
             [_sds((2, T, N), BF16), _sds((T, N), u_dtype)], [])
    if carried is None:
        return _call(body, name, *specs, sem=("parallel", "parallel"))(*args)
    ab, u, *got = _carry_call(body, name, *specs, "gather", carried)(*args, *carried)
    return ab, u, got


def _norm_swiglu_mm(name, x, g, w2, tm=1024, tn=256, carried=None):
    T, K = x.shape
    N = w2.shape[2]
    tm, tn = _tile(T, tm), _tile(N, tn, LANES)
    halves = 2 if tm % 32 == 0 else 1

    def body(x_ref, g_ref, w_ref, h_ref, ab_ref, u_ref, h_scr):
        @pl.when(pl.program_id(1) == 0)
        def _():
            xv = x_ref[...]
            r = lax.rsqrt(jnp.mean(xv * xv, axis=-1, keepdims=True) + RMS_EPS)
            hb = (xv * r * g_ref[...]).astype(BF16)
            h_scr[...] = hb
            h_ref[...] = hb

        for r_ in range(halves):
            rows = pl.ds(r_ * (tm // halves), tm // halves)
            hv = h_scr[rows, :]
            a = _dot(hv, w_ref[0])
            b = _dot(hv, w_ref[1])
            ab_ref[0, rows, :] = a.astype(BF16)
            ab_ref[1, rows, :] = b.astype(BF16)
            u_ref[rows, :] = (a * _sig(a) * b).astype(BF16)

    row = pl.BlockSpec((tm, K), lambda i, j: (i, 0))
    specs = ((T // tm, N // tn), [row, pl.BlockSpec((1, K), lambda i, j: (0, 0)), pl.BlockSpec((2, K, tn), lambda i, j: (0, 0, j))],
             [row, pl.BlockSpec((2, tm, tn), lambda i, j: (0, i, j)), pl.BlockSpec((tm, tn), lambda i, j: (i, j))],
             [_sds((T, K), BF16), _sds((2, T, N), BF16), _sds((T, N), BF16)], [pltpu.VMEM((tm, K), BF16)])
    if carried is None:
        return _call(body, name, *specs, sem=("parallel", "arbitrary"))(x, g, w2)
    h, ab, u, *got = _carry_call(body, name, *specs, "gather", carried)(x, g, w2, *carried)
    return h, ab, u, got


def _swiglu_bwd_mm(name, dxb, wd, ab, alpha, tm=1024, tn=256):
    T, D = dxb.shape
    N = wd.shape[0]
    tm, tn = _tile(T, tm), _tile(N, tn, LANES)
    halves = 4 if tm % 64 == 0 else 1

    def body(dx_ref, wd_ref, ab_ref, dab_ref, u_ref):
        wdv = wd_ref[...]
        for r in range(halves):
            rows = pl.ds(r * (tm // halves), tm // halves)
            du = _dot(dx_ref[rows, :], wdv, "nt") * alpha
            a = ab_ref[0, rows, :].astype(F32)
            b = ab_ref[1, rows, :].astype(F32)
            sg = _sig(a)
            sa = a * sg
            dab_ref[0, rows, :] = (du * b * (sg * (1.0 + a * (1.0 - sg)))).astype(BF16)
            dab_ref[1, rows, :] = (du * sa).astype(BF16)
            u_ref[rows, :] = (sa * b).astype(BF16)

    ab_spec = pl.BlockSpec((2, tm, tn), lambda i, j: (0, i, j))
    return _call(body, name, (T // tm, N // tn),
                 [pl.BlockSpec((tm, D), lambda i, j: (i, 0)), pl.BlockSpec((tn, D), lambda i, j: (j, 0)), ab_spec],
                 [ab_spec, pl.BlockSpec((tm, tn), lambda i, j: (i, j))], [_sds((2, T, N), BF16), _sds((T, N), BF16)],
                 sem=("parallel", "parallel"))(dxb, wd, ab)


def _ffn_fwd(tag, x, g, wgu, wd, carried=None):
    if carried is None:
        h, ab, u = _norm_swiglu_mm(f"ffn_gu_{tag}", x, g, wgu)
        return _mm_res(f"ffn_down_{tag}", u, wd, x, 0.5), (x, h, ab)
    h, ab, u, got = _norm_swiglu_mm(f"ffn_gu_{tag}", x, g, wgu, carried=carried)
    return _mm_res(f"ffn_down_{tag}", u, wd, x, 0.5), (x, h, ab), got


def _ffn_bwd(tag, dxp, saved, g, wgu, wd):
    x, h, ab = saved
    dx, dxb = dxp
    dab, u = _swiglu_bwd_mm(f"ffn_dgu_{tag}", dxb, wd, ab, 0.5)
    dwd = _mm_tn(f"ffn_dwd_{tag}", u, dxb, alpha=0.5)
    dwgu = _mm_tn3(f"ffn_dwgu_{tag}", h, dab)
    dh = _mm_nt3(f"ffn_dh_{tag}", dab, wgu)
    dx2, dg = _rms_bwd(f"ffn_drms_{tag}", dh, x, g, dx)
    return dx2, dg, dwgu, dwd


HALO = 32


def _conv_fwd(u3, dw32, dwb, lng, lnb):
    B, S, D = u3.shape
    W = dw32.shape[0]
    taps = CONV_TAPS
    tt = _tile(S, 256, HALO)
    hb = tt // HALO

    def body(u_ref, halo_ref, dw_ref, dwb_ref, g_ref, b_ref, v_ref, s_ref):
        i = pl.program_id(1)
        halo = jnp.where(i > 0, halo_ref[...], 0.0)
        ext = jnp.concatenate([halo, u_ref[...]], axis=0)
        acc = jnp.zeros((tt, D), F32) + dwb_ref[...]
        for j in range(taps):
            acc = acc + dw_ref[pl.ds(j, 1), :] * _roll(ext, taps - 1 - j)[HALO:]
        v_ref[...] = acc
        mu = jnp.mean(acc, axis=-1, keepdims=True)
        xc = acc - mu
        ln = xc * lax.rsqrt(jnp.mean(xc * xc, axis=-1, keepdims=True) + LN_EPS) * g_ref[...] + b_ref[...]
        s_ref[...] = (ln * _sig(ln)).astype(BF16)

    main = pl.BlockSpec((None, tt, D), lambda b, i: (b, i, 0))
    halo = pl.BlockSpec((None, HALO, D), lambda b, i: (b, jnp.maximum(i * hb - 1, 0), 0))
    vec = pl.BlockSpec((1, D), lambda b, i: (0, 0))
    return _call(body, "conv_fwd", (B, S // tt), [main, halo, pl.BlockSpec((W, D), lambda b, i: (0, 0)), vec, vec, vec],
                 [main, main], [_sds((B, S, D), F32), _sds((B, S, D), BF16)], sem=("parallel", "parallel"))(
        u3, u3, dw32, dwb, lng, lnb)


def _conv_bwd_ln(v, ds, lng, lnb):
    T, D = v.shape
    tt = _tile(T, 256)

    def body(v_ref, ds_ref, g_ref, b_ref, dv_ref, red_ref):
        vv = v_ref[...]
        mu = jnp.mean(vv, axis=-1, keepdims=True)
        xc = vv - mu
        rstd = lax.rsqrt(jnp.mean(xc * xc, axis=-1, keepdims=True) + LN_EPS)
        xh = xc * rstd
        ln = xh * g_ref[...] + b_ref[...]
        sg = _sig(ln)
        dln = ds_ref[...] * (sg * (1.0 + ln * (1.0 - sg)))
        dxh = dln * g_ref[...]
        dv = rstd * (dxh - jnp.mean(dxh, axis=-1, keepdims=True) - xh * jnp.mean(dxh * xh, axis=-1, keepdims=True))
        dv_ref[...] = dv
        parts = (jnp.sum(dln * xh, axis=0, keepdims=True), jnp.sum(dln, axis=0, keepdims=True),
                 jnp.sum(dv, axis=0, keepdims=True))

        @pl.when(pl.program_id(0) == 0)
        def _():
            for k in range(3):
                red_ref[k] = parts[k]

        @pl.when(pl.program_id(0) > 0)
        def _():
            for k in range(3):
                red_ref[k] += parts[k]

    row = pl.BlockSpec((tt, D), lambda i: (i, 0))
    vec = pl.BlockSpec((1, D), lambda i: (0, 0))
    return _call(body, "conv_bwd_ln", (T // tt,), [row, row, vec, vec], [row, pl.BlockSpec((3, 1, D), lambda i: (0, 0, 0))],
                 [_sds((T, D), F32), _sds((3, 1, D), F32)], sem=("arbitrary",))(v, ds, lng, lnb)


def _conv_bwd_dw(dv3, u3, ab, dw32, carried=None):
    B, S, D = u3.shape
    W = dw32.shape[0]
    taps = CONV_TAPS
    tt = _tile(S, 256, HALO)
    hb = tt // HALO
    nt = S // tt
    L = tt + HALO

    def body(dv_ref, dvn_ref, u_ref, up_ref, ab_ref, dw_ref, dab_ref, ddw_ref):
        b, i = pl.program_id(0), pl.program_id(1)
        dv = dv_ref[...]
        ext_dv = jnp.concatenate([dv, jnp.where(i < nt - 1, dvn_ref[...], 0.0)], axis=0)
        ext_u = jnp.concatenate([jnp.where(i > 0, up_ref[...], 0.0), u_ref[...]], axis=0)
        du = jnp.zeros((tt, D), F32)
        first = jnp.logical_and(b == 0, i == 0)

        @pl.when(first)
        def _():
            ddw_ref[...] = jnp.zeros((W, D), F32)

        for j in range(taps):
            sh = taps - 1 - j
            du = du + dw_ref[pl.ds(j, 1), :] * _roll(ext_dv, L - sh)[:tt]
            ddw_ref[pl.ds(j, 1), :] += jnp.sum(dv * _roll(ext_u, sh)[HALO:], axis=0, keepdims=True)
        a = ab_ref[0].astype(F32)
        sb = _sig(ab_ref[1].astype(F32))
        dab_ref[0] = (du * sb).astype(BF16)
        dab_ref[1] = (du * a * sb * (1.0 - sb)).astype(BF16)

    main = pl.BlockSpec((None, tt, D), lambda b, i: (b, i, 0))
    prev = pl.BlockSpec((None, HALO, D), lambda b, i: (b, jnp.maximum(i * hb - 1, 0), 0))
    nxt = pl.BlockSpec((None, HALO, D), lambda b, i: (b, jnp.minimum((i + 1) * hb, S // HALO - 1), 0))
    abs_ = pl.BlockSpec((2, tt, D), lambda b, i: (0, b * nt + i, 0))
    wsp = pl.BlockSpec((W, D), lambda b, i: (0, 0))
    specs = ((B, nt), [main, nxt, main, prev, abs_, wsp], [abs_, wsp], [_sds((2, B * S, D), BF16), _sds((W, D), F32)], [])
    args = (dv3, dv3, u3, u3, ab, dw32)
    if carried is None:
        return _call(body, "conv_bwd_dw", *specs, sem=("arbitrary", "arbitrary"))(*args)
    dab, ddw, *got = _carry_call(body, "conv_bwd_dw", *specs, "scatter", carried)(*args, *carried)
    return dab, ddw, got


CONV_TAPS = 31


def _conv_mixer_fwd(x, B, p):
    T, D = x.shape
    h = _rms_fwd("conv_rms", x, p["mix_norm"][0:1])
    ab, u = _glu_mm("conv_in", h, p["conv_w_in2"], p["conv_b_in2"], "glu", F32)
    v, s = _conv_fwd(u.reshape(B, T // B, D), p["conv_dw32"], p["conv_dw_b"], p["conv_ln_g"], p["conv_ln_b"])
    v, s = v.reshape(T, D), s.reshape(T, D)
    return _mm_res("conv_out", s, p["conv_w_out"], x, 1.0), (x, h, ab, u, v, s)


def _conv_mixer_bwd(dxp, saved, B, p):
    x, h, ab, u, v, s = saved
    T, D = x.shape
    dx, dxb = dxp
    g = {}
    ds = _mm_nt("conv_ds", [(dxb, p["conv_w_out"])], F32)
    g["conv_w_out"] = _mm_tn("conv_dwout", s, dxb)
    dv, red = _conv_bwd_ln(v, ds, p["conv_ln_g"], p["conv_ln_b"])
    g["conv_ln_g"], g["conv_ln_b"], g["conv_dw_b"] = red[0], red[1], red[2]
    if "scatter" in p:
        dab, ddw, p["scattered"] = _conv_bwd_dw(dv.reshape(B, T // B, D), u.reshape(B, T // B, D), ab, p["conv_dw32"],
                                               p.pop("scatter"))
    else:
        dab, ddw = _conv_bwd_dw(dv.reshape(B, T // B, D), u.reshape(B, T // B, D), ab, p["conv_dw32"])
    g["conv_dw"] = ddw[:CONV_TAPS][None]
    g["conv_b_in"] = _colsum3("conv_dbin", dab).reshape(1, 2 * D)
    dwin = _mm_tn3("conv_dwin", h, dab)
    g["conv_w_in"] = jnp.moveaxis(dwin, 0, 1).reshape(1, D, 2 * D)
    dh = _mm_nt3("conv_dh", dab, p["conv_w_in2"])
    dx2, dg = _rms_bwd("conv_drms", dh, x, p["mix_norm"][0:1], dx)
    return dx2, dg, g


def _log_sigmoid(z):
    return jnp.minimum(z, 0.0) - jnp.log(1.0 + jnp.exp(-jnp.abs(z)))


def _fox_gate_fwd(fl3, bf):
    B, S, N = fl3.shape
    tt = _tile(S, 512)

    def body(fl_ref, bf_ref, c_ref, carry):
        @pl.when(pl.program_id(1) == 0)
        def _():
            carry[...] = jnp.zeros((1, N), F32)

        c = _scan_rows(_log_sigmoid(fl_ref[...] + bf_ref[...])) + carry[...]
        c_ref[...] = c
        carry[...] = c_ref[pl.ds(tt - 1, 1), :]

    row = pl.BlockSpec((None, tt, N), lambda b, i: (b, i, 0))
    return _call(body, "fox_gate_fwd", (B, S // tt), [row, pl.BlockSpec((1, N), lambda b, i: (0, 0))], row,
                 _sds((B, S, N), F32), [pltpu.VMEM((1, N), F32)], sem=("parallel", "arbitrary"))(fl3, bf)


def _fox_gate_bwd(dc3, fl3, bf):
    B, S, N = fl3.shape
    tt = _tile(S, 512)
    nt = S // tt

    def body(dc_ref, fl_ref, bf_ref, dfl_ref, dbf_ref, carry):
        b, i = pl.program_id(0), pl.program_id(1)

        @pl.when(i == 0)
        def _():
            carry[...] = jnp.zeros((1, N), F32)

        dc = dc_ref[0] - dc_ref[1]
        dlf = _scan_rows(dc, reverse=True) + carry[...]
        dfl = dlf * _sig(-(fl_ref[...] + bf_ref[...]))
        dfl_ref[...] = dfl
        carry[...] += jnp.sum(dc, axis=0, keepdims=True)
        part = jnp.sum(dfl, axis=0, keepdims=True)

        @pl.when(jnp.logical_and(b == 0, i == 0))
        def _():
            dbf_ref[...] = part

        @pl.when(jnp.logical_or(b > 0, i > 0))
        def _():
            dbf_ref[...] += part

    row = pl.BlockSpec((None, tt, N), lambda b, i: (b, nt - 1 - i, 0))
    vec = pl.BlockSpec((1, N), lambda b, i: (0, 0))
    row2 = pl.BlockSpec((2, None, tt, N), lambda b, i: (0, b, nt - 1 - i, 0))
    return _call(body, "fox_gate_bwd", (B, nt), [row2, row, vec], [row, vec], [_sds((B, S, N), F32), _sds((1, N), F32)],
                 [pltpu.VMEM((1, N), F32)], sem=("arbitrary", "arbitrary"))(dc3, fl3, bf)


NEG = -1e30


def _fox_attn_fwd(qa, ka, vat, crow, ckb, dh, scale, carried=None):
    B, H, S, P = qa.shape
    tq = _tile(S, 512, LANES)
    tk = tq
    nl = tq // LANES

    def body(q_ref, k_ref, vt_ref, cr_ref, ck_ref, o_ref, lse_ref, s_scr, p_scr, m_scr, al_scr, acc_scr):
        i = pl.program_id(2)
        qv = q_ref[...]
        m_scr[...] = jnp.full((1, tq), NEG, F32)
        acc_scr[...] = jnp.zeros((P, tq), F32)
        key = lax.broadcasted_iota(jnp.int32, (tk, LANES), 0)
        qry = lax.broadcasted_iota(jnp.int32, (tk, LANES), 1)

        def kv_step(j, diagonal):
            off = pl.multiple_of(j * tk, tk)
            s_scr[...] = _dot(k_ref[pl.ds(off, tk), :], qv, "nt")
            ck = ck_ref[pl.ds(off, tk), :]
            for lt in range(nl):
                ls = slice(lt * LANES, (lt + 1) * LANES)
                s = s_scr[:, ls] * scale + cr_ref[:, ls] - ck
                if diagonal:
                    s = jnp.where(key <= qry + lt * LANES, s, -jnp.inf)
                m1 = m_scr[:, ls]
                m2 = jnp.maximum(m1, jnp.max(s, axis=0, keepdims=True))
                p_scr[:, ls] = jnp.exp(s - m2).astype(BF16)
                al_scr[:, ls] = jnp.exp(m1 - m2)
                m_scr[:, ls] = m2
            acc_scr[...] = al_scr[...] * acc_scr[...] + _dot(vt_ref[:, pl.ds(off, tk)], p_scr[...])

        def before(j, carry):
            kv_step(j, False)
            return carry

        lax.fori_loop(0, i, before, 0)
        kv_step(i, True)
        l = acc_scr[pl.ds(dh, 1), :]
        o_ref[...] = (acc_scr[...] / l).T
        lse_ref[...] = m_scr[...] + jnp.log(l)

    qs = pl.BlockSpec((None, None, tq, P), lambda b, h, i: (b, h, i, 0))
    fullk = pl.BlockSpec((None, None, S, P), lambda b, h, i: (b, h, 0, 0))
    fullt = pl.BlockSpec((None, None, P, S), lambda b, h, i: (b, h, 0, 0))
    fullc = pl.BlockSpec((None, None, S, LANES), lambda b, h, i: (b, h, 0, 0))
    rowt = pl.BlockSpec((None, None, 1, tq), lambda b, h, i: (b, h, 0, i))
    scratch = [pltpu.VMEM((tk, tq), F32), pltpu.VMEM((tk, tq), BF16), pltpu.VMEM((1, tq), F32), pltpu.VMEM((1, tq), F32),
               pltpu.VMEM((P, tq), F32)]
    specs = ((B, H, S // tq), [qs, fullk, fullt, rowt, fullc], [qs, rowt], [_sds((B, H, S, P), F32), _sds((B, H, 1, S), F32)],
             scratch)
    if carried is None:
        return _call(body, "fox_attn_fwd", *specs, sem=("parallel", "parallel", "parallel"))(qa, ka, vat, crow, ckb)
    o, lse, *got = _carry_call(body, "fox_attn_fwd", *specs, "gather", carried)(qa, ka, vat, crow, ckb, *carried)
    return o, lse, got


def _fox_rowstats(do, o):
    B, H, S, P = o.shape
    tq = _tile(S, 4096)

    def body(do_ref, o_ref, dl_ref):
        dl_ref[...] = jnp.sum(do_ref[...] * o_ref[...], axis=-1, keepdims=True)

    qs = pl.BlockSpec((None, None, tq, P), lambda b, h, i: (b, h, i, 0))
    col = pl.BlockSpec((None, None, tq, 1), lambda b, h, i: (b, h, i, 0))
    return _call(body, "fox_rowstats", (B, H, S // tq), [qs, qs], col, _sds((B, H, S, 1), F32), sem=("parallel",) * 3)(do, o)


def _fox_attn_bwd(qa, ka, kat, va, do, crow, lse, delta, ckb, dh, scale, carried=None):
    B, H, S, P = qa.shape
    tk = _tile(S, 512, LANES)
    tq = tk
    nq = S // tq
    nl = tq // LANES

    def body(q_ref, k_ref, kt_ref, v_ref, do_ref, cr_ref, lse_ref, dl_ref, ck_ref, dqt_ref, dk_ref, dv_ref, rs_ref,
             s_scr, dp_scr, p_scr, ds_scr, dk_acc, dv_acc):
        j = pl.program_id(2)

        @pl.when(j == 0)
        def _():
            dqt_ref[...] = jnp.zeros((P, S), F32)

        kj, vj, ck = k_ref[...], v_ref[...], ck_ref[...]
        dk_acc[...] = jnp.zeros((tk, P), F32)
        dv_acc[...] = jnp.zeros((tk, P), F32)
        key = lax.broadcasted_iota(jnp.int32, (tk, LANES), 0)
        qry = lax.broadcasted_iota(jnp.int32, (tk, LANES), 1)

        def q_step(i, diagonal):
            off = pl.multiple_of(i * tq, tq)
            qi, doi = q_ref[pl.ds(off, tq), :], do_ref[pl.ds(off, tq), :].astype(BF16)
            s_scr[...] = _dot(kj, qi, "nt")
            dp_scr[...] = _dot(vj, doi, "nt")
            for lt in range(nl):
                ls = slice(lt * LANES, (lt + 1) * LANES)
                gl = pl.ds(pl.multiple_of(off + lt * LANES, LANES), LANES)
                p = jnp.exp(s_scr[:, ls] * scale + (cr_ref[:, gl] - lse_ref[:, gl]) - ck)
                if diagonal:
                    p = jnp.where(key <= qry + lt * LANES, p, 0.0)
                p_scr[:, ls] = p.astype(BF16)
                ds_scr[:, ls] = (p * (dp_scr[:, ls] - dl_ref[:, gl])).astype(BF16)
            ds = ds_scr[...]
            dqt_ref[:, pl.ds(off, tq)] += _dot(kt_ref[...], ds)
            dk_acc[...] += _dot(ds, qi)
            dv_acc[...] += _dot(p_scr[...], doi)

        def after(i, carry):
            q_step(i, False)
            return carry

        q_step(j, True)
        lax.fori_loop(j + 1, nq, after, 0)
        dk_ref[...] = dk_acc[...] * jnp.where(lax.broadcasted_iota(jnp.int32, (tk, P), 1) < dh, scale, 1.0)
        dv_ref[...] = dv_acc[...]

        @pl.when(j == S // tk - 1)
        def _():
            rs_ref[...] = dqt_ref[pl.ds(dh, 1), :]
            dqt_ref[...] = dqt_ref[...] * jnp.where(lax.broadcasted_iota(jnp.int32, (P, S), 0) < dh, scale, 1.0)

    ks = pl.BlockSpec((None, None, tk, P), lambda b, h, j: (b, h, j, 0))
    kts = pl.BlockSpec((None, None, P, tk), lambda b, h, j: (b, h, 0, j))
    cks = pl.BlockSpec((None, None, tk, LANES), lambda b, h, j: (b, h, j, 0))
    full = pl.BlockSpec((None, None, S, P), lambda b, h, j: (b, h, 0, 0))
    fullt = pl.BlockSpec((None, None, P, S), lambda b, h, j: (b, h, 0, 0))
    rowf = pl.BlockSpec((None, None, 1, S), lambda b, h, j: (b, h, 0, 0))
    scratch = [pltpu.VMEM((tk, tq), F32), pltpu.VMEM((tk, tq), F32), pltpu.VMEM((tk, tq), BF16), pltpu.VMEM((tk, tq), BF16),
               pltpu.VMEM((tk, P), F32), pltpu.VMEM((tk, P), F32)]
    specs = ((B, H, S // tk), [full, ks, kts, ks, full, rowf, rowf, rowf, cks], [fullt, ks, ks, rowf],
             [_sds((B, H, P, S), F32), _sds((B, H, S, P), F32), _sds((B, H, S, P), F32), _sds((B, H, 1, S), F32)], scratch)
    args = (qa, ka, kat, va, do, crow, lse, delta, ckb)
    if carried is None:
        return _call(body, "fox_attn_bwd", *specs, sem=("parallel", "parallel", "arbitrary"))(*args)
    dqt, dk, dv, rs, *got = _carry_call(body, "fox_attn_bwd", *specs, "scatter", carried)(*args, *carried)
    return dqt, dk, dv, rs, got


def _heads(t, B, H):
    T, D = t.shape
    return t.reshape(B, T // B, H, D // H).transpose(0, 2, 1, 3)


def _unheads(t):
    B, H, S, dh = t.shape
    return t.transpose(0, 2, 1, 3).reshape(B * S, H * dh)


def _fox_mixer_fwd(x, B, p):
    T, D = x.shape
    H = FOX_HEADS
    S = T // B
    scale = (D // H) ** -0.5
    h = _rms_fwd("fox_rms", x, p["mix_norm"][1:2])
    qkv = _mm_nn("fox_qkv", h, p["fox_w_qkv"], BF16)
    fl = _mm_nn("fox_fl", h, p["fox_w_f"], F32)
    c = _fox_gate_fwd(fl.reshape(B, S, LANES), p["fox_b_f128"])
    ch = c[:, :, :H].transpose(0, 2, 1)
    crow = ch[:, :, None, :]
    ckb = jnp.broadcast_to(ch[..., None], (B, H, S, LANES))
    q, k, v = _heads(qkv[:, :D], B, H), _heads(qkv[:, D:2 * D], B, H), _heads(qkv[:, 2 * D:], B, H)
    dh = D // H
    P = -(-(dh + 2) // LANES) * LANES
    one, zero = jnp.ones((B, H, S, 1), q.dtype), jnp.zeros((B, H, S, 1), q.dtype)
    rest = jnp.zeros((B, H, S, P - dh - 2), q.dtype)
    qa = jnp.concatenate([q, zero, one, rest], axis=-1)
    ka = jnp.concatenate([k, one, zero, rest], axis=-1)
    va = jnp.concatenate([v, one, zero, rest], axis=-1)
    if "late_gather" in p:
        o, lse, p["late_gathered"] = _fox_attn_fwd(qa, ka, va.transpose(0, 1, 3, 2), crow, ckb, dh, scale, p.pop("late_gather"))
    else:
        o, lse = _fox_attn_fwd(qa, ka, va.transpose(0, 1, 3, 2), crow, ckb, dh, scale)
    of = _unheads(o[..., :dh])
    return _mm_res("fox_out", of, p["fox_w_out"], x, 1.0), (x, h, fl, qa, ka, va, crow, ckb, o, lse, of)


def _fox_mixer_bwd(dxp, saved, B, p):
    x, h, fl, qa, ka, va, crow, ckb, o, lse, of = saved
    T, D = x.shape
    H = FOX_HEADS
    S = T // B
    dh_ = D // H
    P = qa.shape[-1]
    scale = dh_ ** -0.5
    dx, dxb = dxp
    g = {}
    do = _heads(_mm_nt("fox_do", [(dxb, p["fox_w_out"])], F32), B, H)
    do = jnp.pad(do, ((0, 0), (0, 0), (0, 0), (0, P - dh_)))
    g["fox_w_out"] = _mm_tn("fox_dwout", of, dxb)[None]
    delta = _fox_rowstats(do, o).reshape(B, H, 1, S)
    if "scatter" in p:
        dqt, dk, dv, rowsum, p["scattered"] = _fox_attn_bwd(qa, ka, ka.transpose(0, 1, 3, 2), va, do, crow, lse, delta, ckb,
                                                          dh_, scale, p.pop("scatter"))
    else:
        dqt, dk, dv, rowsum = _fox_attn_bwd(qa, ka, ka.transpose(0, 1, 3, 2), va, do, crow, lse, delta, ckb, dh_, scale)
    dq = dqt[:, :, :dh_, :].transpose(0, 3, 1, 2).reshape(T, D)
    dqkv = jnp.concatenate([dq, _unheads(dk[..., :dh_]), _unheads(dv[..., :dh_])], axis=1).astype(BF16)
    dc = jnp.stack([rowsum[:, :, 0, :], dk[..., dh_ + 1]])
    dc = jnp.pad(dc.transpose(0, 1, 3, 2), ((0, 0), (0, 0), (0, 0), (0, LANES - H)))
    dfl, dbf = _fox_gate_bwd(dc, fl.reshape(B, S, LANES), p["fox_b_f128"])
    dfl = dfl.reshape(T, LANES)
    g["fox_b_f"] = dbf[:, :H]
    dwqkv = _mm_tn("fox_dwqkv", h, dqkv)
    dwf = _mm_tn("fox_dwf", h, dfl)
    g["fox_w_in"] = jnp.concatenate([dwqkv, dwf[:, :H]], axis=1)[None]
    dh = _mm_nt("fox_dh", [(dqkv, p["fox_w_qkv"]), (dfl, p["fox_w_f"])], F32)
    dx2, dg = _rms_bwd("fox_drms", dh, x, p["mix_norm"][1:2], dx)
    return dx2, dg, g


def _lb_fwd(logits):
    L, D = logits.shape

    def body(l_ref, lb_ref):
        z = l_ref[...]
        e = jnp.exp(z - jnp.max(z, axis=0, keepdims=True))
        p = e / jnp.sum(e, axis=0, keepdims=True)
        lb_ref[...] = jnp.sum(jnp.where(_lb_rows(z.shape), p, 0.0), axis=0, keepdims=True)

    return _call(body, "hgrn_lb", (1,), [pl.BlockSpec((L, D), lambda i: (0, 0))], pl.BlockSpec((1, D), lambda i: (0, 0)),
                 _sds((1, D), F32))(logits)


def _lb_rows(shape):
    r = lax.broadcasted_iota(jnp.int32, shape, 0)
    return jnp.logical_and(r >= 1, r <= HGRN_LAYER)


HGRN_LAYER = 2


def _lb_bwd(logits, dlb):
    L, D = logits.shape

    def body(l_ref, d_ref, o_ref):
        z = l_ref[...]
        e = jnp.exp(z - jnp.max(z, axis=0, keepdims=True))
        p = e / jnp.sum(e, axis=0, keepdims=True)
        dp = jnp.where(_lb_rows(z.shape), d_ref[...], 0.0)
        o_ref[...] = p * (dp - jnp.sum(p * dp, axis=0, keepdims=True))

    full = pl.BlockSpec((L, D), lambda i: (0, 0))
    return _call(body, "hgrn_dlb", (1,), [full, pl.BlockSpec((1, D), lambda i: (0, 0))], full, _sds((L, D), F32))(logits, dlb)


def _hgrn_gates(qr, fr, lbv):
    e = jnp.exp(-jnp.abs(fr))
    big, small = 1.0 / (1.0 + e), e / (1.0 + e)
    sf = jnp.where(fr >= 0, big, small)
    snf = jnp.where(fr >= 0, small, big)
    f = lbv + (1.0 - lbv) * sf
    sq = _sig(qr)
    return qr * sq, (1.0 - lbv) * snf, jnp.log(f), sf, snf, f, sq


def _hgrn_intra(G, q, kk, g_scr, q_scr):
    C = HGRN_CHUNK
    g_scr[...] = G
    q_scr[...] = q
    srow = lax.broadcasted_iota(jnp.int32, (C, LANES), 0)
    lane = lax.broadcasted_iota(jnp.int32, (C, LANES), 1)
    at = jnp.zeros((C, LANES), F32)
    for t in range(C):
        e = jnp.where(srow <= t, jnp.exp(g_scr[pl.ds(t, 1), :] - G), 0.0)
        col = jnp.sum(e * kk * q_scr[pl.ds(t, 1), :], axis=-1, keepdims=True)
        at = jnp.where(lane == t, col, at)
    return at


def _hgrn_fwd(proj3, lb, ng):
    B, S, D4 = proj3.shape
    D = D4 // 4
    H = D // HGRN_EXPAND
    C = HGRN_CHUNK
    R = _tile(S, 256, 2 * C)
    ncb = R // C
    dk = HGRN_EXPAND

    def body(q_ref, f_ref, i_ref, go_ref, lb_ref, ng_ref, y_ref, o_ref, st_ref, st, g_scr, q_scr):
        @pl.when(pl.program_id(2) == 0)
        def _():
            st[...] = jnp.zeros((dk, dk), F32)

        lbv = lb_ref[...]

        def chunk(c, slot):
            r0 = pl.multiple_of(c * C, C)
            rows = pl.ds(r0, C)
            q, kk, lf, *_ = _hgrn_gates(q_ref[rows, :], f_ref[rows, :], lbv)
            vv = i_ref[rows, :]
            G = _scan_rows(lf)
            at = _hgrn_intra(G, q, kk, g_scr.at[slot], q_scr.at[slot])
            gl = g_scr[slot, pl.ds(C - 1, 1), :]
            stv = st[...]
            st_ref[c] = stv
            o = _dot(q * jnp.exp(G), stv, "nt") + _dot(at, vv, "tn")[:C]
            st[...] = stv * jnp.exp(gl) + _dot(vv, kk * jnp.exp(gl - G), "tn")
            o_ref[rows, :] = o
            gv = go_ref[rows, :]
            y = o * lax.rsqrt(jnp.mean(o * o, axis=-1, keepdims=True) + RMS_EPS) * ng_ref[...] * (gv * _sig(gv))
            y_ref[rows, :] = y.astype(BF16)

        def pair(c2, carry):
            chunk(2 * c2, 0)
            chunk(2 * c2 + 1, 1)
            return carry

        lax.fori_loop(0, ncb // 2, pair, 0)

    def col(k):
        return pl.BlockSpec((None, R, dk), lambda b, h, i: (b, i, h + k * H))

    vec = pl.BlockSpec((1, dk), lambda b, h, i: (0, h))
    out = pl.BlockSpec((None, R, dk), lambda b, h, i: (b, i, h))
    return _call(body, "hgrn_fwd", (B, H, S // R), [col(0), col(1), col(2), col(3), vec, vec],
                 [out, out, pl.BlockSpec((None, None, ncb, dk, dk), lambda b, h, i: (b, h, i, 0, 0))],
                 [_sds((B, S, D), BF16), _sds((B, S, D), F32), _sds((B, H, S // C, dk, dk), F32)],
                 [pltpu.VMEM((dk, dk), F32), pltpu.VMEM((2, C, dk), F32), pltpu.VMEM((2, C, dk), F32)],
                 sem=("parallel", "parallel", "arbitrary"))(proj3, proj3, proj3, proj3, lb, ng)


def _hgrn_bwd(proj3, o3, dy3, states, lb, ng, carried=None):
    B, S, D4 = proj3.shape
    D = D4 // 4
    H = D // HGRN_EXPAND
    C = HGRN_CHUNK
    R = _tile(S, 256, 2 * C)
    ncb = R // C
    nb = S // R
    dk = HGRN_EXPAND

    def body(q_ref, f_ref, i_ref, go_ref, o_ref, dy_ref, st_ref, lb_ref, ng_ref,
             dp_ref, red_ref, dst, g_scr, q_scr, dq_scr, acc):
        b, i = pl.program_id(1), pl.program_id(2)

        @pl.when(i == 0)
        def _():
            dst[...] = jnp.zeros((dk, dk), F32)

        @pl.when(jnp.logical_and(b == 0, i == 0))
        def _():
            acc[...] = jnp.zeros((2, dk), F32)

        lbv = lb_ref[...]
        ngv = ng_ref[...]
        srow = lax.broadcasted_iota(jnp.int32, (C, LANES), 0)
        lane = lax.broadcasted_iota(jnp.int32, (C, LANES), 1)

        def chunk(c, slot):
            r0 = pl.multiple_of(c * C, C)
            rows = pl.ds(r0, C)
            qr, fr, vv, gv = q_ref[rows, :], f_ref[rows, :], i_ref[rows, :], go_ref[rows, :]
            q, kk, lf, sf, snf, f, sq = _hgrn_gates(qr, fr, lbv)
            o = o_ref[rows, :]
            dy = dy_ref[rows, :]
            rinv = lax.rsqrt(jnp.mean(o * o, axis=-1, keepdims=True) + RMS_EPS)
            on = o * rinv
            sgv = _sig(gv)
            dz = dy * (gv * sgv)
            dp_ref[3, rows, :] = (dy * on * ngv * (sgv * (1.0 + gv * (1.0 - sgv)))).astype(BF16)
            acc[pl.ds(1, 1), :] += jnp.sum(dz * on, axis=0, keepdims=True)
            don = dz * ngv
            do = rinv * (don - on * jnp.mean(don * on, axis=-1, keepdims=True))
            G = _scan_rows(lf)
            g_scr[slot] = G
            q_scr[slot] = q
            gl = g_scr[slot, pl.ds(C - 1, 1), :]
            egl = jnp.exp(gl)
            eG = jnp.exp(G)
            eK = jnp.exp(gl - G)
            qg, kg = q * eG, kk * eK
            stv = st_ref[c]
            dsv = dst[...]
            dqg = _dot(do, stv)
            do_pad = jnp.concatenate([do, jnp.zeros((LANES - C, dk), F32)], axis=0)
            dat = _dot(vv, do_pad, "nt")
            dkg = _dot(vv, dsv)
            dgl = egl * jnp.sum(stv * dsv, axis=0, keepdims=True) + jnp.sum(dkg * kg, axis=0, keepdims=True)
            at = jnp.zeros((C, LANES), F32)
            dki = jnp.zeros((C, dk), F32)
            for t in range(C):
                e = jnp.where(srow <= t, jnp.exp(g_scr[slot, pl.ds(t, 1), :] - G), 0.0)
                qt = q_scr[slot, pl.ds(t, 1), :]
                at = jnp.where(lane == t, jnp.sum(e * kk * qt, axis=-1, keepdims=True), at)
                z = e * jnp.sum(jnp.where(lane == t, dat, 0.0), axis=-1, keepdims=True)
                dq_scr[slot, pl.ds(t, 1), :] = jnp.sum(z * kk, axis=0, keepdims=True)
                dki = dki + z * qt
            dqi = dq_scr[slot]
            dp_ref[2, rows, :] = (_dot(at, do_pad) + _dot(kg, dsv, "nt")).astype(BF16)
            dst[...] = dsv * egl + _dot(do, qg, "tn")
            dq = dqg * eG + dqi
            dkk = dkg * eK + dki
            dG = dqg * qg - dkg * kg + q * dqi - kk * dki
            dG = dG + jnp.where(srow == C - 1, dgl, 0.0)
            dlf = _scan_rows(dG, reverse=True)
            dsf = (1.0 - lbv) * sf * snf
            dp_ref[1, rows, :] = (dlf * dsf / f - dkk * dsf).astype(BF16)
            acc[pl.ds(0, 1), :] += jnp.sum(dlf * snf / f - dkk * snf, axis=0, keepdims=True)
            dp_ref[0, rows, :] = (dq * (sq * (1.0 + qr * (1.0 - sq)))).astype(BF16)

        def pair(c2, carry):
            chunk(ncb - 1 - 2 * c2, 0)
            chunk(ncb - 2 - 2 * c2, 1)
            return carry

        lax.fori_loop(0, ncb // 2, pair, 0)

        @pl.when(jnp.logical_and(b == B - 1, i == nb - 1))
        def _():
            red_ref[0] = acc[pl.ds(0, 1), :]
            red_ref[1] = acc[pl.ds(1, 1), :]

    def col(k):
        return pl.BlockSpec((None, R, dk), lambda h, b, i: (b, nb - 1 - i, h + k * H))

    vec = pl.BlockSpec((1, dk), lambda h, b, i: (0, h))
    row = pl.BlockSpec((None, R, dk), lambda h, b, i: (b, nb - 1 - i, h))
    stsp = pl.BlockSpec((None, None, ncb, dk, dk), lambda h, b, i: (b, h, nb - 1 - i, 0, 0))
    specs = ((H, B, nb), [col(0), col(1), col(2), col(3), row, row, stsp, vec, vec],
             [pl.BlockSpec((4, None, R, dk), lambda h, b, i: (0, b, nb - 1 - i, h)),
              pl.BlockSpec((2, 1, dk), lambda h, b, i: (0, 0, h))],
             [_sds((4, B, S, D), BF16), _sds((2, 1, D), F32)],
             [pltpu.VMEM((dk, dk), F32), pltpu.VMEM((2, C, dk), F32), pltpu.VMEM((2, C, dk), F32),
              pltpu.VMEM((2, C, dk), F32), pltpu.VMEM((2, dk), F32)])
    args = (proj3, proj3, proj3, proj3, o3, dy3, states, lb, ng)
    if carried is None:
        return _call(body, "hgrn_bwd", *specs, sem=("parallel", "arbitrary", "arbitrary"))(*args)
    dp, red, *got = _carry_call(body, "hgrn_bwd", *specs, "scatter", carried)(*args, *carried)
    return dp, red, got


def _hgrn_mixer_fwd(x, B, p):
    T, D = x.shape
    S = T // B
    h = _rms_fwd("hgrn_rms", x, p["mix_norm"][2:3])
    proj = _mm_nn("hgrn_in", h, p["hgrn_w_in"], F32)
    lb = _lb_fwd(p["hgrn_lb_logits"])
    y, o, states = _hgrn_fwd(proj.reshape(B, S, 4 * D), lb, p["hgrn_norm"])
    y = y.reshape(T, D)
    return _mm_res("hgrn_out", y, p["hgrn_w_out"], x, 1.0), (x, h, proj, lb, y, o, states)


def _hgrn_mixer_bwd(dxp, saved, B, p):
    x, h, proj, lb, y, o, states = saved
    T, D = x.shape
    S = T // B
    dx, dxb = dxp
    g = {}
    dy = _mm_nt("hgrn_dy", [(dxb, p["hgrn_w_out"])], F32)
    g["hgrn_w_out"] = _mm_tn("hgrn_dwout", y, dxb)[None]
    if "scatter" in p:
        dp, red, p["scattered"] = _hgrn_bwd(proj.reshape(B, S, 4 * D), o, dy.reshape(B, S, D), states, lb, p["hgrn_norm"],
                                           p.pop("scatter"))
    else:
        dp, red = _hgrn_bwd(proj.reshape(B, S, 4 * D), o, dy.reshape(B, S, D), states, lb, p["hgrn_norm"])
    dp = dp.reshape(4, T, D)
    g["hgrn_norm"] = red[1]
    g["hgrn_lb_logits"] = _lb_bwd(p["hgrn_lb_logits"], red[0])
    dwin = _mm_tn3("hgrn_dwin", h, dp)
    g["hgrn_w_in"] = jnp.moveaxis(dwin, 0, 1).reshape(1, D, 4 * D)
    dh = _mm_nt3("hgrn_dh", dp, p["hgrn_w_in4"])
    dx2, dg = _rms_bwd("hgrn_drms", dh, x, p["mix_norm"][2:3], dx)
    return dx2, dg, g


POOL_HALO = 16


def _pool_fwd(x3, g):
    B, S, D = x3.shape
    tt = _tile(S, 256, POOL_HALO)
    hb = tt // POOL_HALO
    G = D // len(POOL_WINDOWS)

    def body(x_ref, halo_ref, g_ref, m_ref):
        i = pl.program_id(1)

        def norm(xv):
            return xv * lax.rsqrt(jnp.mean(xv * xv, axis=-1, keepdims=True) + RMS_EPS) * g_ref[...]

        hm = norm(x_ref[...])
        ext = jnp.concatenate([jnp.where(i > 0, norm(halo_ref[...]), 0.0), hm], axis=0)
        pos = (i * tt + lax.broadcasted_iota(jnp.int32, (tt, 1), 0) + 1).astype(F32)
        for gi, win in enumerate(POOL_WINDOWS):
            s = ext[:, gi * G:(gi + 1) * G]
            w = 1
            while w < win:
                s = s + _roll(s, w)
                w *= 2
            m_ref[:, gi * G:(gi + 1) * G] = (s[POOL_HALO:] / jnp.minimum(pos, float(win)) - hm[:, gi * G:(gi + 1) * G]).astype(BF16)

    main = pl.BlockSpec((None, tt, D), lambda b, i: (b, i, 0))
    halo = pl.BlockSpec((None, POOL_HALO, D), lambda b, i: (b, jnp.maximum(i * hb - 1, 0), 0))
    return _call(body, "pool_fwd", (B, S // tt), [main, halo, pl.BlockSpec((1, D), lambda b, i: (0, 0))], main,
                 _sds((B, S, D), BF16), sem=("parallel", "parallel"))(x3, x3, g)


def _pool_bwd(dm3):
    B, S, D = dm3.shape
    tt = _tile(S, 256, POOL_HALO)
    hb = tt // POOL_HALO
    nt = S // tt
    G = D // len(POOL_WINDOWS)
    L = tt + POOL_HALO

    def body(dm_ref, nxt_ref, dh_ref):
        i = pl.program_id(1)
        posm = (i * tt + lax.broadcasted_iota(jnp.int32, (tt, 1), 0) + 1).astype(F32)
        posn = ((i + 1) * tt + lax.broadcasted_iota(jnp.int32, (POOL_HALO, 1), 0) + 1).astype(F32)
        for gi, win in enumerate(POOL_WINDOWS):
            sl = slice(gi * G, (gi + 1) * G)
            dm = dm_ref[:, sl]
            s = jnp.concatenate([dm / jnp.minimum(posm, float(win)),
                                 jnp.where(i < nt - 1, nxt_ref[:, sl] / jnp.minimum(posn, float(win)), 0.0)], axis=0)
            w = 1
            while w < win:
                s = s + _roll(s, L - w)
                w *= 2
            dh_ref[:, sl] = s[:tt] - dm

    main = pl.BlockSpec((None, tt, D), lambda b, i: (b, i, 0))
    nxt = pl.BlockSpec((None, POOL_HALO, D), lambda b, i: (b, jnp.minimum((i + 1) * hb, S // POOL_HALO - 1), 0))
    return _call(body, "pool_bwd", (B, nt), [main, nxt], main, _sds((B, S, D), F32), sem=("parallel", "parallel"))(dm3, dm3)


def _pool_mixer_fwd(x, B, p):
    T, D = x.shape
    NG = len(POOL_WINDOWS)
    G = D // NG
    tm = _tile(T, 512)
    m = _pool_fwd(x.reshape(B, T // B, D), p["mix_norm"][3:4]).reshape(T, D)

    def epi(res, ex, outs):
        outs[0][...] = ex[1][...] + res * ex[0][...]

    blk = pl.BlockSpec((tm, G), lambda i, g: (i, g))
    x2 = _mm("pool_out", "nn", [(m, p["pool_w4"])], [(blk, pl.BlockSpec((None, G, G), lambda i, g: (g, 0, 0)))],
             (T // tm, NG), None, [_sds((T, D), F32)], [blk], extras=[p["pool_scale"], x],
             extra_specs=[pl.BlockSpec((1, G), lambda i, g: (0, g)), blk], epilogue=epi)[0]
    return x2, (x, m)


def _pool_mixer_bwd(dxp, saved, B, p):
    x, m = saved
    dx, dxb = dxp
    T, D = x.shape
    NG = len(POOL_WINDOWS)
    G = D // NG
    tm = _tile(T, 512)
    g = {}

    def epi(zz, ex, outs):
        dy = ex[0][...]
        outs[0][...] = (dy * ex[1][...]).astype(BF16)
        part = jnp.sum(dy * zz, axis=0, keepdims=True)

        @pl.when(pl.program_id(1) == 0)
        def _():
            outs[1][...] = part

        @pl.when(pl.program_id(1) > 0)
        def _():
            outs[1][...] += part

    blk = pl.BlockSpec((tm, G), lambda g_, i: (i, g_))
    wsp = pl.BlockSpec((None, G, G), lambda g_, i: (g_, 0, 0))
    vec = pl.BlockSpec((1, G), lambda g_, i: (0, g_))
    dz, dsc = _mm("pool_dz", "nn", [(m, p["pool_w4"])], [(blk, wsp)], (NG, T // tm), None,
                  [_sds((T, D), BF16), _sds((1, D), F32)], [blk, vec], extras=[dx, p["pool_scale"]], extra_specs=[blk, vec],
                  epilogue=epi, sem=("parallel", "arbitrary"))
    g["pool_scale"] = dsc
    tk = _tile(T, 512)
    kb = pl.BlockSpec((tk, G), lambda g_, k: (k, g_))
    g["pool_w"] = _mm("pool_dw", "tn", [(m, dz)], [(kb, kb)], (NG, T // tk), 1, [_sds((NG, G, G), F32)],
                      [pl.BlockSpec((None, G, G), lambda g_, k: (g_, 0, 0))], acc_shape=(G, G))[0][None]
    blk2 = pl.BlockSpec((tm, G), lambda i, g_: (i, g_))
    dm = _mm("pool_dm", "nt", [(dz, p["pool_w4"])], [(blk2, pl.BlockSpec((None, G, G), lambda i, g_: (g_, 0, 0)))],
             (T // tm, NG), None, [_sds((T, D), F32)], [blk2])[0]
    dh = _pool_bwd(dm.reshape(B, T // B, D)).reshape(T, D)
    dx2, dg = _rms_bwd("pool_drms", dh, x, p["mix_norm"][3:4], dx)
    return dx2, dg, g


_MIXERS = ((_conv_mixer_fwd, _conv_mixer_bwd), (_fox_mixer_fwd, _fox_mixer_bwd), (_hgrn_mixer_fwd, _hgrn_mixer_bwd),
           (_pool_mixer_fwd, _pool_mixer_bwd))


EARLY_LAYERS = 2
LATE = ("ffn_w_gate", "ffn_w_up", "ffn_w_down", "hgrn_w_in", "hgrn_w_out", "pool_w")
MIXER_MATRICES = (("conv_w_in", "conv_w_out"), ("fox_w_in", "fox_w_out"), ("hgrn_w_in", "hgrn_w_out"), ("pool_w",))


def _local_step(x3, tgt3, w, plan=None):
    B, S, D = x3.shape
    T = B * S
    depth = w["ffn_norm"].shape[0]
    H = FOX_HEADS
    p = dict(w)
    p["conv_w_in2"] = w["conv_w_in"][0].reshape(D, 2, D).transpose(1, 0, 2)
    p["conv_b_in2"] = w["conv_b_in"].reshape(2, 1, D)
    p["conv_dw32"] = jnp.pad(w["conv_dw"][0], ((0, HALO - CONV_TAPS), (0, 0)))
    p["conv_w_out"] = w["conv_w_out"][0]
    p["fox_b_f128"] = jnp.pad(w["fox_b_f"], ((0, 0), (0, LANES - H)))
    ffn = {}

    def install(first, d):
        if "ffn_w_gate" in d:
            down = d["ffn_w_down"] if "ffn_w_down" in d else w["ffn_w_down"][first:]
            for k in range(d["ffn_w_gate"].shape[0]):
                ffn[first + k] = (jnp.stack((d["ffn_w_gate"][k], d["ffn_w_up"][k]), axis=1), down[k])
        if "fox_w_in" in d:
            p["fox_w_qkv"] = d["fox_w_in"][0][:, :3 * D]
            p["fox_w_f"] = jnp.pad(d["fox_w_in"][0][:, 3 * D:], ((0, 0), (0, LANES - H)))
            p["fox_w_out"] = d["fox_w_out"][0]
        if "hgrn_w_in" in d:
            p["hgrn_w_in"] = d["hgrn_w_in"][0]
            p["hgrn_w_in4"] = d["hgrn_w_in"][0].reshape(D, 4, D).transpose(1, 0, 2)
            p["hgrn_w_out"] = d["hgrn_w_out"][0]
            p["pool_w4"] = d["pool_w"][0]

    install(0, w)
    beside = {} if plan is None else plan["beside_ffn"]
    if plan is not None:
        p["late_gather"] = plan["late_shards"]

    def ffn_fwd(i, half, x):
        tag = f"{i}{'ab'[half]}"
        args = (tag, x, w["ffn_norm"][i, half:half + 1], ffn[i][0][half], ffn[i][1][half])
        if tag not in beside:
            return _ffn_fwd(*args)
        shards, first, assemble = beside[tag]
        x, s, got = _ffn_fwd(*args, carried=shards)
        install(first, assemble(got))
        return x, s

    x = x3.reshape(T, D)
    saved = []
    for i in range(depth):
        x, s0 = ffn_fwd(i, 0, x)
        x, s1 = _MIXERS[i % 4][0](x, B, p)
        if "late_gathered" in p:
            install(EARLY_LAYERS, plan["assemble"](p.pop("late_gathered")))
        x, s2 = ffn_fwd(i, 1, x)
        saved.append((s0, s1, s2))
    loss, dx, dfinal = _loss_head(x, w["final_norm"].reshape(1, D), tgt3.reshape(T, D))

    g = {"final_norm": dfinal}
    dffn_norm = [[None, None] for _ in range(depth)]
    dwgu = [[None, None] for _ in range(depth)]
    dwd = [[None, None] for _ in range(depth)]
    dmix = [None] * depth
    travelled = []

    def layer_grads(i, extra):
        gate, up = (jnp.stack([dwgu[i][0][s], dwgu[i][1][s]])[None] for s in (0, 1))
        return [gate, up, jnp.stack(dwd[i])[None]] + [g.pop(n) for n in extra]

    for i in reversed(range(depth)):
        s0, s1, s2 = saved[i]
        dx, dffn_norm[i][1], dwgu[i][1], dwd[i][1] = _ffn_bwd(f"{i}b", dx, s2, w["ffn_norm"][i, 1:2], ffn[i][0][1], ffn[i][1][1])
        if plan is not None and i < depth - 1:
            extra = MIXER_MATRICES[i + 1] if i > 0 else ()
            names = LATE[:3] + extra
            p["scatter"] = plan["dev_major"](names, layer_grads(i + 1, extra))
            travelled.append([names, p["scatter"]])
        dx, dmix[i], gm = _MIXERS[i % 4][1](dx, s1, B, p)
        if "scattered" in p:
            travelled[-1].append(p.pop("scattered"))
        g.update(gm)
        dx, dffn_norm[i][0], dwgu[i][0], dwd[i][0] = _ffn_bwd(f"{i}a", dx, s0, w["ffn_norm"][i, 0:1], ffn[i][0][0], ffn[i][1][0])
    kept = range(depth) if plan is None else range(1)
    g["ffn_norm"] = jnp.stack([jnp.stack([a[0], b[0]]) for a, b in dffn_norm])
    g["ffn_w_gate"] = jnp.stack([jnp.stack([dwgu[i][0][0], dwgu[i][1][0]]) for i in kept])
    g["ffn_w_up"] = jnp.stack([jnp.stack([dwgu[i][0][1], dwgu[i][1][1]]) for i in kept])
    g["ffn_w_down"] = jnp.stack([jnp.stack(dwd[i]) for i in kept])
    g["mix_norm"] = jnp.concatenate(dmix, axis=0)
    g = {n: (t.reshape(w[n].shape) if n in w and n not in LATE else t) for n, t in g.items()}
    return loss, dx[0].reshape(B, S, D), g, travelled


def _adamw(name, w, m, v, parts):
    shape = w.shape
    cols = shape[-1]
    rows = w.size // cols
    tr = _tile(rows, max(8, (1 << 19) // cols))
    n = len(parts)
    c1 = 1.0 - ADAM_B1 ** ADAM_STEP
    c2 = 1.0 - ADAM_B2 ** ADAM_STEP

    def body(*refs):
        w_ref, m_ref, v_ref = refs[:3]
        g_ref, d_ref, m2_ref, v2_ref = refs[3 + n:]
        g = refs[3][...].astype(F32)
        for k in range(1, n):
            g = g + refs[3 + k][...].astype(F32)
        m2 = ADAM_B1 * m_ref[...] + (1.0 - ADAM_B1) * g
        v2 = ADAM_B2 * v_ref[...] + (1.0 - ADAM_B2) * (g * g)
        g_ref[...] = g
        m2_ref[...] = m2
        v2_ref[...] = v2
        d_ref[...] = -ADAM_LR * ((m2 / c1) / (jnp.sqrt(v2 / c2) + ADAM_EPS) + ADAM_WD * w_ref[...])

    blk = pl.BlockSpec((tr, cols), lambda i: (i, 0))
    outs = _call(body, name, (rows // tr,), [blk] * (3 + n), [blk] * 4, [_sds((rows, cols), F32)] * 4, sem=("parallel",))(
        *[t.reshape(rows, cols) for t in (w, m, v, *parts)])
    return [o.reshape(shape) for o in outs]


ANY = pl.BlockSpec(memory_space=pl.ANY)
FLAT_COLS = 1024


def _place():
    return lax.axis_index("x"), lax.axis_index("y"), lax.axis_index("c")


def _all_gather(name, xs):
    K = len(xs)

    def body(*refs):
        x_refs, out_refs = refs[:K], refs[K:2 * K]
        send_sems, recv_sems, local_sems = refs[2 * K:]
        xi, yi, ci = _place()
        me, sibling = (xi, yi, ci), (xi, yi, 1 - ci)
        chips = [(1 - xi, yi), (xi, 1 - yi), (1 - xi, 1 - yi)]

        def slot(a, px, py, pc):
            return out_refs[a].at[4 * px + 2 * py + pc]

        def copy(a, k, block, to, own=False):
            return pltpu.make_async_remote_copy(src_ref=x_refs[a] if own else slot(a, *block), dst_ref=slot(a, *block),
                                                send_sem=send_sems.at[7 * a + k], recv_sem=recv_sems.at[7 * a + k],
                                                device_id=to, device_id_type=MESH)

        mine = [pltpu.make_async_copy(x_refs[a], slot(a, *me), local_sems.at[a]) for a in range(K)]
        first = [copy(a, 1 + j, me, (*chip, ci), own=True) for j, chip in enumerate(chips) for a in range(K)]
        first += [copy(a, 0, me, sibling, own=True) for a in range(K)]
        for cp in mine + first:
            cp.start()
        passed = []
        for j, chip in enumerate(chips):
            for a in range(K):
                copy(a, 1 + j, (*chip, ci), me).wait_recv()
                passed.append(copy(a, 4 + j, (*chip, ci), sibling))
                passed[-1].start()
        for a in range(K):
            copy(a, 0, sibling, me).wait_recv()
            for j, chip in enumerate(chips):
                copy(a, 4 + j, (*chip, 1 - ci), me).wait_recv()
        for cp in first + passed:
            cp.wait_send()
        for cp in mine:
            cp.wait()

    return pl.pallas_call(body, name=name, out_shape=[_sds((N_DEV,) + x.shape, x.dtype) for x in xs], in_specs=[ANY] * K,
                          out_specs=[ANY] * K,
                          scratch_shapes=[pltpu.SemaphoreType.DMA((7 * K,)), pltpu.SemaphoreType.DMA((7 * K,)),
                                          pltpu.SemaphoreType.DMA((K,))])(*xs)


def _swap_sibling(name, ts):
    K = len(ts)

    def body(*refs):
        t_refs, out_refs, send_sems, recv_sems = refs[:K], refs[K:2 * K], refs[2 * K], refs[2 * K + 1]
        xi, yi, ci = _place()
        cps = [pltpu.make_async_remote_copy(src_ref=t_refs[a], dst_ref=out_refs[a], send_sem=send_sems.at[a],
                                            recv_sem=recv_sems.at[a], device_id=(xi, yi, 1 - ci), device_id_type=MESH)
               for a in range(K)]
        for cp in cps:
            cp.start()
        for cp in cps:
            cp.wait()

    return pl.pallas_call(body, name=name, out_shape=[_sds(t.shape, t.dtype) for t in ts], in_specs=[ANY] * K,
                          out_specs=[ANY] * K,
                          scratch_shapes=[pltpu.SemaphoreType.DMA((K,)), pltpu.SemaphoreType.DMA((K,))])(*ts)


def _scatter_chips(name, ts):
    K = len(ts)

    def body(*refs):
        t_refs, out_refs, send_sems, recv_sems = refs[:K], refs[K:2 * K], refs[2 * K], refs[2 * K + 1]
        xi, yi, ci = _place()
        chips = [(1 - xi, yi), (xi, 1 - yi), (1 - xi, 1 - yi)]
        cps = [pltpu.make_async_remote_copy(src_ref=t_refs[a].at[2 * cx + cy], dst_ref=out_refs[a].at[j],
                                            send_sem=send_sems.at[3 * a + j], recv_sem=recv_sems.at[3 * a + j],
                                            device_id=(cx, cy, ci), device_id_type=MESH)
               for j, (cx, cy) in enumerate(chips) for a in range(K)]
        for cp in cps:
            cp.start()
        for cp in cps:
            cp.wait()

    return pl.pallas_call(body, name=name, out_shape=[_sds((3,) + t.shape[1:], t.dtype) for t in ts], in_specs=[ANY] * K,
                          out_specs=[ANY] * K,
                          scratch_shapes=[pltpu.SemaphoreType.DMA((3 * K,)), pltpu.SemaphoreType.DMA((3 * K,))])(*ts)


def _add_bf16(name, a, b):
    shape = a.shape
    N, C = shape[0], shape[-1]
    R = a.size // (N * C)
    tr = _tile(R, max(8, (1 << 19) // C))

    def body(a_ref, b_ref, o_ref):
        o_ref[...] = (a_ref[...].astype(F32) + b_ref[...].astype(F32)).astype(BF16)

    blk = pl.BlockSpec((None, tr, C), lambda n, i: (n, i, 0))
    return _call(body, name, (N, R // tr), [blk, blk], blk, _sds((N, R, C), BF16), sem=("parallel", "parallel"))(
        a.reshape(N, R, C), b.reshape(N, R, C)).reshape(shape)


def _sum_parts(name, parts):
    N, C = parts.shape

    def body(p_ref, o_ref):
        s = p_ref[pl.ds(0, 1), :]
        for d in range(1, N):
            s = s + p_ref[pl.ds(d, 1), :]
        o_ref[...] = s

    return _call(body, name, (1,), [pl.BlockSpec((N, C), lambda i: (0, 0))], pl.BlockSpec((1, C), lambda i: (0, 0)),
                 _sds((1, C), F32))(parts)


def _flat(parts, dtype, lead=()):
    flat = jnp.concatenate([t.reshape(lead + (-1,)).astype(dtype) for t in parts], axis=-1)
    n = flat.shape[-1]
    unit = 16 * FLAT_COLS
    padded = -(-n // unit) * unit
    flat = jnp.pad(flat, [(0, 0)] * len(lead) + [(0, padded - n)])
    return flat.reshape(lead + (padded // FLAT_COLS, FLAT_COLS))


def _unflat(flat, shapes, lead=()):
    flat = flat.reshape(lead + (-1,))
    out, off = [], 0
    for shp in shapes:
        n = math.prod(shp)
        out.append(flat[..., off:off + n].reshape(lead + tuple(shp)))
        off += n
    return out


def _dev_major(full, ax):
    shp = full.shape
    return jnp.moveaxis(full.reshape(shp[:ax] + (N_DEV, shp[ax] // N_DEV) + shp[ax + 1:]), ax, 0)


def _from_dev_major(blocks, ax):
    t = jnp.moveaxis(blocks, 0, ax)
    shp = t.shape
    return t.reshape(shp[:ax] + (shp[ax] * shp[ax + 1],) + shp[ax + 2:])


def kernel(x, *rest):
    nw = len(WEIGHTS)
    w = dict(zip(WEIGHTS, rest[:nw]))
    tgt = rest[nw]
    m = dict(zip(WEIGHTS, rest[nw + 1:2 * nw + 1]))
    v = dict(zip(WEIGHTS, rest[2 * nw + 1:3 * nw + 1]))
    xi, yi, ci = _place()
    dev = 4 * xi + 2 * yi + ci

    ffn3 = LATE[:3]
    first = ffn3 + MIXER_MATRICES[0]
    now = {"ffn_w_gate": 1, "ffn_w_up": 1, "ffn_w_down": EARLY_LAYERS}
    shard = {n: (w[n][:now[n]] if n in ffn3 else w[n]).astype(BF16) for n in first}
    late_shard = {n: (w[n][EARLY_LAYERS:] if n in ffn3 else w[n]).astype(BF16) for n in LATE}

    def joined(names):
        return lambda got: {n: _from_dev_major(t, SHARDED[n]) for n, t in zip(names, got)}

    big = _all_gather("gather_matrices", [shard[n] for n in first])
    small = _all_gather("gather_vectors", [_flat([w[n] for n in SMALL], F32)])[0]
    full = {n: w[n] for n in REPL}
    for n, blocks in zip(first, big):
        full[n] = _from_dev_major(blocks, SHARDED[n])
    for n, blocks in zip(SMALL, _unflat(small, [w[n].shape for n in SMALL], (N_DEV,))):
        full[n] = _from_dev_major(blocks, SHARDED[n])

    def by_owner(names, grads):
        return [_dev_major(t.reshape(t.shape[:1] + w[n].shape[1:SHARDED[n]] + (-1,) + w[n].shape[SHARDED[n] + 1:]),
                           SHARDED[n]).astype(BF16) for n, t in zip(names, grads)]

    beside = {"0a": ([w[n][1:EARLY_LAYERS].astype(BF16) for n in ffn3[:2]], 1, joined(ffn3[:2])),
              "0b": ([w[n].astype(BF16) for n in MIXER_MATRICES[1]], None, joined(MIXER_MATRICES[1]))}
    plan = dict(late_shards=[late_shard[n] for n in LATE], dev_major=by_owner, assemble=joined(LATE), beside_ffn=beside)
    loss, gx, g, travelled = _local_step(x, tgt, full, plan)

    vec_names = SMALL + REPL
    vec = _all_gather("gather_vector_grads", [_flat([g[n] for n in vec_names] + [loss], F32)])[0]
    *vec_list, losses = _unflat(vec, [g[n].shape for n in vec_names] + [(1, 1)], (N_DEV,))
    vec_parts = dict(zip(vec_names, vec_list))
    loss = _sum_parts("loss_sum", losses.reshape(N_DEV, 1))[0, 0]

    last = ffn3 + MIXER_MATRICES[0] + MIXER_MATRICES[1]
    keep, send = [], []
    for gd in by_owner(last, [g[n] for n in last]):
        gd = gd.reshape((4, 2) + gd.shape[1:])
        keep.append(lax.dynamic_index_in_dim(gd, ci, 1, keepdims=False))
        send.append(lax.dynamic_index_in_dim(gd, 1 - ci, 1, keepdims=False))
    got_sib = _swap_sibling("grads_to_sibling", send)
    pair = [_add_bf16(f"grad_pair_sum_{n}", a, b) for n, a, b in zip(last, keep, got_sib)]
    got_chips = _scatter_chips("grads_to_chips", pair)
    chip = 2 * xi + yi
    parts = {}
    for k, n in enumerate(last):
        parts[n] = [[lax.dynamic_index_in_dim(keep[k], chip, 0, keepdims=False),
                     lax.dynamic_index_in_dim(got_sib[k], chip, 0, keepdims=False), got_chips[k][0], got_chips[k][1], got_chips[k][2]]]
    for names, sent, got in reversed(travelled):
        for n, s, t in zip(names, sent, got):
            parts.setdefault(n, []).append([lax.dynamic_index_in_dim(s, dev, 0, keepdims=False)] + [t[k] for k in range(len(FLIPS))])

    res = {}
    for n in BIG:
        runs = parts[n]
        count = max(len(r) for r in runs)
        whole = [jnp.concatenate([r[k] if k < len(r) else jnp.zeros_like(r[0]) for r in runs], axis=0) if len(runs) > 1 else runs[0][k]
                 for k in range(count)]
        res[n] = _adamw(f"adamw_{n}", w[n], m[n], v[n], whole)
    for n in vec_names:
        parts = vec_parts[n]
        if n in SHARDED:
            ax = SHARDED[n]
            parts = lax.dynamic_slice_in_dim(parts, dev * w[n].shape[ax], w[n].shape[ax], ax + 1)
        res[n] = _adamw(f"adamw_{n}", w[n], m[n], v[n], [parts[d] for d in range(N_DEV)])
    return (loss, gx, *[res[n][0] for n in WEIGHTS], *[res[n][1] for n in WEIGHTS], *[res[n][2] for n in WEIGHTS],
            *[res[n][3] for n in WEIGHTS])
```
